```python
import math
import jax, jax.numpy as jnp
from jax import lax
import numpy as np

D_MODEL = 4096
BATCH = 2
SEQ = 4096
DEPTH = 2

N_EVEN = (DEPTH + 1) // 2
N_ODD = DEPTH // 2

SSM_D_INNER = D_MODEL
SSM_HEAD_DIM = 64
SSM_HEADS = SSM_D_INNER // SSM_HEAD_DIM
SSM_GROUPS = 8
SSM_HEADS_PER_GROUP = SSM_HEADS // SSM_GROUPS
SSM_STATE = 128
SSM_CONV = 4
SSM_CHUNK = 128
SSM_CONV_CH = SSM_D_INNER + 2 * SSM_GROUPS * SSM_STATE
SSM_NORM_EPS = 1e-5

ATT_HEADS = 64
ATT_KV_HEADS = 8
ATT_HEAD_DIM = 64
ATT_REP = ATT_HEADS // ATT_KV_HEADS
ATT_WINDOW = 128
ATT_BLOCK = ATT_WINDOW
ROPE_DIM = ATT_HEAD_DIM // 4
ROPE_THETA = 500000.0

Q_COLS = ATT_HEADS * ATT_HEAD_DIM
KV_COLS = ATT_KV_HEADS * ATT_HEAD_DIM
QKV_COLS = Q_COLS + 2 * KV_COLS
IN_COLS = SSM_D_INNER + SSM_CONV_CH + SSM_HEADS + QKV_COLS
MIX_WIDTH = SSM_D_INNER + Q_COLS

RWKV_HEAD = 64
RWKV_HEADS = D_MODEL // RWKV_HEAD
DECAY_LORA = max(32, int(round(1.8 * D_MODEL ** 0.5 / 32)) * 32)
AAA_LORA = max(32, int(round(1.8 * D_MODEL ** 0.5 / 32)) * 32)
GATE_LORA = max(32, int(round(0.6 * D_MODEL ** 0.8 / 32)) * 32)
RWKV_GN_EPS = 64e-5

N_EXPERTS = 16
N_EXPERT_GROUPS = 4
EXPERTS_PER_GROUP = N_EXPERTS // N_EXPERT_GROUPS
TOP_K = 2
D_EXPERT = 1536

ALPHA = (2 * DEPTH) ** 0.25
BETA = (8 * DEPTH) ** -0.25
LN_EPS = 1e-5

kernel_name = 'hybrid_ssd_swa_rwkv7_grouped_moe'

F32 = jnp.float32


def layer_norm(x, g, b):
    xf = x.astype(F32)
    mu = jnp.mean(xf, -1, keepdims=True)
    var = jnp.mean(jnp.square(xf - mu), -1, keepdims=True)
    return ((xf - mu) * lax.rsqrt(var + LN_EPS) * g + b).astype(x.dtype)


def causal_depthwise_conv(u, w, b):
    k = w.shape[-1]
    up = jnp.pad(u, ((0, 0), (k - 1, 0), (0, 0)))
    out = lax.conv_general_dilated(up, w.T[:, None, :].astype(u.dtype), window_strides=(1,), padding='VALID',
                                   dimension_numbers=('NWC', 'WIO', 'NWC'), feature_group_count=u.shape[-1])
    return out + b


def segsum_exp(a_cs):
    l = a_cs.shape[-1]
    diff = a_cs[..., :, None] - a_cs[..., None, :]
    mask = jnp.tril(jnp.ones((l, l), dtype=bool))
    return jnp.where(mask, jnp.exp(jnp.where(mask, diff, 0.0)), 0.0)


def ssd_chunked_scan(xh, dt, a, bm, cm):
    b, s, g, r, p = xh.shape
    n = bm.shape[-1]
    c, l = s // SSM_CHUNK, SSM_CHUNK
    xdt = (xh * dt[..., None]).reshape(b, c, l, g, r, p)
    a_cs = jnp.cumsum((dt * a).reshape(b, c, l, g, r).transpose(0, 1, 3, 4, 2), axis=-1)
    bc = bm.reshape(b, c, l, g, n)
    cc = cm.reshape(b, c, l, g, n)
    cb = jnp.einsum('bclgn,bcsgn->bcgls', cc, bc)
    y_diag = jnp.einsum('bcgls,bcgrls,bcsgrp->bclgrp', cb, segsum_exp(a_cs), xdt)
    decay_states = jnp.exp(a_cs[..., -1:] - a_cs)
    states = jnp.einsum('bclgn,bcgrl,bclgrp->bcgrpn', bc, decay_states, xdt)
    chunk_decay = jnp.exp(a_cs[..., -1])

    def step(h, inp):
        st, dec = inp
        return h * dec[..., None, None] + st, h

    h0 = jnp.zeros((b, g, r, p, n), F32)
    _, prev = lax.scan(step, h0, (states.swapaxes(0, 1), chunk_decay.swapaxes(0, 1)))
    prev = prev.swapaxes(0, 1)
    y_off = jnp.einsum('bclgn,bcgrpn,bcgrl->bclgrp', cc, prev, jnp.exp(a_cs))
    return (y_diag + y_off).reshape(b, s, g, r, p)


def gated_group_rmsnorm(y, z, w):
    b, s, d = y.shape
    u = (y * jax.nn.silu(z.astype(F32))).reshape(b, s, SSM_GROUPS, d // SSM_GROUPS)
    u = u * lax.rsqrt(jnp.mean(u * u, -1, keepdims=True) + SSM_NORM_EPS)
    return u.reshape(b, s, d) * w


def ssd_mixer(z, xbc, dt_raw, conv_w, conv_b, dt_bias, a_log, d_skip, norm_w):
    b, s, _ = z.shape
    xbc = jax.nn.silu(causal_depthwise_conv(xbc, conv_w, conv_b)).astype(F32)
    xs, bm, cm = jnp.split(xbc, [SSM_D_INNER, SSM_D_INNER + SSM_GROUPS * SSM_STATE], axis=-1)
    xh = xs.reshape(b, s, SSM_GROUPS, SSM_HEADS_PER_GROUP, SSM_HEAD_DIM)
    bm = bm.reshape(b, s, SSM_GROUPS, SSM_STATE)
    cm = cm.reshape(b, s, SSM_GROUPS, SSM_STATE)
    dt = jax.nn.softplus(dt_raw.astype(F32) + dt_bias).reshape(b, s, SSM_GROUPS, SSM_HEADS_PER_GROUP)
    a = -jnp.exp(a_log.astype(F32)).reshape(SSM_GROUPS, SSM_HEADS_PER_GROUP)
    y = ssd_chunked_scan(xh, dt, a, bm, cm)
    y = y + d_skip.astype(F32).reshape(SSM_GROUPS, SSM_HEADS_PER_GROUP)[..., None] * xh
    return gated_group_rmsnorm(y.reshape(b, s, SSM_D_INNER), z, norm_w)


def partial_rotary(t, pos):
    half = ROPE_DIM // 2
    inv_freq = ROPE_THETA ** (-jnp.arange(half, dtype=F32) / half)
    ang = pos[:, None] * inv_freq[None, :]
    cos = jnp.cos(ang)[:, None, :]
    sin = jnp.sin(ang)[:, None, :]
    t1 = t[..., :half].astype(F32)
    t2 = t[..., half:ROPE_DIM].astype(F32)
    rot = jnp.concatenate([t1 * cos - t2 * sin, t2 * cos + t1 * sin], -1).astype(t.dtype)
    return jnp.concatenate([rot, t[..., ROPE_DIM:]], -1)


def sliding_window_sink_attention(q, k, v, sinks):
    b, s, _, dh = q.shape
    nb, w = s // ATT_BLOCK, ATT_BLOCK
    qb = q.reshape(b, nb, w, ATT_KV_HEADS, ATT_REP, dh)

    def with_prev(t):
        prev = jnp.pad(t, ((0, 0), (w, 0), (0, 0), (0, 0)))[:, :s]
        return jnp.concatenate([prev.reshape(b, nb, w, ATT_KV_HEADS, dh),
                                t.reshape(b, nb, w, ATT_KV_HEADS, dh)], axis=2)

    kb, vb = with_prev(k), with_prev(v)
    scores = jnp.einsum('bnqhrd,bnkhd->bnhrqk', qb, kb).astype(F32) * (dh ** -0.5)
    blk = jnp.arange(nb)[:, None, None]
    qi = jnp.arange(w)[None, :, None] + w
    ki = jnp.arange(2 * w)[None, None, :]
    rel = qi - ki
    mask = (rel >= 0) & (rel < ATT_WINDOW) & ((blk > 0) | (ki >= w))
    scores = jnp.where(mask[None, :, None, None], scores, -jnp.inf)
    sink = sinks.astype(F32).reshape(ATT_KV_HEADS, ATT_REP)[None, None, :, :, None, None]
    m = jnp.maximum(jnp.max(scores, -1, keepdims=True), sink)
    e = jnp.exp(scores - m)
    probs = e / (jnp.sum(e, -1, keepdims=True) + jnp.exp(sink - m))
    out = jnp.einsum('bnhrqk,bnkhd->bnqhrd', probs.astype(v.dtype), vb)
    return out.reshape(b, s, Q_COLS)


def ssd_swa_mixer(x, pos, w_in, b_qkv, conv_w, conv_b, dt_bias, a_log, d_skip, norm_w, sinks, w_out, b_out):
    b, s, _ = x.shape
    u = x @ w_in
    o1 = SSM_D_INNER
    o2 = o1 + SSM_CONV_CH
    o3 = o2 + SSM_HEADS
    z, xbc, dt_raw, qkv = u[..., :o1], u[..., o1:o2], u[..., o2:o3], u[..., o3:] + b_qkv
    y_ssm = ssd_mixer(z, xbc, dt_raw, conv_w, conv_b, dt_bias, a_log, d_skip, norm_w)
    q, k, v = jnp.split(qkv, [Q_COLS, Q_COLS + KV_COLS], axis=-1)
    q = partial_rotary(q.reshape(b, s, ATT_HEADS, ATT_HEAD_DIM), pos)
    k = partial_rotary(k.reshape(b, s, ATT_KV_HEADS, ATT_HEAD_DIM), pos)
    v = v.reshape(b, s, ATT_KV_HEADS, ATT_HEAD_DIM)
    y_att = sliding_window_sink_attention(q, k, v, sinks)
    y = jnp.concatenate([y_ssm.astype(x.dtype), y_att.astype(x.dtype)], axis=-1)
    return y @ w_out + b_out


def wkv7_scan(r, w, k, v, a, bb):
    b, s, h, n = r.shape

    def step(st, inp):
        r_t, w_t, k_t, v_t, a_t, b_t = inp
        sa = jnp.einsum('bhij,bhj->bhi', st, a_t)
        st = st * w_t[:, :, None, :] + sa[..., None] * b_t[:, :, None, :] + v_t[..., None] * k_t[:, :, None, :]
        return st, jnp.einsum('bhij,bhj->bhi', st, r_t)

    s0 = jnp.zeros((b, h, n, n), F32)
    _, y = lax.scan(step, s0, (r.swapaxes(0, 1), w.swapaxes(0, 1), k.swapaxes(0, 1),
                                v.swapaxes(0, 1), a.swapaxes(0, 1), bb.swapaxes(0, 1)))
    return y.swapaxes(0, 1)


def rwkv7_time_mix(x, mix, w_r, w_k, w_v, w_o, w0, w1, w2, a0, a1, a2, g1, g2, k_k, k_a, r_k, ln_w, ln_b):
    b, s, d = x.shape
    h, n = RWKV_HEADS, RWKV_HEAD
    xx = jnp.pad(x, ((0, 0), (1, 0), (0, 0)))[:, :-1] - x
    xr, xw, xk, xv, xa, xg = [x + xx * mix[i] for i in range(6)]
    r = (xr @ w_r).astype(F32)
    k = (xk @ w_k).astype(F32)
    v = (xv @ w_v).astype(F32)
    w_log = -jax.nn.softplus(-(w0 + jnp.tanh(xw @ w1) @ w2).astype(F32)) - 0.5
    decay = jnp.exp(-jnp.exp(w_log))
    a = jax.nn.sigmoid((a0 + (xa @ a1) @ a2).astype(F32))
    g = jax.nn.sigmoid(xg @ g1) @ g2
    kk = (k * k_k).reshape(b, s, h, n)
    kk = kk / jnp.maximum(jnp.sqrt(jnp.sum(kk * kk, -1, keepdims=True)), 1e-12)
    k = k * (1.0 + (a - 1.0) * k_k.astype(F32) * 0.0 + (a - 1.0) * k_a) if False else k * (1.0 + (a - 1.0) * k_a)
    rh = r.reshape(b, s, h, n)
    kh = k.reshape(b, s, h, n)
    vh = v.reshape(b, s, h, n)
    ah = a.reshape(b, s, h, n)
    y = wkv7_scan(rh, decay.reshape(b, s, h, n), kh, vh, -kk, kk * ah)
    mu = jnp.mean(y, -1, keepdims=True)
    var = jnp.mean(jnp.square(y - mu), -1, keepdims=True)
    y = ((y - mu) * lax.rsqrt(var + RWKV_GN_EPS)).reshape(b, s, d) * ln_w + ln_b
    bonus = jnp.sum(rh * kh * r_k.astype(F32), -1, keepdims=True) * vh
    y = y + bonus.reshape(b, s, d)
    return (y.astype(x.dtype) * g) @ w_o


def grouped_moe(h, router_w, router_b, w_gate, w_up, w_down):
    b, s, d = h.shape
    t = h.reshape(b * s, d)
    n_tok = t.shape[0]
    probs = jax.nn.softmax((t @ router_w).astype(F32) + router_b, axis=-1)
    pg = probs.reshape(n_tok, N_EXPERT_GROUPS, EXPERTS_PER_GROUP)
    group_score = jnp.sum(lax.top_k(pg, TOP_K)[0], -1)
    grp = jnp.argmax(group_score, -1)
    in_grp = pg[jnp.arange(n_tok), grp]
    top_v, top_i = lax.top_k(in_grp, TOP_K)
    gate_w = top_v / jnp.sum(top_v, -1, keepdims=True)
    expert_id = grp[:, None] * EXPERTS_PER_GROUP + top_i
    gates = jnp.einsum('tk,tke->te', gate_w, jax.nn.one_hot(expert_id, N_EXPERTS, dtype=F32)).astype(t.dtype)
    out = jnp.zeros_like(t)
    for e in range(N_EXPERTS):
        he = jax.nn.silu(t @ w_gate[e]) * (t @ w_up[e])
        out = out + gates[:, e:e + 1] * (he @ w_down[e])
    return out.reshape(b, s, d)


def setup_inputs(seed: int = 0) -> dict:
    key = jax.random.key(seed)
    ks = iter(jax.random.split(key, 48))

    def nrm(shape, scale):
        return jax.random.normal(next(ks), shape, F32) * scale

    def uni(shape, lo, hi):
        return jax.random.uniform(next(ks), shape, F32, lo, hi)

    dt0 = jnp.exp(uni((N_EVEN, SSM_HEADS), math.log(1e-3), math.log(1e-1)))
    return {
        'x': nrm((BATCH, SEQ, D_MODEL), 1.0),
        'ab_w_in': nrm((N_EVEN, D_MODEL, IN_COLS), D_MODEL ** -0.5),
        'ab_b_qkv': nrm((N_EVEN, QKV_COLS), 0.01),
        'ssm_conv_w': nrm((N_EVEN, SSM_CONV_CH, SSM_CONV), SSM_CONV ** -0.5),
        'ssm_conv_b': nrm((N_EVEN, SSM_CONV_CH), 0.01),
        'ssm_dt_bias': dt0 + jnp.log(-jnp.expm1(-dt0)),
        'ssm_a_log': jnp.log(uni((N_EVEN, SSM_HEADS), 1.0, 16.0)),
        'ssm_d': 1.0 + nrm((N_EVEN, SSM_HEADS), 0.01),
        'ssm_norm_w': 1.0 + nrm((N_EVEN, SSM_D_INNER), 0.01),
        'attn_sinks': nrm((N_EVEN, ATT_HEADS), 0.5),
        'ab_w_out': nrm((N_EVEN, MIX_WIDTH, D_MODEL), MIX_WIDTH ** -0.5 * BETA),
        'ab_b_out': nrm((N_EVEN, D_MODEL), 0.01),
        'rwkv_mix': uni((N_ODD, 6, D_MODEL), 0.0, 1.0),
        'rwkv_w_r': nrm((N_ODD, D_MODEL, D_MODEL), D_MODEL ** -0.5),
        'rwkv_w_k': nrm((N_ODD, D_MODEL, D_MODEL), D_MODEL ** -0.5),
        'rwkv_w_v': nrm((N_ODD, D_MODEL, D_MODEL), D_MODEL ** -0.5),
        'rwkv_w_o': nrm((N_ODD, D_MODEL, D_MODEL), D_MODEL ** -0.5 * BETA),
        'rwkv_w0': uni((N_ODD, D_MODEL), -5.5, -0.5),
        'rwkv_w1': nrm((N_ODD, D_MODEL, DECAY_LORA), D_MODEL ** -0.5),
        'rwkv_w2': nrm((N_ODD, DECAY_LORA, D_MODEL), 0.1 * DECAY_LORA ** -0.5),
        'rwkv_a0': nrm((N_ODD, D_MODEL), 0.1),
        'rwkv_a1': nrm((N_ODD, D_MODEL, AAA_LORA), D_MODEL ** -0.5),
        'rwkv_a2': nrm((N_ODD, AAA_LORA, D_MODEL), 0.1 * AAA_LORA ** -0.5),
        'rwkv_g1': nrm((N_ODD, D_MODEL, GATE_LORA), D_MODEL ** -0.5),
        'rwkv_g2': nrm((N_ODD, GATE_LORA, D_MODEL), GATE_LORA ** -0.5),
        'rwkv_k_k': 0.85 + nrm((N_ODD, D_MODEL), 0.01),
        'rwkv_k_a': 1.0 + nrm((N_ODD, D_MODEL), 0.01),
        'rwkv_r_k': nrm((N_ODD, RWKV_HEADS, RWKV_HEAD), 0.1),
        'rwkv_ln_w': 1.0 + nrm((N_ODD, D_MODEL), 0.01),
        'rwkv_ln_b': nrm((N_ODD, D_MODEL), 0.01),
        'ln_mix_g': 1.0 + nrm((DEPTH, D_MODEL), 0.01),
        'ln_mix_b': nrm((DEPTH, D_MODEL), 0.01),
        'ln_ffn_g': 1.0 + nrm((DEPTH, D_MODEL), 0.01),
        'ln_ffn_b': nrm((DEPTH, D_MODEL), 0.01),
        'router_w': nrm((D_MODEL, N_EXPERTS), D_MODEL ** -0.5),
        'router_b': nrm((N_EXPERTS,), 0.01),
        'moe_w_gate': nrm((DEPTH, N_EXPERTS, D_MODEL, D_EXPERT), D_MODEL ** -0.5),
        'moe_w_up': nrm((DEPTH, N_EXPERTS, D_MODEL, D_EXPERT), D_MODEL ** -0.5),
        'moe_w_down': nrm((DEPTH, N_EXPERTS, D_EXPERT, D_MODEL), D_EXPERT ** -0.5 * BETA),
    }


def reference(x, ab_w_in, ab_b_qkv, ssm_conv_w, ssm_conv_b, ssm_dt_bias, ssm_a_log, ssm_d, ssm_norm_w,
              attn_sinks, ab_w_out, ab_b_out, rwkv_mix, rwkv_w_r, rwkv_w_k, rwkv_w_v, rwkv_w_o, rwkv_w0,
              rwkv_w1, rwkv_w2, rwkv_a0, rwkv_a1, rwkv_a2, rwkv_g1, rwkv_g2, rwkv_k_k, rwkv_k_a, rwkv_r_k,
              rwkv_ln_w, rwkv_ln_b, ln_mix_g, ln_mix_b, ln_ffn_g, ln_ffn_b, router_w, router_b,
              moe_w_gate, moe_w_up, moe_w_down):
    pos = jnp.arange(x.shape[1], dtype=F32)
    for layer in range(DEPTH):
        i = layer // 2
        if layer % 2 == 0:
            mix = ssd_swa_mixer(x, pos, ab_w_in[i], ab_b_qkv[i], ssm_conv_w[i], ssm_conv_b[i], ssm_dt_bias[i],
                                ssm_a_log[i], ssm_d[i], ssm_norm_w[i], attn_sinks[i], ab_w_out[i], ab_b_out[i])
        else:
            mix = rwkv7_time_mix(x, rwkv_mix[i], rwkv_w_r[i], rwkv_w_k[i], rwkv_w_v[i], rwkv_w_o[i], rwkv_w0[i],
                                 rwkv_w1[i], rwkv_w2[i], rwkv_a0[i], rwkv_a1[i], rwkv_a2[i], rwkv_g1[i], rwkv_g2[i],
                                 rwkv_k_k[i], rwkv_k_a[i], rwkv_r_k[i], rwkv_ln_w[i], rwkv_ln_b[i])
        h = layer_norm(ALPHA * x + mix, ln_mix_g[layer], ln_mix_b[layer])
        ffn = grouped_moe(h, router_w, router_b, moe_w_gate[layer], moe_w_up[layer], moe_w_down[layer])
        x = layer_norm(ALPHA * h + ffn, ln_ffn_g[layer], ln_ffn_b[layer])
    return x
```

```python
import functools
import math

import jax
import jax.numpy as jnp
import numpy as np
from jax import lax
from jax.experimental import pallas as pl
from jax.experimental.pallas import tpu as pltpu

F32 = jnp.float32
BF16 = jnp.bfloat16
HIGHEST = lax.Precision.HIGHEST

D_MODEL = 4096
DEPTH = 2
SSM_HEAD_DIM = 64
SSM_HEADS = 64
SSM_GROUPS = 8
SSM_HEADS_PER_GROUP = 8
SSM_STATE = 128
SSM_CONV = 4
SSM_CHUNK = 128
SSM_GROUP_WIDTH = SSM_HEADS_PER_GROUP * SSM_HEAD_DIM
SSM_CONV_CH = D_MODEL + 2 * SSM_GROUPS * SSM_STATE
SSM_NORM_EPS = 1e-5
ATT_HEADS = 64
ATT_KV_HEADS = 8
ATT_HEAD_DIM = 64
ATT_REP = 8
ATT_WINDOW = 128
ROPE_DIM = 16
ROPE_THETA = 500000.0
Q_COLS = 4096
KV_COLS = 512
RWKV_HEAD = 64
RWKV_GN_EPS = 64e-5
N_EXPERTS = 16
N_EXPERT_GROUPS = 4
EXPERTS_PER_GROUP = 4
TOP_K = 2
D_EXPERT = 1536
ALPHA = (2 * DEPTH) ** 0.25
LN_EPS = 1e-5

VMEM_LIMIT_BYTES = 56 * 1024 * 1024
LANES = 128
SUBLANES = 8


def _cparams(semantics):
    return pltpu.CompilerParams(dimension_semantics=semantics, vmem_limit_bytes=VMEM_LIMIT_BYTES)


def _tile(dim, pref):
    if dim <= pref:
        return dim
    t = pref
    while dim % t:
        t //= 2
    return t


def _silu(v):
    return v * jax.nn.sigmoid(v)


def _softplus(v):
    return jnp.maximum(v, 0.0) + jnp.log1p(jnp.exp(-jnp.abs(v)))


def _mm_kernel(a_ref, w_ref, b_ref, o_ref, acc_ref, *, nk, act, precision):
    k = pl.program_id(2)

    @pl.when(k == 0)
    def _():
        acc_ref[...] = jnp.zeros_like(acc_ref)

    acc_ref[...] += jnp.dot(a_ref[...], w_ref[...], preferred_element_type=F32, precision=precision)

    @pl.when(k == nk - 1)
    def _():
        r = acc_ref[...] + b_ref[...]
        if act == "tanh":
            r = jnp.tanh(r)
        elif act == "sigmoid":
            r = jax.nn.sigmoid(r)
        o_ref[...] = r.astype(o_ref.dtype)


def _matmul(a, w, bias=None, *, act=None, out_dtype=F32, precision=None, tm=1024, tn=512, tk=2048):
    m, kdim = a.shape
    n = w.shape[1]
    tm, tn, tk = _tile(m, tm), _tile(n, tn), _tile(kdim, tk)
    nk = kdim // tk
    if bias is None:
        bias = jnp.zeros((n,), F32)
    bias = bias.reshape(1, n).astype(F32)
    return pl.pallas_call(
        functools.partial(_mm_kernel, nk=nk, act=act, precision=precision),
        grid=(m // tm, n // tn, nk),
        in_specs=[
            pl.BlockSpec((tm, tk), lambda i, j, k: (i, k)),
            pl.BlockSpec((tk, tn), lambda i, j, k: (k, j)),
            pl.BlockSpec((1, tn), lambda i, j, k: (0, j)),
        ],
        out_specs=pl.BlockSpec((tm, tn), lambda i, j, k: (i, j)),
        out_shape=jax.ShapeDtypeStruct((m, n), out_dtype),
        scratch_shapes=[pltpu.VMEM((tm, tn), F32)],
        compiler_params=_cparams(("parallel", "parallel", "arbitrary")),
        name="matmul",
    )(a, w, bias)


def _layer_norm_rows(v, g, b):
    mu = jnp.mean(v, -1, keepdims=True)
    var = jnp.mean(jnp.square(v - mu), -1, keepdims=True)
    return (v - mu) * lax.rsqrt(var + LN_EPS) * g + b


def _add_ln_kernel(x_ref, y_ref, g_ref, b_ref, o_ref, ob_ref):
    r = _layer_norm_rows(ALPHA * x_ref[...] + y_ref[...], g_ref[...], b_ref[...])
    o_ref[...] = r
    ob_ref[...] = r.astype(BF16)


def _add_layer_norm(x, y, g, b, *, tm=128):
    t, d = x.shape
    tm = _tile(t, tm)
    row = pl.BlockSpec((tm, d), lambda i: (i, 0))
    vec = pl.BlockSpec((1, d), lambda i: (0, 0))
    return pl.pallas_call(
        _add_ln_kernel,
        grid=(t // tm,),
        in_specs=[row, row, vec, vec],
        out_specs=[row, row],
        out_shape=[jax.ShapeDtypeStruct((t, d), F32), jax.ShapeDtypeStruct((t, d), BF16)],
        compiler_params=_cparams(("parallel",)),
        name="add_layer_norm",
    )(x, y, g.reshape(1, d), b.reshape(1, d))


CONV_HALO = SUBLANES
SSD_CH = SSM_GROUP_WIDTH + 2 * SSM_STATE


def _ssd_kernel(z_ref, x_ref, b_ref, c_ref, wx_ref, wb_ref, wc_ref, bx_ref, bb_ref, bc_ref,
                dtc_ref, dtr_ref, dbc_ref, dbr_ref, alc_ref, alr_ref, d_ref, nw_ref,
                o_ref, state_ref, pad_ref):
    ch, gw, ns, hd = SSM_CHUNK, SSM_GROUP_WIDTH, SSM_STATE, SSM_HEAD_DIM

    @pl.when(pl.program_id(2) == 0)
    def _():
        state_ref[...] = jnp.zeros_like(state_ref)
        pad_ref[0:CONV_HALO, :] = jnp.zeros((CONV_HALO, SSD_CH), F32)

    pad_ref[CONV_HALO:CONV_HALO + ch, 0:gw] = x_ref[...]
    pad_ref[CONV_HALO:CONV_HALO + ch, gw:gw + ns] = b_ref[...]
    pad_ref[CONV_HALO:CONV_HALO + ch, gw + ns:SSD_CH] = c_ref[...]
    w = jnp.concatenate([wx_ref[...], wb_ref[...], wc_ref[...]], axis=1)
    acc = jnp.concatenate([bx_ref[...], bb_ref[...], bc_ref[...]], axis=1)
    base = CONV_HALO - (SSM_CONV - 1)
    for k in range(SSM_CONV):
        acc = acc + w[k:k + 1, :] * pad_ref[base + k:base + k + ch, :]
    pad_ref[0:CONV_HALO, :] = pad_ref[ch:ch + CONV_HALO, :]
    xbc = _silu(acc)
    xc, bm, cm = xbc[:, :gw], xbc[:, gw:gw + ns], xbc[:, gw + ns:]

    dt_c = _softplus(dtc_ref[0] + dbc_ref[0])
    dt_r = _softplus(dtr_ref[...] + dbr_ref[...])
    a_c = -jnp.exp(alc_ref[0])
    a_r = -jnp.exp(alr_ref[...])
    row = lax.broadcasted_iota(jnp.int32, (ch, ch), 0)
    col = lax.broadcasted_iota(jnp.int32, (ch, ch), 1)
    causal = row >= col
    acs_c = jnp.dot(causal.astype(F32), dt_c * a_c, precision=HIGHEST, preferred_element_type=F32)
    acs_r = jnp.dot(dt_r * a_r, (row <= col).astype(F32), precision=HIGHEST, preferred_element_type=F32)

    hrow = lax.broadcasted_iota(jnp.int32, (SSM_HEADS_PER_GROUP, gw), 0)
    hcol = lax.broadcasted_iota(jnp.int32, (SSM_HEADS_PER_GROUP, gw), 1)
    expand = (hcol // hd == hrow).astype(F32)
    dt_x = jnp.dot(dt_c, expand, precision=HIGHEST, preferred_element_type=F32)
    acs_x = jnp.dot(acs_c, expand, precision=HIGHEST, preferred_element_type=F32)

    xdt = xc * dt_x
    xdt_b = xdt.astype(BF16)
    bm_b, cm_b = bm.astype(BF16), cm.astype(BF16)
    cb = lax.dot_general(cm_b, bm_b, (((1,), (1,)), ((), ())), preferred_element_type=F32)
    prev = state_ref[...]
    y_off = jnp.dot(cm_b, prev.astype(BF16), preferred_element_type=F32) * jnp.exp(acs_x)

    lane = lax.broadcasted_iota(jnp.int32, (ch, 2 * hd), 1)
    pairs = []
    for pr in range(SSM_HEADS_PER_GROUP // 2):
        xp = xdt_b[:, pr * 2 * hd:(pr + 1) * 2 * hd]
        halves = []
        for q in range(2):
            r = 2 * pr + q
            diff = acs_c[:, r:r + 1] - acs_r[r:r + 1, :]
            seg = jnp.where(causal, jnp.exp(jnp.where(causal, diff, 0.0)), 0.0)
            halves.append(jnp.dot((cb * seg).astype(BF16), xp, preferred_element_type=F32))
        pairs.append(jnp.where(lane < hd, halves[0], halves[1]))
    y = jnp.concatenate(pairs, axis=1) + y_off + d_ref[...] * xc

    acs_last = acs_x[ch - 1:ch, :]
    contrib = lax.dot_general(bm_b, (xdt * jnp.exp(acs_last - acs_x)).astype(BF16),
                              (((0,), (0,)), ((), ())), preferred_element_type=F32)
    state_ref[...] = prev * jnp.exp(acs_last) + contrib

    u = y * _silu(z_ref[...])
    u = u * lax.rsqrt(jnp.mean(u * u, -1, keepdims=True) + SSM_NORM_EPS)
    o_ref[...] = (u * nw_ref[...]).astype(o_ref.dtype)


def _ssd_mixer(u_z, u_xbc, u_dt, conv_w, conv_b, dt_bias, a_log, d_skip, norm_w, batch, seq):
    t = batch * seq
    nc = seq // SSM_CHUNK
    g, r, gw, ns = SSM_GROUPS, SSM_HEADS_PER_GROUP, SSM_GROUP_WIDTH, SSM_STATE
    xblocks = D_MODEL // ns
    conv_wt = conv_w.T
    conv_b2 = conv_b.reshape(1, SSM_CONV_CH)
    dt_col = u_dt.reshape(t, g, r).transpose(1, 0, 2)
    dt_row = u_dt.T
    rows = lambda b, gi, c: (b * nc + c, gi)
    in_specs = [
        pl.BlockSpec((SSM_CHUNK, gw), rows),
        pl.BlockSpec((SSM_CHUNK, gw), rows),
        pl.BlockSpec((SSM_CHUNK, ns), lambda b, gi, c: (b * nc + c, xblocks + gi)),
        pl.BlockSpec((SSM_CHUNK, ns), lambda b, gi, c: (b * nc + c, xblocks + g + gi)),
        pl.BlockSpec((SSM_CONV, gw), lambda b, gi, c: (0, gi)),
        pl.BlockSpec((SSM_CONV, ns), lambda b, gi, c: (0, xblocks + gi)),
        pl.BlockSpec((SSM_CONV, ns), lambda b, gi, c: (0, xblocks + g + gi)),
        pl.BlockSpec((1, gw), lambda b, gi, c: (0, gi)),
        pl.BlockSpec((1, ns), lambda b, gi, c: (0, xblocks + gi)),
        pl.BlockSpec((1, ns), lambda b, gi, c: (0, xblocks + g + gi)),
        pl.BlockSpec((1, SSM_CHUNK, r), lambda b, gi, c: (gi, b * nc + c, 0)),
        pl.BlockSpec((r, SSM_CHUNK), lambda b, gi, c: (gi, b * nc + c)),
        pl.BlockSpec((1, 1, r), lambda b, gi, c: (gi, 0, 0)),
        pl.BlockSpec((r, 1), lambda b, gi, c: (gi, 0)),
        pl.BlockSpec((1, 1, r), lambda b, gi, c: (gi, 0, 0)),
        pl.BlockSpec((r, 1), lambda b, gi, c: (gi, 0)),
        pl.BlockSpec((1, gw), lambda b, gi, c: (0, gi)),
        pl.BlockSpec((1, gw), lambda b, gi, c: (0, gi)),
    ]
    return pl.pallas_call(
        _ssd_kernel,
        grid=(batch, g, nc),
        in_specs=in_specs,
        out_specs=pl.BlockSpec((SSM_CHUNK, gw), rows),
        out_shape=jax.ShapeDtypeStruct((t, D_MODEL), BF16),
        scratch_shapes=[pltpu.VMEM((ns, gw), F32), pltpu.VMEM((CONV_HALO + SSM_CHUNK, SSD_CH), F32)],
        compiler_params=_cparams(("parallel", "parallel", "arbitrary")),
        name="ssd_mixer",
    )(u_z, u_xbc, u_xbc, u_xbc, conv_wt, conv_wt, conv_wt, conv_b2, conv_b2, conv_b2,
      dt_col, dt_row, dt_bias.reshape(g, 1, r), dt_bias.reshape(SSM_HEADS, 1),
      a_log.reshape(g, 1, r), a_log.reshape(SSM_HEADS, 1),
      jnp.repeat(d_skip, SSM_HEAD_DIM).reshape(1, D_MODEL), norm_w.reshape(1, D_MODEL))


def _rotary_tables(seq):
    half = ROPE_DIM // 2
    inv_freq = ROPE_THETA ** (-jnp.arange(half, dtype=F32) / half)
    ang = jnp.arange(seq, dtype=F32)[:, None] * inv_freq[None, :]
    ones = jnp.ones((seq, ATT_HEAD_DIM - ROPE_DIM), F32)
    cos = jnp.concatenate([jnp.cos(ang), jnp.cos(ang), ones], -1)
    sin = jnp.concatenate([jnp.sin(ang), jnp.sin(ang), 0.0 * ones], -1)
    return jnp.tile(cos, (1, ATT_REP)), jnp.tile(sin, (1, ATT_REP))


def _rotate_half_matrix():
    width = ATT_REP * ATT_HEAD_DIM
    half = ROPE_DIM // 2
    p = np.zeros((width, width), np.float32)
    for j in range(width):
        if j % ATT_HEAD_DIM < half:
            p[j + half, j] = -1.0
        elif j % ATT_HEAD_DIM < ROPE_DIM:
            p[j - half, j] = 1.0
    return jnp.asarray(p, BF16)


def _rotate(v, cos, sin, perm):
    hi = v.astype(BF16)
    lo = (v - hi.astype(F32)).astype(BF16)
    partner = jnp.dot(hi, perm, preferred_element_type=F32) + jnp.dot(lo, perm, preferred_element_type=F32)
    return v * cos + partner * sin


def _swa_kernel(sink_ref, q_ref, kc_ref, kp_ref, vc_ref, vp_ref, cosc_ref, sinc_ref, cosp_ref, sinp_ref,
                perm_ref, o_ref):
    w, hd = ATT_WINDOW, ATT_HEAD_DIM
    h = pl.program_id(1)
    n = pl.program_id(2)
    perm = perm_ref[...]
    perm_k = perm[:hd, :hd]
    q = _rotate(q_ref[...], cosc_ref[...], sinc_ref[...], perm)
    k_cur = _rotate(kc_ref[0, 0], cosc_ref[:, :hd], sinc_ref[:, :hd], perm_k)
    k_prev = _rotate(kp_ref[0, 0], cosp_ref[:, :hd], sinp_ref[:, :hd], perm_k)
    k_all = jnp.concatenate([k_prev, k_cur], axis=0).astype(BF16)
    v_all = jnp.concatenate([vp_ref[0, 0], vc_ref[0, 0]], axis=0).astype(BF16)
    qi = lax.broadcasted_iota(jnp.int32, (w, 2 * w), 0) + w
    ki = lax.broadcasted_iota(jnp.int32, (w, 2 * w), 1)
    rel = qi - ki
    first = jnp.where(n > 0, 0, w)
    mask = (rel >= 0) & (rel < ATT_WINDOW) & (ki >= first)
    outs = []
    for r in range(ATT_REP):
        qr = q[:, r * hd:(r + 1) * hd].astype(BF16)
        s = lax.dot_general(qr, k_all, (((1,), (1,)), ((), ())), preferred_element_type=F32) * (hd ** -0.5)
        s = jnp.where(mask, s, -jnp.inf)
        sink = sink_ref[h * ATT_REP + r]
        m = jnp.maximum(jnp.max(s, -1, keepdims=True), sink)
        e = jnp.exp(s - m)
        probs = e / (jnp.sum(e, -1, keepdims=True) + jnp.exp(sink - m))
        outs.append(jnp.dot(probs.astype(BF16), v_all, preferred_element_type=F32))
    o_ref[...] = jnp.concatenate(outs, axis=1).astype(o_ref.dtype)


def _swa_attention(qkv, sinks, batch, seq):
    t = batch * seq
    nb = seq // ATT_WINDOW
    hd, w = ATT_HEAD_DIM, ATT_WINDOW
    qw = ATT_REP * hd
    k4 = qkv[:, Q_COLS:Q_COLS + KV_COLS].reshape(batch, seq, ATT_KV_HEADS, hd).transpose(0, 2, 1, 3)
    v4 = qkv[:, Q_COLS + KV_COLS:].reshape(batch, seq, ATT_KV_HEADS, hd).transpose(0, 2, 1, 3)
    cos, sin = _rotary_tables(seq)
    cur = pl.BlockSpec((1, 1, w, hd), lambda b, h, n: (b, h, n, 0))
    prev = pl.BlockSpec((1, 1, w, hd), lambda b, h, n: (b, h, jnp.maximum(n - 1, 0), 0))
    tab_cur = pl.BlockSpec((w, qw), lambda b, h, n: (n, 0))
    tab_prev = pl.BlockSpec((w, qw), lambda b, h, n: (jnp.maximum(n - 1, 0), 0))
    return pl.pallas_call(
        _swa_kernel,
        grid=(batch, ATT_KV_HEADS, nb),
        in_specs=[
            pl.BlockSpec(memory_space=pltpu.SMEM),
            pl.BlockSpec((w, qw), lambda b, h, n: (b * nb + n, h)),
            cur, prev, cur, prev, tab_cur, tab_cur, tab_prev, tab_prev,
            pl.BlockSpec((qw, qw), lambda b, h, n: (0, 0)),
        ],
        out_specs=pl.BlockSpec((w, qw), lambda b, h, n: (b * nb + n, h)),
        out_shape=jax.ShapeDtypeStruct((t, Q_COLS), BF16),
        compiler_params=_cparams(("parallel", "parallel", "arbitrary")),
        name="swa_attention",
    )(sinks, qkv, k4, k4, v4, v4, cos, sin, cos, sin, _rotate_half_matrix())


def _ssd_swa_mixer(x2, xb, batch, seq, w_in, b_qkv, conv_w, conv_b, dt_bias, a_log, d_skip, norm_w, sinks,
                   w_out, b_out):
    o1 = D_MODEL
    o2 = o1 + SSM_CONV_CH
    o3 = o2 + SSM_HEADS
    u_z = _matmul(xb, w_in[:, :o1].astype(BF16))
    u_xbc = _matmul(xb, w_in[:, o1:o2].astype(BF16))
    u_dt = _matmul(x2, w_in[:, o2:o3], precision=HIGHEST, tk=512)
    qkv = _matmul(xb, w_in[:, o3:].astype(BF16), b_qkv)
    y_ssm = _ssd_mixer(u_z, u_xbc, u_dt, conv_w, conv_b, dt_bias, a_log, d_skip, norm_w, batch, seq)
    y_att = _swa_attention(qkv, sinks, batch, seq)
    y = jnp.concatenate([y_ssm, y_att], axis=-1)
    return _matmul(y, w_out.astype(BF16), b_out)


MOE_TILE = 512
ROUTER_TILE = 256


def _router_kernel(h_ref, wt_ref, b_ref, ids_ref, wts_ref):
    logits = lax.dot_general(wt_ref[...], h_ref[...], (((1,), (1,)), ((), ())),
                             precision=HIGHEST, preferred_element_type=F32) + b_ref[...]
    e = jnp.exp(logits - jnp.max(logits, 0, keepdims=True))
    probs = e / jnp.sum(e, 0, keepdims=True)
    p = [probs[i:i + 1, :] for i in range(N_EXPERTS)]

    keep, score = [], []
    for g in range(N_EXPERT_GROUPS):
        members = range(g * EXPERTS_PER_GROUP, (g + 1) * EXPERTS_PER_GROUP)
        s = jnp.zeros_like(p[0])
        for i in members:
            rank = jnp.zeros_like(p[0])
            for j in members:
                if j != i:
                    beats = (p[j] > p[i]) | ((p[j] == p[i]) & (j < i)) if j < i else (p[j] > p[i])
                    rank = rank + beats.astype(F32)
            keep.append(rank < TOP_K)
            s = s + jnp.where(keep[i], p[i], 0.0)
        score.append(s)
    chosen = []
    for g in range(N_EXPERT_GROUPS):
        c = jnp.ones_like(p[0]) > 0
        for g2 in range(N_EXPERT_GROUPS):
            if g2 < g:
                c = c & (score[g] > score[g2])
            elif g2 > g:
                c = c & (score[g] >= score[g2])
        chosen.append(c)
    denom = jnp.zeros_like(p[0])
    for g in range(N_EXPERT_GROUPS):
        denom = denom + jnp.where(chosen[g], score[g], 0.0)
    count = jnp.zeros_like(p[0])
    id0 = jnp.zeros_like(p[0])
    id1 = jnp.zeros_like(p[0])
    w0 = jnp.zeros_like(p[0])
    w1 = jnp.zeros_like(p[0])
    for i in range(N_EXPERTS):
        sel = keep[i] & chosen[i // EXPERTS_PER_GROUP]
        gate = p[i] / denom
        first = sel & (count == 0.0)
        second = sel & (count == 1.0)
        id0 = jnp.where(first, float(i), id0)
        w0 = jnp.where(first, gate, w0)
        id1 = jnp.where(second, float(i), id1)
        w1 = jnp.where(second, gate, w1)
        count = count + sel.astype(F32)
    ids_ref[0:1, :] = id0.astype(jnp.int32)
    ids_ref[1:2, :] = id1.astype(jnp.int32)
    wts_ref[0:1, :] = w0
    wts_ref[1:2, :] = w1


def _router(h, router_w, router_b):
    t, d = h.shape
    tm = _tile(t, ROUTER_TILE)
    return pl.pallas_call(
        _router_kernel,
        grid=(t // tm,),
        in_specs=[
            pl.BlockSpec((tm, d), lambda i: (i, 0)),
            pl.BlockSpec((N_EXPERTS, d), lambda i: (0, 0)),
            pl.BlockSpec((N_EXPERTS, 1), lambda i: (0, 0)),
        ],
        out_specs=[pl.BlockSpec((TOP_K, tm), lambda i: (0, i)), pl.BlockSpec((TOP_K, tm), lambda i: (0, i))],
        out_shape=[jax.ShapeDtypeStruct((TOP_K, t), jnp.int32), jax.ShapeDtypeStruct((TOP_K, t), F32)],
        compiler_params=_cparams(("parallel",)),
        name="moe_router",
    )(h, router_w.T, router_b.reshape(N_EXPERTS, 1))


def _route_metadata(ids, wts, tm):
    t = ids.shape[1]
    na = TOP_K * t
    eid = ids.T.reshape(na)
    onehot = (eid[:, None] == jnp.arange(N_EXPERTS, dtype=jnp.int32)[None, :]).astype(jnp.int32)
    csum = jnp.cumsum(onehot, axis=0)
    counts = csum[-1]
    rank = jnp.sum(csum * onehot, axis=1) - 1
    padded = (counts + tm - 1) // tm * tm
    pend = jnp.cumsum(padded)
    dest = (pend - padded)[eid] + rank
    n_tiles = na // tm + N_EXPERTS
    n_rows = n_tiles * tm
    row_token = jnp.zeros((n_rows,), jnp.int32).at[dest].set(jnp.arange(na, dtype=jnp.int32) // TOP_K)
    row_weight = jnp.zeros((n_rows,), F32).at[dest].set(wts.T.reshape(na))
    tile_start = jnp.arange(n_tiles, dtype=jnp.int32) * tm
    tile_valid = (tile_start < pend[-1]).astype(jnp.int32)
    last_expert = jnp.max(jnp.where(counts > 0, jnp.arange(N_EXPERTS, dtype=jnp.int32), 0))
    tile_expert = jnp.minimum(jnp.searchsorted(pend, tile_start, side="right").astype(jnp.int32), last_expert)
    return row_token, row_weight, dest, tile_expert, tile_valid


def _gather_rows(src_hbm, dst_ref, idx_ref, n, sem):
    def row_copy(r, src_row):
        return pltpu.make_async_copy(src_hbm.at[pl.ds(src_row, 1), :], dst_ref.at[pl.ds(r, 1), :], sem)

    def start(r, carry):
        row_copy(r, idx_ref[0, 0, r]).start()
        return carry

    def wait(r, carry):
        row_copy(r, 0).wait()
        return carry

    lax.fori_loop(0, n, start, 0)
    lax.fori_loop(0, n, wait, 0)


def _moe_up_kernel(te_ref, tv_ref, rt_ref, h_hbm, wg_ref, wu_ref, o_ref, xg_ref, xb_ref, sem, *, tm):
    i = pl.program_id(0)
    valid = tv_ref[i] == 1

    @pl.when(valid & (pl.program_id(1) == 0))
    def _():
        _gather_rows(h_hbm, xg_ref, rt_ref, tm, sem)
        xb_ref[...] = xg_ref[...].astype(BF16)

    @pl.when(valid)
    def _():
        xt = xb_ref[...]
        gate = jnp.dot(xt, wg_ref[0], preferred_element_type=F32)
        up = jnp.dot(xt, wu_ref[0], preferred_element_type=F32)
        o_ref[...] = (_silu(gate) * up).astype(o_ref.dtype)

    @pl.when(jnp.logical_not(valid))
    def _():
        o_ref[...] = jnp.zeros_like(o_ref)


def _moe_down_kernel(te_ref, tv_ref, he_ref, wd_ref, rw_ref, o_ref):
    valid = tv_ref[pl.program_id(0)] == 1

    @pl.when(valid)
    def _():
        o_ref[...] = jnp.dot(he_ref[...], wd_ref[0], preferred_element_type=F32) * rw_ref[...]

    @pl.when(jnp.logical_not(valid))
    def _():
        o_ref[...] = jnp.zeros_like(o_ref)


def _moe_combine_ln_kernel(pos_ref, h_ref, y_hbm, g_ref, b_ref, o_ref, ob_ref, yg_ref, sem, *, tm):
    _gather_rows(y_hbm, yg_ref, pos_ref, TOP_K * tm, sem)
    ffn = yg_ref[0:tm, :] + yg_ref[tm:TOP_K * tm, :]
    r = _layer_norm_rows(ALPHA * h_ref[...] + ffn, g_ref[...], b_ref[...])
    o_ref[...] = r
    ob_ref[...] = r.astype(BF16)


def _moe_block(h, router_w, router_b, w_gate, w_up, w_down, ln_g, ln_b, *, tm=MOE_TILE, tn_up=512, tn_down=1024,
               tm_out=128):
    t, d = h.shape
    ids, wts = _router(h, router_w, router_b)
    row_token, row_weight, dest, tile_expert, tile_valid = _route_metadata(ids, wts, tm)
    n_rows = row_token.shape[0]
    n_tiles = n_rows // tm
    de = w_gate.shape[-1]
    tn_up, tn_down = _tile(de, tn_up), _tile(d, tn_down)

    he = pl.pallas_call(
        functools.partial(_moe_up_kernel, tm=tm),
        grid_spec=pltpu.PrefetchScalarGridSpec(
            num_scalar_prefetch=2,
            grid=(n_tiles, de // tn_up),
            in_specs=[
                pl.BlockSpec((1, 1, tm), lambda i, j, te, tv: (i, 0, 0), memory_space=pltpu.SMEM),
                pl.BlockSpec(memory_space=pl.ANY),
                pl.BlockSpec((1, d, tn_up), lambda i, j, te, tv: (te[i], 0, j)),
                pl.BlockSpec((1, d, tn_up), lambda i, j, te, tv: (te[i], 0, j)),
            ],
            out_specs=pl.BlockSpec((tm, tn_up), lambda i, j, te, tv: (i, j)),
            scratch_shapes=[pltpu.VMEM((tm, d), F32), pltpu.VMEM((tm, d), BF16), pltpu.SemaphoreType.DMA(())],
        ),
        out_shape=jax.ShapeDtypeStruct((n_rows, de), BF16),
        compiler_params=_cparams(("arbitrary", "arbitrary")),
        name="moe_gate_up",
    )(tile_expert, tile_valid, row_token.reshape(n_tiles, 1, tm), h, w_gate, w_up)

    y_rows = pl.pallas_call(
        _moe_down_kernel,
        grid_spec=pltpu.PrefetchScalarGridSpec(
            num_scalar_prefetch=2,
            grid=(n_tiles, d // tn_down),
            in_specs=[
                pl.BlockSpec((tm, de), lambda i, j, te, tv: (i, 0)),
                pl.BlockSpec((1, de, tn_down), lambda i, j, te, tv: (te[i], 0, j)),
                pl.BlockSpec((tm, 1), lambda i, j, te, tv: (i, 0)),
            ],
            out_specs=pl.BlockSpec((tm, tn_down), lambda i, j, te, tv: (i, j)),
        ),
        out_shape=jax.ShapeDtypeStruct((n_rows, d), F32),
        compiler_params=_cparams(("parallel", "parallel")),
        name="moe_down",
    )(tile_expert, tile_valid, he, w_down, row_weight.reshape(n_rows, 1))

    tm_out = _tile(t, tm_out)
    n_out = t // tm_out
    pos = dest.reshape(n_out, tm_out, TOP_K).transpose(0, 2, 1).reshape(n_out, 1, TOP_K * tm_out)
    row = pl.BlockSpec((tm_out, d), lambda i: (i, 0))
    vec = pl.BlockSpec((1, d), lambda i: (0, 0))
    return pl.pallas_call(
        functools.partial(_moe_combine_ln_kernel, tm=tm_out),
        grid=(n_out,),
        in_specs=[
            pl.BlockSpec((1, 1, TOP_K * tm_out), lambda i: (i, 0, 0), memory_space=pltpu.SMEM),
            row,
            pl.BlockSpec(memory_space=pl.ANY),
            vec, vec,
        ],
        out_specs=[row, row],
        out_shape=[jax.ShapeDtypeStruct((t, d), F32), jax.ShapeDtypeStruct((t, d), BF16)],
        scratch_shapes=[pltpu.VMEM((TOP_K * tm_out, d), F32), pltpu.SemaphoreType.DMA(())],
        compiler_params=_cparams(("arbitrary",)),
        name="moe_combine_ln",
    )(pos, h, y_rows, ln_g.reshape(1, d), ln_b.reshape(1, d))


def _shift_mix_kernel(x_ref, xp_ref, mix_ref, o_ref, pad_ref, *, tm, seq):
    i = pl.program_id(0)
    starts_sequence = (i * tm) % seq == 0
    pad_ref[0:SUBLANES, :] = jnp.where(starts_sequence, 0.0, xp_ref[...])
    pad_ref[SUBLANES:SUBLANES + tm, :] = x_ref[...]
    xv = x_ref[...]
    xx = pad_ref[SUBLANES - 1:SUBLANES - 1 + tm, :] - xv
    for m in range(mix_ref.shape[0]):
        o_ref[m] = (xv + xx * mix_ref[m:m + 1, :]).astype(o_ref.dtype)


def _shift_mix(x, mix, seq, *, tm=128):
    t, d = x.shape
    nm = mix.shape[0]
    tm = _tile(seq, tm)
    per = tm // SUBLANES
    return pl.pallas_call(
        functools.partial(_shift_mix_kernel, tm=tm, seq=seq),
        grid=(t // tm,),
        in_specs=[
            pl.BlockSpec((tm, d), lambda i: (i, 0)),
            pl.BlockSpec((SUBLANES, d), lambda i: (jnp.maximum(i * per - 1, 0), 0)),
            pl.BlockSpec((nm, d), lambda i: (0, 0)),
        ],
        out_specs=pl.BlockSpec((nm, tm, d), lambda i: (0, i, 0)),
        out_shape=jax.ShapeDtypeStruct((nm, t, d), BF16),
        scratch_shapes=[pltpu.VMEM((SUBLANES + tm, d), F32)],
        compiler_params=_cparams(("parallel",)),
        name="rwkv_shift_mix",
    )(x, x, mix)


WKV_CHUNK = 64
WKV_HEADS = 4
WKV_WIDTH = WKV_HEADS * RWKV_HEAD
WKV_BLOCK = 256
NEUMANN_STEPS = 5


def _wkv_kernel(r_ref, k_ref, v_ref, wl_ref, ar_ref, g_ref, kk_ref, ka_ref, rk_ref, lnw_ref, lnb_ref,
                o_ref, state_ref, *, tb):
    lc, hw, wd = WKV_CHUNK, RWKV_HEAD, WKV_WIDTH
    nch = tb // lc

    @pl.when(pl.program_id(2) == 0)
    def _():
        state_ref[...] = jnp.zeros_like(state_ref)

    wrow = lax.broadcasted_iota(jnp.int32, (wd, wd), 0)
    wcol = lax.broadcasted_iota(jnp.int32, (wd, wd), 1)
    same_head = wrow // hw == wcol // hw
    ones_bd = same_head.astype(BF16)

    def block_diag(m):
        return jnp.where(same_head, jnp.concatenate([m] * WKV_HEADS, axis=0), 0.0).astype(BF16)

    def head_sum(m):
        hi = m.astype(BF16)
        lo = (m - hi.astype(F32)).astype(BF16)
        return jnp.dot(hi, ones_bd, preferred_element_type=F32) + jnp.dot(lo, ones_bd, preferred_element_type=F32)

    def mm(a, b):
        return jnp.dot(a.astype(BF16), b, preferred_element_type=F32)

    r = r_ref[...]
    k = k_ref[...]
    v = v_ref[...]
    w_log = -_softplus(-wl_ref[...]) - 0.5
    lw = -jnp.exp(w_log)
    a_sig = jax.nn.sigmoid(ar_ref[...])
    kx = k * kk_ref[...]
    kk = kx / jnp.maximum(jnp.sqrt(head_sum(kx * kx)), 1e-12)
    k2 = k * (1.0 + (a_sig - 1.0) * ka_ref[...])
    a_s = -kk
    b_s = kk * a_sig

    trow = lax.broadcasted_iota(jnp.int32, (tb, tb), 0)
    tcol = lax.broadcasted_iota(jnp.int32, (tb, tb), 1)
    same_chunk = trow // lc == tcol // lc
    cum = jnp.dot((same_chunk & (trow >= tcol)).astype(F32), lw, precision=HIGHEST, preferred_element_type=F32)
    tot = jnp.dot(same_chunk.astype(F32), lw, precision=HIGHEST, preferred_element_type=F32)
    grow = jnp.exp(-cum)
    rt = r * jnp.exp(cum)
    at = a_s * jnp.exp(cum - lw)
    bt = b_s * grow
    kt = k2 * grow
    rest = jnp.exp(tot - cum)
    bh = b_s * rest
    kh = k2 * rest
    p_end = jnp.exp(tot)

    t_idx = lax.broadcasted_iota(jnp.int32, (lc, wd), 0)
    s_idx = lax.broadcasted_iota(jnp.int32, (lc, wd), 1) % hw
    strict = t_idx > s_idx
    incl = t_idx >= s_idx
    eye = (t_idx == s_idx).astype(F32)
    nt = (((1,), (1,)), ((), ()))

    ys = []
    for c in range(nch):
        sl = slice(c * lc, (c + 1) * lc)
        ar = jnp.concatenate([at[sl], rt[sl]], axis=0).astype(BF16)
        xb = lax.dot_general(ar, block_diag(bt[sl]), nt, preferred_element_type=F32)
        xk = lax.dot_general(ar, block_diag(kt[sl]), nt, preferred_element_type=F32)
        a_ab = jnp.where(strict, xb[:lc], 0.0)
        a_rb = jnp.where(incl, xb[lc:], 0.0)
        a_ak = jnp.where(strict, xk[:lc], 0.0)
        a_rk = jnp.where(incl, xk[lc:], 0.0)
        inv = eye + a_ab
        pw = a_ab
        for _ in range(NEUMANN_STEPS):
            pw = mm(pw, block_diag(pw))
            inv = inv + mm(inv, block_diag(pw))
        v_bd = block_diag(v[sl])
        ta = mm(inv, block_diag(at[sl]))
        u0 = mm(inv, block_diag(mm(a_ak, v_bd)))
        y0 = mm(a_rk, v_bd)
        state = state_ref[...]
        tr = jnp.concatenate([ta, rt[sl]], axis=0).astype(BF16)
        xs = lax.dot_general(tr, state.astype(BF16), nt, preferred_element_type=F32)
        u = xs[:lc] + u0
        ys.append(xs[lc:] + mm(a_rb, block_diag(u)) + y0)
        uv = jnp.concatenate([u, v[sl]], axis=0).astype(BF16)
        bk = jnp.concatenate([bh[sl], kh[sl]], axis=0).astype(BF16)
        upd = lax.dot_general(uv, bk, (((0,), (0,)), ((), ())), preferred_element_type=F32)
        state_ref[...] = state * p_end[c * lc:c * lc + 1, :] + jnp.where(same_head, upd, 0.0)

    y = jnp.concatenate(ys, axis=0)
    mu = head_sum(y) / hw
    dev = y - mu
    var = head_sum(dev * dev) / hw
    yn = dev * lax.rsqrt(var + RWKV_GN_EPS) * lnw_ref[...] + lnb_ref[...]
    bonus = head_sum(r * k2 * rk_ref[...]) * v
    o_ref[...] = ((yn + bonus) * g_ref[...]).astype(o_ref.dtype)


def _wkv(r, k, v, wl, araw, g, k_k, k_a, r_k, ln_w, ln_b, batch, seq):
    t, d = r.shape
    tb = _tile(seq, WKV_BLOCK)
    nt = seq // tb
    wd = WKV_WIDTH
    rows = pl.BlockSpec((tb, wd), lambda b, h, n: (b * nt + n, h))
    vec = pl.BlockSpec((1, wd), lambda b, h, n: (0, h))
    return pl.pallas_call(
        functools.partial(_wkv_kernel, tb=tb),
        grid=(batch, d // wd, nt),
        in_specs=[rows] * 6 + [vec] * 5,
        out_specs=rows,
        out_shape=jax.ShapeDtypeStruct((t, d), BF16),
        scratch_shapes=[pltpu.VMEM((wd, wd), F32)],
        compiler_params=_cparams(("parallel", "parallel", "arbitrary")),
        name="rwkv_wkv",
    )(r, k, v, wl, araw, g, k_k.reshape(1, d), k_a.reshape(1, d), r_k.reshape(1, d),
      ln_w.reshape(1, d), ln_b.reshape(1, d))


def _rwkv7_time_mix(x2, batch, seq, mix, w_r, w_k, w_v, w_o, w0, w1, w2, a0, a1, a2, g1, g2, k_k, k_a, r_k,
                    ln_w, ln_b):
    xm = _shift_mix(x2, mix, seq)
    bf = lambda w: w.astype(BF16)
    r = _matmul(xm[0], bf(w_r))
    k = _matmul(xm[2], bf(w_k))
    v = _matmul(xm[3], bf(w_v))
    wl = _matmul(_matmul(xm[1], bf(w1), act="tanh", out_dtype=BF16), bf(w2), w0)
    araw = _matmul(_matmul(xm[4], bf(a1), out_dtype=BF16), bf(a2), a0)
    g = _matmul(_matmul(xm[5], bf(g1), act="sigmoid", out_dtype=BF16), bf(g2))
    yg = _wkv(r, k, v, wl, araw, g, k_k, k_a, r_k, ln_w, ln_b, batch, seq)
    return _matmul(yg, bf(w_o))


def kernel(x, ab_w_in, ab_b_qkv, ssm_conv_w, ssm_conv_b, ssm_dt_bias, ssm_a_log, ssm_d, ssm_norm_w, attn_sinks,
           ab_w_out, ab_b_out, rwkv_mix, rwkv_w_r, rwkv_w_k, rwkv_w_v, rwkv_w_o, rwkv_w0, rwkv_w1, rwkv_w2,
           rwkv_a0, rwkv_a1, rwkv_a2, rwkv_g1, rwkv_g2, rwkv_k_k, rwkv_k_a, rwkv_r_k, rwkv_ln_w, rwkv_ln_b,
           ln_mix_g, ln_mix_b, ln_ffn_g, ln_ffn_b, router_w, router_b, moe_w_gate, moe_w_up, moe_w_down):
    batch, seq, d = x.shape
    x2 = x.reshape(batch * seq, d)
    xb = x2.astype(BF16)
    for layer in range(DEPTH):
        i = layer // 2
        if layer % 2 == 0:
            mix = _ssd_swa_mixer(x2, xb, batch, seq, ab_w_in[i], ab_b_qkv[i], ssm_conv_w[i], ssm_conv_b[i],
                                 ssm_dt_bias[i], ssm_a_log[i], ssm_d[i], ssm_norm_w[i], attn_sinks[i],
                                 ab_w_out[i], ab_b_out[i])
        else:
            mix = _rwkv7_time_mix(x2, batch, seq, rwkv_mix[i], rwkv_w_r[i], rwkv_w_k[i], rwkv_w_v[i], rwkv_w_o[i],
                                  rwkv_w0[i], rwkv_w1[i], rwkv_w2[i], rwkv_a0[i], rwkv_a1[i], rwkv_a2[i],
                                  rwkv_g1[i], rwkv_g2[i], rwkv_k_k[i], rwkv_k_a[i], rwkv_r_k[i].reshape(-1),
                                  rwkv_ln_w[i], rwkv_ln_b[i])
        h, _ = _add_layer_norm(x2, mix, ln_mix_g[layer], ln_mix_b[layer])
        x2, xb = _moe_block(h, router_w, router_b, moe_w_gate[layer].astype(BF16), moe_w_up[layer].astype(BF16),
                            moe_w_down[layer].astype(BF16), ln_ffn_g[layer], ln_ffn_b[layer])
    return x2.reshape(batch, seq, d)
```

```python
import functools
import math

import jax
import jax.numpy as jnp
import numpy as np
from jax import lax
from jax.experimental import pallas as pl
from jax.experimental.pallas import tpu as pltpu

F32 = jnp.float32
BF16 = jnp.bfloat16
HIGHEST = lax.Precision.HIGHEST

D_MODEL = 4096
DEPTH = 2
SSM_HEAD_DIM = 64
SSM_HEADS = 64
SSM_GROUPS = 8
SSM_HEADS_PER_GROUP = 8
SSM_STATE = 128
SSM_CONV = 4
SSM_CHUNK = 128
SSM_GROUP_WIDTH = SSM_HEADS_PER_GROUP * SSM_HEAD_DIM
SSM_CONV_CH = D_MODEL + 2 * SSM_GROUPS * SSM_STATE
SSM_NORM_EPS = 1e-5
ATT_HEADS = 64
ATT_KV_HEADS = 8
ATT_HEAD_DIM = 64
ATT_REP = 8
ATT_WINDOW = 128
ROPE_DIM = 16
ROPE_THETA = 500000.0
Q_COLS = 4096
KV_COLS = 512
RWKV_HEAD = 64
RWKV_GN_EPS = 64e-5
N_EXPERTS = 16
N_EXPERT_GROUPS = 4
EXPERTS_PER_GROUP = 4
TOP_K = 2
D_EXPERT = 1536
ALPHA = (2 * DEPTH) ** 0.25
LN_EPS = 1e-5

VMEM_LIMIT_BYTES = 56 * 1024 * 1024
LANES = 128
SUBLANES = 8


def _cparams(semantics):
    return pltpu.CompilerParams(dimension_semantics=semantics, vmem_limit_bytes=VMEM_LIMIT_BYTES)


def _tile(dim, pref):
    if dim <= pref:
        return dim
    t = pref
    while dim % t:
        t //= 2
    return t


def _silu(v):
    return v * jax.nn.sigmoid(v)


def _softplus(v):
    return jnp.maximum(v, 0.0) + jnp.log1p(jnp.exp(-jnp.abs(v)))


def _mm_kernel(a_ref, w_ref, b_ref, o_ref, acc_ref, *, nk, act, precision):
    k = pl.program_id(2)

    @pl.when(k == 0)
    def _():
        acc_ref[...] = jnp.zeros_like(acc_ref)

    acc_ref[...] += jnp.dot(a_ref[...], w_ref[...], preferred_element_type=F32, precision=precision)

    @pl.when(k == nk - 1)
    def _():
        r = acc_ref[...] + b_ref[...]
        if act == "tanh":
            r = jnp.tanh(r)
        elif act == "sigmoid":
            r = jax.nn.sigmoid(r)
        o_ref[...] = r.astype(o_ref.dtype)


def _matmul(a, w, bias=None, *, act=None, out_dtype=F32, precision=None, tm=1024, tn=512, tk=2048,
            col_start=0, n=None):
    m, kdim = a.shape
    n = w.shape[1] if n is None else n
    tm, tn, tk = _tile(m, tm), _tile(n, tn), _tile(kdim, tk)
    nk = kdim // tk
    assert col_start % tn == 0
    col0 = col_start // tn
    if bias is None:
        bias = jnp.zeros((n,), F32)
    bias = bias.reshape(1, n).astype(F32)
    return pl.pallas_call(
        functools.partial(_mm_kernel, nk=nk, act=act, precision=precision),
        grid=(m // tm, n // tn, nk),
        in_specs=[
            pl.BlockSpec((tm, tk), lambda i, j, k: (i, k)),
            pl.BlockSpec((tk, tn), lambda i, j, k: (k, col0 + j)),
            pl.BlockSpec((1, tn), lambda i, j, k: (0, j)),
        ],
        out_specs=pl.BlockSpec((tm, tn), lambda i, j, k: (i, j)),
        out_shape=jax.ShapeDtypeStruct((m, n), out_dtype),
        scratch_shapes=[pltpu.VMEM((tm, tn), F32)],
        compiler_params=_cparams(("parallel", "parallel", "arbitrary")),
        name="matmul",
    )(a, w, bias)


def _layer_norm_rows(v, g, b):
    mu = jnp.mean(v, -1, keepdims=True)
    var = jnp.mean(jnp.square(v - mu), -1, keepdims=True)
    return (v - mu) * lax.rsqrt(var + LN_EPS) * g + b


def _add_ln_kernel(x_ref, y_ref, g_ref, b_ref, o_ref, ob_ref):
    r = _layer_norm_rows(ALPHA * x_ref[...] + y_ref[...], g_ref[...], b_ref[...])
    o_ref[...] = r
    ob_ref[...] = r.astype(BF16)


def _add_layer_norm(x, y, g, b, *, tm=128):
    t, d = x.shape
    tm = _tile(t, tm)
    row = pl.BlockSpec((tm, d), lambda i: (i, 0))
    vec = pl.BlockSpec((1, d), lambda i: (0, 0))
    return pl.pallas_call(
        _add_ln_kernel,
        grid=(t // tm,),
        in_specs=[row, row, vec, vec],
        out_specs=[row, row],
        out_shape=[jax.ShapeDtypeStruct((t, d), F32), jax.ShapeDtypeStruct((t, d), BF16)],
        compiler_params=_cparams(("parallel",)),
        name="add_layer_norm",
    )(x, y, g.reshape(1, d), b.reshape(1, d))


CONV_HALO = SUBLANES
SSD_CH = SSM_GROUP_WIDTH + 2 * SSM_STATE


def _ssd_kernel(z_ref, x_ref, b_ref, c_ref, wx_ref, wb_ref, wc_ref, bx_ref, bb_ref, bc_ref,
                dtc_ref, dtr_ref, dbc_ref, dbr_ref, alc_ref, alr_ref, d_ref, nw_ref,
                o_ref, state_ref, pad_ref):
    ch, gw, ns, hd = SSM_CHUNK, SSM_GROUP_WIDTH, SSM_STATE, SSM_HEAD_DIM

    @pl.when(pl.program_id(2) == 0)
    def _():
        state_ref[...] = jnp.zeros_like(state_ref)
        pad_ref[0:CONV_HALO, :] = jnp.zeros((CONV_HALO, SSD_CH), F32)

    pad_ref[CONV_HALO:CONV_HALO + ch, 0:gw] = x_ref[...]
    pad_ref[CONV_HALO:CONV_HALO + ch, gw:gw + ns] = b_ref[...]
    pad_ref[CONV_HALO:CONV_HALO + ch, gw + ns:SSD_CH] = c_ref[...]
    w = jnp.concatenate([wx_ref[...], wb_ref[...], wc_ref[...]], axis=1)
    acc = jnp.concatenate([bx_ref[...], bb_ref[...], bc_ref[...]], axis=1)
    base = CONV_HALO - (SSM_CONV - 1)
    for k in range(SSM_CONV):
        acc = acc + w[k:k + 1, :] * pad_ref[base + k:base + k + ch, :]
    pad_ref[0:CONV_HALO, :] = pad_ref[ch:ch + CONV_HALO, :]
    xbc = _silu(acc)
    xc, bm, cm = xbc[:, :gw], xbc[:, gw:gw + ns], xbc[:, gw + ns:]

    dt_c = _softplus(dtc_ref[0] + dbc_ref[0])
    dt_r = _softplus(dtr_ref[...] + dbr_ref[...])
    a_c = -jnp.exp(alc_ref[0])
    a_r = -jnp.exp(alr_ref[...])
    row = lax.broadcasted_iota(jnp.int32, (ch, ch), 0)
    col = lax.broadcasted_iota(jnp.int32, (ch, ch), 1)
    causal = row >= col
    acs_c = jnp.dot(causal.astype(F32), dt_c * a_c, precision=HIGHEST, preferred_element_type=F32)
    acs_r = jnp.dot(dt_r * a_r, (row <= col).astype(F32), precision=HIGHEST, preferred_element_type=F32)

    hrow = lax.broadcasted_iota(jnp.int32, (SSM_HEADS_PER_GROUP, gw), 0)
    hcol = lax.broadcasted_iota(jnp.int32, (SSM_HEADS_PER_GROUP, gw), 1)
    expand = (hcol // hd == hrow).astype(F32)
    dt_x = jnp.dot(dt_c, expand, precision=HIGHEST, preferred_element_type=F32)
    acs_x = jnp.dot(acs_c, expand, precision=HIGHEST, preferred_element_type=F32)

    xdt = xc * dt_x
    xdt_b = xdt.astype(BF16)
    bm_b, cm_b = bm.astype(BF16), cm.astype(BF16)
    cb = lax.dot_general(cm_b, bm_b, (((1,), (1,)), ((), ())), preferred_element_type=F32)
    prev = state_ref[...]
    y_off = jnp.dot(cm_b, prev.astype(BF16), preferred_element_type=F32) * jnp.exp(acs_x)

    lane = lax.broadcasted_iota(jnp.int32, (ch, 2 * hd), 1)
    pairs = []
    for pr in range(SSM_HEADS_PER_GROUP // 2):
        xp = xdt_b[:, pr * 2 * hd:(pr + 1) * 2 * hd]
        halves = []
        for q in range(2):
            r = 2 * pr + q
            diff = acs_c[:, r:r + 1] - acs_r[r:r + 1, :]
            seg = jnp.where(causal, jnp.exp(jnp.where(causal, diff, 0.0)), 0.0)
            halves.append(jnp.dot((cb * seg).astype(BF16), xp, preferred_element_type=F32))
        pairs.append(jnp.where(lane < hd, halves[0], halves[1]))
    y = jnp.concatenate(pairs, axis=1) + y_off + d_ref[...] * xc

    acs_last = acs_x[ch - 1:ch, :]
    contrib = lax.dot_general(bm_b, (xdt * jnp.exp(acs_last - acs_x)).astype(BF16),
                              (((0,), (0,)), ((), ())), preferred_element_type=F32)
    state_ref[...] = prev * jnp.exp(acs_last) + contrib

    u = y * _silu(z_ref[...])
    u = u * lax.rsqrt(jnp.mean(u * u, -1, keepdims=True) + SSM_NORM_EPS)
    o_ref[...] = (u * nw_ref[...]).astype(o_ref.dtype)


def _ssd_mixer(u_z, u_xbc, u_dt, conv_w, conv_b, dt_bias, a_log, d_skip, norm_w, batch, seq):
    t = batch * seq
    nc = seq // SSM_CHUNK
    g, r, gw, ns = SSM_GROUPS, SSM_HEADS_PER_GROUP, SSM_GROUP_WIDTH, SSM_STATE
    xblocks = D_MODEL // ns
    conv_wt = conv_w.T
    conv_b2 = conv_b.reshape(1, SSM_CONV_CH)
    dt_col = u_dt.reshape(t, g, r).transpose(1, 0, 2)
    dt_row = u_dt.T
    rows = lambda b, gi, c: (b * nc + c, gi)
    in_specs = [
        pl.BlockSpec((SSM_CHUNK, gw), rows),
        pl.BlockSpec((SSM_CHUNK, gw), rows),
        pl.BlockSpec((SSM_CHUNK, ns), lambda b, gi, c: (b * nc + c, xblocks + gi)),
        pl.BlockSpec((SSM_CHUNK, ns), lambda b, gi, c: (b * nc + c, xblocks + g + gi)),
        pl.BlockSpec((SSM_CONV, gw), lambda b, gi, c: (0, gi)),
        pl.BlockSpec((SSM_CONV, ns), lambda b, gi, c: (0, xblocks + gi)),
        pl.BlockSpec((SSM_CONV, ns), lambda b, gi, c: (0, xblocks + g + gi)),
        pl.BlockSpec((1, gw), lambda b, gi, c: (0, gi)),
        pl.BlockSpec((1, ns), lambda b, gi, c: (0, xblocks + gi)),
        pl.BlockSpec((1, ns), lambda b, gi, c: (0, xblocks + g + gi)),
        pl.BlockSpec((1, SSM_CHUNK, r), lambda b, gi, c: (gi, b * nc + c, 0)),
        pl.BlockSpec((r, SSM_CHUNK), lambda b, gi, c: (gi, b * nc + c)),
        pl.BlockSpec((1, 1, r), lambda b, gi, c: (gi, 0, 0)),
        pl.BlockSpec((r, 1), lambda b, gi, c: (gi, 0)),
        pl.BlockSpec((1, 1, r), lambda b, gi, c: (gi, 0, 0)),
        pl.BlockSpec((r, 1), lambda b, gi, c: (gi, 0)),
        pl.BlockSpec((1, gw), lambda b, gi, c: (0, gi)),
        pl.BlockSpec((1, gw), lambda b, gi, c: (0, gi)),
    ]
    return pl.pallas_call(
        _ssd_kernel,
        grid=(batch, g, nc),
        in_specs=in_specs,
        out_specs=pl.BlockSpec((SSM_CHUNK, gw), rows),
        out_shape=jax.ShapeDtypeStruct((t, D_MODEL), BF16),
        scratch_shapes=[pltpu.VMEM((ns, gw), F32), pltpu.VMEM((CONV_HALO + SSM_CHUNK, SSD_CH), F32)],
        compiler_params=_cparams(("parallel", "parallel", "arbitrary")),
        name="ssd_mixer",
    )(u_z, u_xbc, u_xbc, u_xbc, conv_wt, conv_wt, conv_wt, conv_b2, conv_b2, conv_b2,
      dt_col, dt_row, dt_bias.reshape(g, 1, r), dt_bias.reshape(SSM_HEADS, 1),
      a_log.reshape(g, 1, r), a_log.reshape(SSM_HEADS, 1),
      jnp.repeat(d_skip, SSM_HEAD_DIM).reshape(1, D_MODEL), norm_w.reshape(1, D_MODEL))


def _rotary_tables(seq):
    half = ROPE_DIM // 2
    inv_freq = ROPE_THETA ** (-jnp.arange(half, dtype=F32) / half)
    ang = jnp.arange(seq, dtype=F32)[:, None] * inv_freq[None, :]
    ones = jnp.ones((seq, ATT_HEAD_DIM - ROPE_DIM), F32)
    cos = jnp.concatenate([jnp.cos(ang), jnp.cos(ang), ones], -1)
    sin = jnp.concatenate([jnp.sin(ang), jnp.sin(ang), 0.0 * ones], -1)
    return jnp.tile(cos, (1, ATT_REP)), jnp.tile(sin, (1, ATT_REP))


def _rotate_half_matrix():
    width = ATT_REP * ATT_HEAD_DIM
    half = ROPE_DIM // 2
    p = np.zeros((width, width), np.float32)
    for j in range(width):
        if j % ATT_HEAD_DIM < half:
            p[j + half, j] = -1.0
        elif j % ATT_HEAD_DIM < ROPE_DIM:
            p[j - half, j] = 1.0
    return jnp.asarray(p, BF16)


def _rotate(v, cos, sin, perm):
    hi = v.astype(BF16)
    lo = (v - hi.astype(F32)).astype(BF16)
    partner = jnp.dot(hi, perm, preferred_element_type=F32) + jnp.dot(lo, perm, preferred_element_type=F32)
    return v * cos + partner * sin


def _swa_kernel(sink_ref, q_ref, kc_ref, kp_ref, vc_ref, vp_ref, cosc_ref, sinc_ref, cosp_ref, sinp_ref,
                perm_ref, o_ref):
    w, hd = ATT_WINDOW, ATT_HEAD_DIM
    h = pl.program_id(1)
    n = pl.program_id(2)
    perm = perm_ref[...]
    perm_k = perm[:hd, :hd]
    q = _rotate(q_ref[...], cosc_ref[...], sinc_ref[...], perm)
    k_cur = _rotate(kc_ref[0, 0], cosc_ref[:, :hd], sinc_ref[:, :hd], perm_k)
    k_prev = _rotate(kp_ref[0, 0], cosp_ref[:, :hd], sinp_ref[:, :hd], perm_k)
    k_all = jnp.concatenate([k_prev, k_cur], axis=0).astype(BF16)
    v_all = jnp.concatenate([vp_ref[0, 0], vc_ref[0, 0]], axis=0).astype(BF16)
    qi = lax.broadcasted_iota(jnp.int32, (w, 2 * w), 0) + w
    ki = lax.broadcasted_iota(jnp.int32, (w, 2 * w), 1)
    rel = qi - ki
    first = jnp.where(n > 0, 0, w)
    mask = (rel >= 0) & (rel < ATT_WINDOW) & (ki >= first)
    outs = []
    for r in range(ATT_REP):
        qr = q[:, r * hd:(r + 1) * hd].astype(BF16)
        s = lax.dot_general(qr, k_all, (((1,), (1,)), ((), ())), preferred_element_type=F32) * (hd ** -0.5)
        s = jnp.where(mask, s, -jnp.inf)
        sink = sink_ref[h * ATT_REP + r]
        m = jnp.maximum(jnp.max(s, -1, keepdims=True), sink)
        e = jnp.exp(s - m)
        probs = e / (jnp.sum(e, -1, keepdims=True) + jnp.exp(sink - m))
        outs.append(jnp.dot(probs.astype(BF16), v_all, preferred_element_type=F32))
    o_ref[...] = jnp.concatenate(outs, axis=1).astype(o_ref.dtype)


def _swa_attention(qkv, sinks, batch, seq):
    t = batch * seq
    nb = seq // ATT_WINDOW
    hd, w = ATT_HEAD_DIM, ATT_WINDOW
    qw = ATT_REP * hd
    k4 = qkv[:, Q_COLS:Q_COLS + KV_COLS].reshape(batch, seq, ATT_KV_HEADS, hd).transpose(0, 2, 1, 3)
    v4 = qkv[:, Q_COLS + KV_COLS:].reshape(batch, seq, ATT_KV_HEADS, hd).transpose(0, 2, 1, 3)
    cos, sin = _rotary_tables(seq)
    cur = pl.BlockSpec((1, 1, w, hd), lambda b, h, n: (b, h, n, 0))
    prev = pl.BlockSpec((1, 1, w, hd), lambda b, h, n: (b, h, jnp.maximum(n - 1, 0), 0))
    tab_cur = pl.BlockSpec((w, qw), lambda b, h, n: (n, 0))
    tab_prev = pl.BlockSpec((w, qw), lambda b, h, n: (jnp.maximum(n - 1, 0), 0))
    return pl.pallas_call(
        _swa_kernel,
        grid=(batch, ATT_KV_HEADS, nb),
        in_specs=[
            pl.BlockSpec(memory_space=pltpu.SMEM),
            pl.BlockSpec((w, qw), lambda b, h, n: (b * nb + n, h)),
            cur, prev, cur, prev, tab_cur, tab_cur, tab_prev, tab_prev,
            pl.BlockSpec((qw, qw), lambda b, h, n: (0, 0)),
        ],
        out_specs=pl.BlockSpec((w, qw), lambda b, h, n: (b * nb + n, h)),
        out_shape=jax.ShapeDtypeStruct((t, Q_COLS), BF16),
        compiler_params=_cparams(("parallel", "parallel", "arbitrary")),
        name="swa_attention",
    )(sinks, qkv, k4, k4, v4, v4, cos, sin, cos, sin, _rotate_half_matrix())


def _ssd_swa_mixer(x2, xb, batch, seq, w_in, b_qkv, conv_w, conv_b, dt_bias, a_log, d_skip, norm_w, sinks,
                   w_out, b_out):
    o1 = D_MODEL
    o2 = o1 + SSM_CONV_CH
    o3 = o2 + SSM_HEADS
    w_in_b = w_in.astype(BF16)
    u_z = _matmul(xb, w_in_b, n=o1)
    u_xbc = _matmul(xb, w_in_b, col_start=o1, n=SSM_CONV_CH)
    u_dt = _matmul(x2, w_in[:, o2:o3], precision=HIGHEST, tk=512)
    qkv = _matmul(xb, w_in_b[:, o3:], b_qkv)
    y_ssm = _ssd_mixer(u_z, u_xbc, u_dt, conv_w, conv_b, dt_bias, a_log, d_skip, norm_w, batch, seq)
    y_att = _swa_attention(qkv, sinks, batch, seq)
    y = jnp.concatenate([y_ssm, y_att], axis=-1)
    return _matmul(y, w_out.astype(BF16), b_out)


MOE_TILE = 512
MOE_GATHER_TILE = 256
ROUTER_TILE = 256


def _router_kernel(h_ref, wt_ref, b_ref, ids_ref, wts_ref):
    logits = lax.dot_general(wt_ref[...], h_ref[...], (((1,), (1,)), ((), ())),
                             precision=HIGHEST, preferred_element_type=F32) + b_ref[...]
    e = jnp.exp(logits - jnp.max(logits, 0, keepdims=True))
    probs = e / jnp.sum(e, 0, keepdims=True)
    p = [probs[i:i + 1, :] for i in range(N_EXPERTS)]

    keep, score = [], []
    for g in range(N_EXPERT_GROUPS):
        members = range(g * EXPERTS_PER_GROUP, (g + 1) * EXPERTS_PER_GROUP)
        s = jnp.zeros_like(p[0])
        for i in members:
            rank = jnp.zeros_like(p[0])
            for j in members:
                if j != i:
                    beats = (p[j] > p[i]) | ((p[j] == p[i]) & (j < i)) if j < i else (p[j] > p[i])
                    rank = rank + beats.astype(F32)
            keep.append(rank < TOP_K)
            s = s + jnp.where(keep[i], p[i], 0.0)
        score.append(s)
    chosen = []
    for g in range(N_EXPERT_GROUPS):
        c = jnp.ones_like(p[0]) > 0
        for g2 in range(N_EXPERT_GROUPS):
            if g2 < g:
                c = c & (score[g] > score[g2])
            elif g2 > g:
                c = c & (score[g] >= score[g2])
        chosen.append(c)
    denom = jnp.zeros_like(p[0])
    for g in range(N_EXPERT_GROUPS):
        denom = denom + jnp.where(chosen[g], score[g], 0.0)
    count = jnp.zeros_like(p[0])
    id0 = jnp.zeros_like(p[0])
    id1 = jnp.zeros_like(p[0])
    w0 = jnp.zeros_like(p[0])
    w1 = jnp.zeros_like(p[0])
    for i in range(N_EXPERTS):
        sel = keep[i] & chosen[i // EXPERTS_PER_GROUP]
        gate = p[i] / denom
        first = sel & (count == 0.0)
        second = sel & (count == 1.0)
        id0 = jnp.where(first, float(i), id0)
        w0 = jnp.where(first, gate, w0)
        id1 = jnp.where(second, float(i), id1)
        w1 = jnp.where(second, gate, w1)
        count = count + sel.astype(F32)
    ids_ref[0:1, :] = id0.astype(jnp.int32)
    ids_ref[1:2, :] = id1.astype(jnp.int32)
    wts_ref[0:1, :] = w0
    wts_ref[1:2, :] = w1


def _router(h, router_w, router_b):
    t, d = h.shape
    tm = _tile(t, ROUTER_TILE)
    return pl.pallas_call(
        _router_kernel,
        grid=(t // tm,),
        in_specs=[
            pl.BlockSpec((tm, d), lambda i: (i, 0)),
            pl.BlockSpec((N_EXPERTS, d), lambda i: (0, 0)),
            pl.BlockSpec((N_EXPERTS, 1), lambda i: (0, 0)),
        ],
        out_specs=[pl.BlockSpec((TOP_K, tm), lambda i: (0, i)), pl.BlockSpec((TOP_K, tm), lambda i: (0, i))],
        out_shape=[jax.ShapeDtypeStruct((TOP_K, t), jnp.int32), jax.ShapeDtypeStruct((TOP_K, t), F32)],
        compiler_params=_cparams(("parallel",)),
        name="moe_router",
    )(h, router_w.T, router_b.reshape(N_EXPERTS, 1))


def _route_metadata(ids, wts, tm):
    t = ids.shape[1]
    na = TOP_K * t
    eid = ids.T.reshape(na)
    onehot = (eid[:, None] == jnp.arange(N_EXPERTS, dtype=jnp.int32)[None, :]).astype(jnp.int32)
    csum = jnp.cumsum(onehot, axis=0)
    counts = csum[-1]
    rank = jnp.sum(csum * onehot, axis=1) - 1
    padded = (counts + tm - 1) // tm * tm
    pend = jnp.cumsum(padded)
    dest = (pend - padded)[eid] + rank
    n_tiles = na // tm + N_EXPERTS
    n_rows = n_tiles * tm
    row_token = jnp.zeros((n_rows,), jnp.int32).at[dest].set(jnp.arange(na, dtype=jnp.int32) // TOP_K)
    row_weight = jnp.zeros((n_rows,), F32).at[dest].set(wts.T.reshape(na))
    tile_start = jnp.arange(n_tiles, dtype=jnp.int32) * tm
    tile_valid = (tile_start < pend[-1]).astype(jnp.int32)
    last_expert = jnp.max(jnp.where(counts > 0, jnp.arange(N_EXPERTS, dtype=jnp.int32), 0))
    tile_expert = jnp.minimum(jnp.searchsorted(pend, tile_start, side="right").astype(jnp.int32), last_expert)
    return row_token, row_weight, dest, tile_expert, tile_valid


def _gather_rows(src_hbm, dst_ref, idx_ref, n, sem):
    def row_copy(r, src_row):
        return pltpu.make_async_copy(src_hbm.at[pl.ds(src_row, 1), :], dst_ref.at[pl.ds(r, 1), :], sem)

    def start(r, carry):
        row_copy(r, idx_ref[0, 0, r]).start()
        return carry

    def wait(r, carry):
        row_copy(r, 0).wait()
        return carry

    lax.fori_loop(0, n, start, 0)
    lax.fori_loop(0, n, wait, 0)


def _moe_gather_kernel(tv_ref, rt_ref, h_hbm, o_ref, xg_ref, sem, *, tm):
    valid = tv_ref[pl.program_id(0)] == 1

    @pl.when(valid)
    def _():
        _gather_rows(h_hbm, xg_ref, rt_ref, tm, sem)
        o_ref[...] = xg_ref[...].astype(o_ref.dtype)

    @pl.when(jnp.logical_not(valid))
    def _():
        o_ref[...] = jnp.zeros_like(o_ref)


def _expert_changed(te_ref, i):
    return (i == 0) | (te_ref[i] != te_ref[jnp.maximum(i - 1, 0)])


def _moe_up_kernel(te_ref, tv_ref, x_ref, wg_ref, wu_ref, o_ref, wgb_ref, wub_ref):
    i = pl.program_id(1)
    valid = tv_ref[i] == 1

    @pl.when(_expert_changed(te_ref, i))
    def _():
        wgb_ref[...] = wg_ref[0].astype(BF16)
        wub_ref[...] = wu_ref[0].astype(BF16)

    @pl.when(valid)
    def _():
        xt = x_ref[...]
        gate = jnp.dot(xt, wgb_ref[...], preferred_element_type=F32)
        up = jnp.dot(xt, wub_ref[...], preferred_element_type=F32)
        o_ref[...] = (_silu(gate) * up).astype(o_ref.dtype)

    @pl.when(jnp.logical_not(valid))
    def _():
        o_ref[...] = jnp.zeros_like(o_ref)


def _moe_down_kernel(te_ref, tv_ref, he_ref, wd_ref, rw_ref, o_ref, wdb_ref):
    i = pl.program_id(1)
    valid = tv_ref[i] == 1

    @pl.when(_expert_changed(te_ref, i))
    def _():
        wdb_ref[...] = wd_ref[0].astype(BF16)

    @pl.when(valid)
    def _():
        o_ref[...] = jnp.dot(he_ref[...], wdb_ref[...], preferred_element_type=F32) * rw_ref[...]

    @pl.when(jnp.logical_not(valid))
    def _():
        o_ref[...] = jnp.zeros_like(o_ref)


def _moe_combine_ln_kernel(pos_ref, h_ref, y_hbm, g_ref, b_ref, o_ref, ob_ref, yg_ref, sem, *, tm):
    _gather_rows(y_hbm, yg_ref, pos_ref, TOP_K * tm, sem)
    ffn = yg_ref[0:tm, :] + yg_ref[tm:TOP_K * tm, :]
    r = _layer_norm_rows(ALPHA * h_ref[...] + ffn, g_ref[...], b_ref[...])
    o_ref[...] = r
    ob_ref[...] = r.astype(BF16)


def _moe_block(h, router_w, router_b, w_gate, w_up, w_down, layer, ln_g, ln_b, *, tm=MOE_TILE, tn_up=512,
               tn_down=1024, tm_out=128):
    t, d = h.shape
    ids, wts = _router(h, router_w, router_b)
    row_token, row_weight, dest, tile_expert, tile_valid = _route_metadata(ids, wts, tm)
    n_rows = row_token.shape[0]
    n_tiles = n_rows // tm
    de = w_gate.shape[-1]
    tn_up, tn_down = _tile(de, tn_up), _tile(d, tn_down)

    tg = _tile(tm, MOE_GATHER_TILE)
    per = tm // tg
    x_rows = pl.pallas_call(
        functools.partial(_moe_gather_kernel, tm=tg),
        grid_spec=pltpu.PrefetchScalarGridSpec(
            num_scalar_prefetch=1,
            grid=(n_rows // tg,),
            in_specs=[
                pl.BlockSpec((1, 1, tg), lambda i, tv: (i, 0, 0), memory_space=pltpu.SMEM),
                pl.BlockSpec(memory_space=pl.ANY),
            ],
            out_specs=pl.BlockSpec((tg, d), lambda i, tv: (i, 0)),
            scratch_shapes=[pltpu.VMEM((tg, d), F32), pltpu.SemaphoreType.DMA(())],
        ),
        out_shape=jax.ShapeDtypeStruct((n_rows, d), BF16),
        compiler_params=_cparams(("arbitrary",)),
        name="moe_gather",
    )(jnp.repeat(tile_valid, per), row_token.reshape(n_rows // tg, 1, tg), h)

    he = pl.pallas_call(
        _moe_up_kernel,
        grid_spec=pltpu.PrefetchScalarGridSpec(
            num_scalar_prefetch=2,
            grid=(de // tn_up, n_tiles),
            in_specs=[
                pl.BlockSpec((tm, d), lambda j, i, te, tv: (i, 0)),
                pl.BlockSpec((None, 1, d, tn_up), lambda j, i, te, tv: (layer, te[i], 0, j)),
                pl.BlockSpec((None, 1, d, tn_up), lambda j, i, te, tv: (layer, te[i], 0, j)),
            ],
            out_specs=pl.BlockSpec((tm, tn_up), lambda j, i, te, tv: (i, j)),
            scratch_shapes=[pltpu.VMEM((d, tn_up), BF16), pltpu.VMEM((d, tn_up), BF16)],
        ),
        out_shape=jax.ShapeDtypeStruct((n_rows, de), BF16),
        compiler_params=_cparams(("arbitrary", "arbitrary")),
        name="moe_gate_up",
    )(tile_expert, tile_valid, x_rows, w_gate, w_up)

    y_rows = pl.pallas_call(
        _moe_down_kernel,
        grid_spec=pltpu.PrefetchScalarGridSpec(
            num_scalar_prefetch=2,
            grid=(d // tn_down, n_tiles),
            in_specs=[
                pl.BlockSpec((tm, de), lambda j, i, te, tv: (i, 0)),
                pl.BlockSpec((None, 1, de, tn_down), lambda j, i, te, tv: (layer, te[i], 0, j)),
                pl.BlockSpec((tm, 1), lambda j, i, te, tv: (i, 0)),
            ],
            out_specs=pl.BlockSpec((tm, tn_down), lambda j, i, te, tv: (i, j)),
            scratch_shapes=[pltpu.VMEM((de, tn_down), BF16)],
        ),
        out_shape=jax.ShapeDtypeStruct((n_rows, d), F32),
        compiler_params=_cparams(("arbitrary", "arbitrary")),
        name="moe_down",
    )(tile_expert, tile_valid, he, w_down, row_weight.reshape(n_rows, 1))

    tm_out = _tile(t, tm_out)
    n_out = t // tm_out
    pos = dest.reshape(n_out, tm_out, TOP_K).transpose(0, 2, 1).reshape(n_out, 1, TOP_K * tm_out)
    row = pl.BlockSpec((tm_out, d), lambda i: (i, 0))
    vec = pl.BlockSpec((1, d), lambda i: (0, 0))
    return pl.pallas_call(
        functools.partial(_moe_combine_ln_kernel, tm=tm_out),
        grid=(n_out,),
        in_specs=[
            pl.BlockSpec((1, 1, TOP_K * tm_out), lambda i: (i, 0, 0), memory_space=pltpu.SMEM),
            row,
            pl.BlockSpec(memory_space=pl.ANY),
            vec, vec,
        ],
        out_specs=[row, row],
        out_shape=[jax.ShapeDtypeStruct((t, d), F32), jax.ShapeDtypeStruct((t, d), BF16)],
        scratch_shapes=[pltpu.VMEM((TOP_K * tm_out, d), F32), pltpu.SemaphoreType.DMA(())],
        compiler_params=_cparams(("arbitrary",)),
        name="moe_combine_ln",
    )(pos, h, y_rows, ln_g.reshape(1, d), ln_b.reshape(1, d))


def _shift_mix_kernel(x_ref, xp_ref, mix_ref, o_ref, pad_ref, *, tm, seq):
    i = pl.program_id(0)
    starts_sequence = (i * tm) % seq == 0
    pad_ref[0:SUBLANES, :] = jnp.where(starts_sequence, 0.0, xp_ref[...])
    pad_ref[SUBLANES:SUBLANES + tm, :] = x_ref[...]
    xv = x_ref[...]
    xx = pad_ref[SUBLANES - 1:SUBLANES - 1 + tm, :] - xv
    for m in range(mix_ref.shape[0]):
        o_ref[m] = (xv + xx * mix_ref[m:m + 1, :]).astype(o_ref.dtype)


def _shift_mix(x, mix, seq, *, tm=128):
    t, d = x.shape
    nm = mix.shape[0]
    tm = _tile(seq, tm)
    per = tm // SUBLANES
    return pl.pallas_call(
        functools.partial(_shift_mix_kernel, tm=tm, seq=seq),
        grid=(t // tm,),
        in_specs=[
            pl.BlockSpec((tm, d), lambda i: (i, 0)),
            pl.BlockSpec((SUBLANES, d), lambda i: (jnp.maximum(i * per - 1, 0), 0)),
            pl.BlockSpec((nm, d), lambda i: (0, 0)),
        ],
        out_specs=pl.BlockSpec((nm, tm, d), lambda i: (0, i, 0)),
        out_shape=jax.ShapeDtypeStruct((nm, t, d), BF16),
        scratch_shapes=[pltpu.VMEM((SUBLANES + tm, d), F32)],
        compiler_params=_cparams(("parallel",)),
        name="rwkv_shift_mix",
    )(x, x, mix)


WKV_CHUNK = 64
WKV_HEADS = 4
WKV_WIDTH = WKV_HEADS * RWKV_HEAD
WKV_BLOCK = 512
NEUMANN_STEPS = 5


def _wkv_kernel(r_ref, k_ref, v_ref, wl_ref, ar_ref, g_ref, kk_ref, ka_ref, rk_ref, lnw_ref, lnb_ref,
                o_ref, state_ref, *, tb):
    lc, hw, wd = WKV_CHUNK, RWKV_HEAD, WKV_WIDTH
    nch = tb // lc

    @pl.when(pl.program_id(2) == 0)
    def _():
        state_ref[...] = jnp.zeros_like(state_ref)

    wrow = lax.broadcasted_iota(jnp.int32, (wd, wd), 0)
    wcol = lax.broadcasted_iota(jnp.int32, (wd, wd), 1)
    same_head = wrow // hw == wcol // hw
    ones_bd = same_head.astype(BF16)

    def block_diag(m):
        return jnp.where(same_head, jnp.concatenate([m] * WKV_HEADS, axis=0), 0.0).astype(BF16)

    def head_sum(m):
        hi = m.astype(BF16)
        lo = (m - hi.astype(F32)).astype(BF16)
        return jnp.dot(hi, ones_bd, preferred_element_type=F32) + jnp.dot(lo, ones_bd, preferred_element_type=F32)

    def mm(a, b):
        return jnp.dot(a.astype(BF16), b, preferred_element_type=F32)

    r = r_ref[...]
    k = k_ref[...]
    v = v_ref[...]
    w_log = -_softplus(-wl_ref[...]) - 0.5
    lw = -jnp.exp(w_log)
    a_sig = jax.nn.sigmoid(ar_ref[...])
    kx = k * kk_ref[...]
    kk = kx / jnp.maximum(jnp.sqrt(head_sum(kx * kx)), 1e-12)
    k2 = k * (1.0 + (a_sig - 1.0) * ka_ref[...])
    a_s = -kk
    b_s = kk * a_sig

    trow = lax.broadcasted_iota(jnp.int32, (tb, tb), 0)
    tcol = lax.broadcasted_iota(jnp.int32, (tb, tb), 1)
    same_chunk = trow // lc == tcol // lc
    tril = (same_chunk & (trow >= tcol)).astype(BF16)
    lw_hi = lw.astype(BF16)
    lw_r1 = lw - lw_hi.astype(F32)
    lw_mid = lw_r1.astype(BF16)
    lw_lo = (lw_r1 - lw_mid.astype(F32)).astype(BF16)
    cum = (jnp.dot(tril, lw_hi, preferred_element_type=F32) + jnp.dot(tril, lw_mid, preferred_element_type=F32)
           + jnp.dot(tril, lw_lo, preferred_element_type=F32))
    tot = jnp.concatenate([jnp.broadcast_to(cum[(c + 1) * lc - 1:(c + 1) * lc, :], (lc, wd)) for c in range(nch)],
                          axis=0)
    grow = jnp.exp(-cum)
    rt = r * jnp.exp(cum)
    at = a_s * jnp.exp(cum - lw)
    bt = b_s * grow
    kt = k2 * grow
    rest = jnp.exp(tot - cum)
    bh = b_s * rest
    kh = k2 * rest
    p_end = jnp.exp(tot)

    t_idx = lax.broadcasted_iota(jnp.int32, (lc, wd), 0)
    s_idx = lax.broadcasted_iota(jnp.int32, (lc, wd), 1) % hw
    strict = t_idx > s_idx
    incl = t_idx >= s_idx
    eye = (t_idx == s_idx).astype(F32)
    nt = (((1,), (1,)), ((), ()))

    tn = (((0,), (0,)), ((), ()))
    chunks = [slice(c * lc, (c + 1) * lc) for c in range(nch)]

    a_ab, a_rb, a_ak, a_rk = [], [], [], []
    for sl in chunks:
        ar = jnp.concatenate([at[sl], rt[sl]], axis=0).astype(BF16)
        xb = lax.dot_general(ar, block_diag(bt[sl]), nt, preferred_element_type=F32)
        xk = lax.dot_general(ar, block_diag(kt[sl]), nt, preferred_element_type=F32)
        a_ab.append(jnp.where(strict, xb[:lc], 0.0))
        a_rb.append(jnp.where(incl, xb[lc:], 0.0))
        a_ak.append(jnp.where(strict, xk[:lc], 0.0))
        a_rk.append(jnp.where(incl, xk[lc:], 0.0))
    inv = [eye + a for a in a_ab]
    pw = [mm(a, block_diag(a)) for a in a_ab]
    for step in range(1, NEUMANN_STEPS + 1):
        last = step == NEUMANN_STEPS
        lhs = inv if last else [jnp.concatenate([t, p], axis=0) for t, p in zip(inv, pw)]
        prod = [mm(x, block_diag(p)) for x, p in zip(lhs, pw)]
        inv = [t + q[:lc] for t, q in zip(inv, prod)]
        if not last:
            pw = [q[lc:] for q in prod]
    v_bd = [block_diag(v[sl]) for sl in chunks]
    ta = [mm(t, block_diag(at[sl])) for t, sl in zip(inv, chunks)]
    av = [mm(jnp.concatenate([ak, rk], axis=0), vb) for ak, rk, vb in zip(a_ak, a_rk, v_bd)]
    y0 = [x[lc:] for x in av]
    u0 = [mm(t, block_diag(x[:lc])) for t, x in zip(inv, av)]
    gain, add = [], []
    for c, sl in enumerate(chunks):
        bh_b = bh[sl].astype(BF16)
        gain.append(jnp.where(same_head, lax.dot_general(ta[c].astype(BF16), bh_b, tn, preferred_element_type=F32),
                              0.0).astype(BF16))
        uv = jnp.concatenate([u0[c], v[sl]], axis=0).astype(BF16)
        bk = jnp.concatenate([bh_b, kh[sl].astype(BF16)], axis=0)
        add.append(jnp.where(same_head, lax.dot_general(uv, bk, tn, preferred_element_type=F32), 0.0))

    states = [state_ref[...]]
    for c in range(nch):
        s0 = states[-1]
        states.append(s0 * p_end[c * lc:c * lc + 1, :] + mm(s0, gain[c]) + add[c])
    state_ref[...] = states[nch]

    ys = []
    for c, sl in enumerate(chunks):
        tr = jnp.concatenate([ta[c], rt[sl]], axis=0).astype(BF16)
        xs = lax.dot_general(tr, states[c].astype(BF16), nt, preferred_element_type=F32)
        u = xs[:lc] + u0[c]
        ys.append(xs[lc:] + mm(a_rb[c], block_diag(u)) + y0[c])

    y = jnp.concatenate(ys, axis=0)
    mu = head_sum(y) / hw
    dev = y - mu
    var = head_sum(dev * dev) / hw
    yn = dev * lax.rsqrt(var + RWKV_GN_EPS) * lnw_ref[...] + lnb_ref[...]
    bonus = head_sum(r * k2 * rk_ref[...]) * v
    o_ref[...] = ((yn + bonus) * g_ref[...]).astype(o_ref.dtype)


def _wkv(r, k, v, wl, araw, g, k_k, k_a, r_k, ln_w, ln_b, batch, seq):
    t, d = r.shape
    tb = _tile(seq, WKV_BLOCK)
    nt = seq // tb
    wd = WKV_WIDTH
    rows = pl.BlockSpec((tb, wd), lambda b, h, n: (b * nt + n, h))
    vec = pl.BlockSpec((1, wd), lambda b, h, n: (0, h))
    return pl.pallas_call(
        functools.partial(_wkv_kernel, tb=tb),
        grid=(batch, d // wd, nt),
        in_specs=[rows] * 6 + [vec] * 5,
        out_specs=rows,
        out_shape=jax.ShapeDtypeStruct((t, d), BF16),
        scratch_shapes=[pltpu.VMEM((wd, wd), F32)],
        compiler_params=_cparams(("parallel", "parallel", "arbitrary")),
        name="rwkv_wkv",
    )(r, k, v, wl, araw, g, k_k.reshape(1, d), k_a.reshape(1, d), r_k.reshape(1, d),
      ln_w.reshape(1, d), ln_b.reshape(1, d))


def _rwkv7_time_mix(x2, batch, seq, mix, w_r, w_k, w_v, w_o, w0, w1, w2, a0, a1, a2, g1, g2, k_k, k_a, r_k,
                    ln_w, ln_b):
    xm = _shift_mix(x2, mix, seq)
    bf = lambda w: w.astype(BF16)
    r = _matmul(xm[0], bf(w_r))
    k = _matmul(xm[2], bf(w_k))
    v = _matmul(xm[3], bf(w_v))
    wl = _matmul(_matmul(xm[1], bf(w1), act="tanh", out_dtype=BF16), bf(w2), w0)
    araw = _matmul(_matmul(xm[4], bf(a1), out_dtype=BF16), bf(a2), a0)
    g = _matmul(_matmul(xm[5], bf(g1), act="sigmoid", out_dtype=BF16), bf(g2))
    yg = _wkv(r, k, v, wl, araw, g, k_k, k_a, r_k, ln_w, ln_b, batch, seq)
    return _matmul(yg, bf(w_o))


def kernel(x, ab_w_in, ab_b_qkv, ssm_conv_w, ssm_conv_b, ssm_dt_bias, ssm_a_log, ssm_d, ssm_norm_w, attn_sinks,
           ab_w_out, ab_b_out, rwkv_mix, rwkv_w_r, rwkv_w_k, rwkv_w_v, rwkv_w_o, rwkv_w0, rwkv_w1, rwkv_w2,
           rwkv_a0, rwkv_a1, rwkv_a2, rwkv_g1, rwkv_g2, rwkv_k_k, rwkv_k_a, rwkv_r_k, rwkv_ln_w, rwkv_ln_b,
           ln_mix_g, ln_mix_b, ln_ffn_g, ln_ffn_b, router_w, router_b, moe_w_gate, moe_w_up, moe_w_down):
    batch, seq, d = x.shape
    x2 = x.reshape(batch * seq, d)
    xb = x2.astype(BF16)
    for layer in range(DEPTH):
        i = layer // 2
        if layer % 2 == 0:
            mix = _ssd_swa_mixer(x2, xb, batch, seq, ab_w_in[i], ab_b_qkv[i], ssm_conv_w[i], ssm_conv_b[i],
                                 ssm_dt_bias[i], ssm_a_log[i], ssm_d[i], ssm_norm_w[i], attn_sinks[i],
                                 ab_w_out[i], ab_b_out[i])
        else:
            mix = _rwkv7_time_mix(x2, batch, seq, rwkv_mix[i], rwkv_w_r[i], rwkv_w_k[i], rwkv_w_v[i], rwkv_w_o[i],
                                  rwkv_w0[i], rwkv_w1[i], rwkv_w2[i], rwkv_a0[i], rwkv_a1[i], rwkv_a2[i],
                                  rwkv_g1[i], rwkv_g2[i], rwkv_k_k[i], rwkv_k_a[i], rwkv_r_k[i].reshape(-1),
                                  rwkv_ln_w[i], rwkv_ln_b[i])
        h, _ = _add_layer_norm(x2, mix, ln_mix_g[layer], ln_mix_b[layer])
        x2, xb = _moe_block(h, router_w, router_b, moe_w_gate, moe_w_up, moe_w_down, layer,
                            ln_ffn_g[layer], ln_ffn_b[layer])
    return x2.reshape(batch, seq, d)
```

```python
import functools
import math

import jax
import jax.numpy as jnp
import numpy as np
from jax import lax
from jax.experimental import pallas as pl
from jax.experimental.pallas import tpu as pltpu

F32 = jnp.float32
BF16 = jnp.bfloat16
HIGHEST = lax.Precision.HIGHEST

D_MODEL = 4096
DEPTH = 2
SSM_HEAD_DIM = 64
SSM_HEADS = 64
SSM_GROUPS = 8
SSM_HEADS_PER_GROUP = 8
SSM_STATE = 128
SSM_CONV = 4
SSM_CHUNK = 128
SSM_GROUP_WIDTH = SSM_HEADS_PER_GROUP * SSM_HEAD_DIM
SSM_CONV_CH = D_MODEL + 2 * SSM_GROUPS * SSM_STATE
SSM_NORM_EPS = 1e-5
ATT_HEADS = 64
ATT_KV_HEADS = 8
ATT_HEAD_DIM = 64
ATT_REP = 8
ATT_WINDOW = 128
ROPE_DIM = 16
ROPE_THETA = 500000.0
Q_COLS = 4096
KV_COLS = 512
RWKV_HEAD = 64
RWKV_GN_EPS = 64e-5
N_EXPERTS = 16
N_EXPERT_GROUPS = 4
EXPERTS_PER_GROUP = 4
TOP_K = 2
D_EXPERT = 1536
ALPHA = (2 * DEPTH) ** 0.25
LN_EPS = 1e-5

VMEM_LIMIT_BYTES = 56 * 1024 * 1024
LANES = 128
SUBLANES = 8


def _cparams(semantics):
    return pltpu.CompilerParams(dimension_semantics=semantics, vmem_limit_bytes=VMEM_LIMIT_BYTES)


def _tile(dim, pref):
    if dim <= pref:
        return dim
    t = pref
    while dim % t:
        t //= 2
    return t


def _silu(v):
    return v * jax.nn.sigmoid(v)


def _softplus(v):
    return jnp.maximum(v, 0.0) + jnp.log1p(jnp.exp(-jnp.abs(v)))


def _mm_kernel(*refs, act, precision):
    a_refs, (w_ref, b_ref, o_ref) = refs[:-3], refs[-3:]
    r = b_ref[...]
    k0 = 0
    for a_ref in a_refs:
        k1 = k0 + a_ref.shape[1]
        r = r + jnp.dot(a_ref[...], w_ref[k0:k1, :], preferred_element_type=F32, precision=precision)
        k0 = k1
    if act == "tanh":
        r = jnp.tanh(r)
    elif act == "sigmoid":
        r = jax.nn.sigmoid(r)
    o_ref[...] = r.astype(o_ref.dtype)


MATMUL_VMEM_BUDGET = 44 * 1024 * 1024


def _matmul(a, w, bias=None, *, act=None, out_dtype=F32, precision=None, tm=1024, col_start=0, n=None):
    panels = a if isinstance(a, (tuple, list)) else (a,)
    m = panels[0].shape[0]
    kdim = sum(p.shape[1] for p in panels)
    n = w.shape[1] if n is None else n
    tm = _tile(m, tm)
    out_bytes = jnp.dtype(out_dtype).itemsize
    for tn in (512, 256, LANES):
        tn = _tile(n, tn)
        need = 2 * (tm * kdim * panels[0].dtype.itemsize + kdim * tn * w.dtype.itemsize + tm * tn * out_bytes)
        if need <= MATMUL_VMEM_BUDGET:
            break
    assert need <= MATMUL_VMEM_BUDGET and col_start % tn == 0 and w.shape[0] == kdim
    col0 = col_start // tn
    if bias is None:
        bias = jnp.zeros((n,), F32)
    bias = bias.reshape(1, n).astype(F32)
    return pl.pallas_call(
        functools.partial(_mm_kernel, act=act, precision=precision),
        grid=(m // tm, n // tn),
        in_specs=[pl.BlockSpec((tm, p.shape[1]), lambda i, j: (i, 0)) for p in panels] + [
            pl.BlockSpec((kdim, tn), lambda i, j: (0, col0 + j)),
            pl.BlockSpec((1, tn), lambda i, j: (0, j)),
        ],
        out_specs=pl.BlockSpec((tm, tn), lambda i, j: (i, j)),
        out_shape=jax.ShapeDtypeStruct((m, n), out_dtype),
        compiler_params=_cparams(("parallel", "arbitrary")),
        name="matmul",
    )(*panels, w, bias)


def _layer_norm_rows(v, g, b):
    mu = jnp.mean(v, -1, keepdims=True)
    var = jnp.mean(jnp.square(v - mu), -1, keepdims=True)
    return (v - mu) * lax.rsqrt(var + LN_EPS) * g + b


def _to_token_major(o3_ref, v):
    for s in range(v.shape[1] // LANES):
        o3_ref[:, s, :] = v[:, s * LANES:(s + 1) * LANES].astype(o3_ref.dtype)


def _from_token_major(v3_ref, rows):
    return jnp.concatenate([v3_ref[rows, s, :] for s in range(v3_ref.shape[1])], axis=1)


def _add_ln_kernel(x_ref, y_ref, g_ref, b_ref, o_ref, o3_ref):
    r = _layer_norm_rows(ALPHA * x_ref[...] + y_ref[...], g_ref[...], b_ref[...])
    o_ref[...] = r
    _to_token_major(o3_ref, r)


def _add_layer_norm(x, y, g, b, *, tm=128):
    t, d = x.shape
    tm = _tile(t, tm)
    row = pl.BlockSpec((tm, d), lambda i: (i, 0))
    row3 = pl.BlockSpec((tm, d // LANES, LANES), lambda i: (i, 0, 0))
    vec = pl.BlockSpec((1, d), lambda i: (0, 0))
    return pl.pallas_call(
        _add_ln_kernel,
        grid=(t // tm,),
        in_specs=[row, row, vec, vec],
        out_specs=[row, row3],
        out_shape=[jax.ShapeDtypeStruct((t, d), F32), jax.ShapeDtypeStruct((t, d // LANES, LANES), F32)],
        compiler_params=_cparams(("parallel",)),
        name="add_layer_norm",
    )(x, y, g.reshape(1, d), b.reshape(1, d))


CONV_HALO = SUBLANES
SSD_CH = SSM_GROUP_WIDTH + 2 * SSM_STATE


def _ssd_kernel(z_ref, x_ref, b_ref, c_ref, wx_ref, wb_ref, wc_ref, bx_ref, bb_ref, bc_ref,
                dtc_ref, dtr_ref, dbc_ref, dbr_ref, alc_ref, alr_ref, d_ref, nw_ref,
                o_ref, state_ref, pad_ref):
    ch, gw, ns, hd = SSM_CHUNK, SSM_GROUP_WIDTH, SSM_STATE, SSM_HEAD_DIM

    @pl.when(pl.program_id(2) == 0)
    def _():
        state_ref[...] = jnp.zeros_like(state_ref)
        pad_ref[0:CONV_HALO, :] = jnp.zeros((CONV_HALO, SSD_CH), F32)

    pad_ref[CONV_HALO:CONV_HALO + ch, 0:gw] = x_ref[...]
    pad_ref[CONV_HALO:CONV_HALO + ch, gw:gw + ns] = b_ref[...]
    pad_ref[CONV_HALO:CONV_HALO + ch, gw + ns:SSD_CH] = c_ref[...]
    w = jnp.concatenate([wx_ref[...], wb_ref[...], wc_ref[...]], axis=1)
    acc = jnp.concatenate([bx_ref[...], bb_ref[...], bc_ref[...]], axis=1)
    base = CONV_HALO - (SSM_CONV - 1)
    for k in range(SSM_CONV):
        acc = acc + w[k:k + 1, :] * pad_ref[base + k:base + k + ch, :]
    pad_ref[0:CONV_HALO, :] = pad_ref[ch:ch + CONV_HALO, :]
    xbc = _silu(acc)
    xc, bm, cm = xbc[:, :gw], xbc[:, gw:gw + ns], xbc[:, gw + ns:]

    dt_c = _softplus(dtc_ref[0] + dbc_ref[0])
    dt_r = _softplus(dtr_ref[...] + dbr_ref[...])
    a_c = -jnp.exp(alc_ref[0])
    a_r = -jnp.exp(alr_ref[...])
    row = lax.broadcasted_iota(jnp.int32, (ch, ch), 0)
    col = lax.broadcasted_iota(jnp.int32, (ch, ch), 1)
    causal = row >= col
    acs_c = jnp.dot(causal.astype(F32), dt_c * a_c, precision=HIGHEST, preferred_element_type=F32)
    acs_r = jnp.dot(dt_r * a_r, (row <= col).astype(F32), precision=HIGHEST, preferred_element_type=F32)

    hrow = lax.broadcasted_iota(jnp.int32, (SSM_HEADS_PER_GROUP, gw), 0)
    hcol = lax.broadcasted_iota(jnp.int32, (SSM_HEADS_PER_GROUP, gw), 1)
    expand = (hcol // hd == hrow).astype(F32)
    dt_x = jnp.dot(dt_c, expand, precision=HIGHEST, preferred_element_type=F32)
    acs_x = jnp.dot(acs_c, expand, precision=HIGHEST, preferred_element_type=F32)

    xdt = xc * dt_x
    xdt_b = xdt.astype(BF16)
    bm_b, cm_b = bm.astype(BF16), cm.astype(BF16)
    cb = lax.dot_general(cm_b, bm_b, (((1,), (1,)), ((), ())), preferred_element_type=F32)
    prev = state_ref[...]
    y_off = jnp.dot(cm_b, prev.astype(BF16), preferred_element_type=F32) * jnp.exp(acs_x)

    lane = lax.broadcasted_iota(jnp.int32, (ch, 2 * hd), 1)
    pairs = []
    for pr in range(SSM_HEADS_PER_GROUP // 2):
        xp = xdt_b[:, pr * 2 * hd:(pr + 1) * 2 * hd]
        halves = []
        for q in range(2):
            r = 2 * pr + q
            diff = acs_c[:, r:r + 1] - acs_r[r:r + 1, :]
            seg = jnp.where(causal, jnp.exp(jnp.where(causal, diff, 0.0)), 0.0)
            halves.append(jnp.dot((cb * seg).astype(BF16), xp, preferred_element_type=F32))
        pairs.append(jnp.where(lane < hd, halves[0], halves[1]))
    y = jnp.concatenate(pairs, axis=1) + y_off + d_ref[...] * xc

    acs_last = acs_x[ch - 1:ch, :]
    contrib = lax.dot_general(bm_b, (xdt * jnp.exp(acs_last - acs_x)).astype(BF16),
                              (((0,), (0,)), ((), ())), preferred_element_type=F32)
    state_ref[...] = prev * jnp.exp(acs_last) + contrib

    u = y * _silu(z_ref[...])
    u = u * lax.rsqrt(jnp.mean(u * u, -1, keepdims=True) + SSM_NORM_EPS)
    o_ref[...] = (u * nw_ref[...]).astype(o_ref.dtype)


def _ssd_mixer(u_z, u_xbc, u_dt, conv_w, conv_b, dt_bias, a_log, d_skip, norm_w, batch, seq):
    t = batch * seq
    nc = seq // SSM_CHUNK
    g, r, gw, ns = SSM_GROUPS, SSM_HEADS_PER_GROUP, SSM_GROUP_WIDTH, SSM_STATE
    xblocks = D_MODEL // ns
    conv_wt = conv_w.T
    conv_b2 = conv_b.reshape(1, SSM_CONV_CH)
    dt_col = u_dt.reshape(t, g, r).transpose(1, 0, 2)
    dt_row = u_dt.T
    rows = lambda b, gi, c: (b * nc + c, gi)
    in_specs = [
        pl.BlockSpec((SSM_CHUNK, gw), rows),
        pl.BlockSpec((SSM_CHUNK, gw), rows),
        pl.BlockSpec((SSM_CHUNK, ns), lambda b, gi, c: (b * nc + c, xblocks + gi)),
        pl.BlockSpec((SSM_CHUNK, ns), lambda b, gi, c: (b * nc + c, xblocks + g + gi)),
        pl.BlockSpec((SSM_CONV, gw), lambda b, gi, c: (0, gi)),
        pl.BlockSpec((SSM_CONV, ns), lambda b, gi, c: (0, xblocks + gi)),
        pl.BlockSpec((SSM_CONV, ns), lambda b, gi, c: (0, xblocks + g + gi)),
        pl.BlockSpec((1, gw), lambda b, gi, c: (0, gi)),
        pl.BlockSpec((1, ns), lambda b, gi, c: (0, xblocks + gi)),
        pl.BlockSpec((1, ns), lambda b, gi, c: (0, xblocks + g + gi)),
        pl.BlockSpec((1, SSM_CHUNK, r), lambda b, gi, c: (gi, b * nc + c, 0)),
        pl.BlockSpec((r, SSM_CHUNK), lambda b, gi, c: (gi, b * nc + c)),
        pl.BlockSpec((1, 1, r), lambda b, gi, c: (gi, 0, 0)),
        pl.BlockSpec((r, 1), lambda b, gi, c: (gi, 0)),
        pl.BlockSpec((1, 1, r), lambda b, gi, c: (gi, 0, 0)),
        pl.BlockSpec((r, 1), lambda b, gi, c: (gi, 0)),
        pl.BlockSpec((1, gw), lambda b, gi, c: (0, gi)),
        pl.BlockSpec((1, gw), lambda b, gi, c: (0, gi)),
    ]
    return pl.pallas_call(
        _ssd_kernel,
        grid=(batch, g, nc),
        in_specs=in_specs,
        out_specs=pl.BlockSpec((SSM_CHUNK, gw), rows),
        out_shape=jax.ShapeDtypeStruct((t, D_MODEL), BF16),
        scratch_shapes=[pltpu.VMEM((ns, gw), F32), pltpu.VMEM((CONV_HALO + SSM_CHUNK, SSD_CH), F32)],
        compiler_params=_cparams(("parallel", "parallel", "arbitrary")),
        name="ssd_mixer",
    )(u_z, u_xbc, u_xbc, u_xbc, conv_wt, conv_wt, conv_wt, conv_b2, conv_b2, conv_b2,
      dt_col, dt_row, dt_bias.reshape(g, 1, r), dt_bias.reshape(SSM_HEADS, 1),
      a_log.reshape(g, 1, r), a_log.reshape(SSM_HEADS, 1),
      jnp.repeat(d_skip, SSM_HEAD_DIM).reshape(1, D_MODEL), norm_w.reshape(1, D_MODEL))


def _rotary_tables(seq):
    half = ROPE_DIM // 2
    inv_freq = ROPE_THETA ** (-jnp.arange(half, dtype=F32) / half)
    ang = jnp.arange(seq, dtype=F32)[:, None] * inv_freq[None, :]
    ones = jnp.ones((seq, ATT_HEAD_DIM - ROPE_DIM), F32)
    cos = jnp.concatenate([jnp.cos(ang), jnp.cos(ang), ones], -1)
    sin = jnp.concatenate([jnp.sin(ang), jnp.sin(ang), 0.0 * ones], -1)
    return jnp.tile(cos, (1, ATT_REP)), jnp.tile(sin, (1, ATT_REP))


def _rotate_half_matrix():
    width = ATT_REP * ATT_HEAD_DIM
    half = ROPE_DIM // 2
    p = np.zeros((width, width), np.float32)
    for j in range(width):
        if j % ATT_HEAD_DIM < half:
            p[j + half, j] = -1.0
        elif j % ATT_HEAD_DIM < ROPE_DIM:
            p[j - half, j] = 1.0
    return jnp.asarray(p, BF16)


def _rotate(v, cos, sin, perm):
    hi = v.astype(BF16)
    lo = (v - hi.astype(F32)).astype(BF16)
    partner = jnp.dot(hi, perm, preferred_element_type=F32) + jnp.dot(lo, perm, preferred_element_type=F32)
    return v * cos + partner * sin


def _swa_kernel(sink_ref, q_ref, kc_ref, kp_ref, vc_ref, vp_ref, cosc_ref, sinc_ref, cosp_ref, sinp_ref,
                perm_ref, o_ref):
    w, hd = ATT_WINDOW, ATT_HEAD_DIM
    h = pl.program_id(1)
    n = pl.program_id(2)
    perm = perm_ref[...]
    perm_k = perm[:hd, :hd]
    q = _rotate(q_ref[...], cosc_ref[...], sinc_ref[...], perm)
    k_cur = _rotate(kc_ref[0, 0], cosc_ref[:, :hd], sinc_ref[:, :hd], perm_k)
    k_prev = _rotate(kp_ref[0, 0], cosp_ref[:, :hd], sinp_ref[:, :hd], perm_k)
    k_all = jnp.concatenate([k_prev, k_cur], axis=0).astype(BF16)
    v_all = jnp.concatenate([vp_ref[0, 0], vc_ref[0, 0]], axis=0).astype(BF16)
    qi = lax.broadcasted_iota(jnp.int32, (w, 2 * w), 0) + w
    ki = lax.broadcasted_iota(jnp.int32, (w, 2 * w), 1)
    rel = qi - ki
    first = jnp.where(n > 0, 0, w)
    mask = (rel >= 0) & (rel < ATT_WINDOW) & (ki >= first)
    outs = []
    for r in range(ATT_REP):
        qr = q[:, r * hd:(r + 1) * hd].astype(BF16)
        s = lax.dot_general(qr, k_all, (((1,), (1,)), ((), ())), preferred_element_type=F32) * (hd ** -0.5)
        s = jnp.where(mask, s, -jnp.inf)
        sink = sink_ref[h * ATT_REP + r]
        m = jnp.maximum(jnp.max(s, -1, keepdims=True), sink)
        e = jnp.exp(s - m)
        probs = e / (jnp.sum(e, -1, keepdims=True) + jnp.exp(sink - m))
        outs.append(jnp.dot(probs.astype(BF16), v_all, preferred_element_type=F32))
    o_ref[...] = jnp.concatenate(outs, axis=1).astype(o_ref.dtype)


def _swa_attention(qkv, sinks, batch, seq):
    t = batch * seq
    nb = seq // ATT_WINDOW
    hd, w = ATT_HEAD_DIM, ATT_WINDOW
    qw = ATT_REP * hd
    k4 = qkv[:, Q_COLS:Q_COLS + KV_COLS].reshape(batch, seq, ATT_KV_HEADS, hd).transpose(0, 2, 1, 3)
    v4 = qkv[:, Q_COLS + KV_COLS:].reshape(batch, seq, ATT_KV_HEADS, hd).transpose(0, 2, 1, 3)
    cos, sin = _rotary_tables(seq)
    cur = pl.BlockSpec((1, 1, w, hd), lambda b, h, n: (b, h, n, 0))
    prev = pl.BlockSpec((1, 1, w, hd), lambda b, h, n: (b, h, jnp.maximum(n - 1, 0), 0))
    tab_cur = pl.BlockSpec((w, qw), lambda b, h, n: (n, 0))
    tab_prev = pl.BlockSpec((w, qw), lambda b, h, n: (jnp.maximum(n - 1, 0), 0))
    return pl.pallas_call(
        _swa_kernel,
        grid=(batch, ATT_KV_HEADS, nb),
        in_specs=[
            pl.BlockSpec(memory_space=pltpu.SMEM),
            pl.BlockSpec((w, qw), lambda b, h, n: (b * nb + n, h)),
            cur, prev, cur, prev, tab_cur, tab_cur, tab_prev, tab_prev,
            pl.BlockSpec((qw, qw), lambda b, h, n: (0, 0)),
        ],
        out_specs=pl.BlockSpec((w, qw), lambda b, h, n: (b * nb + n, h)),
        out_shape=jax.ShapeDtypeStruct((t, Q_COLS), BF16),
        compiler_params=_cparams(("parallel", "parallel", "arbitrary")),
        name="swa_attention",
    )(sinks, qkv, k4, k4, v4, v4, cos, sin, cos, sin, _rotate_half_matrix())


def _ssd_swa_mixer(x2, xb, batch, seq, w_in, b_qkv, conv_w, conv_b, dt_bias, a_log, d_skip, norm_w, sinks,
                   w_out, b_out):
    o1 = D_MODEL
    o2 = o1 + SSM_CONV_CH
    o3 = o2 + SSM_HEADS
    w_in_b = w_in.astype(BF16)
    u_z = _matmul(xb, w_in_b, n=o1)
    u_xbc = _matmul(xb, w_in_b, col_start=o1, n=SSM_CONV_CH)
    u_dt = _matmul(x2, w_in[:, o2:o3], precision=HIGHEST)
    qkv = _matmul(xb, w_in_b[:, o3:], b_qkv)
    y_ssm = _ssd_mixer(u_z, u_xbc, u_dt, conv_w, conv_b, dt_bias, a_log, d_skip, norm_w, batch, seq)
    y_att = _swa_attention(qkv, sinks, batch, seq)
    return _matmul((y_ssm, y_att), w_out.astype(BF16), b_out)


MOE_TILE = 512
MOE_GATHER_TILE = 256
ROUTER_TILE = 256


def _router_kernel(h_ref, wt_ref, b_ref, ids_ref, wts_ref):
    logits = lax.dot_general(wt_ref[...], h_ref[...], (((1,), (1,)), ((), ())),
                             precision=HIGHEST, preferred_element_type=F32) + b_ref[...]
    e = jnp.exp(logits - jnp.max(logits, 0, keepdims=True))
    probs = e / jnp.sum(e, 0, keepdims=True)
    p = [probs[i:i + 1, :] for i in range(N_EXPERTS)]

    keep, score = [], []
    for g in range(N_EXPERT_GROUPS):
        members = range(g * EXPERTS_PER_GROUP, (g + 1) * EXPERTS_PER_GROUP)
        s = jnp.zeros_like(p[0])
        for i in members:
            rank = jnp.zeros_like(p[0])
            for j in members:
                if j != i:
                    beats = (p[j] > p[i]) | ((p[j] == p[i]) & (j < i)) if j < i else (p[j] > p[i])
                    rank = rank + beats.astype(F32)
            keep.append(rank < TOP_K)
            s = s + jnp.where(keep[i], p[i], 0.0)
        score.append(s)
    chosen = []
    for g in range(N_EXPERT_GROUPS):
        c = jnp.ones_like(p[0]) > 0
        for g2 in range(N_EXPERT_GROUPS):
            if g2 < g:
                c = c & (score[g] > score[g2])
            elif g2 > g:
                c = c & (score[g] >= score[g2])
        chosen.append(c)
    denom = jnp.zeros_like(p[0])
    for g in range(N_EXPERT_GROUPS):
        denom = denom + jnp.where(chosen[g], score[g], 0.0)
    count = jnp.zeros_like(p[0])
    id0 = jnp.zeros_like(p[0])
    id1 = jnp.zeros_like(p[0])
    w0 = jnp.zeros_like(p[0])
    w1 = jnp.zeros_like(p[0])
    for i in range(N_EXPERTS):
        sel = keep[i] & chosen[i // EXPERTS_PER_GROUP]
        gate = p[i] / denom
        first = sel & (count == 0.0)
        second = sel & (count == 1.0)
        id0 = jnp.where(first, float(i), id0)
        w0 = jnp.where(first, gate, w0)
        id1 = jnp.where(second, float(i), id1)
        w1 = jnp.where(second, gate, w1)
        count = count + sel.astype(F32)
    ids_ref[0:1, :] = id0.astype(jnp.int32)
    ids_ref[1:2, :] = id1.astype(jnp.int32)
    wts_ref[0:1, :] = w0
    wts_ref[1:2, :] = w1


def _router(h, router_w, router_b):
    t, d = h.shape
    tm = _tile(t, ROUTER_TILE)
    return pl.pallas_call(
        _router_kernel,
        grid=(t // tm,),
        in_specs=[
            pl.BlockSpec((tm, d), lambda i: (i, 0)),
            pl.BlockSpec((N_EXPERTS, d), lambda i: (0, 0)),
            pl.BlockSpec((N_EXPERTS, 1), lambda i: (0, 0)),
        ],
        out_specs=[pl.BlockSpec((TOP_K, tm), lambda i: (0, i)), pl.BlockSpec((TOP_K, tm), lambda i: (0, i))],
        out_shape=[jax.ShapeDtypeStruct((TOP_K, t), jnp.int32), jax.ShapeDtypeStruct((TOP_K, t), F32)],
        compiler_params=_cparams(("parallel",)),
        name="moe_router",
    )(h, router_w.T, router_b.reshape(N_EXPERTS, 1))


def _route_metadata(ids, wts, tm):
    t = ids.shape[1]
    na = TOP_K * t
    eid = ids.T.reshape(na)
    onehot = (eid[:, None] == jnp.arange(N_EXPERTS, dtype=jnp.int32)[None, :]).astype(jnp.int32)
    csum = jnp.cumsum(onehot, axis=0)
    counts = csum[-1]
    rank = jnp.sum(csum * onehot, axis=1) - 1
    padded = (counts + tm - 1) // tm * tm
    pend = jnp.cumsum(padded)
    dest = (pend - padded)[eid] + rank
    n_tiles = na // tm + N_EXPERTS
    n_rows = n_tiles * tm
    row_token = jnp.zeros((n_rows,), jnp.int32).at[dest].set(jnp.arange(na, dtype=jnp.int32) // TOP_K)
    row_weight = jnp.zeros((n_rows,), F32).at[dest].set(wts.T.reshape(na))
    tile_start = jnp.arange(n_tiles, dtype=jnp.int32) * tm
    tile_valid = (tile_start < pend[-1]).astype(jnp.int32)
    last_expert = jnp.max(jnp.where(counts > 0, jnp.arange(N_EXPERTS, dtype=jnp.int32), 0))
    tile_expert = jnp.minimum(jnp.searchsorted(pend, tile_start, side="right").astype(jnp.int32), last_expert)
    return row_token, row_weight, dest, tile_expert, tile_valid


def _gather_rows(src_hbm, dst_ref, idx_ref, n, sem):
    def row_copy(r, src_row):
        return pltpu.make_async_copy(src_hbm.at[pl.ds(src_row, 1)], dst_ref.at[pl.ds(r, 1)], sem)

    def start(r, carry):
        row_copy(r, idx_ref[0, 0, r]).start()
        return carry

    def wait(r, carry):
        row_copy(r, 0).wait()
        return carry

    lax.fori_loop(0, n, start, 0)
    lax.fori_loop(0, n, wait, 0)


def _moe_gather_kernel(tv_ref, rt_ref, h_hbm, o_ref, xg_ref, sem, *, tm):
    valid = tv_ref[pl.program_id(0)] == 1

    @pl.when(valid)
    def _():
        _gather_rows(h_hbm, xg_ref, rt_ref, tm, sem)
        o_ref[...] = _from_token_major(xg_ref, slice(None)).astype(o_ref.dtype)

    @pl.when(jnp.logical_not(valid))
    def _():
        o_ref[...] = jnp.zeros_like(o_ref)


def _expert_changed(te_ref, i):
    return (i == 0) | (te_ref[i] != te_ref[jnp.maximum(i - 1, 0)])


def _moe_up_kernel(te_ref, tv_ref, x_ref, wg_ref, wu_ref, o_ref, wgb_ref, wub_ref):
    i = pl.program_id(1)
    valid = tv_ref[i] == 1

    @pl.when(_expert_changed(te_ref, i))
    def _():
        wgb_ref[...] = wg_ref[0].astype(BF16)
        wub_ref[...] = wu_ref[0].astype(BF16)

    @pl.when(valid)
    def _():
        xt = x_ref[...]
        gate = jnp.dot(xt, wgb_ref[...], preferred_element_type=F32)
        up = jnp.dot(xt, wub_ref[...], preferred_element_type=F32)
        o_ref[...] = (_silu(gate) * up).astype(o_ref.dtype)

    @pl.when(jnp.logical_not(valid))
    def _():
        o_ref[...] = jnp.zeros_like(o_ref)


def _moe_down_kernel(te_ref, tv_ref, he_ref, wd_ref, rw_ref, o_ref, wdb_ref):
    i = pl.program_id(1)
    valid = tv_ref[i] == 1

    @pl.when(_expert_changed(te_ref, i))
    def _():
        wdb_ref[...] = wd_ref[0].astype(BF16)

    @pl.when(valid)
    def _():
        _to_token_major(o_ref, jnp.dot(he_ref[...], wdb_ref[...], preferred_element_type=F32) * rw_ref[...])

    @pl.when(jnp.logical_not(valid))
    def _():
        o_ref[...] = jnp.zeros_like(o_ref)


def _moe_combine_ln_kernel(pos_ref, h_ref, y_hbm, g_ref, b_ref, o_ref, ob_ref, yg_ref, sem, *, tm):
    _gather_rows(y_hbm, yg_ref, pos_ref, TOP_K * tm, sem)
    ffn = _from_token_major(yg_ref, slice(0, tm)) + _from_token_major(yg_ref, slice(tm, TOP_K * tm))
    r = _layer_norm_rows(ALPHA * h_ref[...] + ffn, g_ref[...], b_ref[...])
    o_ref[...] = r
    ob_ref[...] = r.astype(BF16)


def _moe_block(h, h3, router_w, router_b, w_gate, w_up, w_down, layer, ln_g, ln_b, *, tm=MOE_TILE, tn_up=512,
               tn_down=1024, tm_out=128):
    t, d = h.shape
    slabs = d // LANES
    ids, wts = _router(h, router_w, router_b)
    row_token, row_weight, dest, tile_expert, tile_valid = _route_metadata(ids, wts, tm)
    n_rows = row_token.shape[0]
    n_tiles = n_rows // tm
    de = w_gate.shape[-1]
    tn_up, tn_down = _tile(de, tn_up), _tile(d, tn_down)

    tg = _tile(tm, MOE_GATHER_TILE)
    per = tm // tg
    x_rows = pl.pallas_call(
        functools.partial(_moe_gather_kernel, tm=tg),
        grid_spec=pltpu.PrefetchScalarGridSpec(
            num_scalar_prefetch=1,
            grid=(n_rows // tg,),
            in_specs=[
                pl.BlockSpec((1, 1, tg), lambda i, tv: (i, 0, 0), memory_space=pltpu.SMEM),
                pl.BlockSpec(memory_space=pl.ANY),
            ],
            out_specs=pl.BlockSpec((tg, d), lambda i, tv: (i, 0)),
            scratch_shapes=[pltpu.VMEM((tg, slabs, LANES), F32), pltpu.SemaphoreType.DMA(())],
        ),
        out_shape=jax.ShapeDtypeStruct((n_rows, d), BF16),
        compiler_params=_cparams(("arbitrary",)),
        name="moe_gather",
    )(jnp.repeat(tile_valid, per), row_token.reshape(n_rows // tg, 1, tg), h3)

    he = pl.pallas_call(
        _moe_up_kernel,
        grid_spec=pltpu.PrefetchScalarGridSpec(
            num_scalar_prefetch=2,
            grid=(de // tn_up, n_tiles),
            in_specs=[
                pl.BlockSpec((tm, d), lambda j, i, te, tv: (i, 0)),
                pl.BlockSpec((None, 1, d, tn_up), lambda j, i, te, tv: (layer, te[i], 0, j)),
                pl.BlockSpec((None, 1, d, tn_up), lambda j, i, te, tv: (layer, te[i], 0, j)),
            ],
            out_specs=pl.BlockSpec((tm, tn_up), lambda j, i, te, tv: (i, j)),
            scratch_shapes=[pltpu.VMEM((d, tn_up), BF16), pltpu.VMEM((d, tn_up), BF16)],
        ),
        out_shape=jax.ShapeDtypeStruct((n_rows, de), BF16),
        compiler_params=_cparams(("arbitrary", "arbitrary")),
        name="moe_gate_up",
    )(tile_expert, tile_valid, x_rows, w_gate, w_up)

    y_rows = pl.pallas_call(
        _moe_down_kernel,
        grid_spec=pltpu.PrefetchScalarGridSpec(
            num_scalar_prefetch=2,
            grid=(d // tn_down, n_tiles),
            in_specs=[
                pl.BlockSpec((tm, de), lambda j, i, te, tv: (i, 0)),
                pl.BlockSpec((None, 1, de, tn_down), lambda j, i, te, tv: (layer, te[i], 0, j)),
                pl.BlockSpec((tm, 1), lambda j, i, te, tv: (i, 0)),
            ],
            out_specs=pl.BlockSpec((tm, tn_down // LANES, LANES), lambda j, i, te, tv: (i, j, 0)),
            scratch_shapes=[pltpu.VMEM((de, tn_down), BF16)],
        ),
        out_shape=jax.ShapeDtypeStruct((n_rows, slabs, LANES), F32),
        compiler_params=_cparams(("arbitrary", "arbitrary")),
        name="moe_down",
    )(tile_expert, tile_valid, he, w_down, row_weight.reshape(n_rows, 1))

    tm_out = _tile(t, tm_out)
    n_out = t // tm_out
    pos = dest.reshape(n_out, tm_out, TOP_K).transpose(0, 2, 1).reshape(n_out, 1, TOP_K * tm_out)
    row = pl.BlockSpec((tm_out, d), lambda i: (i, 0))
    vec = pl.BlockSpec((1, d), lambda i: (0, 0))
    return pl.pallas_call(
        functools.partial(_moe_combine_ln_kernel, tm=tm_out),
        grid=(n_out,),
        in_specs=[
            pl.BlockSpec((1, 1, TOP_K * tm_out), lambda i: (i, 0, 0), memory_space=pltpu.SMEM),
            row,
            pl.BlockSpec(memory_space=pl.ANY),
            vec, vec,
        ],
        out_specs=[row, row],
        out_shape=[jax.ShapeDtypeStruct((t, d), F32), jax.ShapeDtypeStruct((t, d), BF16)],
        scratch_shapes=[pltpu.VMEM((TOP_K * tm_out, slabs, LANES), F32), pltpu.SemaphoreType.DMA(())],
        compiler_params=_cparams(("arbitrary",)),
        name="moe_combine_ln",
    )(pos, h, y_rows, ln_g.reshape(1, d), ln_b.reshape(1, d))


def _shift_mix_kernel(x_ref, xp_ref, mix_ref, o_ref, pad_ref, *, tm, seq):
    i = pl.program_id(0)
    starts_sequence = (i * tm) % seq == 0
    pad_ref[0:SUBLANES, :] = jnp.where(starts_sequence, 0.0, xp_ref[...])
    pad_ref[SUBLANES:SUBLANES + tm, :] = x_ref[...]
    xv = x_ref[...]
    xx = pad_ref[SUBLANES - 1:SUBLANES - 1 + tm, :] - xv
    for m in range(mix_ref.shape[0]):
        o_ref[m] = (xv + xx * mix_ref[m:m + 1, :]).astype(o_ref.dtype)


def _shift_mix(x, mix, seq, *, tm=128):
    t, d = x.shape
    nm = mix.shape[0]
    tm = _tile(seq, tm)
    per = tm // SUBLANES
    return pl.pallas_call(
        functools.partial(_shift_mix_kernel, tm=tm, seq=seq),
        grid=(t // tm,),
        in_specs=[
            pl.BlockSpec((tm, d), lambda i: (i, 0)),
            pl.BlockSpec((SUBLANES, d), lambda i: (jnp.maximum(i * per - 1, 0), 0)),
            pl.BlockSpec((nm, d), lambda i: (0, 0)),
        ],
        out_specs=pl.BlockSpec((nm, tm, d), lambda i: (0, i, 0)),
        out_shape=jax.ShapeDtypeStruct((nm, t, d), BF16),
        scratch_shapes=[pltpu.VMEM((SUBLANES + tm, d), F32)],
        compiler_params=_cparams(("parallel",)),
        name="rwkv_shift_mix",
    )(x, x, mix)


WKV_CHUNK = 64
WKV_HEADS = 4
WKV_WIDTH = WKV_HEADS * RWKV_HEAD
WKV_BLOCK = 512
NEUMANN_STEPS = 5


def _wkv_kernel(r_ref, k_ref, v_ref, wl_ref, ar_ref, g_ref, kk_ref, ka_ref, rk_ref, lnw_ref, lnb_ref,
                o_ref, state_ref, *, tb):
    lc, hw, wd = WKV_CHUNK, RWKV_HEAD, WKV_WIDTH
    nch = tb // lc

    @pl.when(pl.program_id(2) == 0)
    def _():
        state_ref[...] = jnp.zeros_like(state_ref)

    wrow = lax.broadcasted_iota(jnp.int32, (wd, wd), 0)
    wcol = lax.broadcasted_iota(jnp.int32, (wd, wd), 1)
    same_head = wrow // hw == wcol // hw
    ones_bd = same_head.astype(BF16)

    def block_diag(m):
        return jnp.where(same_head, jnp.concatenate([m] * WKV_HEADS, axis=0), 0.0).astype(BF16)

    def head_sum(m):
        hi = m.astype(BF16)
        lo = (m - hi.astype(F32)).astype(BF16)
        return jnp.dot(hi, ones_bd, preferred_element_type=F32) + jnp.dot(lo, ones_bd, preferred_element_type=F32)

    def mm(a, b):
        return jnp.dot(a.astype(BF16), b, preferred_element_type=F32)

    r = r_ref[...]
    k = k_ref[...]
    v = v_ref[...]
    w_log = -_softplus(-wl_ref[...]) - 0.5
    lw = -jnp.exp(w_log)
    a_sig = jax.nn.sigmoid(ar_ref[...])
    kx = k * kk_ref[...]
    kk = kx / jnp.maximum(jnp.sqrt(head_sum(kx * kx)), 1e-12)
    k2 = k * (1.0 + (a_sig - 1.0) * ka_ref[...])
    a_s = -kk
    b_s = kk * a_sig

    trow = lax.broadcasted_iota(jnp.int32, (tb, tb), 0)
    tcol = lax.broadcasted_iota(jnp.int32, (tb, tb), 1)
    same_chunk = trow // lc == tcol // lc
    tril = (same_chunk & (trow >= tcol)).astype(BF16)
    lw_hi = lw.astype(BF16)
    lw_r1 = lw - lw_hi.astype(F32)
    lw_mid = lw_r1.astype(BF16)
    lw_lo = (lw_r1 - lw_mid.astype(F32)).astype(BF16)
    cum = (jnp.dot(tril, lw_hi, preferred_element_type=F32) + jnp.dot(tril, lw_mid, preferred_element_type=F32)
           + jnp.dot(tril, lw_lo, preferred_element_type=F32))
    tot = jnp.concatenate([jnp.broadcast_to(cum[(c + 1) * lc - 1:(c + 1) * lc, :], (lc, wd)) for c in range(nch)],
                          axis=0)
    grow = jnp.exp(-cum)
    rt = r * jnp.exp(cum)
    at = a_s * jnp.exp(cum - lw)
    bt = b_s * grow
    kt = k2 * grow
    rest = jnp.exp(tot - cum)
    bh = b_s * rest
    kh = k2 * rest
    p_end = jnp.exp(tot)

    t_idx = lax.broadcasted_iota(jnp.int32, (lc, wd), 0)
    s_idx = lax.broadcasted_iota(jnp.int32, (lc, wd), 1) % hw
    strict = t_idx > s_idx
    incl = t_idx >= s_idx
    eye = (t_idx == s_idx).astype(F32)
    nt = (((1,), (1,)), ((), ()))

    tn = (((0,), (0,)), ((), ()))
    chunks = [slice(c * lc, (c + 1) * lc) for c in range(nch)]

    a_ab, a_rb, a_ak, a_rk = [], [], [], []
    for sl in chunks:
        ar = jnp.concatenate([at[sl], rt[sl]], axis=0).astype(BF16)
        xb = lax.dot_general(ar, block_diag(bt[sl]), nt, preferred_element_type=F32)
        xk = lax.dot_general(ar, block_diag(kt[sl]), nt, preferred_element_type=F32)
        a_ab.append(jnp.where(strict, xb[:lc], 0.0))
        a_rb.append(jnp.where(incl, xb[lc:], 0.0))
        a_ak.append(jnp.where(strict, xk[:lc], 0.0))
        a_rk.append(jnp.where(incl, xk[lc:], 0.0))
    inv = [eye + a for a in a_ab]
    pw = [mm(a, block_diag(a)) for a in a_ab]
    for step in range(1, NEUMANN_STEPS + 1):
        last = step == NEUMANN_STEPS
        lhs = inv if last else [jnp.concatenate([t, p], axis=0) for t, p in zip(inv, pw)]
        prod = [mm(x, block_diag(p)) for x, p in zip(lhs, pw)]
        inv = [t + q[:lc] for t, q in zip(inv, prod)]
        if not last:
            pw = [q[lc:] for q in prod]
    v_bd = [block_diag(v[sl]) for sl in chunks]
    ta = [mm(t, block_diag(at[sl])) for t, sl in zip(inv, chunks)]
    av = [mm(jnp.concatenate([ak, rk], axis=0), vb) for ak, rk, vb in zip(a_ak, a_rk, v_bd)]
    y0 = [x[lc:] for x in av]
    u0 = [mm(t, block_diag(x[:lc])) for t, x in zip(inv, av)]
    gain, add = [], []
    for c, sl in enumerate(chunks):
        bh_b = bh[sl].astype(BF16)
        gain.append(jnp.where(same_head, lax.dot_general(ta[c].astype(BF16), bh_b, tn, preferred_element_type=F32),
                              0.0).astype(BF16))
        uv = jnp.concatenate([u0[c], v[sl]], axis=0).astype(BF16)
        bk = jnp.concatenate([bh_b, kh[sl].astype(BF16)], axis=0)
        add.append(jnp.where(same_head, lax.dot_general(uv, bk, tn, preferred_element_type=F32), 0.0))

    states = [state_ref[...]]
    for c in range(nch):
        s0 = states[-1]
        states.append(s0 * p_end[c * lc:c * lc + 1, :] + mm(s0, gain[c]) + add[c])
    state_ref[...] = states[nch]

    ys = []
    for c, sl in enumerate(chunks):
        tr = jnp.concatenate([ta[c], rt[sl]], axis=0).astype(BF16)
        xs = lax.dot_general(tr, states[c].astype(BF16), nt, preferred_element_type=F32)
        u = xs[:lc] + u0[c]
        ys.append(xs[lc:] + mm(a_rb[c], block_diag(u)) + y0[c])

    y = jnp.concatenate(ys, axis=0)
    mu = head_sum(y) / hw
    dev = y - mu
    var = head_sum(dev * dev) / hw
    yn = dev * lax.rsqrt(var + RWKV_GN_EPS) * lnw_ref[...] + lnb_ref[...]
    bonus = head_sum(r * k2 * rk_ref[...]) * v
    o_ref[...] = ((yn + bonus) * g_ref[...]).astype(o_ref.dtype)


def _wkv(r, k, v, wl, araw, g, k_k, k_a, r_k, ln_w, ln_b, batch, seq):
    t, d = r.shape
    tb = _tile(seq, WKV_BLOCK)
    nt = seq // tb
    wd = WKV_WIDTH
    rows = pl.BlockSpec((tb, wd), lambda b, h, n: (b * nt + n, h))
    vec = pl.BlockSpec((1, wd), lambda b, h, n: (0, h))
    return pl.pallas_call(
        functools.partial(_wkv_kernel, tb=tb),
        grid=(batch, d // wd, nt),
        in_specs=[rows] * 6 + [vec] * 5,
        out_specs=rows,
        out_shape=jax.ShapeDtypeStruct((t, d), BF16),
        scratch_shapes=[pltpu.VMEM((wd, wd), F32)],
        compiler_params=_cparams(("parallel", "parallel", "arbitrary")),
        name="rwkv_wkv",
    )(r, k, v, wl, araw, g, k_k.reshape(1, d), k_a.reshape(1, d), r_k.reshape(1, d),
      ln_w.reshape(1, d), ln_b.reshape(1, d))


def _rwkv7_time_mix(x2, batch, seq, mix, w_r, w_k, w_v, w_o, w0, w1, w2, a0, a1, a2, g1, g2, k_k, k_a, r_k,
                    ln_w, ln_b):
    xm = _shift_mix(x2, mix, seq)
    bf = lambda w: w.astype(BF16)
    r = _matmul(xm[0], bf(w_r))
    k = _matmul(xm[2], bf(w_k))
    v = _matmul(xm[3], bf(w_v))
    wl = _matmul(_matmul(xm[1], bf(w1), act="tanh", out_dtype=BF16), bf(w2), w0)
    araw = _matmul(_matmul(xm[4], bf(a1), out_dtype=BF16), bf(a2), a0)
    g = _matmul(_matmul(xm[5], bf(g1), act="sigmoid", out_dtype=BF16), bf(g2))
    yg = _wkv(r, k, v, wl, araw, g, k_k, k_a, r_k, ln_w, ln_b, batch, seq)
    return _matmul(yg, bf(w_o))


def kernel(x, ab_w_in, ab_b_qkv, ssm_conv_w, ssm_conv_b, ssm_dt_bias, ssm_a_log, ssm_d, ssm_norm_w, attn_sinks,
           ab_w_out, ab_b_out, rwkv_mix, rwkv_w_r, rwkv_w_k, rwkv_w_v, rwkv_w_o, rwkv_w0, rwkv_w1, rwkv_w2,
           rwkv_a0, rwkv_a1, rwkv_a2, rwkv_g1, rwkv_g2, rwkv_k_k, rwkv_k_a, rwkv_r_k, rwkv_ln_w, rwkv_ln_b,
           ln_mix_g, ln_mix_b, ln_ffn_g, ln_ffn_b, router_w, router_b, moe_w_gate, moe_w_up, moe_w_down):
    batch, seq, d = x.shape
    x2 = x.reshape(batch * seq, d)
    xb = x2.astype(BF16)
    for layer in range(DEPTH):
        i = layer // 2
        if layer % 2 == 0:
            mix = _ssd_swa_mixer(x2, xb, batch, seq, ab_w_in[i], ab_b_qkv[i], ssm_conv_w[i], ssm_conv_b[i],
                                 ssm_dt_bias[i], ssm_a_log[i], ssm_d[i], ssm_norm_w[i], attn_sinks[i],
                                 ab_w_out[i], ab_b_out[i])
        else:
            mix = _rwkv7_time_mix(x2, batch, seq, rwkv_mix[i], rwkv_w_r[i], rwkv_w_k[i], rwkv_w_v[i], rwkv_w_o[i],
                                  rwkv_w0[i], rwkv_w1[i], rwkv_w2[i], rwkv_a0[i], rwkv_a1[i], rwkv_a2[i],
                                  rwkv_g1[i], rwkv_g2[i], rwkv_k_k[i], rwkv_k_a[i], rwkv_r_k[i].reshape(-1),
                                  rwkv_ln_w[i], rwkv_ln_b[i])
        h, h3 = _add_layer_norm(x2, mix, ln_mix_g[layer], ln_mix_b[layer])
        x2, xb = _moe_block(h, h3, router_w, router_b, moe_w_gate, moe_w_up, moe_w_down, layer,
                            ln_ffn_g[layer], ln_ffn_b[layer])
    return x2.reshape(batch, seq, d)
```

```python
import functools
import math

import jax
import jax.numpy as jnp
import numpy as np
from jax import lax
from jax.experimental import pallas as pl
from jax.experimental.pallas import tpu as pltpu

F32 = jnp.float32
BF16 = jnp.bfloat16
HIGHEST = lax.Precision.HIGHEST

D_MODEL = 4096
DEPTH = 2
SSM_HEAD_DIM = 64
SSM_HEADS = 64
SSM_GROUPS = 8
SSM_HEADS_PER_GROUP = 8
SSM_STATE = 128
SSM_CONV = 4
SSM_CHUNK = 128
SSM_GROUP_WIDTH = SSM_HEADS_PER_GROUP * SSM_HEAD_DIM
SSM_CONV_CH = D_MODEL + 2 * SSM_GROUPS * SSM_STATE
SSM_NORM_EPS = 1e-5
ATT_HEADS = 64
ATT_KV_HEADS = 8
ATT_HEAD_DIM = 64
ATT_REP = 8
ATT_WINDOW = 128
ROPE_DIM = 16
ROPE_THETA = 500000.0
Q_COLS = 4096
KV_COLS = 512
RWKV_HEAD = 64
RWKV_GN_EPS = 64e-5
N_EXPERTS = 16
N_EXPERT_GROUPS = 4
EXPERTS_PER_GROUP = 4
TOP_K = 2
D_EXPERT = 1536
ALPHA = (2 * DEPTH) ** 0.25
LN_EPS = 1e-5

VMEM_LIMIT_BYTES = 56 * 1024 * 1024
LANES = 128
SUBLANES = 8


def _cparams(semantics):
    return pltpu.CompilerParams(dimension_semantics=semantics, vmem_limit_bytes=VMEM_LIMIT_BYTES)


def _tile(dim, pref):
    if dim <= pref:
        return dim
    t = pref
    while dim % t:
        t //= 2
    return t


def _silu(v):
    return v * jax.nn.sigmoid(v)


def _softplus(v):
    return jnp.maximum(v, 0.0) + jnp.log1p(jnp.exp(-jnp.abs(v)))


def _mm_kernel(*refs, act, precision):
    a_refs, (w_ref, b_ref, o_ref) = refs[:-3], refs[-3:]
    r = b_ref[...]
    k0 = 0
    for a_ref in a_refs:
        k1 = k0 + a_ref.shape[1]
        r = r + jnp.dot(a_ref[...], w_ref[k0:k1, :], preferred_element_type=F32, precision=precision)
        k0 = k1
    if act == "tanh":
        r = jnp.tanh(r)
    elif act == "sigmoid":
        r = jax.nn.sigmoid(r)
    o_ref[...] = r.astype(o_ref.dtype)


MATMUL_VMEM_BUDGET = 44 * 1024 * 1024


def _matmul(a, w, bias=None, *, act=None, out_dtype=F32, precision=None, tm=1024, col_start=0, n=None):
    panels = a if isinstance(a, (tuple, list)) else (a,)
    m = panels[0].shape[0]
    kdim = sum(p.shape[1] for p in panels)
    n = w.shape[1] if n is None else n
    tm = _tile(m, tm)
    out_bytes = jnp.dtype(out_dtype).itemsize
    for tn in (512, 256, LANES):
        tn = _tile(n, tn)
        need = 2 * (tm * kdim * panels[0].dtype.itemsize + kdim * tn * w.dtype.itemsize + tm * tn * out_bytes)
        if need <= MATMUL_VMEM_BUDGET:
            break
    assert need <= MATMUL_VMEM_BUDGET and col_start % tn == 0 and w.shape[0] == kdim
    col0 = col_start // tn
    if bias is None:
        bias = jnp.zeros((n,), F32)
    bias = bias.reshape(1, n).astype(F32)
    return pl.pallas_call(
        functools.partial(_mm_kernel, act=act, precision=precision),
        grid=(m // tm, n // tn),
        in_specs=[pl.BlockSpec((tm, p.shape[1]), lambda i, j: (i, 0)) for p in panels] + [
            pl.BlockSpec((kdim, tn), lambda i, j: (0, col0 + j)),
            pl.BlockSpec((1, tn), lambda i, j: (0, j)),
        ],
        out_specs=pl.BlockSpec((tm, tn), lambda i, j: (i, j)),
        out_shape=jax.ShapeDtypeStruct((m, n), out_dtype),
        compiler_params=_cparams(("parallel", "arbitrary")),
        name="matmul",
    )(*panels, w, bias)


def _layer_norm_rows(v, g, b):
    mu = jnp.mean(v, -1, keepdims=True)
    var = jnp.mean(jnp.square(v - mu), -1, keepdims=True)
    return (v - mu) * lax.rsqrt(var + LN_EPS) * g + b


def _to_token_major(o3_ref, v):
    for s in range(v.shape[1] // LANES):
        o3_ref[:, s, :] = v[:, s * LANES:(s + 1) * LANES].astype(o3_ref.dtype)


def _tiles_to_rows(tiles_ref, t0, t1):
    return jnp.concatenate([tiles_ref[t0:t1, s].reshape((t1 - t0) * SUBLANES, LANES)
                            for s in range(tiles_ref.shape[1])], axis=1)


def _add_ln_kernel(x_ref, y_ref, g_ref, b_ref, o_ref, o3_ref):
    r = _layer_norm_rows(ALPHA * x_ref[...] + y_ref[...], g_ref[...], b_ref[...])
    o_ref[...] = r
    _to_token_major(o3_ref, r)


def _add_layer_norm(x, y, g, b, *, tm=128):
    t, d = x.shape
    tm = _tile(t, tm)
    row = pl.BlockSpec((tm, d), lambda i: (i, 0))
    row3 = pl.BlockSpec((tm, d // LANES, LANES), lambda i: (i, 0, 0))
    vec = pl.BlockSpec((1, d), lambda i: (0, 0))
    return pl.pallas_call(
        _add_ln_kernel,
        grid=(t // tm,),
        in_specs=[row, row, vec, vec],
        out_specs=[row, row3],
        out_shape=[jax.ShapeDtypeStruct((t, d), F32), jax.ShapeDtypeStruct((t, d // LANES, LANES), F32)],
        compiler_params=_cparams(("parallel",)),
        name="add_layer_norm",
    )(x, y, g.reshape(1, d), b.reshape(1, d))


CONV_HALO = SUBLANES
SSD_CH = SSM_GROUP_WIDTH + 2 * SSM_STATE


def _ssd_kernel(z_ref, x_ref, b_ref, c_ref, wx_ref, wb_ref, wc_ref, bx_ref, bb_ref, bc_ref,
                dtc_ref, dtr_ref, dbc_ref, dbr_ref, alc_ref, alr_ref, d_ref, nw_ref,
                o_ref, state_ref, pad_ref):
    ch, gw, ns, hd = SSM_CHUNK, SSM_GROUP_WIDTH, SSM_STATE, SSM_HEAD_DIM

    @pl.when(pl.program_id(2) == 0)
    def _():
        state_ref[...] = jnp.zeros_like(state_ref)
        pad_ref[0:CONV_HALO, :] = jnp.zeros((CONV_HALO, SSD_CH), F32)

    pad_ref[CONV_HALO:CONV_HALO + ch, 0:gw] = x_ref[...]
    pad_ref[CONV_HALO:CONV_HALO + ch, gw:gw + ns] = b_ref[...]
    pad_ref[CONV_HALO:CONV_HALO + ch, gw + ns:SSD_CH] = c_ref[...]
    w = jnp.concatenate([wx_ref[...], wb_ref[...], wc_ref[...]], axis=1)
    acc = jnp.concatenate([bx_ref[...], bb_ref[...], bc_ref[...]], axis=1)
    base = CONV_HALO - (SSM_CONV - 1)
    for k in range(SSM_CONV):
        acc = acc + w[k:k + 1, :] * pad_ref[base + k:base + k + ch, :]
    pad_ref[0:CONV_HALO, :] = pad_ref[ch:ch + CONV_HALO, :]
    xbc = _silu(acc)
    xc, bm, cm = xbc[:, :gw], xbc[:, gw:gw + ns], xbc[:, gw + ns:]

    dt_c = _softplus(dtc_ref[0] + dbc_ref[0])
    dt_r = _softplus(dtr_ref[...] + dbr_ref[...])
    a_c = -jnp.exp(alc_ref[0])
    a_r = -jnp.exp(alr_ref[...])
    row = lax.broadcasted_iota(jnp.int32, (ch, ch), 0)
    col = lax.broadcasted_iota(jnp.int32, (ch, ch), 1)
    causal = row >= col
    acs_c = jnp.dot(causal.astype(F32), dt_c * a_c, precision=HIGHEST, preferred_element_type=F32)
    acs_r = jnp.dot(dt_r * a_r, (row <= col).astype(F32), precision=HIGHEST, preferred_element_type=F32)

    hrow = lax.broadcasted_iota(jnp.int32, (SSM_HEADS_PER_GROUP, gw), 0)
    hcol = lax.broadcasted_iota(jnp.int32, (SSM_HEADS_PER_GROUP, gw), 1)
    expand = (hcol // hd == hrow).astype(F32)
    dt_x = jnp.dot(dt_c, expand, precision=HIGHEST, preferred_element_type=F32)
    acs_x = jnp.dot(acs_c, expand, precision=HIGHEST, preferred_element_type=F32)

    xdt = xc * dt_x
    xdt_b = xdt.astype(BF16)
    bm_b, cm_b = bm.astype(BF16), cm.astype(BF16)
    cb = lax.dot_general(cm_b, bm_b, (((1,), (1,)), ((), ())), preferred_element_type=F32)
    prev = state_ref[...]
    y_off = jnp.dot(cm_b, prev.astype(BF16), preferred_element_type=F32) * jnp.exp(acs_x)

    lane = lax.broadcasted_iota(jnp.int32, (ch, 2 * hd), 1)
    pairs = []
    for pr in range(SSM_HEADS_PER_GROUP // 2):
        xp = xdt_b[:, pr * 2 * hd:(pr + 1) * 2 * hd]
        halves = []
        for q in range(2):
            r = 2 * pr + q
            diff = acs_c[:, r:r + 1] - acs_r[r:r + 1, :]
            seg = jnp.where(causal, jnp.exp(jnp.where(causal, diff, 0.0)), 0.0)
            halves.append(jnp.dot((cb * seg).astype(BF16), xp, preferred_element_type=F32))
        pairs.append(jnp.where(lane < hd, halves[0], halves[1]))
    y = jnp.concatenate(pairs, axis=1) + y_off + d_ref[...] * xc

    acs_last = acs_x[ch - 1:ch, :]
    contrib = lax.dot_general(bm_b, (xdt * jnp.exp(acs_last - acs_x)).astype(BF16),
                              (((0,), (0,)), ((), ())), preferred_element_type=F32)
    state_ref[...] = prev * jnp.exp(acs_last) + contrib

    u = y * _silu(z_ref[...])
    u = u * lax.rsqrt(jnp.mean(u * u, -1, keepdims=True) + SSM_NORM_EPS)
    o_ref[...] = (u * nw_ref[...]).astype(o_ref.dtype)


def _ssd_mixer(u_z, u_xbc, u_dt, conv_w, conv_b, dt_bias, a_log, d_skip, norm_w, batch, seq):
    t = batch * seq
    nc = seq // SSM_CHUNK
    g, r, gw, ns = SSM_GROUPS, SSM_HEADS_PER_GROUP, SSM_GROUP_WIDTH, SSM_STATE
    xblocks = D_MODEL // ns
    conv_wt = conv_w.T
    conv_b2 = conv_b.reshape(1, SSM_CONV_CH)
    dt_col = u_dt.reshape(t, g, r).transpose(1, 0, 2)
    dt_row = u_dt.T
    rows = lambda b, gi, c: (b * nc + c, gi)
    in_specs = [
        pl.BlockSpec((SSM_CHUNK, gw), rows),
        pl.BlockSpec((SSM_CHUNK, gw), rows),
        pl.BlockSpec((SSM_CHUNK, ns), lambda b, gi, c: (b * nc + c, xblocks + gi)),
        pl.BlockSpec((SSM_CHUNK, ns), lambda b, gi, c: (b * nc + c, xblocks + g + gi)),
        pl.BlockSpec((SSM_CONV, gw), lambda b, gi, c: (0, gi)),
        pl.BlockSpec((SSM_CONV, ns), lambda b, gi, c: (0, xblocks + gi)),
        pl.BlockSpec((SSM_CONV, ns), lambda b, gi, c: (0, xblocks + g + gi)),
        pl.BlockSpec((1, gw), lambda b, gi, c: (0, gi)),
        pl.BlockSpec((1, ns), lambda b, gi, c: (0, xblocks + gi)),
        pl.BlockSpec((1, ns), lambda b, gi, c: (0, xblocks + g + gi)),
        pl.BlockSpec((1, SSM_CHUNK, r), lambda b, gi, c: (gi, b * nc + c, 0)),
        pl.BlockSpec((r, SSM_CHUNK), lambda b, gi, c: (gi, b * nc + c)),
        pl.BlockSpec((1, 1, r), lambda b, gi, c: (gi, 0, 0)),
        pl.BlockSpec((r, 1), lambda b, gi, c: (gi, 0)),
        pl.BlockSpec((1, 1, r), lambda b, gi, c: (gi, 0, 0)),
        pl.BlockSpec((r, 1), lambda b, gi, c: (gi, 0)),
        pl.BlockSpec((1, gw), lambda b, gi, c: (0, gi)),
        pl.BlockSpec((1, gw), lambda b, gi, c: (0, gi)),
    ]
    return pl.pallas_call(
        _ssd_kernel,
        grid=(batch, g, nc),
        in_specs=in_specs,
        out_specs=pl.BlockSpec((SSM_CHUNK, gw), rows),
        out_shape=jax.ShapeDtypeStruct((t, D_MODEL), BF16),
        scratch_shapes=[pltpu.VMEM((ns, gw), F32), pltpu.VMEM((CONV_HALO + SSM_CHUNK, SSD_CH), F32)],
        compiler_params=_cparams(("parallel", "parallel", "arbitrary")),
        name="ssd_mixer",
    )(u_z, u_xbc, u_xbc, u_xbc, conv_wt, conv_wt, conv_wt, conv_b2, conv_b2, conv_b2,
      dt_col, dt_row, dt_bias.reshape(g, 1, r), dt_bias.reshape(SSM_HEADS, 1),
      a_log.reshape(g, 1, r), a_log.reshape(SSM_HEADS, 1),
      jnp.repeat(d_skip, SSM_HEAD_DIM).reshape(1, D_MODEL), norm_w.reshape(1, D_MODEL))


def _rotary_tables(seq):
    half = ROPE_DIM // 2
    inv_freq = ROPE_THETA ** (-jnp.arange(half, dtype=F32) / half)
    ang = jnp.arange(seq, dtype=F32)[:, None] * inv_freq[None, :]
    ones = jnp.ones((seq, ATT_HEAD_DIM - ROPE_DIM), F32)
    cos = jnp.concatenate([jnp.cos(ang), jnp.cos(ang), ones], -1)
    sin = jnp.concatenate([jnp.sin(ang), jnp.sin(ang), 0.0 * ones], -1)
    return jnp.tile(cos, (1, ATT_REP)), jnp.tile(sin, (1, ATT_REP))


def _rotate_half_matrix():
    width = ATT_REP * ATT_HEAD_DIM
    half = ROPE_DIM // 2
    p = np.zeros((width, width), np.float32)
    for j in range(width):
        if j % ATT_HEAD_DIM < half:
            p[j + half, j] = -1.0
        elif j % ATT_HEAD_DIM < ROPE_DIM:
            p[j - half, j] = 1.0
    return jnp.asarray(p, BF16)


def _rotate(v, cos, sin, perm):
    hi = v.astype(BF16)
    lo = (v - hi.astype(F32)).astype(BF16)
    partner = jnp.dot(hi, perm, preferred_element_type=F32) + jnp.dot(lo, perm, preferred_element_type=F32)
    return v * cos + partner * sin


def _swa_kernel(sink_ref, q_ref, kc_ref, kp_ref, vc_ref, vp_ref, cosc_ref, sinc_ref, cosp_ref, sinp_ref,
                perm_ref, o_ref):
    w, hd = ATT_WINDOW, ATT_HEAD_DIM
    h = pl.program_id(1)
    n = pl.program_id(2)
    perm = perm_ref[...]
    perm_k = perm[:hd, :hd]
    q = _rotate(q_ref[...], cosc_ref[...], sinc_ref[...], perm)
    k_cur = _rotate(kc_ref[0, 0], cosc_ref[:, :hd], sinc_ref[:, :hd], perm_k)
    k_prev = _rotate(kp_ref[0, 0], cosp_ref[:, :hd], sinp_ref[:, :hd], perm_k)
    k_all = jnp.concatenate([k_prev, k_cur], axis=0).astype(BF16)
    v_all = jnp.concatenate([vp_ref[0, 0], vc_ref[0, 0]], axis=0).astype(BF16)
    qi = lax.broadcasted_iota(jnp.int32, (w, 2 * w), 0) + w
    ki = lax.broadcasted_iota(jnp.int32, (w, 2 * w), 1)
    rel = qi - ki
    first = jnp.where(n > 0, 0, w)
    mask = (rel >= 0) & (rel < ATT_WINDOW) & (ki >= first)
    outs = []
    for r in range(ATT_REP):
        qr = q[:, r * hd:(r + 1) * hd].astype(BF16)
        s = lax.dot_general(qr, k_all, (((1,), (1,)), ((), ())), preferred_element_type=F32) * (hd ** -0.5)
        s = jnp.where(mask, s, -jnp.inf)
        sink = sink_ref[h * ATT_REP + r]
        m = jnp.maximum(jnp.max(s, -1, keepdims=True), sink)
        e = jnp.exp(s - m)
        probs = e / (jnp.sum(e, -1, keepdims=True) + jnp.exp(sink - m))
        outs.append(jnp.dot(probs.astype(BF16), v_all, preferred_element_type=F32))
    o_ref[...] = jnp.concatenate(outs, axis=1).astype(o_ref.dtype)


def _swa_attention(qkv, sinks, batch, seq):
    t = batch * seq
    nb = seq // ATT_WINDOW
    hd, w = ATT_HEAD_DIM, ATT_WINDOW
    qw = ATT_REP * hd
    k4 = qkv[:, Q_COLS:Q_COLS + KV_COLS].reshape(batch, seq, ATT_KV_HEADS, hd).transpose(0, 2, 1, 3)
    v4 = qkv[:, Q_COLS + KV_COLS:].reshape(batch, seq, ATT_KV_HEADS, hd).transpose(0, 2, 1, 3)
    cos, sin = _rotary_tables(seq)
    cur = pl.BlockSpec((1, 1, w, hd), lambda b, h, n: (b, h, n, 0))
    prev = pl.BlockSpec((1, 1, w, hd), lambda b, h, n: (b, h, jnp.maximum(n - 1, 0), 0))
    tab_cur = pl.BlockSpec((w, qw), lambda b, h, n: (n, 0))
    tab_prev = pl.BlockSpec((w, qw), lambda b, h, n: (jnp.maximum(n - 1, 0), 0))
    return pl.pallas_call(
        _swa_kernel,
        grid=(batch, ATT_KV_HEADS, nb),
        in_specs=[
            pl.BlockSpec(memory_space=pltpu.SMEM),
            pl.BlockSpec((w, qw), lambda b, h, n: (b * nb + n, h)),
            cur, prev, cur, prev, tab_cur, tab_cur, tab_prev, tab_prev,
            pl.BlockSpec((qw, qw), lambda b, h, n: (0, 0)),
        ],
        out_specs=pl.BlockSpec((w, qw), lambda b, h, n: (b * nb + n, h)),
        out_shape=jax.ShapeDtypeStruct((t, Q_COLS), BF16),
        compiler_params=_cparams(("parallel", "parallel", "arbitrary")),
        name="swa_attention",
    )(sinks, qkv, k4, k4, v4, v4, cos, sin, cos, sin, _rotate_half_matrix())


def _ssd_swa_mixer(x2, xb, batch, seq, w_in, b_qkv, conv_w, conv_b, dt_bias, a_log, d_skip, norm_w, sinks,
                   w_out, b_out):
    o1 = D_MODEL
    o2 = o1 + SSM_CONV_CH
    o3 = o2 + SSM_HEADS
    w_in_b = w_in.astype(BF16)
    u_z = _matmul(xb, w_in_b, n=o1)
    u_xbc = _matmul(xb, w_in_b, col_start=o1, n=SSM_CONV_CH)
    u_dt = _matmul(x2, w_in[:, o2:o3], precision=HIGHEST)
    qkv = _matmul(xb, w_in_b[:, o3:], b_qkv)
    y_ssm = _ssd_mixer(u_z, u_xbc, u_dt, conv_w, conv_b, dt_bias, a_log, d_skip, norm_w, batch, seq)
    y_att = _swa_attention(qkv, sinks, batch, seq)
    return _matmul((y_ssm, y_att), w_out.astype(BF16), b_out)


MOE_TILE = 512
MOE_GATHER_TILE = 256
ROUTER_TILE = 256


def _router_kernel(h_ref, wt_ref, b_ref, ids_ref, wts_ref):
    logits = lax.dot_general(wt_ref[...], h_ref[...], (((1,), (1,)), ((), ())),
                             precision=HIGHEST, preferred_element_type=F32) + b_ref[...]
    e = jnp.exp(logits - jnp.max(logits, 0, keepdims=True))
    probs = e / jnp.sum(e, 0, keepdims=True)
    p = [probs[i:i + 1, :] for i in range(N_EXPERTS)]

    keep, score = [], []
    for g in range(N_EXPERT_GROUPS):
        members = range(g * EXPERTS_PER_GROUP, (g + 1) * EXPERTS_PER_GROUP)
        s = jnp.zeros_like(p[0])
        for i in members:
            rank = jnp.zeros_like(p[0])
            for j in members:
                if j != i:
                    beats = (p[j] > p[i]) | ((p[j] == p[i]) & (j < i)) if j < i else (p[j] > p[i])
                    rank = rank + beats.astype(F32)
            keep.append(rank < TOP_K)
            s = s + jnp.where(keep[i], p[i], 0.0)
        score.append(s)
    chosen = []
    for g in range(N_EXPERT_GROUPS):
        c = jnp.ones_like(p[0]) > 0
        for g2 in range(N_EXPERT_GROUPS):
            if g2 < g:
                c = c & (score[g] > score[g2])
            elif g2 > g:
                c = c & (score[g] >= score[g2])
        chosen.append(c)
    denom = jnp.zeros_like(p[0])
    for g in range(N_EXPERT_GROUPS):
        denom = denom + jnp.where(chosen[g], score[g], 0.0)
    count = jnp.zeros_like(p[0])
    id0 = jnp.zeros_like(p[0])
    id1 = jnp.zeros_like(p[0])
    w0 = jnp.zeros_like(p[0])
    w1 = jnp.zeros_like(p[0])
    for i in range(N_EXPERTS):
        sel = keep[i] & chosen[i // EXPERTS_PER_GROUP]
        gate = p[i] / denom
        first = sel & (count == 0.0)
        second = sel & (count == 1.0)
        id0 = jnp.where(first, float(i), id0)
        w0 = jnp.where(first, gate, w0)
        id1 = jnp.where(second, float(i), id1)
        w1 = jnp.where(second, gate, w1)
        count = count + sel.astype(F32)
    ids_ref[0:1, :] = id0.astype(jnp.int32)
    ids_ref[1:2, :] = id1.astype(jnp.int32)
    wts_ref[0:1, :] = w0
    wts_ref[1:2, :] = w1


def _router(h, router_w, router_b):
    t, d = h.shape
    tm = _tile(t, ROUTER_TILE)
    return pl.pallas_call(
        _router_kernel,
        grid=(t // tm,),
        in_specs=[
            pl.BlockSpec((tm, d), lambda i: (i, 0)),
            pl.BlockSpec((N_EXPERTS, d), lambda i: (0, 0)),
            pl.BlockSpec((N_EXPERTS, 1), lambda i: (0, 0)),
        ],
        out_specs=[pl.BlockSpec((TOP_K, tm), lambda i: (0, i)), pl.BlockSpec((TOP_K, tm), lambda i: (0, i))],
        out_shape=[jax.ShapeDtypeStruct((TOP_K, t), jnp.int32), jax.ShapeDtypeStruct((TOP_K, t), F32)],
        compiler_params=_cparams(("parallel",)),
        name="moe_router",
    )(h, router_w.T, router_b.reshape(N_EXPERTS, 1))


def _route_metadata(ids, wts, tm):
    t = ids.shape[1]
    na = TOP_K * t
    eid = ids.T.reshape(na)
    onehot = (eid[:, None] == jnp.arange(N_EXPERTS, dtype=jnp.int32)[None, :]).astype(jnp.int32)
    csum = jnp.cumsum(onehot, axis=0)
    counts = csum[-1]
    rank = jnp.sum(csum * onehot, axis=1) - 1
    padded = (counts + tm - 1) // tm * tm
    pend = jnp.cumsum(padded)
    dest = (pend - padded)[eid] + rank
    n_tiles = na // tm + N_EXPERTS
    n_rows = n_tiles * tm
    row_token = jnp.zeros((n_rows,), jnp.int32).at[dest].set(jnp.arange(na, dtype=jnp.int32) // TOP_K)
    row_weight = jnp.zeros((n_rows,), F32).at[dest].set(wts.T.reshape(na))
    tile_start = jnp.arange(n_tiles, dtype=jnp.int32) * tm
    tile_valid = (tile_start < pend[-1]).astype(jnp.int32)
    last_expert = jnp.max(jnp.where(counts > 0, jnp.arange(N_EXPERTS, dtype=jnp.int32), 0))
    tile_expert = jnp.minimum(jnp.searchsorted(pend, tile_start, side="right").astype(jnp.int32), last_expert)
    return row_token, row_weight, dest, tile_expert, tile_valid


def _gather_rows(src_hbm, tiles_ref, idx_ref, n, sem):
    def row_copy(tile, sub, src_row):
        return pltpu.make_async_copy(src_hbm.at[src_row], tiles_ref.at[tile, :, sub, :], sem)

    def start(tile, carry):
        for sub in range(SUBLANES):
            row_copy(tile, sub, idx_ref[0, 0, tile * SUBLANES + sub]).start()
        return carry

    def wait(tile, carry):
        for sub in range(SUBLANES):
            row_copy(tile, sub, 0).wait()
        return carry

    lax.fori_loop(0, n // SUBLANES, start, 0)
    lax.fori_loop(0, n // SUBLANES, wait, 0)


def _moe_gather_kernel(tv_ref, rt_ref, h_hbm, o_ref, xg_ref, sem, *, tm):
    valid = tv_ref[pl.program_id(0)] == 1

    @pl.when(valid)
    def _():
        _gather_rows(h_hbm, xg_ref, rt_ref, tm, sem)
        o_ref[...] = _tiles_to_rows(xg_ref, 0, tm // SUBLANES).astype(o_ref.dtype)

    @pl.when(jnp.logical_not(valid))
    def _():
        o_ref[...] = jnp.zeros_like(o_ref)


def _expert_changed(te_ref, i):
    return (i == 0) | (te_ref[i] != te_ref[jnp.maximum(i - 1, 0)])


def _moe_up_kernel(te_ref, tv_ref, x_ref, wg_ref, wu_ref, o_ref, wgb_ref, wub_ref):
    i = pl.program_id(1)
    valid = tv_ref[i] == 1

    @pl.when(_expert_changed(te_ref, i))
    def _():
        wgb_ref[...] = wg_ref[0].astype(BF16)
        wub_ref[...] = wu_ref[0].astype(BF16)

    @pl.when(valid)
    def _():
        xt = x_ref[...]
        gate = jnp.dot(xt, wgb_ref[...], preferred_element_type=F32)
        up = jnp.dot(xt, wub_ref[...], preferred_element_type=F32)
        o_ref[...] = (_silu(gate) * up).astype(o_ref.dtype)

    @pl.when(jnp.logical_not(valid))
    def _():
        o_ref[...] = jnp.zeros_like(o_ref)


def _moe_down_kernel(te_ref, tv_ref, he_ref, wd_ref, rw_ref, o_ref, wdb_ref):
    i = pl.program_id(1)
    valid = tv_ref[i] == 1

    @pl.when(_expert_changed(te_ref, i))
    def _():
        wdb_ref[...] = wd_ref[0].astype(BF16)

    @pl.when(valid)
    def _():
        _to_token_major(o_ref, jnp.dot(he_ref[...], wdb_ref[...], preferred_element_type=F32) * rw_ref[...])

    @pl.when(jnp.logical_not(valid))
    def _():
        o_ref[...] = jnp.zeros_like(o_ref)


def _moe_combine_ln_kernel(pos_ref, h_ref, y_hbm, g_ref, b_ref, o_ref, ob_ref, yg_ref, sem, *, tm):
    _gather_rows(y_hbm, yg_ref, pos_ref, TOP_K * tm, sem)
    per = tm // SUBLANES
    ffn = _tiles_to_rows(yg_ref, 0, per) + _tiles_to_rows(yg_ref, per, TOP_K * per)
    r = _layer_norm_rows(ALPHA * h_ref[...] + ffn, g_ref[...], b_ref[...])
    o_ref[...] = r
    ob_ref[...] = r.astype(BF16)


def _moe_block(h, h3, router_w, router_b, w_gate, w_up, w_down, layer, ln_g, ln_b, *, tm=MOE_TILE, tn_up=512,
               tn_down=1024, tm_out=128):
    t, d = h.shape
    slabs = d // LANES
    ids, wts = _router(h, router_w, router_b)
    row_token, row_weight, dest, tile_expert, tile_valid = _route_metadata(ids, wts, tm)
    n_rows = row_token.shape[0]
    n_tiles = n_rows // tm
    de = w_gate.shape[-1]
    tn_up, tn_down = _tile(de, tn_up), _tile(d, tn_down)

    tg = _tile(tm, MOE_GATHER_TILE)
    per = tm // tg
    x_rows = pl.pallas_call(
        functools.partial(_moe_gather_kernel, tm=tg),
        grid_spec=pltpu.PrefetchScalarGridSpec(
            num_scalar_prefetch=1,
            grid=(n_rows // tg,),
            in_specs=[
                pl.BlockSpec((1, 1, tg), lambda i, tv: (i, 0, 0), memory_space=pltpu.SMEM),
                pl.BlockSpec(memory_space=pl.ANY),
            ],
            out_specs=pl.BlockSpec((tg, d), lambda i, tv: (i, 0)),
            scratch_shapes=[pltpu.VMEM((tg // SUBLANES, slabs, SUBLANES, LANES), F32), pltpu.SemaphoreType.DMA(())],
        ),
        out_shape=jax.ShapeDtypeStruct((n_rows, d), BF16),
        compiler_params=_cparams(("arbitrary",)),
        name="moe_gather",
    )(jnp.repeat(tile_valid, per), row_token.reshape(n_rows // tg, 1, tg), h3)

    he = pl.pallas_call(
        _moe_up_kernel,
        grid_spec=pltpu.PrefetchScalarGridSpec(
            num_scalar_prefetch=2,
            grid=(de // tn_up, n_tiles),
            in_specs=[
                pl.BlockSpec((tm, d), lambda j, i, te, tv: (i, 0)),
                pl.BlockSpec((None, 1, d, tn_up), lambda j, i, te, tv: (layer, te[i], 0, j)),
                pl.BlockSpec((None, 1, d, tn_up), lambda j, i, te, tv: (layer, te[i], 0, j)),
            ],
            out_specs=pl.BlockSpec((tm, tn_up), lambda j, i, te, tv: (i, j)),
            scratch_shapes=[pltpu.VMEM((d, tn_up), BF16), pltpu.VMEM((d, tn_up), BF16)],
        ),
        out_shape=jax.ShapeDtypeStruct((n_rows, de), BF16),
        compiler_params=_cparams(("arbitrary", "arbitrary")),
        name="moe_gate_up",
    )(tile_expert, tile_valid, x_rows, w_gate, w_up)

    y_rows = pl.pallas_call(
        _moe_down_kernel,
        grid_spec=pltpu.PrefetchScalarGridSpec(
            num_scalar_prefetch=2,
            grid=(d // tn_down, n_tiles),
            in_specs=[
                pl.BlockSpec((tm, de), lambda j, i, te, tv: (i, 0)),
                pl.BlockSpec((None, 1, de, tn_down), lambda j, i, te, tv: (layer, te[i], 0, j)),
                pl.BlockSpec((tm, 1), lambda j, i, te, tv: (i, 0)),
            ],
            out_specs=pl.BlockSpec((tm, tn_down // LANES, LANES), lambda j, i, te, tv: (i, j, 0)),
            scratch_shapes=[pltpu.VMEM((de, tn_down), BF16)],
        ),
        out_shape=jax.ShapeDtypeStruct((n_rows, slabs, LANES), F32),
        compiler_params=_cparams(("arbitrary", "arbitrary")),
        name="moe_down",
    )(tile_expert, tile_valid, he, w_down, row_weight.reshape(n_rows, 1))

    tm_out = _tile(t, tm_out)
    n_out = t // tm_out
    pos = dest.reshape(n_out, tm_out, TOP_K).transpose(0, 2, 1).reshape(n_out, 1, TOP_K * tm_out)
    row = pl.BlockSpec((tm_out, d), lambda i: (i, 0))
    vec = pl.BlockSpec((1, d), lambda i: (0, 0))
    return pl.pallas_call(
        functools.partial(_moe_combine_ln_kernel, tm=tm_out),
        grid=(n_out,),
        in_specs=[
            pl.BlockSpec((1, 1, TOP_K * tm_out), lambda i: (i, 0, 0), memory_space=pltpu.SMEM),
            row,
            pl.BlockSpec(memory_space=pl.ANY),
            vec, vec,
        ],
        out_specs=[row, row],
        out_shape=[jax.ShapeDtypeStruct((t, d), F32), jax.ShapeDtypeStruct((t, d), BF16)],
        scratch_shapes=[pltpu.VMEM((TOP_K * tm_out // SUBLANES, slabs, SUBLANES, LANES), F32),
                        pltpu.SemaphoreType.DMA(())],
        compiler_params=_cparams(("arbitrary",)),
        name="moe_combine_ln",
    )(pos, h, y_rows, ln_g.reshape(1, d), ln_b.reshape(1, d))


def _shift_mix_kernel(x_ref, xp_ref, mix_ref, o_ref, pad_ref, *, tm, seq):
    i = pl.program_id(0)
    starts_sequence = (i * tm) % seq == 0
    pad_ref[0:SUBLANES, :] = jnp.where(starts_sequence, 0.0, xp_ref[...])
    pad_ref[SUBLANES:SUBLANES + tm, :] = x_ref[...]
    xv = x_ref[...]
    xx = pad_ref[SUBLANES - 1:SUBLANES - 1 + tm, :] - xv
    for m in range(mix_ref.shape[0]):
        o_ref[m] = (xv + xx * mix_ref[m:m + 1, :]).astype(o_ref.dtype)


def _shift_mix(x, mix, seq, *, tm=128):
    t, d = x.shape
    nm = mix.shape[0]
    tm = _tile(seq, tm)
    per = tm // SUBLANES
    return pl.pallas_call(
        functools.partial(_shift_mix_kernel, tm=tm, seq=seq),
        grid=(t // tm,),
        in_specs=[
            pl.BlockSpec((tm, d), lambda i: (i, 0)),
            pl.BlockSpec((SUBLANES, d), lambda i: (jnp.maximum(i * per - 1, 0), 0)),
            pl.BlockSpec((nm, d), lambda i: (0, 0)),
        ],
        out_specs=pl.BlockSpec((nm, tm, d), lambda i: (0, i, 0)),
        out_shape=jax.ShapeDtypeStruct((nm, t, d), BF16),
        scratch_shapes=[pltpu.VMEM((SUBLANES + tm, d), F32)],
        compiler_params=_cparams(("parallel",)),
        name="rwkv_shift_mix",
    )(x, x, mix)


WKV_CHUNK = 64
WKV_HEADS = 4
WKV_WIDTH = WKV_HEADS * RWKV_HEAD
WKV_BLOCK = 512
NEUMANN_STEPS = 5


def _wkv_kernel(r_ref, k_ref, v_ref, wl_ref, ar_ref, g_ref, kk_ref, ka_ref, rk_ref, lnw_ref, lnb_ref,
                o_ref, state_ref, *, tb):
    lc, hw, wd = WKV_CHUNK, RWKV_HEAD, WKV_WIDTH
    nch = tb // lc

    @pl.when(pl.program_id(2) == 0)
    def _():
        state_ref[...] = jnp.zeros_like(state_ref)

    wrow = lax.broadcasted_iota(jnp.int32, (wd, wd), 0)
    wcol = lax.broadcasted_iota(jnp.int32, (wd, wd), 1)
    same_head = wrow // hw == wcol // hw
    ones_bd = same_head.astype(BF16)

    def block_diag(m):
        return jnp.where(same_head, jnp.concatenate([m] * WKV_HEADS, axis=0), 0.0).astype(BF16)

    def head_sum(m):
        hi = m.astype(BF16)
        lo = (m - hi.astype(F32)).astype(BF16)
        return jnp.dot(hi, ones_bd, preferred_element_type=F32) + jnp.dot(lo, ones_bd, preferred_element_type=F32)

    def mm(a, b):
        return jnp.dot(a.astype(BF16), b, preferred_element_type=F32)

    r = r_ref[...]
    k = k_ref[...]
    v = v_ref[...]
    w_log = -_softplus(-wl_ref[...]) - 0.5
    lw = -jnp.exp(w_log)
    a_sig = jax.nn.sigmoid(ar_ref[...])
    kx = k * kk_ref[...]
    kk = kx / jnp.maximum(jnp.sqrt(head_sum(kx * kx)), 1e-12)
    k2 = k * (1.0 + (a_sig - 1.0) * ka_ref[...])
    a_s = -kk
    b_s = kk * a_sig

    trow = lax.broadcasted_iota(jnp.int32, (tb, tb), 0)
    tcol = lax.broadcasted_iota(jnp.int32, (tb, tb), 1)
    same_chunk = trow // lc == tcol // lc
    tril = (same_chunk & (trow >= tcol)).astype(BF16)
    lw_hi = lw.astype(BF16)
    lw_r1 = lw - lw_hi.astype(F32)
    lw_mid = lw_r1.astype(BF16)
    lw_lo = (lw_r1 - lw_mid.astype(F32)).astype(BF16)
    cum = (jnp.dot(tril, lw_hi, preferred_element_type=F32) + jnp.dot(tril, lw_mid, preferred_element_type=F32)
           + jnp.dot(tril, lw_lo, preferred_element_type=F32))
    tot = jnp.concatenate([jnp.broadcast_to(cum[(c + 1) * lc - 1:(c + 1) * lc, :], (lc, wd)) for c in range(nch)],
                          axis=0)
    grow = jnp.exp(-cum)
    rt = r * jnp.exp(cum)
    at = a_s * jnp.exp(cum - lw)
    bt = b_s * grow
    kt = k2 * grow
    rest = jnp.exp(tot - cum)
    bh = b_s * rest
    kh = k2 * rest
    p_end = jnp.exp(tot)

    t_idx = lax.broadcasted_iota(jnp.int32, (lc, wd), 0)
    s_idx = lax.broadcasted_iota(jnp.int32, (lc, wd), 1) % hw
    strict = t_idx > s_idx
    incl = t_idx >= s_idx
    eye = (t_idx == s_idx).astype(F32)
    nt = (((1,), (1,)), ((), ()))

    tn = (((0,), (0,)), ((), ()))
    chunks = [slice(c * lc, (c + 1) * lc) for c in range(nch)]

    a_ab, a_rb, a_ak, a_rk = [], [], [], []
    for sl in chunks:
        ar = jnp.concatenate([at[sl], rt[sl]], axis=0).astype(BF16)
        xb = lax.dot_general(ar, block_diag(bt[sl]), nt, preferred_element_type=F32)
        xk = lax.dot_general(ar, block_diag(kt[sl]), nt, preferred_element_type=F32)
        a_ab.append(jnp.where(strict, xb[:lc], 0.0))
        a_rb.append(jnp.where(incl, xb[lc:], 0.0))
        a_ak.append(jnp.where(strict, xk[:lc], 0.0))
        a_rk.append(jnp.where(incl, xk[lc:], 0.0))
    inv = [eye + a for a in a_ab]
    pw = [mm(a, block_diag(a)) for a in a_ab]
    for step in range(1, NEUMANN_STEPS + 1):
        last = step == NEUMANN_STEPS
        lhs = inv if last else [jnp.concatenate([t, p], axis=0) for t, p in zip(inv, pw)]
        prod = [mm(x, block_diag(p)) for x, p in zip(lhs, pw)]
        inv = [t + q[:lc] for t, q in zip(inv, prod)]
        if not last:
            pw = [q[lc:] for q in prod]
    v_bd = [block_diag(v[sl]) for sl in chunks]
    ta = [mm(t, block_diag(at[sl])) for t, sl in zip(inv, chunks)]
    av = [mm(jnp.concatenate([ak, rk], axis=0), vb) for ak, rk, vb in zip(a_ak, a_rk, v_bd)]
    y0 = [x[lc:] for x in av]
    u0 = [mm(t, block_diag(x[:lc])) for t, x in zip(inv, av)]
    gain, add = [], []
    for c, sl in enumerate(chunks):
        bh_b = bh[sl].astype(BF16)
        gain.append(jnp.where(same_head, lax.dot_general(ta[c].astype(BF16), bh_b, tn, preferred_element_type=F32),
                              0.0).astype(BF16))
        uv = jnp.concatenate([u0[c], v[sl]], axis=0).astype(BF16)
        bk = jnp.concatenate([bh_b, kh[sl].astype(BF16)], axis=0)
        add.append(jnp.where(same_head, lax.dot_general(uv, bk, tn, preferred_element_type=F32), 0.0))

    states = [state_ref[...]]
    for c in range(nch):
        s0 = states[-1]
        states.append(s0 * p_end[c * lc:c * lc + 1, :] + mm(s0, gain[c]) + add[c])
    state_ref[...] = states[nch]

    ys = []
    for c, sl in enumerate(chunks):
        tr = jnp.concatenate([ta[c], rt[sl]], axis=0).astype(BF16)
        xs = lax.dot_general(tr, states[c].astype(BF16), nt, preferred_element_type=F32)
        u = xs[:lc] + u0[c]
        ys.append(xs[lc:] + mm(a_rb[c], block_diag(u)) + y0[c])

    y = jnp.concatenate(ys, axis=0)
    mu = head_sum(y) / hw
    dev = y - mu
    var = head_sum(dev * dev) / hw
    yn = dev * lax.rsqrt(var + RWKV_GN_EPS) * lnw_ref[...] + lnb_ref[...]
    bonus = head_sum(r * k2 * rk_ref[...]) * v
    o_ref[...] = ((yn + bonus) * g_ref[...]).astype(o_ref.dtype)


def _wkv(r, k, v, wl, araw, g, k_k, k_a, r_k, ln_w, ln_b, batch, seq):
    t, d = r.shape
    tb = _tile(seq, WKV_BLOCK)
    nt = seq // tb
    wd = WKV_WIDTH
    rows = pl.BlockSpec((tb, wd), lambda b, h, n: (b * nt + n, h))
    vec = pl.BlockSpec((1, wd), lambda b, h, n: (0, h))
    return pl.pallas_call(
        functools.partial(_wkv_kernel, tb=tb),
        grid=(batch, d // wd, nt),
        in_specs=[rows] * 6 + [vec] * 5,
        out_specs=rows,
        out_shape=jax.ShapeDtypeStruct((t, d), BF16),
        scratch_shapes=[pltpu.VMEM((wd, wd), F32)],
        compiler_params=_cparams(("parallel", "parallel", "arbitrary")),
        name="rwkv_wkv",
    )(r, k, v, wl, araw, g, k_k.reshape(1, d), k_a.reshape(1, d), r_k.reshape(1, d),
      ln_w.reshape(1, d), ln_b.reshape(1, d))


def _rwkv7_time_mix(x2, batch, seq, mix, w_r, w_k, w_v, w_o, w0, w1, w2, a0, a1, a2, g1, g2, k_k, k_a, r_k,
                    ln_w, ln_b):
    xm = _shift_mix(x2, mix, seq)
    bf = lambda w: w.astype(BF16)
    r = _matmul(xm[0], bf(w_r))
    k = _matmul(xm[2], bf(w_k))
    v = _matmul(xm[3], bf(w_v))
    wl = _matmul(_matmul(xm[1], bf(w1), act="tanh", out_dtype=BF16), bf(w2), w0)
    araw = _matmul(_matmul(xm[4], bf(a1), out_dtype=BF16), bf(a2), a0)
    g = _matmul(_matmul(xm[5], bf(g1), act="sigmoid", out_dtype=BF16), bf(g2))
    yg = _wkv(r, k, v, wl, araw, g, k_k, k_a, r_k, ln_w, ln_b, batch, seq)
    return _matmul(yg, bf(w_o))


def kernel(x, ab_w_in, ab_b_qkv, ssm_conv_w, ssm_conv_b, ssm_dt_bias, ssm_a_log, ssm_d, ssm_norm_w, attn_sinks,
           ab_w_out, ab_b_out, rwkv_mix, rwkv_w_r, rwkv_w_k, rwkv_w_v, rwkv_w_o, rwkv_w0, rwkv_w1, rwkv_w2,
           rwkv_a0, rwkv_a1, rwkv_a2, rwkv_g1, rwkv_g2, rwkv_k_k, rwkv_k_a, rwkv_r_k, rwkv_ln_w, rwkv_ln_b,
           ln_mix_g, ln_mix_b, ln_ffn_g, ln_ffn_b, router_w, router_b, moe_w_gate, moe_w_up, moe_w_down):
    batch, seq, d = x.shape
    x2 = x.reshape(batch * seq, d)
    xb = x2.astype(BF16)
    for layer in range(DEPTH):
        i = layer // 2
        if layer % 2 == 0:
            mix = _ssd_swa_mixer(x2, xb, batch, seq, ab_w_in[i], ab_b_qkv[i], ssm_conv_w[i], ssm_conv_b[i],
                                 ssm_dt_bias[i], ssm_a_log[i], ssm_d[i], ssm_norm_w[i], attn_sinks[i],
                                 ab_w_out[i], ab_b_out[i])
        else:
            mix = _rwkv7_time_mix(x2, batch, seq, rwkv_mix[i], rwkv_w_r[i], rwkv_w_k[i], rwkv_w_v[i], rwkv_w_o[i],
                                  rwkv_w0[i], rwkv_w1[i], rwkv_w2[i], rwkv_a0[i], rwkv_a1[i], rwkv_a2[i],
                                  rwkv_g1[i], rwkv_g2[i], rwkv_k_k[i], rwkv_k_a[i], rwkv_r_k[i].reshape(-1),
                                  rwkv_ln_w[i], rwkv_ln_b[i])
        h, h3 = _add_layer_norm(x2, mix, ln_mix_g[layer], ln_mix_b[layer])
        x2, xb = _moe_block(h, h3, router_w, router_b, moe_w_gate, moe_w_up, moe_w_down, layer,
                            ln_ffn_g[layer], ln_ffn_b[layer])
    return x2.reshape(batch, seq, d)
```

```python
import functools
import math

import jax
import jax.numpy as jnp
import numpy as np
from jax import lax
from jax.experimental import pallas as pl
from jax.experimental.pallas import tpu as pltpu

F32 = jnp.float32
BF16 = jnp.bfloat16
HIGHEST = lax.Precision.HIGHEST

D_MODEL = 4096
DEPTH = 2
SSM_HEAD_DIM = 64
SSM_HEADS = 64
SSM_GROUPS = 8
SSM_HEADS_PER_GROUP = 8
SSM_STATE = 128
SSM_CONV = 4
SSM_CHUNK = 128
SSM_GROUP_WIDTH = SSM_HEADS_PER_GROUP * SSM_HEAD_DIM
SSM_CONV_CH = D_MODEL + 2 * SSM_GROUPS * SSM_STATE
SSM_NORM_EPS = 1e-5
ATT_HEADS = 64
ATT_KV_HEADS = 8
ATT_HEAD_DIM = 64
ATT_REP = 8
ATT_WINDOW = 128
ROPE_DIM = 16
ROPE_THETA = 500000.0
Q_COLS = 4096
KV_COLS = 512
RWKV_HEAD = 64
RWKV_GN_EPS = 64e-5
N_EXPERTS = 16
N_EXPERT_GROUPS = 4
EXPERTS_PER_GROUP = 4
TOP_K = 2
D_EXPERT = 1536
ALPHA = (2 * DEPTH) ** 0.25
LN_EPS = 1e-5

VMEM_LIMIT_BYTES = 56 * 1024 * 1024
LANES = 128
SUBLANES = 8


def _cparams(semantics):
    return pltpu.CompilerParams(dimension_semantics=semantics, vmem_limit_bytes=VMEM_LIMIT_BYTES)


def _tile(dim, pref):
    if dim <= pref:
        return dim
    t = pref
    while dim % t:
        t //= 2
    return t


def _silu(v):
    return v * jax.nn.sigmoid(v)


def _softplus(v):
    return jnp.maximum(v, 0.0) + jnp.log1p(jnp.exp(-jnp.abs(v)))


def _split3(v):
    hi = v.astype(BF16)
    rest = v - hi.astype(F32)
    mid = rest.astype(BF16)
    lo = (rest - mid.astype(F32)).astype(BF16)
    return hi, mid, lo


def _dot_split3(v, m01, *, split_lhs):
    if split_lhs:
        return sum(jnp.dot(t, m01, preferred_element_type=F32) for t in _split3(v))
    return sum(jnp.dot(m01, t, preferred_element_type=F32) for t in _split3(v))


def _mm_kernel(*refs, act, precision):
    a_refs, (w_ref, b_ref, o_ref) = refs[:-3], refs[-3:]
    r = b_ref[...]
    k0 = 0
    for a_ref in a_refs:
        k1 = k0 + a_ref.shape[1]
        r = r + jnp.dot(a_ref[...], w_ref[k0:k1, :], preferred_element_type=F32, precision=precision)
        k0 = k1
    if act == "tanh":
        r = jnp.tanh(r)
    elif act == "sigmoid":
        r = jax.nn.sigmoid(r)
    o_ref[...] = r.astype(o_ref.dtype)


MATMUL_VMEM_BUDGET = 44 * 1024 * 1024


def _matmul(a, w, bias=None, *, act=None, out_dtype=F32, precision=None, tm=1024, col_start=0, n=None):
    panels = a if isinstance(a, (tuple, list)) else (a,)
    m = panels[0].shape[0]
    kdim = sum(p.shape[1] for p in panels)
    n = w.shape[1] if n is None else n
    tm = _tile(m, tm)
    out_bytes = jnp.dtype(out_dtype).itemsize
    for tn in (512, 256, LANES):
        tn = _tile(n, tn)
        need = 2 * (tm * kdim * panels[0].dtype.itemsize + kdim * tn * w.dtype.itemsize + tm * tn * out_bytes)
        if need <= MATMUL_VMEM_BUDGET:
            break
    assert need <= MATMUL_VMEM_BUDGET and col_start % tn == 0 and w.shape[0] == kdim
    col0 = col_start // tn
    if bias is None:
        bias = jnp.zeros((n,), F32)
    bias = bias.reshape(1, n).astype(F32)
    return pl.pallas_call(
        functools.partial(_mm_kernel, act=act, precision=precision),
        grid=(m // tm, n // tn),
        in_specs=[pl.BlockSpec((tm, p.shape[1]), lambda i, j: (i, 0)) for p in panels] + [
            pl.BlockSpec((kdim, tn), lambda i, j: (0, col0 + j)),
            pl.BlockSpec((1, tn), lambda i, j: (0, j)),
        ],
        out_specs=pl.BlockSpec((tm, tn), lambda i, j: (i, j)),
        out_shape=jax.ShapeDtypeStruct((m, n), out_dtype),
        compiler_params=_cparams(("parallel", "arbitrary")),
        name="matmul",
    )(*panels, w, bias)


def _layer_norm_rows(v, g, b):
    mu = jnp.mean(v, -1, keepdims=True)
    var = jnp.mean(jnp.square(v - mu), -1, keepdims=True)
    return (v - mu) * lax.rsqrt(var + LN_EPS) * g + b


def _to_token_major(o3_ref, v):
    for s in range(v.shape[1] // LANES):
        o3_ref[:, s, :] = v[:, s * LANES:(s + 1) * LANES].astype(o3_ref.dtype)


def _tiles_to_rows(tiles_ref, t0, t1):
    return jnp.concatenate([tiles_ref[t0:t1, s].reshape((t1 - t0) * SUBLANES, LANES)
                            for s in range(tiles_ref.shape[1])], axis=1)


def _add_ln_kernel(x_ref, y_ref, g_ref, b_ref, o_ref, o3_ref):
    r = _layer_norm_rows(ALPHA * x_ref[...] + y_ref[...], g_ref[...], b_ref[...])
    o_ref[...] = r
    _to_token_major(o3_ref, r)


def _add_layer_norm(x, y, g, b, *, tm=128):
    t, d = x.shape
    tm = _tile(t, tm)
    row = pl.BlockSpec((tm, d), lambda i: (i, 0))
    row3 = pl.BlockSpec((tm, d // LANES, LANES), lambda i: (i, 0, 0))
    vec = pl.BlockSpec((1, d), lambda i: (0, 0))
    return pl.pallas_call(
        _add_ln_kernel,
        grid=(t // tm,),
        in_specs=[row, row, vec, vec],
        out_specs=[row, row3],
        out_shape=[jax.ShapeDtypeStruct((t, d), F32), jax.ShapeDtypeStruct((t, d // LANES, LANES), F32)],
        compiler_params=_cparams(("parallel",)),
        name="add_layer_norm",
    )(x, y, g.reshape(1, d), b.reshape(1, d))


CONV_HALO = SUBLANES
SSD_CH = SSM_GROUP_WIDTH + 2 * SSM_STATE


def _ssd_kernel(z_ref, x_ref, b_ref, c_ref, wx_ref, wb_ref, wc_ref, bx_ref, bb_ref, bc_ref,
                dtc_ref, dtr_ref, dbc_ref, dbr_ref, alc_ref, alr_ref, d_ref, nw_ref,
                o_ref, state_ref, pad_ref):
    ch, gw, ns, hd = SSM_CHUNK, SSM_GROUP_WIDTH, SSM_STATE, SSM_HEAD_DIM

    @pl.when(pl.program_id(2) == 0)
    def _():
        state_ref[...] = jnp.zeros_like(state_ref)
        pad_ref[0:CONV_HALO, :] = jnp.zeros((CONV_HALO, SSD_CH), F32)

    pad_ref[CONV_HALO:CONV_HALO + ch, 0:gw] = x_ref[...]
    pad_ref[CONV_HALO:CONV_HALO + ch, gw:gw + ns] = b_ref[...]
    pad_ref[CONV_HALO:CONV_HALO + ch, gw + ns:SSD_CH] = c_ref[...]
    w = jnp.concatenate([wx_ref[...], wb_ref[...], wc_ref[...]], axis=1)
    acc = jnp.concatenate([bx_ref[...], bb_ref[...], bc_ref[...]], axis=1)
    base = CONV_HALO - (SSM_CONV - 1)
    for k in range(SSM_CONV):
        acc = acc + w[k:k + 1, :] * pad_ref[base + k:base + k + ch, :]
    pad_ref[0:CONV_HALO, :] = pad_ref[ch:ch + CONV_HALO, :]
    xbc = _silu(acc)
    xc, bm, cm = xbc[:, :gw], xbc[:, gw:gw + ns], xbc[:, gw + ns:]

    dt_c = _softplus(dtc_ref[0] + dbc_ref[0])
    dt_r = _softplus(dtr_ref[...] + dbr_ref[...])
    a_c = -jnp.exp(alc_ref[0])
    a_r = -jnp.exp(alr_ref[...])
    row = lax.broadcasted_iota(jnp.int32, (ch, ch), 0)
    col = lax.broadcasted_iota(jnp.int32, (ch, ch), 1)
    causal = row >= col
    acs_c = _dot_split3(dt_c * a_c, causal.astype(BF16), split_lhs=False)
    acs_r = _dot_split3(dt_r * a_r, (row <= col).astype(BF16), split_lhs=True)

    hrow = lax.broadcasted_iota(jnp.int32, (SSM_HEADS_PER_GROUP, gw), 0)
    hcol = lax.broadcasted_iota(jnp.int32, (SSM_HEADS_PER_GROUP, gw), 1)
    expand = (hcol // hd == hrow).astype(BF16)
    dt_x = _dot_split3(dt_c, expand, split_lhs=True)
    acs_x = _dot_split3(acs_c, expand, split_lhs=True)

    xdt = xc * dt_x
    xdt_b = xdt.astype(BF16)
    bm_b, cm_b = bm.astype(BF16), cm.astype(BF16)
    cb = lax.dot_general(cm_b, bm_b, (((1,), (1,)), ((), ())), preferred_element_type=F32)
    prev = state_ref[...]
    y_off = jnp.dot(cm_b, prev.astype(BF16), preferred_element_type=F32) * jnp.exp(acs_x)

    lane = lax.broadcasted_iota(jnp.int32, (ch, 2 * hd), 1)
    pairs = []
    for pr in range(SSM_HEADS_PER_GROUP // 2):
        xp = xdt_b[:, pr * 2 * hd:(pr + 1) * 2 * hd]
        halves = []
        for q in range(2):
            r = 2 * pr + q
            diff = acs_c[:, r:r + 1] - acs_r[r:r + 1, :]
            seg = jnp.where(causal, jnp.exp(jnp.where(causal, diff, 0.0)), 0.0)
            halves.append(jnp.dot((cb * seg).astype(BF16), xp, preferred_element_type=F32))
        pairs.append(jnp.where(lane < hd, halves[0], halves[1]))
    y = jnp.concatenate(pairs, axis=1) + y_off + d_ref[...] * xc

    acs_last = acs_x[ch - 1:ch, :]
    contrib = lax.dot_general(bm_b, (xdt * jnp.exp(acs_last - acs_x)).astype(BF16),
                              (((0,), (0,)), ((), ())), preferred_element_type=F32)
    state_ref[...] = prev * jnp.exp(acs_last) + contrib

    u = y * _silu(z_ref[...])
    u = u * lax.rsqrt(jnp.mean(u * u, -1, keepdims=True) + SSM_NORM_EPS)
    o_ref[...] = (u * nw_ref[...]).astype(o_ref.dtype)


def _ssd_mixer(u_z, u_xbc, u_dt, conv_w, conv_b, dt_bias, a_log, d_skip, norm_w, batch, seq):
    t = batch * seq
    nc = seq // SSM_CHUNK
    g, r, gw, ns = SSM_GROUPS, SSM_HEADS_PER_GROUP, SSM_GROUP_WIDTH, SSM_STATE
    xblocks = D_MODEL // ns
    conv_wt = conv_w.T
    conv_b2 = conv_b.reshape(1, SSM_CONV_CH)
    dt_col = u_dt.reshape(t, g, r).transpose(1, 0, 2)
    dt_row = u_dt.T
    rows = lambda b, gi, c: (b * nc + c, gi)
    in_specs = [
        pl.BlockSpec((SSM_CHUNK, gw), rows),
        pl.BlockSpec((SSM_CHUNK, gw), rows),
        pl.BlockSpec((SSM_CHUNK, ns), lambda b, gi, c: (b * nc + c, xblocks + gi)),
        pl.BlockSpec((SSM_CHUNK, ns), lambda b, gi, c: (b * nc + c, xblocks + g + gi)),
        pl.BlockSpec((SSM_CONV, gw), lambda b, gi, c: (0, gi)),
        pl.BlockSpec((SSM_CONV, ns), lambda b, gi, c: (0, xblocks + gi)),
        pl.BlockSpec((SSM_CONV, ns), lambda b, gi, c: (0, xblocks + g + gi)),
        pl.BlockSpec((1, gw), lambda b, gi, c: (0, gi)),
        pl.BlockSpec((1, ns), lambda b, gi, c: (0, xblocks + gi)),
        pl.BlockSpec((1, ns), lambda b, gi, c: (0, xblocks + g + gi)),
        pl.BlockSpec((1, SSM_CHUNK, r), lambda b, gi, c: (gi, b * nc + c, 0)),
        pl.BlockSpec((r, SSM_CHUNK), lambda b, gi, c: (gi, b * nc + c)),
        pl.BlockSpec((1, 1, r), lambda b, gi, c: (gi, 0, 0)),
        pl.BlockSpec((r, 1), lambda b, gi, c: (gi, 0)),
        pl.BlockSpec((1, 1, r), lambda b, gi, c: (gi, 0, 0)),
        pl.BlockSpec((r, 1), lambda b, gi, c: (gi, 0)),
        pl.BlockSpec((1, gw), lambda b, gi, c: (0, gi)),
        pl.BlockSpec((1, gw), lambda b, gi, c: (0, gi)),
    ]
    return pl.pallas_call(
        _ssd_kernel,
        grid=(batch, g, nc),
        in_specs=in_specs,
        out_specs=pl.BlockSpec((SSM_CHUNK, gw), rows),
        out_shape=jax.ShapeDtypeStruct((t, D_MODEL), BF16),
        scratch_shapes=[pltpu.VMEM((ns, gw), F32), pltpu.VMEM((CONV_HALO + SSM_CHUNK, SSD_CH), F32)],
        compiler_params=_cparams(("parallel", "parallel", "arbitrary")),
        name="ssd_mixer",
    )(u_z, u_xbc, u_xbc, u_xbc, conv_wt, conv_wt, conv_wt, conv_b2, conv_b2, conv_b2,
      dt_col, dt_row, dt_bias.reshape(g, 1, r), dt_bias.reshape(SSM_HEADS, 1),
      a_log.reshape(g, 1, r), a_log.reshape(SSM_HEADS, 1),
      jnp.repeat(d_skip, SSM_HEAD_DIM).reshape(1, D_MODEL), norm_w.reshape(1, D_MODEL))


def _rotary_tables(seq):
    half = ROPE_DIM // 2
    inv_freq = ROPE_THETA ** (-jnp.arange(half, dtype=F32) / half)
    ang = jnp.arange(seq, dtype=F32)[:, None] * inv_freq[None, :]
    ones = jnp.ones((seq, ATT_HEAD_DIM - ROPE_DIM), F32)
    cos = jnp.concatenate([jnp.cos(ang), jnp.cos(ang), ones], -1)
    sin = jnp.concatenate([jnp.sin(ang), jnp.sin(ang), 0.0 * ones], -1)
    return jnp.tile(cos, (1, ATT_REP)), jnp.tile(sin, (1, ATT_REP))


def _rotate_half_matrix():
    width = ATT_REP * ATT_HEAD_DIM
    half = ROPE_DIM // 2
    p = np.zeros((width, width), np.float32)
    for j in range(width):
        if j % ATT_HEAD_DIM < half:
            p[j + half, j] = -1.0
        elif j % ATT_HEAD_DIM < ROPE_DIM:
            p[j - half, j] = 1.0
    return jnp.asarray(p, BF16)


def _rotate(v, cos, sin, perm):
    hi = v.astype(BF16)
    lo = (v - hi.astype(F32)).astype(BF16)
    partner = jnp.dot(hi, perm, preferred_element_type=F32) + jnp.dot(lo, perm, preferred_element_type=F32)
    return v * cos + partner * sin


def _swa_kernel(sink_ref, q_ref, kc_ref, kp_ref, vc_ref, vp_ref, cosc_ref, sinc_ref, cosp_ref, sinp_ref,
                perm_ref, o_ref):
    w, hd = ATT_WINDOW, ATT_HEAD_DIM
    h = pl.program_id(1)
    n = pl.program_id(2)
    perm = perm_ref[...]
    perm_k = perm[:hd, :hd]
    q = _rotate(q_ref[...], cosc_ref[...], sinc_ref[...], perm)
    k_cur = _rotate(kc_ref[0, 0], cosc_ref[:, :hd], sinc_ref[:, :hd], perm_k)
    k_prev = _rotate(kp_ref[0, 0], cosp_ref[:, :hd], sinp_ref[:, :hd], perm_k)
    k_all = jnp.concatenate([k_prev, k_cur], axis=0).astype(BF16)
    v_all = jnp.concatenate([vp_ref[0, 0], vc_ref[0, 0]], axis=0).astype(BF16)
    qi = lax.broadcasted_iota(jnp.int32, (w, 2 * w), 0) + w
    ki = lax.broadcasted_iota(jnp.int32, (w, 2 * w), 1)
    rel = qi - ki
    first = jnp.where(n > 0, 0, w)
    bias = jnp.where((rel >= 0) & (rel < ATT_WINDOW) & (ki >= first), 0.0, -jnp.inf)
    heads = range(ATT_REP)
    qs = [(q[:, r * hd:(r + 1) * hd] * (hd ** -0.5)).astype(BF16) for r in heads]
    sinks = [sink_ref[h * ATT_REP + r] for r in heads]
    ss = [lax.dot_general(qr, k_all, (((1,), (1,)), ((), ())), preferred_element_type=F32) + bias for qr in qs]
    ms = [jnp.maximum(jnp.max(s, -1, keepdims=True), sink) for s, sink in zip(ss, sinks)]
    es = [jnp.exp(s - m) for s, m in zip(ss, ms)]
    dens = [jnp.sum(e, -1, keepdims=True) + jnp.exp(sink - m) for e, sink, m in zip(es, sinks, ms)]
    outs = [jnp.dot(e.astype(BF16), v_all, preferred_element_type=F32) / den for e, den in zip(es, dens)]
    o_ref[...] = jnp.concatenate(outs, axis=1).astype(o_ref.dtype)


def _swa_attention(qkv, sinks, batch, seq):
    t = batch * seq
    nb = seq // ATT_WINDOW
    hd, w = ATT_HEAD_DIM, ATT_WINDOW
    qw = ATT_REP * hd
    k4 = qkv[:, Q_COLS:Q_COLS + KV_COLS].reshape(batch, seq, ATT_KV_HEADS, hd).transpose(0, 2, 1, 3)
    v4 = qkv[:, Q_COLS + KV_COLS:].reshape(batch, seq, ATT_KV_HEADS, hd).transpose(0, 2, 1, 3)
    cos, sin = _rotary_tables(seq)
    cur = pl.BlockSpec((1, 1, w, hd), lambda b, h, n: (b, h, n, 0))
    prev = pl.BlockSpec((1, 1, w, hd), lambda b, h, n: (b, h, jnp.maximum(n - 1, 0), 0))
    tab_cur = pl.BlockSpec((w, qw), lambda b, h, n: (n, 0))
    tab_prev = pl.BlockSpec((w, qw), lambda b, h, n: (jnp.maximum(n - 1, 0), 0))
    return pl.pallas_call(
        _swa_kernel,
        grid=(batch, ATT_KV_HEADS, nb),
        in_specs=[
            pl.BlockSpec(memory_space=pltpu.SMEM),
            pl.BlockSpec((w, qw), lambda b, h, n: (b * nb + n, h)),
            cur, prev, cur, prev, tab_cur, tab_cur, tab_prev, tab_prev,
            pl.BlockSpec((qw, qw), lambda b, h, n: (0, 0)),
        ],
        out_specs=pl.BlockSpec((w, qw), lambda b, h, n: (b * nb + n, h)),
        out_shape=jax.ShapeDtypeStruct((t, Q_COLS), BF16),
        compiler_params=_cparams(("parallel", "parallel", "arbitrary")),
        name="swa_attention",
    )(sinks, qkv, k4, k4, v4, v4, cos, sin, cos, sin, _rotate_half_matrix())


def _ssd_swa_mixer(x2, xb, batch, seq, w_in, b_qkv, conv_w, conv_b, dt_bias, a_log, d_skip, norm_w, sinks,
                   w_out, b_out):
    o1 = D_MODEL
    o2 = o1 + SSM_CONV_CH
    o3 = o2 + SSM_HEADS
    w_in_b = w_in.astype(BF16)
    u_z = _matmul(xb, w_in_b, n=o1)
    u_xbc = _matmul(xb, w_in_b, col_start=o1, n=SSM_CONV_CH)
    u_dt = _matmul(x2, w_in[:, o2:o3], precision=HIGHEST)
    qkv = _matmul(xb, w_in_b[:, o3:], b_qkv)
    y_ssm = _ssd_mixer(u_z, u_xbc, u_dt, conv_w, conv_b, dt_bias, a_log, d_skip, norm_w, batch, seq)
    y_att = _swa_attention(qkv, sinks, batch, seq)
    return _matmul((y_ssm, y_att), w_out.astype(BF16), b_out)


MOE_TILE = 512
MOE_GATHER_TILE = 256
ROUTER_TILE = 256


def _router_kernel(h_ref, wt_ref, b_ref, ids_ref, wts_ref):
    logits = lax.dot_general(wt_ref[...], h_ref[...], (((1,), (1,)), ((), ())),
                             precision=HIGHEST, preferred_element_type=F32) + b_ref[...]
    e = jnp.exp(logits - jnp.max(logits, 0, keepdims=True))
    probs = e / jnp.sum(e, 0, keepdims=True)
    p = [probs[i:i + 1, :] for i in range(N_EXPERTS)]

    keep, score = [], []
    for g in range(N_EXPERT_GROUPS):
        members = range(g * EXPERTS_PER_GROUP, (g + 1) * EXPERTS_PER_GROUP)
        s = jnp.zeros_like(p[0])
        for i in members:
            rank = jnp.zeros_like(p[0])
            for j in members:
                if j != i:
                    beats = (p[j] > p[i]) | ((p[j] == p[i]) & (j < i)) if j < i else (p[j] > p[i])
                    rank = rank + beats.astype(F32)
            keep.append(rank < TOP_K)
            s = s + jnp.where(keep[i], p[i], 0.0)
        score.append(s)
    chosen = []
    for g in range(N_EXPERT_GROUPS):
        c = jnp.ones_like(p[0]) > 0
        for g2 in range(N_EXPERT_GROUPS):
            if g2 < g:
                c = c & (score[g] > score[g2])
            elif g2 > g:
                c = c & (score[g] >= score[g2])
        chosen.append(c)
    denom = jnp.zeros_like(p[0])
    for g in range(N_EXPERT_GROUPS):
        denom = denom + jnp.where(chosen[g], score[g], 0.0)
    count = jnp.zeros_like(p[0])
    id0 = jnp.zeros_like(p[0])
    id1 = jnp.zeros_like(p[0])
    w0 = jnp.zeros_like(p[0])
    w1 = jnp.zeros_like(p[0])
    for i in range(N_EXPERTS):
        sel = keep[i] & chosen[i // EXPERTS_PER_GROUP]
        gate = p[i] / denom
        first = sel & (count == 0.0)
        second = sel & (count == 1.0)
        id0 = jnp.where(first, float(i), id0)
        w0 = jnp.where(first, gate, w0)
        id1 = jnp.where(second, float(i), id1)
        w1 = jnp.where(second, gate, w1)
        count = count + sel.astype(F32)
    ids_ref[0:1, :] = id0.astype(jnp.int32)
    ids_ref[1:2, :] = id1.astype(jnp.int32)
    wts_ref[0:1, :] = w0
    wts_ref[1:2, :] = w1


def _router(h, router_w, router_b):
    t, d = h.shape
    tm = _tile(t, ROUTER_TILE)
    return pl.pallas_call(
        _router_kernel,
        grid=(t // tm,),
        in_specs=[
            pl.BlockSpec((tm, d), lambda i: (i, 0)),
            pl.BlockSpec((N_EXPERTS, d), lambda i: (0, 0)),
            pl.BlockSpec((N_EXPERTS, 1), lambda i: (0, 0)),
        ],
        out_specs=[pl.BlockSpec((TOP_K, tm), lambda i: (0, i)), pl.BlockSpec((TOP_K, tm), lambda i: (0, i))],
        out_shape=[jax.ShapeDtypeStruct((TOP_K, t), jnp.int32), jax.ShapeDtypeStruct((TOP_K, t), F32)],
        compiler_params=_cparams(("parallel",)),
        name="moe_router",
    )(h, router_w.T, router_b.reshape(N_EXPERTS, 1))


def _route_metadata(ids, wts, tm):
    t = ids.shape[1]
    na = TOP_K * t
    eid = ids.T.reshape(na)
    onehot = (eid[:, None] == jnp.arange(N_EXPERTS, dtype=jnp.int32)[None, :]).astype(jnp.int32)
    csum = jnp.cumsum(onehot, axis=0)
    counts = csum[-1]
    rank = jnp.sum(csum * onehot, axis=1) - 1
    padded = (counts + tm - 1) // tm * tm
    pend = jnp.cumsum(padded)
    dest = (pend - padded)[eid] + rank
    n_tiles = na // tm + N_EXPERTS
    n_rows = n_tiles * tm
    row_token = jnp.zeros((n_rows,), jnp.int32).at[dest].set(jnp.arange(na, dtype=jnp.int32) // TOP_K)
    row_weight = jnp.zeros((n_rows,), F32).at[dest].set(wts.T.reshape(na))
    tile_start = jnp.arange(n_tiles, dtype=jnp.int32) * tm
    tile_valid = (tile_start < pend[-1]).astype(jnp.int32)
    last_expert = jnp.max(jnp.where(counts > 0, jnp.arange(N_EXPERTS, dtype=jnp.int32), 0))
    tile_expert = jnp.minimum(jnp.searchsorted(pend, tile_start, side="right").astype(jnp.int32), last_expert)
    return row_token, row_weight, dest, tile_expert, tile_valid


def _gather_rows(src_hbm, tiles_ref, idx_ref, n, sem):
    def row_copy(tile, sub, src_row):
        return pltpu.make_async_copy(src_hbm.at[src_row], tiles_ref.at[tile, :, sub, :], sem)

    def start(tile, carry):
        for sub in range(SUBLANES):
            row_copy(tile, sub, idx_ref[0, 0, tile * SUBLANES + sub]).start(priority=sub % 2)
        return carry

    def wait(tile, carry):
        for sub in range(SUBLANES):
            row_copy(tile, sub, 0).wait()
        return carry

    lax.fori_loop(0, n // SUBLANES, start, 0)
    lax.fori_loop(0, n // SUBLANES, wait, 0)


def _moe_gather_kernel(tv_ref, rt_ref, h_hbm, o_ref, xg_ref, sem, *, tm):
    valid = tv_ref[pl.program_id(0)] == 1

    @pl.when(valid)
    def _():
        _gather_rows(h_hbm, xg_ref, rt_ref, tm, sem)
        o_ref[...] = _tiles_to_rows(xg_ref, 0, tm // SUBLANES).astype(o_ref.dtype)

    @pl.when(jnp.logical_not(valid))
    def _():
        o_ref[...] = jnp.zeros_like(o_ref)


def _expert_changed(te_ref, i):
    return (i == 0) | (te_ref[i] != te_ref[jnp.maximum(i - 1, 0)])


def _moe_up_kernel(te_ref, tv_ref, x_ref, wg_ref, wu_ref, o_ref, wgb_ref, wub_ref):
    i = pl.program_id(1)
    valid = tv_ref[i] == 1

    @pl.when(_expert_changed(te_ref, i))
    def _():
        wgb_ref[...] = wg_ref[0].astype(BF16)
        wub_ref[...] = wu_ref[0].astype(BF16)

    @pl.when(valid)
    def _():
        xt = x_ref[...]
        gate = jnp.dot(xt, wgb_ref[...], preferred_element_type=F32)
        up = jnp.dot(xt, wub_ref[...], preferred_element_type=F32)
        o_ref[...] = (_silu(gate) * up).astype(o_ref.dtype)

    @pl.when(jnp.logical_not(valid))
    def _():
        o_ref[...] = jnp.zeros_like(o_ref)


def _moe_down_kernel(te_ref, tv_ref, he_ref, wd_ref, rw_ref, o_ref, wdb_ref):
    i = pl.program_id(1)
    valid = tv_ref[i] == 1

    @pl.when(_expert_changed(te_ref, i))
    def _():
        wdb_ref[...] = wd_ref[0].astype(BF16)

    @pl.when(valid)
    def _():
        _to_token_major(o_ref, jnp.dot(he_ref[...], wdb_ref[...], preferred_element_type=F32) * rw_ref[...])

    @pl.when(jnp.logical_not(valid))
    def _():
        o_ref[...] = jnp.zeros_like(o_ref)


def _moe_combine_ln_kernel(pos_ref, h_ref, y_hbm, g_ref, b_ref, o_ref, ob_ref, yg_ref, sem, *, tm):
    _gather_rows(y_hbm, yg_ref, pos_ref, TOP_K * tm, sem)
    per = tm // SUBLANES
    ffn = _tiles_to_rows(yg_ref, 0, per) + _tiles_to_rows(yg_ref, per, TOP_K * per)
    r = _layer_norm_rows(ALPHA * h_ref[...] + ffn, g_ref[...], b_ref[...])
    o_ref[...] = r
    ob_ref[...] = r.astype(BF16)


def _moe_block(h, h3, router_w, router_b, w_gate, w_up, w_down, layer, ln_g, ln_b, *, tm=MOE_TILE, tn_up=512,
               tn_down=1024, tm_out=128):
    t, d = h.shape
    slabs = d // LANES
    ids, wts = _router(h, router_w, router_b)
    row_token, row_weight, dest, tile_expert, tile_valid = _route_metadata(ids, wts, tm)
    n_rows = row_token.shape[0]
    n_tiles = n_rows // tm
    de = w_gate.shape[-1]
    tn_up, tn_down = _tile(de, tn_up), _tile(d, tn_down)

    tg = _tile(tm, MOE_GATHER_TILE)
    per = tm // tg
    x_rows = pl.pallas_call(
        functools.partial(_moe_gather_kernel, tm=tg),
        grid_spec=pltpu.PrefetchScalarGridSpec(
            num_scalar_prefetch=1,
            grid=(n_rows // tg,),
            in_specs=[
                pl.BlockSpec((1, 1, tg), lambda i, tv: (i, 0, 0), memory_space=pltpu.SMEM),
                pl.BlockSpec(memory_space=pl.ANY),
            ],
            out_specs=pl.BlockSpec((tg, d), lambda i, tv: (i, 0)),
            scratch_shapes=[pltpu.VMEM((tg // SUBLANES, slabs, SUBLANES, LANES), F32), pltpu.SemaphoreType.DMA(())],
        ),
        out_shape=jax.ShapeDtypeStruct((n_rows, d), BF16),
        compiler_params=_cparams(("arbitrary",)),
        name="moe_gather",
    )(jnp.repeat(tile_valid, per), row_token.reshape(n_rows // tg, 1, tg), h3)

    he = pl.pallas_call(
        _moe_up_kernel,
        grid_spec=pltpu.PrefetchScalarGridSpec(
            num_scalar_prefetch=2,
            grid=(de // tn_up, n_tiles),
            in_specs=[
                pl.BlockSpec((tm, d), lambda j, i, te, tv: (i, 0)),
                pl.BlockSpec((None, 1, d, tn_up), lambda j, i, te, tv: (layer, te[i], 0, j)),
                pl.BlockSpec((None, 1, d, tn_up), lambda j, i, te, tv: (layer, te[i], 0, j)),
            ],
            out_specs=pl.BlockSpec((tm, tn_up), lambda j, i, te, tv: (i, j)),
            scratch_shapes=[pltpu.VMEM((d, tn_up), BF16), pltpu.VMEM((d, tn_up), BF16)],
        ),
        out_shape=jax.ShapeDtypeStruct((n_rows, de), BF16),
        compiler_params=_cparams(("arbitrary", "arbitrary")),
        name="moe_gate_up",
    )(tile_expert, tile_valid, x_rows, w_gate, w_up)

    y_rows = pl.pallas_call(
        _moe_down_kernel,
        grid_spec=pltpu.PrefetchScalarGridSpec(
            num_scalar_prefetch=2,
            grid=(d // tn_down, n_tiles),
            in_specs=[
                pl.BlockSpec((tm, de), lambda j, i, te, tv: (i, 0)),
                pl.BlockSpec((None, 1, de, tn_down), lambda j, i, te, tv: (layer, te[i], 0, j)),
                pl.BlockSpec((tm, 1), lambda j, i, te, tv: (i, 0)),
            ],
            out_specs=pl.BlockSpec((tm, tn_down // LANES, LANES), lambda j, i, te, tv: (i, j, 0)),
            scratch_shapes=[pltpu.VMEM((de, tn_down), BF16)],
        ),
        out_shape=jax.ShapeDtypeStruct((n_rows, slabs, LANES), F32),
        compiler_params=_cparams(("arbitrary", "arbitrary")),
        name="moe_down",
    )(tile_expert, tile_valid, he, w_down, row_weight.reshape(n_rows, 1))

    tm_out = _tile(t, tm_out)
    n_out = t // tm_out
    pos = dest.reshape(n_out, tm_out, TOP_K).transpose(0, 2, 1).reshape(n_out, 1, TOP_K * tm_out)
    row = pl.BlockSpec((tm_out, d), lambda i: (i, 0))
    vec = pl.BlockSpec((1, d), lambda i: (0, 0))
    return pl.pallas_call(
        functools.partial(_moe_combine_ln_kernel, tm=tm_out),
        grid=(n_out,),
        in_specs=[
            pl.BlockSpec((1, 1, TOP_K * tm_out), lambda i: (i, 0, 0), memory_space=pltpu.SMEM),
            row,
            pl.BlockSpec(memory_space=pl.ANY),
            vec, vec,
        ],
        out_specs=[row, row],
        out_shape=[jax.ShapeDtypeStruct((t, d), F32), jax.ShapeDtypeStruct((t, d), BF16)],
        scratch_shapes=[pltpu.VMEM((TOP_K * tm_out // SUBLANES, slabs, SUBLANES, LANES), F32),
                        pltpu.SemaphoreType.DMA(())],
        compiler_params=_cparams(("arbitrary",)),
        name="moe_combine_ln",
    )(pos, h, y_rows, ln_g.reshape(1, d), ln_b.reshape(1, d))


def _shift_mix_kernel(x_ref, xp_ref, mix_ref, o_ref, pad_ref, *, tm, seq):
    i = pl.program_id(0)
    starts_sequence = (i * tm) % seq == 0
    pad_ref[0:SUBLANES, :] = jnp.where(starts_sequence, 0.0, xp_ref[...])
    pad_ref[SUBLANES:SUBLANES + tm, :] = x_ref[...]
    xv = x_ref[...]
    xx = pad_ref[SUBLANES - 1:SUBLANES - 1 + tm, :] - xv
    for m in range(mix_ref.shape[0]):
        o_ref[m] = (xv + xx * mix_ref[m:m + 1, :]).astype(o_ref.dtype)


def _shift_mix(x, mix, seq, *, tm=128):
    t, d = x.shape
    nm = mix.shape[0]
    tm = _tile(seq, tm)
    per = tm // SUBLANES
    return pl.pallas_call(
        functools.partial(_shift_mix_kernel, tm=tm, seq=seq),
        grid=(t // tm,),
        in_specs=[
            pl.BlockSpec((tm, d), lambda i: (i, 0)),
            pl.BlockSpec((SUBLANES, d), lambda i: (jnp.maximum(i * per - 1, 0), 0)),
            pl.BlockSpec((nm, d), lambda i: (0, 0)),
        ],
        out_specs=pl.BlockSpec((nm, tm, d), lambda i: (0, i, 0)),
        out_shape=jax.ShapeDtypeStruct((nm, t, d), BF16),
        scratch_shapes=[pltpu.VMEM((SUBLANES + tm, d), F32)],
        compiler_params=_cparams(("parallel",)),
        name="rwkv_shift_mix",
    )(x, x, mix)


WKV_CHUNK = 64
WKV_HEADS = 4
WKV_WIDTH = WKV_HEADS * RWKV_HEAD
WKV_BLOCK = 512
NEUMANN_STEPS = 5


def _wkv_kernel(r_ref, k_ref, v_ref, wl_ref, ar_ref, g_ref, kk_ref, ka_ref, rk_ref, lnw_ref, lnb_ref,
                o_ref, state_ref, *, tb):
    lc, hw, wd = WKV_CHUNK, RWKV_HEAD, WKV_WIDTH
    nch = tb // lc

    @pl.when(pl.program_id(2) == 0)
    def _():
        state_ref[...] = jnp.zeros_like(state_ref)

    wrow = lax.broadcasted_iota(jnp.int32, (wd, wd), 0)
    wcol = lax.broadcasted_iota(jnp.int32, (wd, wd), 1)
    same_head = wrow // hw == wcol // hw
    ones_bd = same_head.astype(BF16)

    def block_diag(m):
        return jnp.where(same_head, jnp.concatenate([m] * WKV_HEADS, axis=0), 0.0).astype(BF16)

    def head_sum(m):
        hi = m.astype(BF16)
        lo = (m - hi.astype(F32)).astype(BF16)
        return jnp.dot(hi, ones_bd, preferred_element_type=F32) + jnp.dot(lo, ones_bd, preferred_element_type=F32)

    def mm(a, b):
        return jnp.dot(a.astype(BF16), b, preferred_element_type=F32)

    r = r_ref[...]
    k = k_ref[...]
    v = v_ref[...]
    w_log = -_softplus(-wl_ref[...]) - 0.5
    lw = -jnp.exp(w_log)
    a_sig = jax.nn.sigmoid(ar_ref[...])
    kx = k * kk_ref[...]
    kk = kx / jnp.maximum(jnp.sqrt(head_sum(kx * kx)), 1e-12)
    k2 = k * (1.0 + (a_sig - 1.0) * ka_ref[...])
    a_s = -kk
    b_s = kk * a_sig

    trow = lax.broadcasted_iota(jnp.int32, (tb, tb), 0)
    tcol = lax.broadcasted_iota(jnp.int32, (tb, tb), 1)
    same_chunk = trow // lc == tcol // lc
    cum = _dot_split3(lw, (same_chunk & (trow >= tcol)).astype(BF16), split_lhs=False)
    tot = jnp.concatenate([jnp.broadcast_to(cum[(c + 1) * lc - 1:(c + 1) * lc, :], (lc, wd)) for c in range(nch)],
                          axis=0)
    grow = jnp.exp(-cum)
    rt = r * jnp.exp(cum)
    at = a_s * jnp.exp(cum - lw)
    bt = b_s * grow
    kt = k2 * grow
    rest = jnp.exp(tot - cum)
    bh = b_s * rest
    kh = k2 * rest
    p_end = jnp.exp(tot)

    t_idx = lax.broadcasted_iota(jnp.int32, (lc, wd), 0)
    s_idx = lax.broadcasted_iota(jnp.int32, (lc, wd), 1) % hw
    strict = t_idx > s_idx
    incl = t_idx >= s_idx
    eye = (t_idx == s_idx).astype(F32)
    nt = (((1,), (1,)), ((), ()))

    tn = (((0,), (0,)), ((), ()))
    chunks = [slice(c * lc, (c + 1) * lc) for c in range(nch)]

    a_ab, a_rb, a_ak, a_rk = [], [], [], []
    for sl in chunks:
        ar = jnp.concatenate([at[sl], rt[sl]], axis=0).astype(BF16)
        xb = lax.dot_general(ar, block_diag(bt[sl]), nt, preferred_element_type=F32)
        xk = lax.dot_general(ar, block_diag(kt[sl]), nt, preferred_element_type=F32)
        a_ab.append(jnp.where(strict, xb[:lc], 0.0))
        a_rb.append(jnp.where(incl, xb[lc:], 0.0))
        a_ak.append(jnp.where(strict, xk[:lc], 0.0))
        a_rk.append(jnp.where(incl, xk[lc:], 0.0))
    inv = [eye + a for a in a_ab]
    pw = [mm(a, block_diag(a)) for a in a_ab]
    for step in range(1, NEUMANN_STEPS + 1):
        last = step == NEUMANN_STEPS
        lhs = inv if last else [jnp.concatenate([t, p], axis=0) for t, p in zip(inv, pw)]
        prod = [mm(x, block_diag(p)) for x, p in zip(lhs, pw)]
        inv = [t + q[:lc] for t, q in zip(inv, prod)]
        if not last:
            pw = [q[lc:] for q in prod]
    v_bd = [block_diag(v[sl]) for sl in chunks]
    ta = [mm(t, block_diag(at[sl])) for t, sl in zip(inv, chunks)]
    av = [mm(jnp.concatenate([ak, rk], axis=0), vb) for ak, rk, vb in zip(a_ak, a_rk, v_bd)]
    y0 = [x[lc:] for x in av]
    u0 = [mm(t, block_diag(x[:lc])) for t, x in zip(inv, av)]
    gain, add = [], []
    for c, sl in enumerate(chunks):
        bh_b = bh[sl].astype(BF16)
        gain.append(jnp.where(same_head, lax.dot_general(ta[c].astype(BF16), bh_b, tn, preferred_element_type=F32),
                              0.0).astype(BF16))
        uv = jnp.concatenate([u0[c], v[sl]], axis=0).astype(BF16)
        bk = jnp.concatenate([bh_b, kh[sl].astype(BF16)], axis=0)
        add.append(jnp.where(same_head, lax.dot_general(uv, bk, tn, preferred_element_type=F32), 0.0))

    states = [state_ref[...]]
    for c in range(nch):
        s0 = states[-1]
        states.append(s0 * p_end[c * lc:c * lc + 1, :] + mm(s0, gain[c]) + add[c])
    state_ref[...] = states[nch]

    ys = []
    for c, sl in enumerate(chunks):
        tr = jnp.concatenate([ta[c], rt[sl]], axis=0).astype(BF16)
        xs = lax.dot_general(tr, states[c].astype(BF16), nt, preferred_element_type=F32)
        u = xs[:lc] + u0[c]
        ys.append(xs[lc:] + mm(a_rb[c], block_diag(u)) + y0[c])

    y = jnp.concatenate(ys, axis=0)
    mu = head_sum(y) / hw
    dev = y - mu
    var = head_sum(dev * dev) / hw
    yn = dev * lax.rsqrt(var + RWKV_GN_EPS) * lnw_ref[...] + lnb_ref[...]
    bonus = head_sum(r * k2 * rk_ref[...]) * v
    o_ref[...] = ((yn + bonus) * g_ref[...]).astype(o_ref.dtype)


def _wkv(r, k, v, wl, araw, g, k_k, k_a, r_k, ln_w, ln_b, batch, seq):
    t, d = r.shape
    tb = _tile(seq, WKV_BLOCK)
    nt = seq // tb
    wd = WKV_WIDTH
    rows = pl.BlockSpec((tb, wd), lambda b, h, n: (b * nt + n, h))
    vec = pl.BlockSpec((1, wd), lambda b, h, n: (0, h))
    return pl.pallas_call(
        functools.partial(_wkv_kernel, tb=tb),
        grid=(batch, d // wd, nt),
        in_specs=[rows] * 6 + [vec] * 5,
        out_specs=rows,
        out_shape=jax.ShapeDtypeStruct((t, d), BF16),
        scratch_shapes=[pltpu.VMEM((wd, wd), F32)],
        compiler_params=_cparams(("parallel", "parallel", "arbitrary")),
        name="rwkv_wkv",
    )(r, k, v, wl, araw, g, k_k.reshape(1, d), k_a.reshape(1, d), r_k.reshape(1, d),
      ln_w.reshape(1, d), ln_b.reshape(1, d))


def _rwkv7_time_mix(x2, batch, seq, mix, w_r, w_k, w_v, w_o, w0, w1, w2, a0, a1, a2, g1, g2, k_k, k_a, r_k,
                    ln_w, ln_b):
    xm = _shift_mix(x2, mix, seq)
    bf = lambda w: w.astype(BF16)
    r = _matmul(xm[0], bf(w_r))
    k = _matmul(xm[2], bf(w_k))
    v = _matmul(xm[3], bf(w_v))
    wl = _matmul(_matmul(xm[1], bf(w1), act="tanh", out_dtype=BF16), bf(w2), w0)
    araw = _matmul(_matmul(xm[4], bf(a1), out_dtype=BF16), bf(a2), a0)
    g = _matmul(_matmul(xm[5], bf(g1), act="sigmoid", out_dtype=BF16), bf(g2))
    yg = _wkv(r, k, v, wl, araw, g, k_k, k_a, r_k, ln_w, ln_b, batch, seq)
    return _matmul(yg, bf(w_o))


def kernel(x, ab_w_in, ab_b_qkv, ssm_conv_w, ssm_conv_b, ssm_dt_bias, ssm_a_log, ssm_d, ssm_norm_w, attn_sinks,
           ab_w_out, ab_b_out, rwkv_mix, rwkv_w_r, rwkv_w_k, rwkv_w_v, rwkv_w_o, rwkv_w0, rwkv_w1, rwkv_w2,
           rwkv_a0, rwkv_a1, rwkv_a2, rwkv_g1, rwkv_g2, rwkv_k_k, rwkv_k_a, rwkv_r_k, rwkv_ln_w, rwkv_ln_b,
           ln_mix_g, ln_mix_b, ln_ffn_g, ln_ffn_b, router_w, router_b, moe_w_gate, moe_w_up, moe_w_down):
    batch, seq, d = x.shape
    x2 = x.reshape(batch * seq, d)
    xb = x2.astype(BF16)
    for layer in range(DEPTH):
        i = layer // 2
        if layer % 2 == 0:
            mix = _ssd_swa_mixer(x2, xb, batch, seq, ab_w_in[i], ab_b_qkv[i], ssm_conv_w[i], ssm_conv_b[i],
                                 ssm_dt_bias[i], ssm_a_log[i], ssm_d[i], ssm_norm_w[i], attn_sinks[i],
                                 ab_w_out[i], ab_b_out[i])
        else:
            mix = _rwkv7_time_mix(x2, batch, seq, rwkv_mix[i], rwkv_w_r[i], rwkv_w_k[i], rwkv_w_v[i], rwkv_w_o[i],
                                  rwkv_w0[i], rwkv_w1[i], rwkv_w2[i], rwkv_a0[i], rwkv_a1[i], rwkv_a2[i],
                                  rwkv_g1[i], rwkv_g2[i], rwkv_k_k[i], rwkv_k_a[i], rwkv_r_k[i].reshape(-1),
                                  rwkv_ln_w[i], rwkv_ln_b[i])
        h, h3 = _add_layer_norm(x2, mix, ln_mix_g[layer], ln_mix_b[layer])
        x2, xb = _moe_block(h, h3, router_w, router_b, moe_w_gate, moe_w_up, moe_w_down, layer,
                            ln_ffn_g[layer], ln_ffn_b[layer])
    return x2.reshape(batch, seq, d)
```

```python
import functools
import math

import jax
import jax.numpy as jnp
import numpy as np
from jax import lax
from jax.experimental import pallas as pl
from jax.experimental.pallas import tpu as pltpu

F32 = jnp.float32
BF16 = jnp.bfloat16
HIGHEST = lax.Precision.HIGHEST

D_MODEL = 4096
DEPTH = 2
SSM_HEAD_DIM = 64
SSM_HEADS = 64
SSM_GROUPS = 8
SSM_HEADS_PER_GROUP = 8
SSM_STATE = 128
SSM_CONV = 4
SSM_CHUNK = 128
SSM_GROUP_WIDTH = SSM_HEADS_PER_GROUP * SSM_HEAD_DIM
SSM_CONV_CH = D_MODEL + 2 * SSM_GROUPS * SSM_STATE
SSM_NORM_EPS = 1e-5
ATT_HEADS = 64
ATT_KV_HEADS = 8
ATT_HEAD_DIM = 64
ATT_REP = 8
ATT_WINDOW = 128
ROPE_DIM = 16
ROPE_THETA = 500000.0
Q_COLS = 4096
KV_COLS = 512
RWKV_HEAD = 64
RWKV_GN_EPS = 64e-5
N_EXPERTS = 16
N_EXPERT_GROUPS = 4
EXPERTS_PER_GROUP = 4
TOP_K = 2
D_EXPERT = 1536
ALPHA = (2 * DEPTH) ** 0.25
LN_EPS = 1e-5

VMEM_LIMIT_BYTES = 56 * 1024 * 1024
LANES = 128
SUBLANES = 8


def _cparams(semantics):
    return pltpu.CompilerParams(dimension_semantics=semantics, vmem_limit_bytes=VMEM_LIMIT_BYTES)


def _tile(dim, pref):
    if dim <= pref:
        return dim
    t = pref
    while dim % t:
        t //= 2
    return t


def _silu(v):
    return v * jax.nn.sigmoid(v)


def _softplus(v):
    return jnp.maximum(v, 0.0) + jnp.log1p(jnp.exp(-jnp.abs(v)))


def _split3(v):
    hi = v.astype(BF16)
    rest = v - hi.astype(F32)
    mid = rest.astype(BF16)
    lo = (rest - mid.astype(F32)).astype(BF16)
    return hi, mid, lo


def _dot_split3(v, m01, *, split_lhs):
    if split_lhs:
        return sum(jnp.dot(t, m01, preferred_element_type=F32) for t in _split3(v))
    return sum(jnp.dot(m01, t, preferred_element_type=F32) for t in _split3(v))


def _mm_kernel(*refs, act, precision):
    a_refs, (w_ref, b_ref, o_ref) = refs[:-3], refs[-3:]
    r = b_ref[...]
    k0 = 0
    for a_ref in a_refs:
        k1 = k0 + a_ref.shape[1]
        r = r + jnp.dot(a_ref[...], w_ref[k0:k1, :], preferred_element_type=F32, precision=precision)
        k0 = k1
    if act == "tanh":
        r = jnp.tanh(r)
    elif act == "sigmoid":
        r = jax.nn.sigmoid(r)
    o_ref[...] = r.astype(o_ref.dtype)


MATMUL_VMEM_BUDGET = 44 * 1024 * 1024


def _matmul(a, w, bias=None, *, act=None, out_dtype=F32, precision=None, tm=1024, col_start=0, n=None):
    panels = a if isinstance(a, (tuple, list)) else (a,)
    m = panels[0].shape[0]
    kdim = sum(p.shape[1] for p in panels)
    n = w.shape[1] if n is None else n
    tm = _tile(m, tm)
    out_bytes = jnp.dtype(out_dtype).itemsize
    for tn in (512, 256, LANES):
        tn = _tile(n, tn)
        need = 2 * (tm * kdim * panels[0].dtype.itemsize + kdim * tn * w.dtype.itemsize + tm * tn * out_bytes)
        if need <= MATMUL_VMEM_BUDGET:
            break
    assert need <= MATMUL_VMEM_BUDGET and col_start % tn == 0 and w.shape[0] == kdim
    col0 = col_start // tn
    if bias is None:
        bias = jnp.zeros((n,), F32)
    bias = bias.reshape(1, n).astype(F32)
    return pl.pallas_call(
        functools.partial(_mm_kernel, act=act, precision=precision),
        grid=(m // tm, n // tn),
        in_specs=[pl.BlockSpec((tm, p.shape[1]), lambda i, j: (i, 0)) for p in panels] + [
            pl.BlockSpec((kdim, tn), lambda i, j: (0, col0 + j)),
            pl.BlockSpec((1, tn), lambda i, j: (0, j)),
        ],
        out_specs=pl.BlockSpec((tm, tn), lambda i, j: (i, j)),
        out_shape=jax.ShapeDtypeStruct((m, n), out_dtype),
        compiler_params=_cparams(("parallel", "arbitrary")),
        name="matmul",
    )(*panels, w, bias)


def _layer_norm_rows(v, g, b):
    mu = jnp.mean(v, -1, keepdims=True)
    var = jnp.mean(jnp.square(v - mu), -1, keepdims=True)
    return (v - mu) * lax.rsqrt(var + LN_EPS) * g + b


def _to_token_major(o3_ref, v):
    for s in range(v.shape[1] // LANES):
        o3_ref[:, s, :] = v[:, s * LANES:(s + 1) * LANES].astype(o3_ref.dtype)


def _tiles_to_rows(tiles_ref, t0, t1):
    return jnp.concatenate([tiles_ref[t0:t1, s].reshape((t1 - t0) * SUBLANES, LANES)
                            for s in range(tiles_ref.shape[1])], axis=1)


def _add_ln_kernel(x_ref, y_ref, g_ref, b_ref, o_ref, o3_ref):
    r = _layer_norm_rows(ALPHA * x_ref[...] + y_ref[...], g_ref[...], b_ref[...])
    o_ref[...] = r
    _to_token_major(o3_ref, r)


def _add_layer_norm(x, y, g, b, *, tm=128):
    t, d = x.shape
    tm = _tile(t, tm)
    row = pl.BlockSpec((tm, d), lambda i: (i, 0))
    row3 = pl.BlockSpec((tm, d // LANES, LANES), lambda i: (i, 0, 0))
    vec = pl.BlockSpec((1, d), lambda i: (0, 0))
    return pl.pallas_call(
        _add_ln_kernel,
        grid=(t // tm,),
        in_specs=[row, row, vec, vec],
        out_specs=[row, row3],
        out_shape=[jax.ShapeDtypeStruct((t, d), F32), jax.ShapeDtypeStruct((t, d // LANES, LANES), F32)],
        compiler_params=_cparams(("parallel",)),
        name="add_layer_norm",
    )(x, y, g.reshape(1, d), b.reshape(1, d))


CONV_HALO = SUBLANES
SSD_CH = SSM_GROUP_WIDTH + 2 * SSM_STATE


def _ssd_kernel(z_ref, x_ref, b_ref, c_ref, wx_ref, wb_ref, wc_ref, bx_ref, bb_ref, bc_ref,
                dtc_ref, dtr_ref, dbc_ref, dbr_ref, alc_ref, alr_ref, d_ref, nw_ref,
                o_ref, state_ref, pad_ref):
    ch, gw, ns, hd = SSM_CHUNK, SSM_GROUP_WIDTH, SSM_STATE, SSM_HEAD_DIM

    @pl.when(pl.program_id(2) == 0)
    def _():
        state_ref[...] = jnp.zeros_like(state_ref)
        pad_ref[0:CONV_HALO, :] = jnp.zeros((CONV_HALO, SSD_CH), F32)

    pad_ref[CONV_HALO:CONV_HALO + ch, 0:gw] = x_ref[...]
    pad_ref[CONV_HALO:CONV_HALO + ch, gw:gw + ns] = b_ref[...]
    pad_ref[CONV_HALO:CONV_HALO + ch, gw + ns:SSD_CH] = c_ref[...]
    w = jnp.concatenate([wx_ref[...], wb_ref[...], wc_ref[...]], axis=1)
    acc = jnp.concatenate([bx_ref[...], bb_ref[...], bc_ref[...]], axis=1)
    base = CONV_HALO - (SSM_CONV - 1)
    for k in range(SSM_CONV):
        acc = acc + w[k:k + 1, :] * pad_ref[base + k:base + k + ch, :]
    pad_ref[0:CONV_HALO, :] = pad_ref[ch:ch + CONV_HALO, :]
    xbc = _silu(acc)
    xc, bm, cm = xbc[:, :gw], xbc[:, gw:gw + ns], xbc[:, gw + ns:]

    dt_c = _softplus(dtc_ref[0] + dbc_ref[0])
    dt_r = _softplus(dtr_ref[...] + dbr_ref[...])
    a_c = -jnp.exp(alc_ref[0])
    a_r = -jnp.exp(alr_ref[...])
    row = lax.broadcasted_iota(jnp.int32, (ch, ch), 0)
    col = lax.broadcasted_iota(jnp.int32, (ch, ch), 1)
    causal = row >= col
    acs_c = _dot_split3(dt_c * a_c, causal.astype(BF16), split_lhs=False)
    acs_r = _dot_split3(dt_r * a_r, (row <= col).astype(BF16), split_lhs=True)

    hrow = lax.broadcasted_iota(jnp.int32, (SSM_HEADS_PER_GROUP, gw), 0)
    hcol = lax.broadcasted_iota(jnp.int32, (SSM_HEADS_PER_GROUP, gw), 1)
    expand = (hcol // hd == hrow).astype(BF16)
    dt_x = _dot_split3(dt_c, expand, split_lhs=True)
    acs_x = _dot_split3(acs_c, expand, split_lhs=True)

    xdt = xc * dt_x
    xdt_b = xdt.astype(BF16)
    bm_b, cm_b = bm.astype(BF16), cm.astype(BF16)
    cb = lax.dot_general(cm_b, bm_b, (((1,), (1,)), ((), ())), preferred_element_type=F32)
    prev = state_ref[...]
    y_off = jnp.dot(cm_b, prev.astype(BF16), preferred_element_type=F32) * jnp.exp(acs_x)

    lane = lax.broadcasted_iota(jnp.int32, (ch, 2 * hd), 1)
    pairs = []
    for pr in range(SSM_HEADS_PER_GROUP // 2):
        xp = xdt_b[:, pr * 2 * hd:(pr + 1) * 2 * hd]
        halves = []
        for q in range(2):
            r = 2 * pr + q
            diff = acs_c[:, r:r + 1] - acs_r[r:r + 1, :]
            seg = jnp.where(causal, jnp.exp(jnp.where(causal, diff, 0.0)), 0.0)
            halves.append(jnp.dot((cb * seg).astype(BF16), xp, preferred_element_type=F32))
        pairs.append(jnp.where(lane < hd, halves[0], halves[1]))
    y = jnp.concatenate(pairs, axis=1) + y_off + d_ref[...] * xc

    acs_last = acs_x[ch - 1:ch, :]
    contrib = lax.dot_general(bm_b, (xdt * jnp.exp(acs_last - acs_x)).astype(BF16),
                              (((0,), (0,)), ((), ())), preferred_element_type=F32)
    state_ref[...] = prev * jnp.exp(acs_last) + contrib

    u = y * _silu(z_ref[...])
    u = u * lax.rsqrt(jnp.mean(u * u, -1, keepdims=True) + SSM_NORM_EPS)
    o_ref[...] = (u * nw_ref[...]).astype(o_ref.dtype)


def _ssd_mixer(u_z, u_xbc, u_dt, conv_w, conv_b, dt_bias, a_log, d_skip, norm_w, batch, seq):
    t = batch * seq
    nc = seq // SSM_CHUNK
    g, r, gw, ns = SSM_GROUPS, SSM_HEADS_PER_GROUP, SSM_GROUP_WIDTH, SSM_STATE
    xblocks = D_MODEL // ns
    conv_wt = conv_w.T
    conv_b2 = conv_b.reshape(1, SSM_CONV_CH)
    dt_col = u_dt.reshape(t, g, r).transpose(1, 0, 2)
    dt_row = u_dt.T
    rows = lambda b, gi, c: (b * nc + c, gi)
    in_specs = [
        pl.BlockSpec((SSM_CHUNK, gw), rows),
        pl.BlockSpec((SSM_CHUNK, gw), rows),
        pl.BlockSpec((SSM_CHUNK, ns), lambda b, gi, c: (b * nc + c, xblocks + gi)),
        pl.BlockSpec((SSM_CHUNK, ns), lambda b, gi, c: (b * nc + c, xblocks + g + gi)),
        pl.BlockSpec((SSM_CONV, gw), lambda b, gi, c: (0, gi)),
        pl.BlockSpec((SSM_CONV, ns), lambda b, gi, c: (0, xblocks + gi)),
        pl.BlockSpec((SSM_CONV, ns), lambda b, gi, c: (0, xblocks + g + gi)),
        pl.BlockSpec((1, gw), lambda b, gi, c: (0, gi)),
        pl.BlockSpec((1, ns), lambda b, gi, c: (0, xblocks + gi)),
        pl.BlockSpec((1, ns), lambda b, gi, c: (0, xblocks + g + gi)),
        pl.BlockSpec((1, SSM_CHUNK, r), lambda b, gi, c: (gi, b * nc + c, 0)),
        pl.BlockSpec((r, SSM_CHUNK), lambda b, gi, c: (gi, b * nc + c)),
        pl.BlockSpec((1, 1, r), lambda b, gi, c: (gi, 0, 0)),
        pl.BlockSpec((r, 1), lambda b, gi, c: (gi, 0)),
        pl.BlockSpec((1, 1, r), lambda b, gi, c: (gi, 0, 0)),
        pl.BlockSpec((r, 1), lambda b, gi, c: (gi, 0)),
        pl.BlockSpec((1, gw), lambda b, gi, c: (0, gi)),
        pl.BlockSpec((1, gw), lambda b, gi, c: (0, gi)),
    ]
    return pl.pallas_call(
        _ssd_kernel,
        grid=(batch, g, nc),
        in_specs=in_specs,
        out_specs=pl.BlockSpec((SSM_CHUNK, gw), rows),
        out_shape=jax.ShapeDtypeStruct((t, D_MODEL), BF16),
        scratch_shapes=[pltpu.VMEM((ns, gw), F32), pltpu.VMEM((CONV_HALO + SSM_CHUNK, SSD_CH), F32)],
        compiler_params=_cparams(("parallel", "parallel", "arbitrary")),
        name="ssd_mixer",
    )(u_z, u_xbc, u_xbc, u_xbc, conv_wt, conv_wt, conv_wt, conv_b2, conv_b2, conv_b2,
      dt_col, dt_row, dt_bias.reshape(g, 1, r), dt_bias.reshape(SSM_HEADS, 1),
      a_log.reshape(g, 1, r), a_log.reshape(SSM_HEADS, 1),
      jnp.repeat(d_skip, SSM_HEAD_DIM).reshape(1, D_MODEL), norm_w.reshape(1, D_MODEL))


def _rotary_tables(seq):
    half = ROPE_DIM // 2
    inv_freq = ROPE_THETA ** (-jnp.arange(half, dtype=F32) / half)
    ang = jnp.arange(seq, dtype=F32)[:, None] * inv_freq[None, :]
    ones = jnp.ones((seq, ATT_HEAD_DIM - ROPE_DIM), F32)
    cos = jnp.concatenate([jnp.cos(ang), jnp.cos(ang), ones], -1)
    sin = jnp.concatenate([jnp.sin(ang), jnp.sin(ang), 0.0 * ones], -1)
    return jnp.tile(cos, (1, ATT_REP)), jnp.tile(sin, (1, ATT_REP))


def _rotate_half_matrix():
    width = ATT_REP * ATT_HEAD_DIM
    half = ROPE_DIM // 2
    p = np.zeros((width, width), np.float32)
    for j in range(width):
        if j % ATT_HEAD_DIM < half:
            p[j + half, j] = -1.0
        elif j % ATT_HEAD_DIM < ROPE_DIM:
            p[j - half, j] = 1.0
    return jnp.asarray(p, BF16)


def _rotate(v, cos, sin, perm):
    hi = v.astype(BF16)
    lo = (v - hi.astype(F32)).astype(BF16)
    partner = jnp.dot(hi, perm, preferred_element_type=F32) + jnp.dot(lo, perm, preferred_element_type=F32)
    return v * cos + partner * sin


def _swa_kernel(sink_ref, q_ref, kc_ref, kp_ref, vc_ref, vp_ref, cosc_ref, sinc_ref, cosp_ref, sinp_ref,
                perm_ref, o_ref):
    w, hd = ATT_WINDOW, ATT_HEAD_DIM
    h = pl.program_id(1)
    n = pl.program_id(2)
    perm = perm_ref[...]
    perm_k = perm[:hd, :hd]
    q = _rotate(q_ref[...], cosc_ref[...], sinc_ref[...], perm)
    k_cur = _rotate(kc_ref[0, 0], cosc_ref[:, :hd], sinc_ref[:, :hd], perm_k)
    k_prev = _rotate(kp_ref[0, 0], cosp_ref[:, :hd], sinp_ref[:, :hd], perm_k)
    k_all = jnp.concatenate([k_prev, k_cur], axis=0).astype(BF16)
    v_all = jnp.concatenate([vp_ref[0, 0], vc_ref[0, 0]], axis=0).astype(BF16)
    qi = lax.broadcasted_iota(jnp.int32, (w, 2 * w), 0) + w
    ki = lax.broadcasted_iota(jnp.int32, (w, 2 * w), 1)
    rel = qi - ki
    first = jnp.where(n > 0, 0, w)
    bias = jnp.where((rel >= 0) & (rel < ATT_WINDOW) & (ki >= first), 0.0, -jnp.inf)
    heads = range(ATT_REP)
    qs = [(q[:, r * hd:(r + 1) * hd] * (hd ** -0.5)).astype(BF16) for r in heads]
    sinks = [sink_ref[h * ATT_REP + r] for r in heads]
    ss = [lax.dot_general(qr, k_all, (((1,), (1,)), ((), ())), preferred_element_type=F32) + bias for qr in qs]
    ms = [jnp.maximum(jnp.max(s, -1, keepdims=True), sink) for s, sink in zip(ss, sinks)]
    es = [jnp.exp(s - m) for s, m in zip(ss, ms)]
    dens = [jnp.sum(e, -1, keepdims=True) + jnp.exp(sink - m) for e, sink, m in zip(es, sinks, ms)]
    outs = [jnp.dot(e.astype(BF16), v_all, preferred_element_type=F32) / den for e, den in zip(es, dens)]
    o_ref[...] = jnp.concatenate(outs, axis=1).astype(o_ref.dtype)


def _swa_attention(qkv, sinks, batch, seq):
    t = batch * seq
    nb = seq // ATT_WINDOW
    hd, w = ATT_HEAD_DIM, ATT_WINDOW
    qw = ATT_REP * hd
    k4 = qkv[:, Q_COLS:Q_COLS + KV_COLS].reshape(batch, seq, ATT_KV_HEADS, hd).transpose(0, 2, 1, 3)
    v4 = qkv[:, Q_COLS + KV_COLS:].reshape(batch, seq, ATT_KV_HEADS, hd).transpose(0, 2, 1, 3)
    cos, sin = _rotary_tables(seq)
    cur = pl.BlockSpec((1, 1, w, hd), lambda b, h, n: (b, h, n, 0))
    prev = pl.BlockSpec((1, 1, w, hd), lambda b, h, n: (b, h, jnp.maximum(n - 1, 0), 0))
    tab_cur = pl.BlockSpec((w, qw), lambda b, h, n: (n, 0))
    tab_prev = pl.BlockSpec((w, qw), lambda b, h, n: (jnp.maximum(n - 1, 0), 0))
    return pl.pallas_call(
        _swa_kernel,
        grid=(batch, ATT_KV_HEADS, nb),
        in_specs=[
            pl.BlockSpec(memory_space=pltpu.SMEM),
            pl.BlockSpec((w, qw), lambda b, h, n: (b * nb + n, h)),
            cur, prev, cur, prev, tab_cur, tab_cur, tab_prev, tab_prev,
            pl.BlockSpec((qw, qw), lambda b, h, n: (0, 0)),
        ],
        out_specs=pl.BlockSpec((w, qw), lambda b, h, n: (b * nb + n, h)),
        out_shape=jax.ShapeDtypeStruct((t, Q_COLS), BF16),
        compiler_params=_cparams(("parallel", "parallel", "arbitrary")),
        name="swa_attention",
    )(sinks, qkv, k4, k4, v4, v4, cos, sin, cos, sin, _rotate_half_matrix())


def _ssd_swa_mixer(x2, xb, batch, seq, w_in, b_qkv, conv_w, conv_b, dt_bias, a_log, d_skip, norm_w, sinks,
                   w_out, b_out):
    o1 = D_MODEL
    o2 = o1 + SSM_CONV_CH
    o3 = o2 + SSM_HEADS
    w_in_b = w_in.astype(BF16)
    u_z = _matmul(xb, w_in_b, n=o1)
    u_xbc = _matmul(xb, w_in_b, col_start=o1, n=SSM_CONV_CH)
    u_dt = _matmul(x2, w_in[:, o2:o3], precision=HIGHEST)
    qkv = _matmul(xb, w_in_b[:, o3:], b_qkv)
    y_ssm = _ssd_mixer(u_z, u_xbc, u_dt, conv_w, conv_b, dt_bias, a_log, d_skip, norm_w, batch, seq)
    y_att = _swa_attention(qkv, sinks, batch, seq)
    return _matmul((y_ssm, y_att), w_out.astype(BF16), b_out)


MOE_TILE = 512
MOE_GATHER_TILE = 256
ROUTER_TILE = 256


def _router_kernel(h_ref, wt_ref, b_ref, ids_ref, wts_ref):
    logits = lax.dot_general(wt_ref[...], h_ref[...], (((1,), (1,)), ((), ())),
                             precision=HIGHEST, preferred_element_type=F32) + b_ref[...]
    e = jnp.exp(logits - jnp.max(logits, 0, keepdims=True))
    probs = e / jnp.sum(e, 0, keepdims=True)
    p = [probs[i:i + 1, :] for i in range(N_EXPERTS)]

    keep, score = [], []
    for g in range(N_EXPERT_GROUPS):
        members = range(g * EXPERTS_PER_GROUP, (g + 1) * EXPERTS_PER_GROUP)
        s = jnp.zeros_like(p[0])
        for i in members:
            rank = jnp.zeros_like(p[0])
            for j in members:
                if j != i:
                    beats = (p[j] > p[i]) | ((p[j] == p[i]) & (j < i)) if j < i else (p[j] > p[i])
                    rank = rank + beats.astype(F32)
            keep.append(rank < TOP_K)
            s = s + jnp.where(keep[i], p[i], 0.0)
        score.append(s)
    chosen = []
    for g in range(N_EXPERT_GROUPS):
        c = jnp.ones_like(p[0]) > 0
        for g2 in range(N_EXPERT_GROUPS):
            if g2 < g:
                c = c & (score[g] > score[g2])
            elif g2 > g:
                c = c & (score[g] >= score[g2])
        chosen.append(c)
    denom = jnp.zeros_like(p[0])
    for g in range(N_EXPERT_GROUPS):
        denom = denom + jnp.where(chosen[g], score[g], 0.0)
    count = jnp.zeros_like(p[0])
    id0 = jnp.zeros_like(p[0])
    id1 = jnp.zeros_like(p[0])
    w0 = jnp.zeros_like(p[0])
    w1 = jnp.zeros_like(p[0])
    for i in range(N_EXPERTS):
        sel = keep[i] & chosen[i // EXPERTS_PER_GROUP]
        gate = p[i] / denom
        first = sel & (count == 0.0)
        second = sel & (count == 1.0)
        id0 = jnp.where(first, float(i), id0)
        w0 = jnp.where(first, gate, w0)
        id1 = jnp.where(second, float(i), id1)
        w1 = jnp.where(second, gate, w1)
        count = count + sel.astype(F32)
    ids_ref[0:1, :] = id0.astype(jnp.int32)
    ids_ref[1:2, :] = id1.astype(jnp.int32)
    wts_ref[0:1, :] = w0
    wts_ref[1:2, :] = w1


def _router(h, router_w, router_b):
    t, d = h.shape
    tm = _tile(t, ROUTER_TILE)
    return pl.pallas_call(
        _router_kernel,
        grid=(t // tm,),
        in_specs=[
            pl.BlockSpec((tm, d), lambda i: (i, 0)),
            pl.BlockSpec((N_EXPERTS, d), lambda i: (0, 0)),
            pl.BlockSpec((N_EXPERTS, 1), lambda i: (0, 0)),
        ],
        out_specs=[pl.BlockSpec((TOP_K, tm), lambda i: (0, i)), pl.BlockSpec((TOP_K, tm), lambda i: (0, i))],
        out_shape=[jax.ShapeDtypeStruct((TOP_K, t), jnp.int32), jax.ShapeDtypeStruct((TOP_K, t), F32)],
        compiler_params=_cparams(("parallel",)),
        name="moe_router",
    )(h, router_w.T, router_b.reshape(N_EXPERTS, 1))


def _route_metadata(ids, wts, tm):
    t = ids.shape[1]
    na = TOP_K * t
    eid = ids.T.reshape(na)
    onehot = (eid[:, None] == jnp.arange(N_EXPERTS, dtype=jnp.int32)[None, :]).astype(jnp.int32)
    csum = jnp.cumsum(onehot, axis=0)
    counts = csum[-1]
    rank = jnp.sum(csum * onehot, axis=1) - 1
    padded = (counts + tm - 1) // tm * tm
    pend = jnp.cumsum(padded)
    dest = (pend - padded)[eid] + rank
    n_tiles = na // tm + N_EXPERTS
    n_rows = n_tiles * tm
    row_token = jnp.zeros((n_rows,), jnp.int32).at[dest].set(jnp.arange(na, dtype=jnp.int32) // TOP_K)
    row_weight = jnp.zeros((n_rows,), F32).at[dest].set(wts.T.reshape(na))
    tile_start = jnp.arange(n_tiles, dtype=jnp.int32) * tm
    tile_valid = (tile_start < pend[-1]).astype(jnp.int32)
    last_expert = jnp.max(jnp.where(counts > 0, jnp.arange(N_EXPERTS, dtype=jnp.int32), 0))
    tile_expert = jnp.minimum(jnp.searchsorted(pend, tile_start, side="right").astype(jnp.int32), last_expert)
    experts = jnp.arange(N_EXPERTS, dtype=jnp.int32)
    later = jnp.where((counts > 0)[None, :] & (experts[None, :] > experts[:, None]), experts[None, :], N_EXPERTS)
    next_nonempty = jnp.min(later, axis=1)
    next_nonempty = jnp.where(next_nonempty == N_EXPERTS, tile_expert[0], next_nonempty).astype(jnp.int32)
    prev_expert = jnp.concatenate([jnp.full((1,), -1, jnp.int32), tile_expert[:-1]])
    run_first = (tile_valid == 1) & (tile_expert != prev_expert)
    runs = jnp.stack([tile_expert, tile_valid, run_first.astype(jnp.int32), next_nonempty[tile_expert],
                      (tile_expert == last_expert).astype(jnp.int32)])
    return row_token, row_weight, dest, runs


def _row_copy(src_hbm, tiles_ref, sem, tile, sub, src_row):
    return pltpu.make_async_copy(src_hbm.at[src_row], tiles_ref.at[tile, :, sub, :], sem)


def _start_row_gather(src_hbm, tiles_ref, idx_ref, n, sem):
    def body(tile, carry):
        for sub in range(SUBLANES):
            _row_copy(src_hbm, tiles_ref, sem, tile, sub, idx_ref[0, 0, tile * SUBLANES + sub]).start(
                priority=sub % 2)
        return carry

    lax.fori_loop(0, n // SUBLANES, body, 0)


def _wait_row_gather(src_hbm, tiles_ref, n, sem):
    def body(tile, carry):
        for sub in range(SUBLANES):
            _row_copy(src_hbm, tiles_ref, sem, tile, sub, 0).wait()
        return carry

    lax.fori_loop(0, n // SUBLANES, body, 0)


def _gather_step(src_hbm, slots_ref, idx_ref, idx_next_ref, sems, n, *, more):
    i = pl.program_id(0)
    slot = i % 2

    @pl.when(i == 0)
    def _():
        _start_row_gather(src_hbm, slots_ref.at[0], idx_ref, n, sems.at[0])

    @pl.when(more)
    def _():
        _start_row_gather(src_hbm, slots_ref.at[1 - slot], idx_next_ref, n, sems.at[1 - slot])

    return slot


def _moe_gather_kernel(tv_ref, rt_ref, rt_next_ref, h_hbm, o_ref, xg_ref, sems, *, tm):
    i = pl.program_id(0)
    nxt = jnp.minimum(i + 1, pl.num_programs(0) - 1)
    valid = tv_ref[i] == 1
    slot = _gather_step(h_hbm, xg_ref, rt_ref, rt_next_ref, sems, tm, more=(nxt > i) & (tv_ref[nxt] == 1))

    @pl.when(valid)
    def _():
        _wait_row_gather(h_hbm, xg_ref.at[slot], tm, sems.at[slot])
        o_ref[...] = _tiles_to_rows(xg_ref.at[slot], 0, tm // SUBLANES).astype(o_ref.dtype)

    @pl.when(jnp.logical_not(valid))
    def _():
        o_ref[...] = jnp.zeros_like(o_ref)


RUN_EXPERT, RUN_VALID, RUN_FIRST, RUN_NEXT, RUN_LAST = range(5)


def _stream_expert_weights(runs_ref, w_hbm, stage_ref, wb_ref, sems, *, layer, tn):
    j, i = pl.program_id(0), pl.program_id(1)
    last_pass = j == pl.num_programs(0) - 1

    def copies(expert, col_block):
        col = pl.multiple_of(col_block * tn, tn)
        return [pltpu.make_async_copy(w.at[layer, expert, :, pl.ds(col, tn)], stage_ref.at[m], sems.at[m])
                for m, w in enumerate(w_hbm)]

    @pl.when((j == 0) & (i == 0))
    def _():
        for c in copies(runs_ref[RUN_EXPERT, 0], 0):
            c.start()

    @pl.when(runs_ref[RUN_FIRST, i] == 1)
    def _():
        for c in copies(runs_ref[RUN_EXPERT, i], j):
            c.wait()
        for m in range(len(w_hbm)):
            wb_ref[m] = stage_ref[m].astype(BF16)
        last_run = runs_ref[RUN_LAST, i] == 1

        @pl.when(jnp.logical_not(last_run & last_pass))
        def _():
            for c in copies(runs_ref[RUN_NEXT, i], jnp.where(last_run, j + 1, j)):
                c.start()


def _moe_up_kernel(runs_ref, x_ref, wg_hbm, wu_hbm, o_ref, stage_ref, wb_ref, sems, *, layer, tn):
    _stream_expert_weights(runs_ref, (wg_hbm, wu_hbm), stage_ref, wb_ref, sems, layer=layer, tn=tn)
    valid = runs_ref[RUN_VALID, pl.program_id(1)] == 1

    @pl.when(valid)
    def _():
        xt = x_ref[...]
        gate = jnp.dot(xt, wb_ref[0], preferred_element_type=F32)
        up = jnp.dot(xt, wb_ref[1], preferred_element_type=F32)
        o_ref[...] = (_silu(gate) * up).astype(o_ref.dtype)

    @pl.when(jnp.logical_not(valid))
    def _():
        o_ref[...] = jnp.zeros_like(o_ref)


def _moe_down_kernel(runs_ref, he_ref, wd_hbm, rw_ref, o_ref, stage_ref, wb_ref, sems, *, layer, tn):
    _stream_expert_weights(runs_ref, (wd_hbm,), stage_ref, wb_ref, sems, layer=layer, tn=tn)
    valid = runs_ref[RUN_VALID, pl.program_id(1)] == 1

    @pl.when(valid)
    def _():
        _to_token_major(o_ref, jnp.dot(he_ref[...], wb_ref[0], preferred_element_type=F32) * rw_ref[...])

    @pl.when(jnp.logical_not(valid))
    def _():
        o_ref[...] = jnp.zeros_like(o_ref)


def _moe_combine_ln_kernel(pos_ref, pos_next_ref, h_ref, y_hbm, g_ref, b_ref, o_ref, ob_ref, yg_ref, sems, *, tm):
    n = TOP_K * tm
    slot = _gather_step(y_hbm, yg_ref, pos_ref, pos_next_ref, sems, n,
                        more=pl.program_id(0) + 1 < pl.num_programs(0))
    _wait_row_gather(y_hbm, yg_ref.at[slot], n, sems.at[slot])
    per = tm // SUBLANES
    ffn = _tiles_to_rows(yg_ref.at[slot], 0, per) + _tiles_to_rows(yg_ref.at[slot], per, TOP_K * per)
    r = _layer_norm_rows(ALPHA * h_ref[...] + ffn, g_ref[...], b_ref[...])
    o_ref[...] = r
    ob_ref[...] = r.astype(BF16)


def _moe_block(h, h3, router_w, router_b, w_gate, w_up, w_down, layer, ln_g, ln_b, *, tm=MOE_TILE, tn_up=512,
               tn_down=2048, tm_out=128):
    t, d = h.shape
    slabs = d // LANES
    ids, wts = _router(h, router_w, router_b)
    row_token, row_weight, dest, runs = _route_metadata(ids, wts, tm)
    tile_valid = runs[RUN_VALID]
    n_rows = row_token.shape[0]
    n_tiles = n_rows // tm
    de = w_gate.shape[-1]
    tn_up, tn_down = _tile(de, tn_up), _tile(d, tn_down)

    def gather_scratch(n):
        return [pltpu.VMEM((2, n // SUBLANES, slabs, SUBLANES, LANES), F32), pltpu.SemaphoreType.DMA((2,))]

    tg = _tile(tm, MOE_GATHER_TILE)
    per = tm // tg
    n_gather = n_rows // tg
    row_token3 = row_token.reshape(n_gather, 1, tg)
    x_rows = pl.pallas_call(
        functools.partial(_moe_gather_kernel, tm=tg),
        grid_spec=pltpu.PrefetchScalarGridSpec(
            num_scalar_prefetch=1,
            grid=(n_gather,),
            in_specs=[
                pl.BlockSpec((1, 1, tg), lambda i, tv: (i, 0, 0), memory_space=pltpu.SMEM),
                pl.BlockSpec((1, 1, tg), lambda i, tv: (jnp.minimum(i + 1, n_gather - 1), 0, 0),
                             memory_space=pltpu.SMEM),
                pl.BlockSpec(memory_space=pl.ANY),
            ],
            out_specs=pl.BlockSpec((tg, d), lambda i, tv: (i, 0)),
            scratch_shapes=gather_scratch(tg),
        ),
        out_shape=jax.ShapeDtypeStruct((n_rows, d), BF16),
        compiler_params=_cparams(("arbitrary",)),
        name="moe_gather",
    )(jnp.repeat(tile_valid, per), row_token3, row_token3, h3)

    def weight_scratch(n_mats, k, tn):
        return [pltpu.VMEM((n_mats, k, tn), F32), pltpu.VMEM((n_mats, k, tn), BF16),
                pltpu.SemaphoreType.DMA((n_mats,))]

    he = pl.pallas_call(
        functools.partial(_moe_up_kernel, layer=layer, tn=tn_up),
        grid_spec=pltpu.PrefetchScalarGridSpec(
            num_scalar_prefetch=1,
            grid=(de // tn_up, n_tiles),
            in_specs=[
                pl.BlockSpec((tm, d), lambda j, i, runs: (i, 0)),
                pl.BlockSpec(memory_space=pl.ANY),
                pl.BlockSpec(memory_space=pl.ANY),
            ],
            out_specs=pl.BlockSpec((tm, tn_up), lambda j, i, runs: (i, j)),
            scratch_shapes=weight_scratch(2, d, tn_up),
        ),
        out_shape=jax.ShapeDtypeStruct((n_rows, de), BF16),
        compiler_params=_cparams(("arbitrary", "arbitrary")),
        name="moe_gate_up",
    )(runs, x_rows, w_gate, w_up)

    y_rows = pl.pallas_call(
        functools.partial(_moe_down_kernel, layer=layer, tn=tn_down),
        grid_spec=pltpu.PrefetchScalarGridSpec(
            num_scalar_prefetch=1,
            grid=(d // tn_down, n_tiles),
            in_specs=[
                pl.BlockSpec((tm, de), lambda j, i, runs: (i, 0)),
                pl.BlockSpec(memory_space=pl.ANY),
                pl.BlockSpec((tm, 1), lambda j, i, runs: (i, 0)),
            ],
            out_specs=pl.BlockSpec((tm, tn_down // LANES, LANES), lambda j, i, runs: (i, j, 0)),
            scratch_shapes=weight_scratch(1, de, tn_down),
        ),
        out_shape=jax.ShapeDtypeStruct((n_rows, slabs, LANES), F32),
        compiler_params=_cparams(("arbitrary", "arbitrary")),
        name="moe_down",
    )(runs, he, w_down, row_weight.reshape(n_rows, 1))

    tm_out = _tile(t, tm_out)
    n_out = t // tm_out
    pos = dest.reshape(n_out, tm_out, TOP_K).transpose(0, 2, 1).reshape(n_out, 1, TOP_K * tm_out)
    row = pl.BlockSpec((tm_out, d), lambda i: (i, 0))
    vec = pl.BlockSpec((1, d), lambda i: (0, 0))
    return pl.pallas_call(
        functools.partial(_moe_combine_ln_kernel, tm=tm_out),
        grid=(n_out,),
        in_specs=[
            pl.BlockSpec((1, 1, TOP_K * tm_out), lambda i: (i, 0, 0), memory_space=pltpu.SMEM),
            pl.BlockSpec((1, 1, TOP_K * tm_out), lambda i: (jnp.minimum(i + 1, n_out - 1), 0, 0),
                         memory_space=pltpu.SMEM),
            row,
            pl.BlockSpec(memory_space=pl.ANY),
            vec, vec,
        ],
        out_specs=[row, row],
        out_shape=[jax.ShapeDtypeStruct((t, d), F32), jax.ShapeDtypeStruct((t, d), BF16)],
        scratch_shapes=gather_scratch(TOP_K * tm_out),
        compiler_params=_cparams(("arbitrary",)),
        name="moe_combine_ln",
    )(pos, pos, h, y_rows, ln_g.reshape(1, d), ln_b.reshape(1, d))


def _shift_mix_kernel(x_ref, xp_ref, mix_ref, o_ref, pad_ref, *, tm, seq):
    i = pl.program_id(0)
    starts_sequence = (i * tm) % seq == 0
    pad_ref[0:SUBLANES, :] = jnp.where(starts_sequence, 0.0, xp_ref[...])
    pad_ref[SUBLANES:SUBLANES + tm, :] = x_ref[...]
    xv = x_ref[...]
    xx = pad_ref[SUBLANES - 1:SUBLANES - 1 + tm, :] - xv
    for m in range(mix_ref.shape[0]):
        o_ref[m] = (xv + xx * mix_ref[m:m + 1, :]).astype(o_ref.dtype)


def _shift_mix(x, mix, seq, *, tm=128):
    t, d = x.shape
    nm = mix.shape[0]
    tm = _tile(seq, tm)
    per = tm // SUBLANES
    return pl.pallas_call(
        functools.partial(_shift_mix_kernel, tm=tm, seq=seq),
        grid=(t // tm,),
        in_specs=[
            pl.BlockSpec((tm, d), lambda i: (i, 0)),
            pl.BlockSpec((SUBLANES, d), lambda i: (jnp.maximum(i * per - 1, 0), 0)),
            pl.BlockSpec((nm, d), lambda i: (0, 0)),
        ],
        out_specs=pl.BlockSpec((nm, tm, d), lambda i: (0, i, 0)),
        out_shape=jax.ShapeDtypeStruct((nm, t, d), BF16),
        scratch_shapes=[pltpu.VMEM((SUBLANES + tm, d), F32)],
        compiler_params=_cparams(("parallel",)),
        name="rwkv_shift_mix",
    )(x, x, mix)


WKV_CHUNK = 64
WKV_HEADS = 4
WKV_WIDTH = WKV_HEADS * RWKV_HEAD
WKV_BLOCK = 512
NEUMANN_STEPS = 5


def _wkv_kernel(r_ref, k_ref, v_ref, wl_ref, ar_ref, g_ref, kk_ref, ka_ref, rk_ref, lnw_ref, lnb_ref,
                o_ref, state_ref, *, tb):
    lc, hw, wd = WKV_CHUNK, RWKV_HEAD, WKV_WIDTH
    nch = tb // lc

    @pl.when(pl.program_id(2) == 0)
    def _():
        state_ref[...] = jnp.zeros_like(state_ref)

    wrow = lax.broadcasted_iota(jnp.int32, (wd, wd), 0)
    wcol = lax.broadcasted_iota(jnp.int32, (wd, wd), 1)
    same_head = wrow // hw == wcol // hw
    ones_bd = same_head.astype(BF16)

    def block_diag(m):
        return jnp.where(same_head, jnp.concatenate([m] * WKV_HEADS, axis=0), 0.0).astype(BF16)

    def head_sum(m):
        hi = m.astype(BF16)
        lo = (m - hi.astype(F32)).astype(BF16)
        return jnp.dot(hi, ones_bd, preferred_element_type=F32) + jnp.dot(lo, ones_bd, preferred_element_type=F32)

    def mm(a, b):
        return jnp.dot(a.astype(BF16), b, preferred_element_type=F32)

    r = r_ref[...]
    k = k_ref[...]
    v = v_ref[...]
    w_log = -_softplus(-wl_ref[...]) - 0.5
    lw = -jnp.exp(w_log)
    a_sig = jax.nn.sigmoid(ar_ref[...])
    kx = k * kk_ref[...]
    kk = kx / jnp.maximum(jnp.sqrt(head_sum(kx * kx)), 1e-12)
    k2 = k * (1.0 + (a_sig - 1.0) * ka_ref[...])
    a_s = -kk
    b_s = kk * a_sig

    trow = lax.broadcasted_iota(jnp.int32, (tb, tb), 0)
    tcol = lax.broadcasted_iota(jnp.int32, (tb, tb), 1)
    same_chunk = trow // lc == tcol // lc
    cum = _dot_split3(lw, (same_chunk & (trow >= tcol)).astype(BF16), split_lhs=False)
    tot = jnp.concatenate([jnp.broadcast_to(cum[(c + 1) * lc - 1:(c + 1) * lc, :], (lc, wd)) for c in range(nch)],
                          axis=0)
    grow = jnp.exp(-cum)
    rt = r * jnp.exp(cum)
    at = a_s * jnp.exp(cum - lw)
    bt = b_s * grow
    kt = k2 * grow
    rest = jnp.exp(tot - cum)
    bh = b_s * rest
    kh = k2 * rest
    p_end = jnp.exp(tot)

    t_idx = lax.broadcasted_iota(jnp.int32, (lc, wd), 0)
    s_idx = lax.broadcasted_iota(jnp.int32, (lc, wd), 1) % hw
    strict = t_idx > s_idx
    incl = t_idx >= s_idx
    eye = (t_idx == s_idx).astype(F32)
    nt = (((1,), (1,)), ((), ()))

    tn = (((0,), (0,)), ((), ()))
    chunks = [slice(c * lc, (c + 1) * lc) for c in range(nch)]

    a_ab, a_rb, a_ak, a_rk = [], [], [], []
    for sl in chunks:
        ar = jnp.concatenate([at[sl], rt[sl]], axis=0).astype(BF16)
        xb = lax.dot_general(ar, block_diag(bt[sl]), nt, preferred_element_type=F32)
        xk = lax.dot_general(ar, block_diag(kt[sl]), nt, preferred_element_type=F32)
        a_ab.append(jnp.where(strict, xb[:lc], 0.0))
        a_rb.append(jnp.where(incl, xb[lc:], 0.0))
        a_ak.append(jnp.where(strict, xk[:lc], 0.0))
        a_rk.append(jnp.where(incl, xk[lc:], 0.0))
    inv = [eye + a for a in a_ab]
    pw = [mm(a, block_diag(a)) for a in a_ab]
    for step in range(1, NEUMANN_STEPS + 1):
        last = step == NEUMANN_STEPS
        lhs = inv if last else [jnp.concatenate([t, p], axis=0) for t, p in zip(inv, pw)]
        prod = [mm(x, block_diag(p)) for x, p in zip(lhs, pw)]
        inv = [t + q[:lc] for t, q in zip(inv, prod)]
        if not last:
            pw = [q[lc:] for q in prod]
    v_bd = [block_diag(v[sl]) for sl in chunks]
    ta = [mm(t, block_diag(at[sl])) for t, sl in zip(inv, chunks)]
    av = [mm(jnp.concatenate([ak, rk], axis=0), vb) for ak, rk, vb in zip(a_ak, a_rk, v_bd)]
    y0 = [x[lc:] for x in av]
    u0 = [mm(t, block_diag(x[:lc])) for t, x in zip(inv, av)]
    gain, add = [], []
    for c, sl in enumerate(chunks):
        bh_b = bh[sl].astype(BF16)
        gain.append(jnp.where(same_head, lax.dot_general(ta[c].astype(BF16), bh_b, tn, preferred_element_type=F32),
                              0.0).astype(BF16))
        uv = jnp.concatenate([u0[c], v[sl]], axis=0).astype(BF16)
        bk = jnp.concatenate([bh_b, kh[sl].astype(BF16)], axis=0)
        add.append(jnp.where(same_head, lax.dot_general(uv, bk, tn, preferred_element_type=F32), 0.0))

    states = [state_ref[...]]
    for c in range(nch):
        s0 = states[-1]
        states.append(s0 * p_end[c * lc:c * lc + 1, :] + mm(s0, gain[c]) + add[c])
    state_ref[...] = states[nch]

    ys = []
    for c, sl in enumerate(chunks):
        tr = jnp.concatenate([ta[c], rt[sl]], axis=0).astype(BF16)
        xs = lax.dot_general(tr, states[c].astype(BF16), nt, preferred_element_type=F32)
        u = xs[:lc] + u0[c]
        ys.append(xs[lc:] + mm(a_rb[c], block_diag(u)) + y0[c])

    y = jnp.concatenate(ys, axis=0)
    mu = head_sum(y) / hw
    dev = y - mu
    var = head_sum(dev * dev) / hw
    yn = dev * lax.rsqrt(var + RWKV_GN_EPS) * lnw_ref[...] + lnb_ref[...]
    bonus = head_sum(r * k2 * rk_ref[...]) * v
    o_ref[...] = ((yn + bonus) * g_ref[...]).astype(o_ref.dtype)


def _wkv(r, k, v, wl, araw, g, k_k, k_a, r_k, ln_w, ln_b, batch, seq):
    t, d = r.shape
    tb = _tile(seq, WKV_BLOCK)
    nt = seq // tb
    wd = WKV_WIDTH
    rows = pl.BlockSpec((tb, wd), lambda b, h, n: (b * nt + n, h))
    vec = pl.BlockSpec((1, wd), lambda b, h, n: (0, h))
    return pl.pallas_call(
        functools.partial(_wkv_kernel, tb=tb),
        grid=(batch, d // wd, nt),
        in_specs=[rows] * 6 + [vec] * 5,
        out_specs=rows,
        out_shape=jax.ShapeDtypeStruct((t, d), BF16),
        scratch_shapes=[pltpu.VMEM((wd, wd), F32)],
        compiler_params=_cparams(("parallel", "parallel", "arbitrary")),
        name="rwkv_wkv",
    )(r, k, v, wl, araw, g, k_k.reshape(1, d), k_a.reshape(1, d), r_k.reshape(1, d),
      ln_w.reshape(1, d), ln_b.reshape(1, d))


def _rwkv7_time_mix(x2, batch, seq, mix, w_r, w_k, w_v, w_o, w0, w1, w2, a0, a1, a2, g1, g2, k_k, k_a, r_k,
                    ln_w, ln_b):
    xm = _shift_mix(x2, mix, seq)
    bf = lambda w: w.astype(BF16)
    r = _matmul(xm[0], bf(w_r))
    k = _matmul(xm[2], bf(w_k))
    v = _matmul(xm[3], bf(w_v))
    wl = _matmul(_matmul(xm[1], bf(w1), act="tanh", out_dtype=BF16), bf(w2), w0)
    araw = _matmul(_matmul(xm[4], bf(a1), out_dtype=BF16), bf(a2), a0)
    g = _matmul(_matmul(xm[5], bf(g1), act="sigmoid", out_dtype=BF16), bf(g2))
    yg = _wkv(r, k, v, wl, araw, g, k_k, k_a, r_k, ln_w, ln_b, batch, seq)
    return _matmul(yg, bf(w_o))


def kernel(x, ab_w_in, ab_b_qkv, ssm_conv_w, ssm_conv_b, ssm_dt_bias, ssm_a_log, ssm_d, ssm_norm_w, attn_sinks,
           ab_w_out, ab_b_out, rwkv_mix, rwkv_w_r, rwkv_w_k, rwkv_w_v, rwkv_w_o, rwkv_w0, rwkv_w1, rwkv_w2,
           rwkv_a0, rwkv_a1, rwkv_a2, rwkv_g1, rwkv_g2, rwkv_k_k, rwkv_k_a, rwkv_r_k, rwkv_ln_w, rwkv_ln_b,
           ln_mix_g, ln_mix_b, ln_ffn_g, ln_ffn_b, router_w, router_b, moe_w_gate, moe_w_up, moe_w_down):
    batch, seq, d = x.shape
    x2 = x.reshape(batch * seq, d)
    xb = x2.astype(BF16)
    for layer in range(DEPTH):
        i = layer // 2
        if layer % 2 == 0:
            mix = _ssd_swa_mixer(x2, xb, batch, seq, ab_w_in[i], ab_b_qkv[i], ssm_conv_w[i], ssm_conv_b[i],
                                 ssm_dt_bias[i], ssm_a_log[i], ssm_d[i], ssm_norm_w[i], attn_sinks[i],
                                 ab_w_out[i], ab_b_out[i])
        else:
            mix = _rwkv7_time_mix(x2, batch, seq, rwkv_mix[i], rwkv_w_r[i], rwkv_w_k[i], rwkv_w_v[i], rwkv_w_o[i],
                                  rwkv_w0[i], rwkv_w1[i], rwkv_w2[i], rwkv_a0[i], rwkv_a1[i], rwkv_a2[i],
                                  rwkv_g1[i], rwkv_g2[i], rwkv_k_k[i], rwkv_k_a[i], rwkv_r_k[i].reshape(-1),
                                  rwkv_ln_w[i], rwkv_ln_b[i])
        h, h3 = _add_layer_norm(x2, mix, ln_mix_g[layer], ln_mix_b[layer])
        x2, xb = _moe_block(h, h3, router_w, router_b, moe_w_gate, moe_w_up, moe_w_down, layer,
                            ln_ffn_g[layer], ln_ffn_b[layer])
    return x2.reshape(batch, seq, d)
```

```python
import functools
import math

import jax
import jax.numpy as jnp
import numpy as np
from jax import lax
from jax.experimental import pallas as pl
from jax.experimental.pallas import tpu as pltpu

F32 = jnp.float32
BF16 = jnp.bfloat16
HIGHEST = lax.Precision.HIGHEST

D_MODEL = 4096
DEPTH = 2
SSM_HEAD_DIM = 64
SSM_HEADS = 64
SSM_GROUPS = 8
SSM_HEADS_PER_GROUP = 8
SSM_STATE = 128
SSM_CONV = 4
SSM_CHUNK = 128
SSM_GROUP_WIDTH = SSM_HEADS_PER_GROUP * SSM_HEAD_DIM
SSM_CONV_CH = D_MODEL + 2 * SSM_GROUPS * SSM_STATE
SSM_NORM_EPS = 1e-5
ATT_HEADS = 64
ATT_KV_HEADS = 8
ATT_HEAD_DIM = 64
ATT_REP = 8
ATT_WINDOW = 128
ROPE_DIM = 16
ROPE_THETA = 500000.0
Q_COLS = 4096
KV_COLS = 512
QKV_COLS = Q_COLS + 2 * KV_COLS
QKV_SHIFT = SSM_HEADS
TAIL_COLS = -(-(QKV_SHIFT + QKV_COLS) // 512) * 512
RWKV_HEAD = 64
RWKV_GN_EPS = 64e-5
N_EXPERTS = 16
N_EXPERT_GROUPS = 4
EXPERTS_PER_GROUP = 4
TOP_K = 2
D_EXPERT = 1536
ALPHA = (2 * DEPTH) ** 0.25
LN_EPS = 1e-5

VMEM_LIMIT_BYTES = 56 * 1024 * 1024
LANES = 128
SUBLANES = 8


def _cparams(semantics):
    return pltpu.CompilerParams(dimension_semantics=semantics, vmem_limit_bytes=VMEM_LIMIT_BYTES)


def _tile(dim, pref):
    if dim <= pref:
        return dim
    t = pref
    while dim % t:
        t //= 2
    return t


def _silu(v):
    return v * jax.nn.sigmoid(v)


def _softplus(v):
    return jnp.maximum(v, 0.0) + jnp.log1p(jnp.exp(-jnp.abs(v)))


def _split3(v):
    hi = v.astype(BF16)
    rest = v - hi.astype(F32)
    mid = rest.astype(BF16)
    lo = (rest - mid.astype(F32)).astype(BF16)
    return hi, mid, lo


def _dot_split3(v, m01, *, split_lhs):
    if split_lhs:
        return sum(jnp.dot(t, m01, preferred_element_type=F32) for t in _split3(v))
    return sum(jnp.dot(m01, t, preferred_element_type=F32) for t in _split3(v))


def _mm_kernel(*refs, act, precision):
    a_refs, (w_ref, b_ref, o_ref) = refs[:-3], refs[-3:]
    r = b_ref[...]
    k0 = 0
    for a_ref in a_refs:
        k1 = k0 + a_ref.shape[1]
        r = r + jnp.dot(a_ref[...], w_ref[k0:k1, :], preferred_element_type=F32, precision=precision)
        k0 = k1
    if act == "tanh":
        r = jnp.tanh(r)
    elif act == "sigmoid":
        r = jax.nn.sigmoid(r)
    o_ref[...] = r.astype(o_ref.dtype)


MATMUL_VMEM_BUDGET = 44 * 1024 * 1024


def _matmul(a, w, bias=None, *, act=None, out_dtype=F32, precision=None, tm=1024, col_start=0, n=None):
    panels = a if isinstance(a, (tuple, list)) else (a,)
    m = panels[0].shape[0]
    kdim = sum(p.shape[1] for p in panels)
    n = w.shape[1] if n is None else n
    tm = _tile(m, tm)
    out_bytes = jnp.dtype(out_dtype).itemsize
    for tn in (512, 256, LANES):
        tn = _tile(n, tn)
        need = 2 * (tm * kdim * panels[0].dtype.itemsize + kdim * tn * w.dtype.itemsize + tm * tn * out_bytes)
        if need <= MATMUL_VMEM_BUDGET:
            break
    assert need <= MATMUL_VMEM_BUDGET and col_start % tn == 0 and w.shape[0] == kdim
    col0 = col_start // tn
    if bias is None:
        bias = jnp.zeros((n,), F32)
    bias = bias.reshape(1, n).astype(F32)
    return pl.pallas_call(
        functools.partial(_mm_kernel, act=act, precision=precision),
        grid=(m // tm, n // tn),
        in_specs=[pl.BlockSpec((tm, p.shape[1]), lambda i, j: (i, 0)) for p in panels] + [
            pl.BlockSpec((kdim, tn), lambda i, j: (0, col0 + j)),
            pl.BlockSpec((1, tn), lambda i, j: (0, j)),
        ],
        out_specs=pl.BlockSpec((tm, tn), lambda i, j: (i, j)),
        out_shape=jax.ShapeDtypeStruct((m, n), out_dtype),
        compiler_params=_cparams(("parallel", "arbitrary")),
        name="matmul",
    )(*panels, w, bias)


def _low_rank_kernel(a_ref, w1_ref, w2_ref, b_ref, o_ref, *, act):
    t = jnp.dot(a_ref[...], w1_ref[...], preferred_element_type=F32)
    if act == "tanh":
        t = jnp.tanh(t)
    elif act == "sigmoid":
        t = jax.nn.sigmoid(t)
    o_ref[...] = jnp.dot(t.astype(BF16), w2_ref[...], preferred_element_type=F32) + b_ref[...]


def _low_rank(a, w1, w2, bias=None, *, act=None, tm=512):
    m, kdim = a.shape
    r, n = w2.shape
    tm = _tile(m, tm)
    if bias is None:
        bias = jnp.zeros((n,), F32)
    return pl.pallas_call(
        functools.partial(_low_rank_kernel, act=act),
        grid=(m // tm,),
        in_specs=[
            pl.BlockSpec((tm, kdim), lambda i: (i, 0)),
            pl.BlockSpec((kdim, r), lambda i: (0, 0)),
            pl.BlockSpec((r, n), lambda i: (0, 0)),
            pl.BlockSpec((1, n), lambda i: (0, 0)),
        ],
        out_specs=pl.BlockSpec((tm, n), lambda i: (i, 0)),
        out_shape=jax.ShapeDtypeStruct((m, n), F32),
        compiler_params=_cparams(("parallel",)),
        name="low_rank",
    )(a, w1, w2, bias.reshape(1, n).astype(F32))


def _layer_norm_rows(v, g, b):
    mu = jnp.mean(v, -1, keepdims=True)
    var = jnp.mean(jnp.square(v - mu), -1, keepdims=True)
    return (v - mu) * lax.rsqrt(var + LN_EPS) * g + b


def _to_token_major(o3_ref, v):
    for s in range(v.shape[1] // LANES):
        o3_ref[:, s, :] = v[:, s * LANES:(s + 1) * LANES].astype(o3_ref.dtype)


def _tiles_to_rows(tiles_ref, t0, t1):
    return jnp.concatenate([tiles_ref[t0:t1, s].reshape((t1 - t0) * SUBLANES, LANES)
                            for s in range(tiles_ref.shape[1])], axis=1)


def _add_ln_kernel(x_ref, y_ref, g_ref, b_ref, o_ref, o3_ref):
    r = _layer_norm_rows(ALPHA * x_ref[...] + y_ref[...], g_ref[...], b_ref[...])
    o_ref[...] = r
    _to_token_major(o3_ref, r)


def _add_layer_norm(x, y, g, b, *, tm=128):
    t, d = x.shape
    tm = _tile(t, tm)
    row = pl.BlockSpec((tm, d), lambda i: (i, 0))
    row3 = pl.BlockSpec((tm, d // LANES, LANES), lambda i: (i, 0, 0))
    vec = pl.BlockSpec((1, d), lambda i: (0, 0))
    return pl.pallas_call(
        _add_ln_kernel,
        grid=(t // tm,),
        in_specs=[row, row, vec, vec],
        out_specs=[row, row3],
        out_shape=[jax.ShapeDtypeStruct((t, d), F32), jax.ShapeDtypeStruct((t, d // LANES, LANES), F32)],
        compiler_params=_cparams(("parallel",)),
        name="add_layer_norm",
    )(x, y, g.reshape(1, d), b.reshape(1, d))


CONV_HALO = SUBLANES
SSD_CH = SSM_GROUP_WIDTH + 2 * SSM_STATE


def _ssd_kernel(z_ref, x_ref, b_ref, c_ref, wx_ref, wb_ref, wc_ref, bx_ref, bb_ref, bc_ref,
                dtc_ref, dtr_ref, dbc_ref, dbr_ref, alc_ref, alr_ref, d_ref, nw_ref,
                o_ref, state_ref, pad_ref):
    ch, gw, ns, hd = SSM_CHUNK, SSM_GROUP_WIDTH, SSM_STATE, SSM_HEAD_DIM

    @pl.when(pl.program_id(2) == 0)
    def _():
        state_ref[...] = jnp.zeros_like(state_ref)
        pad_ref[0:CONV_HALO, :] = jnp.zeros((CONV_HALO, SSD_CH), F32)

    pad_ref[CONV_HALO:CONV_HALO + ch, 0:gw] = x_ref[...]
    pad_ref[CONV_HALO:CONV_HALO + ch, gw:gw + ns] = b_ref[...]
    pad_ref[CONV_HALO:CONV_HALO + ch, gw + ns:SSD_CH] = c_ref[...]
    w = jnp.concatenate([wx_ref[...], wb_ref[...], wc_ref[...]], axis=1)
    acc = jnp.concatenate([bx_ref[...], bb_ref[...], bc_ref[...]], axis=1)
    base = CONV_HALO - (SSM_CONV - 1)
    for k in range(SSM_CONV):
        acc = acc + w[k:k + 1, :] * pad_ref[base + k:base + k + ch, :]
    pad_ref[0:CONV_HALO, :] = pad_ref[ch:ch + CONV_HALO, :]
    xbc = _silu(acc)
    xc, bm, cm = xbc[:, :gw], xbc[:, gw:gw + ns], xbc[:, gw + ns:]

    dt_c = _softplus(dtc_ref[0] + dbc_ref[0])
    dt_r = _softplus(dtr_ref[...] + dbr_ref[...])
    a_c = -jnp.exp(alc_ref[0])
    a_r = -jnp.exp(alr_ref[...])
    row = lax.broadcasted_iota(jnp.int32, (ch, ch), 0)
    col = lax.broadcasted_iota(jnp.int32, (ch, ch), 1)
    causal = row >= col
    acs_c = _dot_split3(dt_c * a_c, causal.astype(BF16), split_lhs=False)
    acs_r = _dot_split3(dt_r * a_r, (row <= col).astype(BF16), split_lhs=True)

    hrow = lax.broadcasted_iota(jnp.int32, (SSM_HEADS_PER_GROUP, gw), 0)
    hcol = lax.broadcasted_iota(jnp.int32, (SSM_HEADS_PER_GROUP, gw), 1)
    expand = (hcol // hd == hrow).astype(BF16)
    dt_x = _dot_split3(dt_c, expand, split_lhs=True)
    acs_x = _dot_split3(acs_c, expand, split_lhs=True)

    xdt = xc * dt_x
    xdt_b = xdt.astype(BF16)
    bm_b, cm_b = bm.astype(BF16), cm.astype(BF16)
    cb = lax.dot_general(cm_b, bm_b, (((1,), (1,)), ((), ())), preferred_element_type=F32)
    prev = state_ref[...]
    y_off = jnp.dot(cm_b, prev.astype(BF16), preferred_element_type=F32) * jnp.exp(acs_x)

    lane = lax.broadcasted_iota(jnp.int32, (ch, 2 * hd), 1)
    pairs = []
    for pr in range(SSM_HEADS_PER_GROUP // 2):
        xp = xdt_b[:, pr * 2 * hd:(pr + 1) * 2 * hd]
        halves = []
        for q in range(2):
            r = 2 * pr + q
            diff = acs_c[:, r:r + 1] - acs_r[r:r + 1, :]
            seg = jnp.where(causal, jnp.exp(jnp.where(causal, diff, 0.0)), 0.0)
            halves.append(jnp.dot((cb * seg).astype(BF16), xp, preferred_element_type=F32))
        pairs.append(jnp.where(lane < hd, halves[0], halves[1]))
    y = jnp.concatenate(pairs, axis=1) + y_off + d_ref[...] * xc

    acs_last = acs_x[ch - 1:ch, :]
    contrib = lax.dot_general(bm_b, (xdt * jnp.exp(acs_last - acs_x)).astype(BF16),
                              (((0,), (0,)), ((), ())), preferred_element_type=F32)
    state_ref[...] = prev * jnp.exp(acs_last) + contrib

    u = y * _silu(z_ref[...])
    u = u * lax.rsqrt(jnp.mean(u * u, -1, keepdims=True) + SSM_NORM_EPS)
    o_ref[...] = (u * nw_ref[...]).astype(o_ref.dtype)


def _ssd_mixer(u_z, u_xbc, u_dt, conv_w, conv_b, dt_bias, a_log, d_skip, norm_w, batch, seq):
    t = batch * seq
    nc = seq // SSM_CHUNK
    g, r, gw, ns = SSM_GROUPS, SSM_HEADS_PER_GROUP, SSM_GROUP_WIDTH, SSM_STATE
    xblocks = D_MODEL // ns
    conv_wt = conv_w.T
    conv_b2 = conv_b.reshape(1, SSM_CONV_CH)
    dt_col = u_dt.reshape(t, g, r).transpose(1, 0, 2)
    dt_row = u_dt.T
    rows = lambda b, gi, c: (b * nc + c, gi)
    in_specs = [
        pl.BlockSpec((SSM_CHUNK, gw), rows),
        pl.BlockSpec((SSM_CHUNK, gw), rows),
        pl.BlockSpec((SSM_CHUNK, ns), lambda b, gi, c: (b * nc + c, xblocks + gi)),
        pl.BlockSpec((SSM_CHUNK, ns), lambda b, gi, c: (b * nc + c, xblocks + g + gi)),
        pl.BlockSpec((SSM_CONV, gw), lambda b, gi, c: (0, gi)),
        pl.BlockSpec((SSM_CONV, ns), lambda b, gi, c: (0, xblocks + gi)),
        pl.BlockSpec((SSM_CONV, ns), lambda b, gi, c: (0, xblocks + g + gi)),
        pl.BlockSpec((1, gw), lambda b, gi, c: (0, gi)),
        pl.BlockSpec((1, ns), lambda b, gi, c: (0, xblocks + gi)),
        pl.BlockSpec((1, ns), lambda b, gi, c: (0, xblocks + g + gi)),
        pl.BlockSpec((1, SSM_CHUNK, r), lambda b, gi, c: (gi, b * nc + c, 0)),
        pl.BlockSpec((r, SSM_CHUNK), lambda b, gi, c: (gi, b * nc + c)),
        pl.BlockSpec((1, 1, r), lambda b, gi, c: (gi, 0, 0)),
        pl.BlockSpec((r, 1), lambda b, gi, c: (gi, 0)),
        pl.BlockSpec((1, 1, r), lambda b, gi, c: (gi, 0, 0)),
        pl.BlockSpec((r, 1), lambda b, gi, c: (gi, 0)),
        pl.BlockSpec((1, gw), lambda b, gi, c: (0, gi)),
        pl.BlockSpec((1, gw), lambda b, gi, c: (0, gi)),
    ]
    return pl.pallas_call(
        _ssd_kernel,
        grid=(batch, g, nc),
        in_specs=in_specs,
        out_specs=pl.BlockSpec((SSM_CHUNK, gw), rows),
        out_shape=jax.ShapeDtypeStruct((t, D_MODEL), BF16),
        scratch_shapes=[pltpu.VMEM((ns, gw), F32), pltpu.VMEM((CONV_HALO + SSM_CHUNK, SSD_CH), F32)],
        compiler_params=_cparams(("parallel", "parallel", "arbitrary")),
        name="ssd_mixer",
    )(u_z, u_xbc, u_xbc, u_xbc, conv_wt, conv_wt, conv_wt, conv_b2, conv_b2, conv_b2,
      dt_col, dt_row, dt_bias.reshape(g, 1, r), dt_bias.reshape(SSM_HEADS, 1),
      a_log.reshape(g, 1, r), a_log.reshape(SSM_HEADS, 1),
      jnp.repeat(d_skip, SSM_HEAD_DIM).reshape(1, D_MODEL), norm_w.reshape(1, D_MODEL))


def _rotary_tables(seq):
    half = ROPE_DIM // 2
    inv_freq = ROPE_THETA ** (-jnp.arange(half, dtype=F32) / half)
    ang = jnp.arange(seq, dtype=F32)[:, None] * inv_freq[None, :]
    ones = jnp.ones((seq, ATT_HEAD_DIM - ROPE_DIM), F32)
    cos = jnp.concatenate([jnp.cos(ang), jnp.cos(ang), ones], -1)
    sin = jnp.concatenate([jnp.sin(ang), jnp.sin(ang), 0.0 * ones], -1)
    return jnp.tile(cos, (1, ATT_REP)), jnp.tile(sin, (1, ATT_REP))


def _rotate_half_matrix():
    width = ATT_REP * ATT_HEAD_DIM
    half = ROPE_DIM // 2
    p = np.zeros((width, width), np.float32)
    for j in range(width):
        if j % ATT_HEAD_DIM < half:
            p[j + half, j] = -1.0
        elif j % ATT_HEAD_DIM < ROPE_DIM:
            p[j - half, j] = 1.0
    return jnp.asarray(p, BF16)


def _rotate(v, cos, sin, perm):
    hi = v.astype(BF16)
    lo = (v - hi.astype(F32)).astype(BF16)
    partner = jnp.dot(hi, perm, preferred_element_type=F32) + jnp.dot(lo, perm, preferred_element_type=F32)
    return v * cos + partner * sin


def _swa_kernel(sink_ref, qa_ref, qb_ref, kc_ref, kp_ref, vc_ref, vp_ref, cosc_ref, sinc_ref, cosp_ref, sinp_ref,
                perm_ref, o_ref):
    w, hd = ATT_WINDOW, ATT_HEAD_DIM
    h = pl.program_id(1)
    n = pl.program_id(2)
    perm = perm_ref[...]
    perm_k = perm[:hd, :hd]
    q_raw = jnp.concatenate([qa_ref[:, QKV_SHIFT:], qb_ref[:, :QKV_SHIFT]], axis=1)
    q = _rotate(q_raw, cosc_ref[...], sinc_ref[...], perm)
    k_cur = _rotate(kc_ref[0, 0], cosc_ref[:, :hd], sinc_ref[:, :hd], perm_k)
    k_prev = _rotate(kp_ref[0, 0], cosp_ref[:, :hd], sinp_ref[:, :hd], perm_k)
    k_all = jnp.concatenate([k_prev, k_cur], axis=0).astype(BF16)
    v_all = jnp.concatenate([vp_ref[0, 0], vc_ref[0, 0]], axis=0).astype(BF16)
    qi = lax.broadcasted_iota(jnp.int32, (w, 2 * w), 0) + w
    ki = lax.broadcasted_iota(jnp.int32, (w, 2 * w), 1)
    rel = qi - ki
    first = jnp.where(n > 0, 0, w)
    bias = jnp.where((rel >= 0) & (rel < ATT_WINDOW) & (ki >= first), 0.0, -jnp.inf)
    heads = range(ATT_REP)
    qs = [(q[:, r * hd:(r + 1) * hd] * (hd ** -0.5)).astype(BF16) for r in heads]
    sinks = [sink_ref[h * ATT_REP + r] for r in heads]
    ss = [lax.dot_general(qr, k_all, (((1,), (1,)), ((), ())), preferred_element_type=F32) + bias for qr in qs]
    ms = [jnp.maximum(jnp.max(s, -1, keepdims=True), sink) for s, sink in zip(ss, sinks)]
    es = [jnp.exp(s - m) for s, m in zip(ss, ms)]
    dens = [jnp.sum(e, -1, keepdims=True) + jnp.exp(sink - m) for e, sink, m in zip(es, sinks, ms)]
    outs = [jnp.dot(e.astype(BF16), v_all, preferred_element_type=F32) / den for e, den in zip(es, dens)]
    o_ref[...] = jnp.concatenate(outs, axis=1).astype(o_ref.dtype)


def _swa_attention(u_tail, sinks, batch, seq):
    t = batch * seq
    nb = seq // ATT_WINDOW
    hd, w = ATT_HEAD_DIM, ATT_WINDOW
    qw = ATT_REP * hd
    k0 = QKV_SHIFT + Q_COLS
    k4 = u_tail[:, k0:k0 + KV_COLS].reshape(batch, seq, ATT_KV_HEADS, hd).transpose(0, 2, 1, 3)
    v4 = u_tail[:, k0 + KV_COLS:k0 + 2 * KV_COLS].reshape(batch, seq, ATT_KV_HEADS, hd).transpose(0, 2, 1, 3)
    cos, sin = _rotary_tables(seq)
    cur = pl.BlockSpec((1, 1, w, hd), lambda b, h, n: (b, h, n, 0))
    prev = pl.BlockSpec((1, 1, w, hd), lambda b, h, n: (b, h, jnp.maximum(n - 1, 0), 0))
    tab_cur = pl.BlockSpec((w, qw), lambda b, h, n: (n, 0))
    tab_prev = pl.BlockSpec((w, qw), lambda b, h, n: (jnp.maximum(n - 1, 0), 0))
    return pl.pallas_call(
        _swa_kernel,
        grid=(batch, ATT_KV_HEADS, nb),
        in_specs=[
            pl.BlockSpec(memory_space=pltpu.SMEM),
            pl.BlockSpec((w, qw), lambda b, h, n: (b * nb + n, h)),
            pl.BlockSpec((w, LANES), lambda b, h, n: (b * nb + n, (h + 1) * (qw // LANES))),
            cur, prev, cur, prev, tab_cur, tab_cur, tab_prev, tab_prev,
            pl.BlockSpec((qw, qw), lambda b, h, n: (0, 0)),
        ],
        out_specs=pl.BlockSpec((w, qw), lambda b, h, n: (b * nb + n, h)),
        out_shape=jax.ShapeDtypeStruct((t, Q_COLS), BF16),
        compiler_params=_cparams(("parallel", "parallel", "arbitrary")),
        name="swa_attention",
    )(sinks, u_tail, u_tail, k4, k4, v4, v4, cos, sin, cos, sin, _rotate_half_matrix())


def _ssd_swa_mixer(x2, xb, batch, seq, w_in, b_qkv, conv_w, conv_b, dt_bias, a_log, d_skip, norm_w, sinks,
                   w_out, b_out):
    o1 = D_MODEL
    o2 = o1 + SSM_CONV_CH
    o3 = o2 + SSM_HEADS
    w_in_b = jnp.pad(w_in.astype(BF16), ((0, 0), (0, o2 + TAIL_COLS - w_in.shape[1])))
    tail_bias = jnp.pad(b_qkv, (QKV_SHIFT, TAIL_COLS - QKV_SHIFT - b_qkv.shape[0]))
    u_z = _matmul(xb, w_in_b, n=o1)
    u_xbc = _matmul(xb, w_in_b, col_start=o1, n=SSM_CONV_CH)
    u_dt = _matmul(x2, w_in[:, o2:o3], precision=HIGHEST)
    u_tail = _matmul(xb, w_in_b, tail_bias, col_start=o2, n=TAIL_COLS)
    y_ssm = _ssd_mixer(u_z, u_xbc, u_dt, conv_w, conv_b, dt_bias, a_log, d_skip, norm_w, batch, seq)
    y_att = _swa_attention(u_tail, sinks, batch, seq)
    return _matmul((y_ssm, y_att), w_out.astype(BF16), b_out)


MOE_TILE = 512
MOE_GATHER_TILE = 256
ROUTER_TILE = 256


def _router_kernel(h_ref, wt_ref, b_ref, ids_ref, wts_ref):
    logits = lax.dot_general(wt_ref[...], h_ref[...], (((1,), (1,)), ((), ())),
                             precision=HIGHEST, preferred_element_type=F32) + b_ref[...]
    e = jnp.exp(logits - jnp.max(logits, 0, keepdims=True))
    probs = e / jnp.sum(e, 0, keepdims=True)
    p = [probs[i:i + 1, :] for i in range(N_EXPERTS)]

    keep, score = [], []
    for g in range(N_EXPERT_GROUPS):
        members = range(g * EXPERTS_PER_GROUP, (g + 1) * EXPERTS_PER_GROUP)
        s = jnp.zeros_like(p[0])
        for i in members:
            rank = jnp.zeros_like(p[0])
            for j in members:
                if j != i:
                    beats = (p[j] > p[i]) | ((p[j] == p[i]) & (j < i)) if j < i else (p[j] > p[i])
                    rank = rank + beats.astype(F32)
            keep.append(rank < TOP_K)
            s = s + jnp.where(keep[i], p[i], 0.0)
        score.append(s)
    chosen = []
    for g in range(N_EXPERT_GROUPS):
        c = jnp.ones_like(p[0]) > 0
        for g2 in range(N_EXPERT_GROUPS):
            if g2 < g:
                c = c & (score[g] > score[g2])
            elif g2 > g:
                c = c & (score[g] >= score[g2])
        chosen.append(c)
    denom = jnp.zeros_like(p[0])
    for g in range(N_EXPERT_GROUPS):
        denom = denom + jnp.where(chosen[g], score[g], 0.0)
    count = jnp.zeros_like(p[0])
    id0 = jnp.zeros_like(p[0])
    id1 = jnp.zeros_like(p[0])
    w0 = jnp.zeros_like(p[0])
    w1 = jnp.zeros_like(p[0])
    for i in range(N_EXPERTS):
        sel = keep[i] & chosen[i // EXPERTS_PER_GROUP]
        gate = p[i] / denom
        first = sel & (count == 0.0)
        second = sel & (count == 1.0)
        id0 = jnp.where(first, float(i), id0)
        w0 = jnp.where(first, gate, w0)
        id1 = jnp.where(second, float(i), id1)
        w1 = jnp.where(second, gate, w1)
        count = count + sel.astype(F32)
    ids_ref[0:1, :] = id0.astype(jnp.int32)
    ids_ref[1:2, :] = id1.astype(jnp.int32)
    wts_ref[0:1, :] = w0
    wts_ref[1:2, :] = w1


def _router(h, router_w, router_b):
    t, d = h.shape
    tm = _tile(t, ROUTER_TILE)
    return pl.pallas_call(
        _router_kernel,
        grid=(t // tm,),
        in_specs=[
            pl.BlockSpec((tm, d), lambda i: (i, 0)),
            pl.BlockSpec((N_EXPERTS, d), lambda i: (0, 0)),
            pl.BlockSpec((N_EXPERTS, 1), lambda i: (0, 0)),
        ],
        out_specs=[pl.BlockSpec((TOP_K, tm), lambda i: (0, i)), pl.BlockSpec((TOP_K, tm), lambda i: (0, i))],
        out_shape=[jax.ShapeDtypeStruct((TOP_K, t), jnp.int32), jax.ShapeDtypeStruct((TOP_K, t), F32)],
        compiler_params=_cparams(("parallel",)),
        name="moe_router",
    )(h, router_w.T, router_b.reshape(N_EXPERTS, 1))


def _route_metadata(ids, wts, tm):
    t = ids.shape[1]
    na = TOP_K * t
    eid = ids.T.reshape(na)
    onehot = (eid[:, None] == jnp.arange(N_EXPERTS, dtype=jnp.int32)[None, :]).astype(jnp.int32)
    csum = jnp.cumsum(onehot, axis=0)
    counts = csum[-1]
    rank = jnp.sum(csum * onehot, axis=1) - 1
    padded = (counts + tm - 1) // tm * tm
    pend = jnp.cumsum(padded)
    dest = (pend - padded)[eid] + rank
    n_tiles = na // tm + N_EXPERTS
    n_rows = n_tiles * tm
    row_token = jnp.zeros((n_rows,), jnp.int32).at[dest].set(jnp.arange(na, dtype=jnp.int32) // TOP_K)
    row_weight = jnp.zeros((n_rows,), F32).at[dest].set(wts.T.reshape(na))
    tile_start = jnp.arange(n_tiles, dtype=jnp.int32) * tm
    tile_valid = (tile_start < pend[-1]).astype(jnp.int32)
    last_expert = jnp.max(jnp.where(counts > 0, jnp.arange(N_EXPERTS, dtype=jnp.int32), 0))
    tile_expert = jnp.minimum(jnp.searchsorted(pend, tile_start, side="right").astype(jnp.int32), last_expert)
    experts = jnp.arange(N_EXPERTS, dtype=jnp.int32)
    later = jnp.where((counts > 0)[None, :] & (experts[None, :] > experts[:, None]), experts[None, :], N_EXPERTS)
    next_nonempty = jnp.min(later, axis=1)
    next_nonempty = jnp.where(next_nonempty == N_EXPERTS, tile_expert[0], next_nonempty).astype(jnp.int32)
    prev_expert = jnp.concatenate([jnp.full((1,), -1, jnp.int32), tile_expert[:-1]])
    run_first = (tile_valid == 1) & (tile_expert != prev_expert)
    runs = jnp.stack([tile_expert, tile_valid, run_first.astype(jnp.int32), next_nonempty[tile_expert],
                      (tile_expert == last_expert).astype(jnp.int32)])
    return row_token, row_weight, dest, runs


def _row_copy(src_hbm, tiles_ref, sem, tile, sub, src_row):
    return pltpu.make_async_copy(src_hbm.at[src_row], tiles_ref.at[tile, :, sub, :], sem)


def _start_row_gather(src_hbm, tiles_ref, idx_ref, n, sem):
    def body(tile, carry):
        for sub in range(SUBLANES):
            _row_copy(src_hbm, tiles_ref, sem, tile, sub, idx_ref[0, 0, tile * SUBLANES + sub]).start(
                priority=sub % 2)
        return carry

    lax.fori_loop(0, n // SUBLANES, body, 0)


def _wait_row_gather(src_hbm, tiles_ref, n, sem):
    def body(tile, carry):
        for sub in range(SUBLANES):
            _row_copy(src_hbm, tiles_ref, sem, tile, sub, 0).wait()
        return carry

    lax.fori_loop(0, n // SUBLANES, body, 0)


def _gather_step(src_hbm, slots_ref, idx_ref, idx_next_ref, sems, n, *, more):
    i = pl.program_id(0)
    slot = i % 2

    @pl.when(i == 0)
    def _():
        _start_row_gather(src_hbm, slots_ref.at[0], idx_ref, n, sems.at[0])

    @pl.when(more)
    def _():
        _start_row_gather(src_hbm, slots_ref.at[1 - slot], idx_next_ref, n, sems.at[1 - slot])

    return slot


def _moe_gather_kernel(tv_ref, rt_ref, rt_next_ref, h_hbm, o_ref, xg_ref, sems, *, tm):
    i = pl.program_id(0)
    nxt = jnp.minimum(i + 1, pl.num_programs(0) - 1)
    valid = tv_ref[i] == 1
    slot = _gather_step(h_hbm, xg_ref, rt_ref, rt_next_ref, sems, tm, more=(nxt > i) & (tv_ref[nxt] == 1))

    @pl.when(valid)
    def _():
        _wait_row_gather(h_hbm, xg_ref.at[slot], tm, sems.at[slot])
        o_ref[...] = _tiles_to_rows(xg_ref.at[slot], 0, tm // SUBLANES).astype(o_ref.dtype)

    @pl.when(jnp.logical_not(valid))
    def _():
        o_ref[...] = jnp.zeros_like(o_ref)


RUN_EXPERT, RUN_VALID, RUN_FIRST, RUN_NEXT, RUN_LAST = range(5)


def _stream_expert_weights(runs_ref, w_hbm, stage_ref, wb_ref, sems, *, layer, tn):
    j, i = pl.program_id(0), pl.program_id(1)
    last_pass = j == pl.num_programs(0) - 1

    def copies(expert, col_block):
        col = pl.multiple_of(col_block * tn, tn)
        return [pltpu.make_async_copy(w.at[layer, expert, :, pl.ds(col, tn)], stage_ref.at[m], sems.at[m])
                for m, w in enumerate(w_hbm)]

    @pl.when((j == 0) & (i == 0))
    def _():
        for c in copies(runs_ref[RUN_EXPERT, 0], 0):
            c.start()

    @pl.when(runs_ref[RUN_FIRST, i] == 1)
    def _():
        for c in copies(runs_ref[RUN_EXPERT, i], j):
            c.wait()
        for m in range(len(w_hbm)):
            wb_ref[m] = stage_ref[m].astype(BF16)
        last_run = runs_ref[RUN_LAST, i] == 1

        @pl.when(jnp.logical_not(last_run & last_pass))
        def _():
            for c in copies(runs_ref[RUN_NEXT, i], jnp.where(last_run, j + 1, j)):
                c.start()


def _moe_up_kernel(runs_ref, x_ref, wg_hbm, wu_hbm, o_ref, stage_ref, wb_ref, sems, *, layer, tn):
    _stream_expert_weights(runs_ref, (wg_hbm, wu_hbm), stage_ref, wb_ref, sems, layer=layer, tn=tn)
    valid = runs_ref[RUN_VALID, pl.program_id(1)] == 1

    @pl.when(valid)
    def _():
        xt = x_ref[...]
        gate = jnp.dot(xt, wb_ref[0], preferred_element_type=F32)
        up = jnp.dot(xt, wb_ref[1], preferred_element_type=F32)
        o_ref[...] = (_silu(gate) * up).astype(o_ref.dtype)

    @pl.when(jnp.logical_not(valid))
    def _():
        o_ref[...] = jnp.zeros_like(o_ref)


def _moe_down_kernel(runs_ref, he_ref, wd_hbm, rw_ref, o_ref, stage_ref, wb_ref, sems, *, layer, tn):
    _stream_expert_weights(runs_ref, (wd_hbm,), stage_ref, wb_ref, sems, layer=layer, tn=tn)
    valid = runs_ref[RUN_VALID, pl.program_id(1)] == 1

    @pl.when(valid)
    def _():
        _to_token_major(o_ref, jnp.dot(he_ref[...], wb_ref[0], preferred_element_type=F32) * rw_ref[...])

    @pl.when(jnp.logical_not(valid))
    def _():
        o_ref[...] = jnp.zeros_like(o_ref)


def _moe_combine_ln_kernel(pos_ref, pos_next_ref, h_ref, y_hbm, g_ref, b_ref, o_ref, ob_ref, yg_ref, sems, *, tm):
    n = TOP_K * tm
    slot = _gather_step(y_hbm, yg_ref, pos_ref, pos_next_ref, sems, n,
                        more=pl.program_id(0) + 1 < pl.num_programs(0))
    _wait_row_gather(y_hbm, yg_ref.at[slot], n, sems.at[slot])
    per = tm // SUBLANES
    ffn = _tiles_to_rows(yg_ref.at[slot], 0, per) + _tiles_to_rows(yg_ref.at[slot], per, TOP_K * per)
    r = _layer_norm_rows(ALPHA * h_ref[...] + ffn, g_ref[...], b_ref[...])
    o_ref[...] = r
    ob_ref[...] = r.astype(BF16)


def _moe_block(h, h3, router_w, router_b, w_gate, w_up, w_down, layer, ln_g, ln_b, *, tm=MOE_TILE, tn_up=512,
               tn_down=2048, tm_out=128):
    t, d = h.shape
    slabs = d // LANES
    ids, wts = _router(h, router_w, router_b)
    row_token, row_weight, dest, runs = _route_metadata(ids, wts, tm)
    tile_valid = runs[RUN_VALID]
    n_rows = row_token.shape[0]
    n_tiles = n_rows // tm
    de = w_gate.shape[-1]
    tn_up, tn_down = _tile(de, tn_up), _tile(d, tn_down)

    def gather_scratch(n):
        return [pltpu.VMEM((2, n // SUBLANES, slabs, SUBLANES, LANES), F32), pltpu.SemaphoreType.DMA((2,))]

    tg = _tile(tm, MOE_GATHER_TILE)
    per = tm // tg
    n_gather = n_rows // tg
    row_token3 = row_token.reshape(n_gather, 1, tg)
    x_rows = pl.pallas_call(
        functools.partial(_moe_gather_kernel, tm=tg),
        grid_spec=pltpu.PrefetchScalarGridSpec(
            num_scalar_prefetch=1,
            grid=(n_gather,),
            in_specs=[
                pl.BlockSpec((1, 1, tg), lambda i, tv: (i, 0, 0), memory_space=pltpu.SMEM),
                pl.BlockSpec((1, 1, tg), lambda i, tv: (jnp.minimum(i + 1, n_gather - 1), 0, 0),
                             memory_space=pltpu.SMEM),
                pl.BlockSpec(memory_space=pl.ANY),
            ],
            out_specs=pl.BlockSpec((tg, d), lambda i, tv: (i, 0)),
            scratch_shapes=gather_scratch(tg),
        ),
        out_shape=jax.ShapeDtypeStruct((n_rows, d), BF16),
        compiler_params=_cparams(("arbitrary",)),
        name="moe_gather",
    )(jnp.repeat(tile_valid, per), row_token3, row_token3, h3)

    def weight_scratch(n_mats, k, tn):
        return [pltpu.VMEM((n_mats, k, tn), F32), pltpu.VMEM((n_mats, k, tn), BF16),
                pltpu.SemaphoreType.DMA((n_mats,))]

    he = pl.pallas_call(
        functools.partial(_moe_up_kernel, layer=layer, tn=tn_up),
        grid_spec=pltpu.PrefetchScalarGridSpec(
            num_scalar_prefetch=1,
            grid=(de // tn_up, n_tiles),
            in_specs=[
                pl.BlockSpec((tm, d), lambda j, i, runs: (i, 0)),
                pl.BlockSpec(memory_space=pl.ANY),
                pl.BlockSpec(memory_space=pl.ANY),
            ],
            out_specs=pl.BlockSpec((tm, tn_up), lambda j, i, runs: (i, j)),
            scratch_shapes=weight_scratch(2, d, tn_up),
        ),
        out_shape=jax.ShapeDtypeStruct((n_rows, de), BF16),
        compiler_params=_cparams(("arbitrary", "arbitrary")),
        name="moe_gate_up",
    )(runs, x_rows, w_gate, w_up)

    y_rows = pl.pallas_call(
        functools.partial(_moe_down_kernel, layer=layer, tn=tn_down),
        grid_spec=pltpu.PrefetchScalarGridSpec(
            num_scalar_prefetch=1,
            grid=(d // tn_down, n_tiles),
            in_specs=[
                pl.BlockSpec((tm, de), lambda j, i, runs: (i, 0)),
                pl.BlockSpec(memory_space=pl.ANY),
                pl.BlockSpec((tm, 1), lambda j, i, runs: (i, 0)),
            ],
            out_specs=pl.BlockSpec((tm, tn_down // LANES, LANES), lambda j, i, runs: (i, j, 0)),
            scratch_shapes=weight_scratch(1, de, tn_down),
        ),
        out_shape=jax.ShapeDtypeStruct((n_rows, slabs, LANES), F32),
        compiler_params=_cparams(("arbitrary", "arbitrary")),
        name="moe_down",
    )(runs, he, w_down, row_weight.reshape(n_rows, 1))

    tm_out = _tile(t, tm_out)
    n_out = t // tm_out
    pos = dest.reshape(n_out, tm_out, TOP_K).transpose(0, 2, 1).reshape(n_out, 1, TOP_K * tm_out)
    row = pl.BlockSpec((tm_out, d), lambda i: (i, 0))
    vec = pl.BlockSpec((1, d), lambda i: (0, 0))
    return pl.pallas_call(
        functools.partial(_moe_combine_ln_kernel, tm=tm_out),
        grid=(n_out,),
        in_specs=[
            pl.BlockSpec((1, 1, TOP_K * tm_out), lambda i: (i, 0, 0), memory_space=pltpu.SMEM),
            pl.BlockSpec((1, 1, TOP_K * tm_out), lambda i: (jnp.minimum(i + 1, n_out - 1), 0, 0),
                         memory_space=pltpu.SMEM),
            row,
            pl.BlockSpec(memory_space=pl.ANY),
            vec, vec,
        ],
        out_specs=[row, row],
        out_shape=[jax.ShapeDtypeStruct((t, d), F32), jax.ShapeDtypeStruct((t, d), BF16)],
        scratch_shapes=gather_scratch(TOP_K * tm_out),
        compiler_params=_cparams(("arbitrary",)),
        name="moe_combine_ln",
    )(pos, pos, h, y_rows, ln_g.reshape(1, d), ln_b.reshape(1, d))


def _shift_mix_kernel(x_ref, xp_ref, mix_ref, o_ref, pad_ref, *, tm, seq):
    i = pl.program_id(0)
    starts_sequence = (i * tm) % seq == 0
    pad_ref[0:SUBLANES, :] = jnp.where(starts_sequence, 0.0, xp_ref[...])
    pad_ref[SUBLANES:SUBLANES + tm, :] = x_ref[...]
    xv = x_ref[...]
    xx = pad_ref[SUBLANES - 1:SUBLANES - 1 + tm, :] - xv
    for m in range(mix_ref.shape[0]):
        o_ref[m] = (xv + xx * mix_ref[m:m + 1, :]).astype(o_ref.dtype)


def _shift_mix(x, mix, seq, *, tm=128):
    t, d = x.shape
    nm = mix.shape[0]
    tm = _tile(seq, tm)
    per = tm // SUBLANES
    return pl.pallas_call(
        functools.partial(_shift_mix_kernel, tm=tm, seq=seq),
        grid=(t // tm,),
        in_specs=[
            pl.BlockSpec((tm, d), lambda i: (i, 0)),
            pl.BlockSpec((SUBLANES, d), lambda i: (jnp.maximum(i * per - 1, 0), 0)),
            pl.BlockSpec((nm, d), lambda i: (0, 0)),
        ],
        out_specs=pl.BlockSpec((nm, tm, d), lambda i: (0, i, 0)),
        out_shape=jax.ShapeDtypeStruct((nm, t, d), BF16),
        scratch_shapes=[pltpu.VMEM((SUBLANES + tm, d), F32)],
        compiler_params=_cparams(("parallel",)),
        name="rwkv_shift_mix",
    )(x, x, mix)


WKV_CHUNK = 64
WKV_HEADS = 4
WKV_WIDTH = WKV_HEADS * RWKV_HEAD
WKV_BLOCK = 512
NEUMANN_STEPS = 5


def _wkv_kernel(r_ref, k_ref, v_ref, wl_ref, ar_ref, g_ref, kk_ref, ka_ref, rk_ref, lnw_ref, lnb_ref,
                o_ref, state_ref, *, tb):
    lc, hw, wd = WKV_CHUNK, RWKV_HEAD, WKV_WIDTH
    nch = tb // lc

    @pl.when(pl.program_id(2) == 0)
    def _():
        state_ref[...] = jnp.zeros_like(state_ref)

    wrow = lax.broadcasted_iota(jnp.int32, (wd, wd), 0)
    wcol = lax.broadcasted_iota(jnp.int32, (wd, wd), 1)
    same_head = wrow // hw == wcol // hw
    ones_bd = same_head.astype(BF16)

    def block_diag(m):
        return jnp.where(same_head, jnp.concatenate([m] * WKV_HEADS, axis=0), 0.0).astype(BF16)

    def head_sum(m):
        hi = m.astype(BF16)
        lo = (m - hi.astype(F32)).astype(BF16)
        return jnp.dot(hi, ones_bd, preferred_element_type=F32) + jnp.dot(lo, ones_bd, preferred_element_type=F32)

    def mm(a, b):
        return jnp.dot(a.astype(BF16), b, preferred_element_type=F32)

    r = r_ref[...]
    k = k_ref[...]
    v = v_ref[...]
    w_log = -_softplus(-wl_ref[...]) - 0.5
    lw = -jnp.exp(w_log)
    a_sig = jax.nn.sigmoid(ar_ref[...])
    kx = k * kk_ref[...]
    kk = kx / jnp.maximum(jnp.sqrt(head_sum(kx * kx)), 1e-12)
    k2 = k * (1.0 + (a_sig - 1.0) * ka_ref[...])
    a_s = -kk
    b_s = kk * a_sig

    trow = lax.broadcasted_iota(jnp.int32, (tb, tb), 0)
    tcol = lax.broadcasted_iota(jnp.int32, (tb, tb), 1)
    same_chunk = trow // lc == tcol // lc
    cum = _dot_split3(lw, (same_chunk & (trow >= tcol)).astype(BF16), split_lhs=False)
    tot = jnp.concatenate([jnp.broadcast_to(cum[(c + 1) * lc - 1:(c + 1) * lc, :], (lc, wd)) for c in range(nch)],
                          axis=0)
    grow = jnp.exp(-cum)
    rt = r * jnp.exp(cum)
    at = a_s * jnp.exp(cum - lw)
    bt = b_s * grow
    kt = k2 * grow
    rest = jnp.exp(tot - cum)
    bh = b_s * rest
    kh = k2 * rest
    p_end = jnp.exp(tot)

    t_idx = lax.broadcasted_iota(jnp.int32, (lc, wd), 0)
    s_idx = lax.broadcasted_iota(jnp.int32, (lc, wd), 1) % hw
    strict = t_idx > s_idx
    incl = t_idx >= s_idx
    eye = (t_idx == s_idx).astype(F32)
    nt = (((1,), (1,)), ((), ()))

    tn = (((0,), (0,)), ((), ()))
    chunks = [slice(c * lc, (c + 1) * lc) for c in range(nch)]

    a_ab, a_rb, a_ak, a_rk = [], [], [], []
    for sl in chunks:
        ar = jnp.concatenate([at[sl], rt[sl]], axis=0).astype(BF16)
        xb = lax.dot_general(ar, block_diag(bt[sl]), nt, preferred_element_type=F32)
        xk = lax.dot_general(ar, block_diag(kt[sl]), nt, preferred_element_type=F32)
        a_ab.append(jnp.where(strict, xb[:lc], 0.0))
        a_rb.append(jnp.where(incl, xb[lc:], 0.0))
        a_ak.append(jnp.where(strict, xk[:lc], 0.0))
        a_rk.append(jnp.where(incl, xk[lc:], 0.0))
    inv = [eye + a for a in a_ab]
    pw = [mm(a, block_diag(a)) for a in a_ab]
    for step in range(1, NEUMANN_STEPS + 1):
        last = step == NEUMANN_STEPS
        lhs = inv if last else [jnp.concatenate([t, p], axis=0) for t, p in zip(inv, pw)]
        prod = [mm(x, block_diag(p)) for x, p in zip(lhs, pw)]
        inv = [t + q[:lc] for t, q in zip(inv, prod)]
        if not last:
            pw = [q[lc:] for q in prod]
    v_bd = [block_diag(v[sl]) for sl in chunks]
    ta = [mm(t, block_diag(at[sl])) for t, sl in zip(inv, chunks)]
    av = [mm(jnp.concatenate([ak, rk], axis=0), vb) for ak, rk, vb in zip(a_ak, a_rk, v_bd)]
    y0 = [x[lc:] for x in av]
    u0 = [mm(t, block_diag(x[:lc])) for t, x in zip(inv, av)]
    gain, add = [], []
    for c, sl in enumerate(chunks):
        bh_b = bh[sl].astype(BF16)
        gain.append(jnp.where(same_head, lax.dot_general(ta[c].astype(BF16), bh_b, tn, preferred_element_type=F32),
                              0.0).astype(BF16))
        uv = jnp.concatenate([u0[c], v[sl]], axis=0).astype(BF16)
        bk = jnp.concatenate([bh_b, kh[sl].astype(BF16)], axis=0)
        add.append(jnp.where(same_head, lax.dot_general(uv, bk, tn, preferred_element_type=F32), 0.0))

    states = [state_ref[...]]
    for c in range(nch):
        s0 = states[-1]
        states.append(s0 * p_end[c * lc:c * lc + 1, :] + mm(s0, gain[c]) + add[c])
    state_ref[...] = states[nch]

    ys = []
    for c, sl in enumerate(chunks):
        tr = jnp.concatenate([ta[c], rt[sl]], axis=0).astype(BF16)
        xs = lax.dot_general(tr, states[c].astype(BF16), nt, preferred_element_type=F32)
        u = xs[:lc] + u0[c]
        ys.append(xs[lc:] + mm(a_rb[c], block_diag(u)) + y0[c])

    y = jnp.concatenate(ys, axis=0)
    mu = head_sum(y) / hw
    dev = y - mu
    var = head_sum(dev * dev) / hw
    yn = dev * lax.rsqrt(var + RWKV_GN_EPS) * lnw_ref[...] + lnb_ref[...]
    bonus = head_sum(r * k2 * rk_ref[...]) * v
    o_ref[...] = ((yn + bonus) * g_ref[...]).astype(o_ref.dtype)


def _wkv(r, k, v, wl, araw, g, k_k, k_a, r_k, ln_w, ln_b, batch, seq):
    t, d = r.shape
    tb = _tile(seq, WKV_BLOCK)
    nt = seq // tb
    wd = WKV_WIDTH
    rows = pl.BlockSpec((tb, wd), lambda b, h, n: (b * nt + n, h))
    vec = pl.BlockSpec((1, wd), lambda b, h, n: (0, h))
    return pl.pallas_call(
        functools.partial(_wkv_kernel, tb=tb),
        grid=(batch, d // wd, nt),
        in_specs=[rows] * 6 + [vec] * 5,
        out_specs=rows,
        out_shape=jax.ShapeDtypeStruct((t, d), BF16),
        scratch_shapes=[pltpu.VMEM((wd, wd), F32)],
        compiler_params=_cparams(("parallel", "parallel", "arbitrary")),
        name="rwkv_wkv",
    )(r, k, v, wl, araw, g, k_k.reshape(1, d), k_a.reshape(1, d), r_k.reshape(1, d),
      ln_w.reshape(1, d), ln_b.reshape(1, d))


def _rwkv7_time_mix(x2, batch, seq, mix, w_r, w_k, w_v, w_o, w0, w1, w2, a0, a1, a2, g1, g2, k_k, k_a, r_k,
                    ln_w, ln_b):
    xm = _shift_mix(x2, mix, seq)
    bf = lambda w: w.astype(BF16)
    r = _matmul(xm[0], bf(w_r))
    k = _matmul(xm[2], bf(w_k))
    v = _matmul(xm[3], bf(w_v))
    wl = _low_rank(xm[1], bf(w1), bf(w2), w0, act="tanh")
    araw = _low_rank(xm[4], bf(a1), bf(a2), a0)
    g = _low_rank(xm[5], bf(g1), bf(g2), act="sigmoid")
    yg = _wkv(r, k, v, wl, araw, g, k_k, k_a, r_k, ln_w, ln_b, batch, seq)
    return _matmul(yg, bf(w_o))


def kernel(x, ab_w_in, ab_b_qkv, ssm_conv_w, ssm_conv_b, ssm_dt_bias, ssm_a_log, ssm_d, ssm_norm_w, attn_sinks,
           ab_w_out, ab_b_out, rwkv_mix, rwkv_w_r, rwkv_w_k, rwkv_w_v, rwkv_w_o, rwkv_w0, rwkv_w1, rwkv_w2,
           rwkv_a0, rwkv_a1, rwkv_a2, rwkv_g1, rwkv_g2, rwkv_k_k, rwkv_k_a, rwkv_r_k, rwkv_ln_w, rwkv_ln_b,
           ln_mix_g, ln_mix_b, ln_ffn_g, ln_ffn_b, router_w, router_b, moe_w_gate, moe_w_up, moe_w_down):
    batch, seq, d = x.shape
    x2 = x.reshape(batch * seq, d)
    xb = x2.astype(BF16)
    for layer in range(DEPTH):
        i = layer // 2
        if layer % 2 == 0:
            mix = _ssd_swa_mixer(x2, xb, batch, seq, ab_w_in[i], ab_b_qkv[i], ssm_conv_w[i], ssm_conv_b[i],
                                 ssm_dt_bias[i], ssm_a_log[i], ssm_d[i], ssm_norm_w[i], attn_sinks[i],
                                 ab_w_out[i], ab_b_out[i])
        else:
            mix = _rwkv7_time_mix(x2, batch, seq, rwkv_mix[i], rwkv_w_r[i], rwkv_w_k[i], rwkv_w_v[i], rwkv_w_o[i],
                                  rwkv_w0[i], rwkv_w1[i], rwkv_w2[i], rwkv_a0[i], rwkv_a1[i], rwkv_a2[i],
                                  rwkv_g1[i], rwkv_g2[i], rwkv_k_k[i], rwkv_k_a[i], rwkv_r_k[i].reshape(-1),
                                  rwkv_ln_w[i], rwkv_ln_b[i])
        h, h3 = _add_layer_norm(x2, mix, ln_mix_g[layer], ln_mix_b[layer])
        x2, xb = _moe_block(h, h3, router_w, router_b, moe_w_gate, moe_w_up, moe_w_down, layer,
                            ln_ffn_g[layer], ln_ffn_b[layer])
    return x2.reshape(batch, seq, d)
```

```python
import functools
import math

import jax
import jax.numpy as jnp
import numpy as np
from jax import lax
from jax.experimental import pallas as pl
from jax.experimental.pallas import tpu as pltpu

F32 = jnp.float32
BF16 = jnp.bfloat16
HIGHEST = lax.Precision.HIGHEST

D_MODEL = 4096
DEPTH = 2
SSM_HEAD_DIM = 64
SSM_HEADS = 64
SSM_GROUPS = 8
SSM_HEADS_PER_GROUP = 8
SSM_STATE = 128
SSM_CONV = 4
SSM_CHUNK = 128
SSM_GROUP_WIDTH = SSM_HEADS_PER_GROUP * SSM_HEAD_DIM
SSM_CONV_CH = D_MODEL + 2 * SSM_GROUPS * SSM_STATE
SSM_NORM_EPS = 1e-5
ATT_HEADS = 64
ATT_KV_HEADS = 8
ATT_HEAD_DIM = 64
ATT_REP = 8
ATT_WINDOW = 128
ROPE_DIM = 16
ROPE_THETA = 500000.0
Q_COLS = 4096
KV_COLS = 512
RWKV_HEAD = 64
RWKV_GN_EPS = 64e-5
N_EXPERTS = 16
N_EXPERT_GROUPS = 4
EXPERTS_PER_GROUP = 4
TOP_K = 2
D_EXPERT = 1536
ALPHA = (2 * DEPTH) ** 0.25
LN_EPS = 1e-5

VMEM_LIMIT_BYTES = 56 * 1024 * 1024
LANES = 128
SUBLANES = 8


def _cparams(semantics):
    return pltpu.CompilerParams(dimension_semantics=semantics, vmem_limit_bytes=VMEM_LIMIT_BYTES)


def _tile(dim, pref):
    if dim <= pref:
        return dim
    t = pref
    while dim % t:
        t //= 2
    return t


def _silu(v):
    return v * jax.nn.sigmoid(v)


def _softplus(v):
    return jnp.maximum(v, 0.0) + jnp.log1p(jnp.exp(-jnp.abs(v)))


def _split3(v):
    hi = v.astype(BF16)
    rest = v - hi.astype(F32)
    mid = rest.astype(BF16)
    lo = (rest - mid.astype(F32)).astype(BF16)
    return hi, mid, lo


def _dot_split3(v, m01, *, split_lhs):
    if split_lhs:
        return sum(jnp.dot(t, m01, preferred_element_type=F32) for t in _split3(v))
    return sum(jnp.dot(m01, t, preferred_element_type=F32) for t in _split3(v))


def _mm_kernel(*refs, act, precision):
    a_refs, (w_ref, b_ref, o_ref) = refs[:-3], refs[-3:]
    r = b_ref[...]
    k0 = 0
    for a_ref in a_refs:
        k1 = k0 + a_ref.shape[1]
        r = r + jnp.dot(a_ref[...], w_ref[k0:k1, :], preferred_element_type=F32, precision=precision)
        k0 = k1
    if act == "tanh":
        r = jnp.tanh(r)
    elif act == "sigmoid":
        r = jax.nn.sigmoid(r)
    o_ref[...] = r.astype(o_ref.dtype)


MATMUL_VMEM_BUDGET = 44 * 1024 * 1024


def _matmul(a, w, bias=None, *, act=None, out_dtype=F32, precision=None, tm=1024, col_start=0, n=None):
    panels = a if isinstance(a, (tuple, list)) else (a,)
    m = panels[0].shape[0]
    kdim = sum(p.shape[1] for p in panels)
    n = w.shape[1] if n is None else n
    tm = _tile(m, tm)
    out_bytes = jnp.dtype(out_dtype).itemsize
    for tn in (512, 256, LANES):
        tn = _tile(n, tn)
        need = 2 * (tm * kdim * panels[0].dtype.itemsize + kdim * tn * w.dtype.itemsize + tm * tn * out_bytes)
        if need <= MATMUL_VMEM_BUDGET:
            break
    assert need <= MATMUL_VMEM_BUDGET and col_start % tn == 0 and w.shape[0] == kdim
    col0 = col_start // tn
    if bias is None:
        bias = jnp.zeros((n,), F32)
    bias = bias.reshape(1, n).astype(F32)
    return pl.pallas_call(
        functools.partial(_mm_kernel, act=act, precision=precision),
        grid=(m // tm, n // tn),
        in_specs=[pl.BlockSpec((tm, p.shape[1]), lambda i, j: (i, 0)) for p in panels] + [
            pl.BlockSpec((kdim, tn), lambda i, j: (0, col0 + j)),
            pl.BlockSpec((1, tn), lambda i, j: (0, j)),
        ],
        out_specs=pl.BlockSpec((tm, tn), lambda i, j: (i, j)),
        out_shape=jax.ShapeDtypeStruct((m, n), out_dtype),
        compiler_params=_cparams(("parallel", "arbitrary")),
        name="matmul",
    )(*panels, w, bias)


def _low_rank_kernel(a_ref, w1_ref, w2_ref, b_ref, o_ref, *, act):
    t = jnp.dot(a_ref[...], w1_ref[...], preferred_element_type=F32)
    if act == "tanh":
        t = jnp.tanh(t)
    elif act == "sigmoid":
        t = jax.nn.sigmoid(t)
    o_ref[...] = jnp.dot(t.astype(BF16), w2_ref[...], preferred_element_type=F32) + b_ref[...]


def _low_rank(a, w1, w2, bias=None, *, act=None, tm=512):
    m, kdim = a.shape
    r, n = w2.shape
    tm = _tile(m, tm)
    if bias is None:
        bias = jnp.zeros((n,), F32)
    return pl.pallas_call(
        functools.partial(_low_rank_kernel, act=act),
        grid=(m // tm,),
        in_specs=[
            pl.BlockSpec((tm, kdim), lambda i: (i, 0)),
            pl.BlockSpec((kdim, r), lambda i: (0, 0)),
            pl.BlockSpec((r, n), lambda i: (0, 0)),
            pl.BlockSpec((1, n), lambda i: (0, 0)),
        ],
        out_specs=pl.BlockSpec((tm, n), lambda i: (i, 0)),
        out_shape=jax.ShapeDtypeStruct((m, n), F32),
        compiler_params=_cparams(("parallel",)),
        name="low_rank",
    )(a, w1, w2, bias.reshape(1, n).astype(F32))


def _layer_norm_rows(v, g, b):
    mu = jnp.mean(v, -1, keepdims=True)
    var = jnp.mean(jnp.square(v - mu), -1, keepdims=True)
    return (v - mu) * lax.rsqrt(var + LN_EPS) * g + b


def _to_token_major(o3_ref, v):
    for s in range(v.shape[1] // LANES):
        o3_ref[:, s, :] = v[:, s * LANES:(s + 1) * LANES].astype(o3_ref.dtype)


def _tiles_to_rows(tiles_ref, t0, t1):
    return jnp.concatenate([tiles_ref[t0:t1, s].reshape((t1 - t0) * SUBLANES, LANES)
                            for s in range(tiles_ref.shape[1])], axis=1)


def _add_ln_kernel(x_ref, y_ref, g_ref, b_ref, o_ref, o3_ref):
    r = _layer_norm_rows(ALPHA * x_ref[...] + y_ref[...], g_ref[...], b_ref[...])
    o_ref[...] = r
    _to_token_major(o3_ref, r)


def _add_layer_norm(x, y, g, b, *, tm=128):
    t, d = x.shape
    tm = _tile(t, tm)
    row = pl.BlockSpec((tm, d), lambda i: (i, 0))
    row3 = pl.BlockSpec((tm, d // LANES, LANES), lambda i: (i, 0, 0))
    vec = pl.BlockSpec((1, d), lambda i: (0, 0))
    return pl.pallas_call(
        _add_ln_kernel,
        grid=(t // tm,),
        in_specs=[row, row, vec, vec],
        out_specs=[row, row3],
        out_shape=[jax.ShapeDtypeStruct((t, d), F32), jax.ShapeDtypeStruct((t, d // LANES, LANES), F32)],
        compiler_params=_cparams(("parallel",)),
        name="add_layer_norm",
    )(x, y, g.reshape(1, d), b.reshape(1, d))


CONV_HALO = SUBLANES
SSD_CH = SSM_GROUP_WIDTH + 2 * SSM_STATE


def _ssd_kernel(z_ref, x_ref, b_ref, c_ref, wx_ref, wb_ref, wc_ref, bx_ref, bb_ref, bc_ref,
                dtr_ref, dbc_ref, dbr_ref, alc_ref, alr_ref, d_ref, nw_ref,
                o_ref, state_ref, pad_ref):
    ch, gw, ns, hd = SSM_CHUNK, SSM_GROUP_WIDTH, SSM_STATE, SSM_HEAD_DIM

    @pl.when(pl.program_id(2) == 0)
    def _():
        state_ref[...] = jnp.zeros_like(state_ref)
        pad_ref[0:CONV_HALO, :] = jnp.zeros((CONV_HALO, SSD_CH), F32)

    pad_ref[CONV_HALO:CONV_HALO + ch, 0:gw] = x_ref[...]
    pad_ref[CONV_HALO:CONV_HALO + ch, gw:gw + ns] = b_ref[...]
    pad_ref[CONV_HALO:CONV_HALO + ch, gw + ns:SSD_CH] = c_ref[...]
    w = jnp.concatenate([wx_ref[...], wb_ref[...], wc_ref[...]], axis=1)
    acc = jnp.concatenate([bx_ref[...], bb_ref[...], bc_ref[...]], axis=1)
    base = CONV_HALO - (SSM_CONV - 1)
    for k in range(SSM_CONV):
        acc = acc + w[k:k + 1, :] * pad_ref[base + k:base + k + ch, :]
    pad_ref[0:CONV_HALO, :] = pad_ref[ch:ch + CONV_HALO, :]
    xbc = _silu(acc)
    xc, bm, cm = xbc[:, :gw], xbc[:, gw:gw + ns], xbc[:, gw + ns:]

    row = lax.broadcasted_iota(jnp.int32, (ch, ch), 0)
    col = lax.broadcasted_iota(jnp.int32, (ch, ch), 1)
    causal = row >= col
    dt_raw_r = dtr_ref[...]
    eye = (row == col).astype(BF16)
    dt_raw_c = sum(lax.dot_general(eye, term, (((1,), (1,)), ((), ())), preferred_element_type=F32)
                   for term in _split3(dt_raw_r))
    dt_c = _softplus(dt_raw_c + dbc_ref[0])
    dt_r = _softplus(dt_raw_r + dbr_ref[...])
    a_c = -jnp.exp(alc_ref[0])
    a_r = -jnp.exp(alr_ref[...])
    acs_c = _dot_split3(dt_c * a_c, causal.astype(BF16), split_lhs=False)
    acs_r = _dot_split3(dt_r * a_r, (row <= col).astype(BF16), split_lhs=True)

    hrow = lax.broadcasted_iota(jnp.int32, (SSM_HEADS_PER_GROUP, gw), 0)
    hcol = lax.broadcasted_iota(jnp.int32, (SSM_HEADS_PER_GROUP, gw), 1)
    expand = (hcol // hd == hrow).astype(BF16)
    dt_x = _dot_split3(dt_c, expand, split_lhs=True)
    acs_x = _dot_split3(acs_c, expand, split_lhs=True)

    xdt = xc * dt_x
    xdt_b = xdt.astype(BF16)
    bm_b, cm_b = bm.astype(BF16), cm.astype(BF16)
    cb = lax.dot_general(cm_b, bm_b, (((1,), (1,)), ((), ())), preferred_element_type=F32)
    prev = state_ref[...]
    y_off = jnp.dot(cm_b, prev.astype(BF16), preferred_element_type=F32) * jnp.exp(acs_x)

    lane = lax.broadcasted_iota(jnp.int32, (ch, 2 * hd), 1)
    pairs = []
    for pr in range(SSM_HEADS_PER_GROUP // 2):
        xp = xdt_b[:, pr * 2 * hd:(pr + 1) * 2 * hd]
        halves = []
        for q in range(2):
            r = 2 * pr + q
            diff = acs_c[:, r:r + 1] - acs_r[r:r + 1, :]
            seg = jnp.where(causal, jnp.exp(jnp.where(causal, diff, 0.0)), 0.0)
            halves.append(jnp.dot((cb * seg).astype(BF16), xp, preferred_element_type=F32))
        pairs.append(jnp.where(lane < hd, halves[0], halves[1]))
    y = jnp.concatenate(pairs, axis=1) + y_off + d_ref[...] * xc

    acs_last = acs_x[ch - 1:ch, :]
    contrib = lax.dot_general(bm_b, (xdt * jnp.exp(acs_last - acs_x)).astype(BF16),
                              (((0,), (0,)), ((), ())), preferred_element_type=F32)
    state_ref[...] = prev * jnp.exp(acs_last) + contrib

    u = y * _silu(z_ref[...])
    u = u * lax.rsqrt(jnp.mean(u * u, -1, keepdims=True) + SSM_NORM_EPS)
    o_ref[...] = (u * nw_ref[...]).astype(o_ref.dtype)


def _ssd_mixer(u_z, u_xbc, u_dt, conv_w, conv_b, dt_bias, a_log, d_skip, norm_w, batch, seq):
    t = batch * seq
    nc = seq // SSM_CHUNK
    g, r, gw, ns = SSM_GROUPS, SSM_HEADS_PER_GROUP, SSM_GROUP_WIDTH, SSM_STATE
    xblocks = D_MODEL // ns
    conv_wt = conv_w.T
    conv_b2 = conv_b.reshape(1, SSM_CONV_CH)
    dt_row = u_dt.T
    rows = lambda b, gi, c: (b * nc + c, gi)
    in_specs = [
        pl.BlockSpec((SSM_CHUNK, gw), rows),
        pl.BlockSpec((SSM_CHUNK, gw), rows),
        pl.BlockSpec((SSM_CHUNK, ns), lambda b, gi, c: (b * nc + c, xblocks + gi)),
        pl.BlockSpec((SSM_CHUNK, ns), lambda b, gi, c: (b * nc + c, xblocks + g + gi)),
        pl.BlockSpec((SSM_CONV, gw), lambda b, gi, c: (0, gi)),
        pl.BlockSpec((SSM_CONV, ns), lambda b, gi, c: (0, xblocks + gi)),
        pl.BlockSpec((SSM_CONV, ns), lambda b, gi, c: (0, xblocks + g + gi)),
        pl.BlockSpec((1, gw), lambda b, gi, c: (0, gi)),
        pl.BlockSpec((1, ns), lambda b, gi, c: (0, xblocks + gi)),
        pl.BlockSpec((1, ns), lambda b, gi, c: (0, xblocks + g + gi)),
        pl.BlockSpec((r, SSM_CHUNK), lambda b, gi, c: (gi, b * nc + c)),
        pl.BlockSpec((1, 1, r), lambda b, gi, c: (gi, 0, 0)),
        pl.BlockSpec((r, 1), lambda b, gi, c: (gi, 0)),
        pl.BlockSpec((1, 1, r), lambda b, gi, c: (gi, 0, 0)),
        pl.BlockSpec((r, 1), lambda b, gi, c: (gi, 0)),
        pl.BlockSpec((1, gw), lambda b, gi, c: (0, gi)),
        pl.BlockSpec((1, gw), lambda b, gi, c: (0, gi)),
    ]
    return pl.pallas_call(
        _ssd_kernel,
        grid=(batch, g, nc),
        in_specs=in_specs,
        out_specs=pl.BlockSpec((SSM_CHUNK, gw), rows),
        out_shape=jax.ShapeDtypeStruct((t, D_MODEL), BF16),
        scratch_shapes=[pltpu.VMEM((ns, gw), F32), pltpu.VMEM((CONV_HALO + SSM_CHUNK, SSD_CH), F32)],
        compiler_params=_cparams(("parallel", "parallel", "arbitrary")),
        name="ssd_mixer",
    )(u_z, u_xbc, u_xbc, u_xbc, conv_wt, conv_wt, conv_wt, conv_b2, conv_b2, conv_b2,
      dt_row, dt_bias.reshape(g, 1, r), dt_bias.reshape(SSM_HEADS, 1),
      a_log.reshape(g, 1, r), a_log.reshape(SSM_HEADS, 1),
      jnp.repeat(d_skip, SSM_HEAD_DIM).reshape(1, D_MODEL), norm_w.reshape(1, D_MODEL))


def _rotary_tables(seq):
    half = ROPE_DIM // 2
    inv_freq = ROPE_THETA ** (-jnp.arange(half, dtype=F32) / half)
    ang = jnp.arange(seq, dtype=F32)[:, None] * inv_freq[None, :]
    ones = jnp.ones((seq, ATT_HEAD_DIM - ROPE_DIM), F32)
    cos = jnp.concatenate([jnp.cos(ang), jnp.cos(ang), ones], -1)
    sin = jnp.concatenate([jnp.sin(ang), jnp.sin(ang), 0.0 * ones], -1)
    return jnp.tile(cos, (1, ATT_REP)), jnp.tile(sin, (1, ATT_REP))


def _rotate_half_matrix():
    width = ATT_REP * ATT_HEAD_DIM
    half = ROPE_DIM // 2
    p = np.zeros((width, width), np.float32)
    for j in range(width):
        if j % ATT_HEAD_DIM < half:
            p[j + half, j] = -1.0
        elif j % ATT_HEAD_DIM < ROPE_DIM:
            p[j - half, j] = 1.0
    return jnp.asarray(p, BF16)


def _rotate(v, cos, sin, perm):
    hi = v.astype(BF16)
    lo = (v - hi.astype(F32)).astype(BF16)
    partner = jnp.dot(hi, perm, preferred_element_type=F32) + jnp.dot(lo, perm, preferred_element_type=F32)
    return v * cos + partner * sin


def _swa_kernel(sink_ref, q_ref, kc_ref, kp_ref, vc_ref, vp_ref, cosc_ref, sinc_ref, cosp_ref, sinp_ref,
                perm_ref, o_ref):
    w, hd = ATT_WINDOW, ATT_HEAD_DIM
    h = pl.program_id(1)
    n = pl.program_id(2)
    perm = perm_ref[...]
    perm_k = perm[:hd, :hd]
    q = _rotate(q_ref[...], cosc_ref[...], sinc_ref[...], perm)
    k_cur = _rotate(kc_ref[0, 0], cosc_ref[:, :hd], sinc_ref[:, :hd], perm_k)
    k_prev = _rotate(kp_ref[0, 0], cosp_ref[:, :hd], sinp_ref[:, :hd], perm_k)
    k_all = jnp.concatenate([k_prev, k_cur], axis=0).astype(BF16)
    v_all = jnp.concatenate([vp_ref[0, 0], vc_ref[0, 0]], axis=0).astype(BF16)
    qi = lax.broadcasted_iota(jnp.int32, (w, 2 * w), 0) + w
    ki = lax.broadcasted_iota(jnp.int32, (w, 2 * w), 1)
    rel = qi - ki
    first = jnp.where(n > 0, 0, w)
    bias = jnp.where((rel >= 0) & (rel < ATT_WINDOW) & (ki >= first), 0.0, -jnp.inf)
    heads = range(ATT_REP)
    qs = [(q[:, r * hd:(r + 1) * hd] * (hd ** -0.5)).astype(BF16) for r in heads]
    sinks = [sink_ref[h * ATT_REP + r] for r in heads]
    ss = [lax.dot_general(qr, k_all, (((1,), (1,)), ((), ())), preferred_element_type=F32) + bias for qr in qs]
    ms = [jnp.maximum(jnp.max(s, -1, keepdims=True), sink) for s, sink in zip(ss, sinks)]
    es = [jnp.exp(s - m) for s, m in zip(ss, ms)]
    dens = [jnp.sum(e, -1, keepdims=True) + jnp.exp(sink - m) for e, sink, m in zip(es, sinks, ms)]
    outs = [jnp.dot(e.astype(BF16), v_all, preferred_element_type=F32) / den for e, den in zip(es, dens)]
    o_ref[...] = jnp.concatenate(outs, axis=1).astype(o_ref.dtype)


def _swa_attention(qkv, sinks, batch, seq):
    t = batch * seq
    nb = seq // ATT_WINDOW
    hd, w = ATT_HEAD_DIM, ATT_WINDOW
    qw = ATT_REP * hd
    k4 = qkv[:, Q_COLS:Q_COLS + KV_COLS].reshape(batch, seq, ATT_KV_HEADS, hd).transpose(0, 2, 1, 3)
    v4 = qkv[:, Q_COLS + KV_COLS:].reshape(batch, seq, ATT_KV_HEADS, hd).transpose(0, 2, 1, 3)
    cos, sin = _rotary_tables(seq)
    cur = pl.BlockSpec((1, 1, w, hd), lambda b, h, n: (b, h, n, 0))
    prev = pl.BlockSpec((1, 1, w, hd), lambda b, h, n: (b, h, jnp.maximum(n - 1, 0), 0))
    tab_cur = pl.BlockSpec((w, qw), lambda b, h, n: (n, 0))
    tab_prev = pl.BlockSpec((w, qw), lambda b, h, n: (jnp.maximum(n - 1, 0), 0))
    return pl.pallas_call(
        _swa_kernel,
        grid=(batch, ATT_KV_HEADS, nb),
        in_specs=[
            pl.BlockSpec(memory_space=pltpu.SMEM),
            pl.BlockSpec((w, qw), lambda b, h, n: (b * nb + n, h)),
            cur, prev, cur, prev, tab_cur, tab_cur, tab_prev, tab_prev,
            pl.BlockSpec((qw, qw), lambda b, h, n: (0, 0)),
        ],
        out_specs=pl.BlockSpec((w, qw), lambda b, h, n: (b * nb + n, h)),
        out_shape=jax.ShapeDtypeStruct((t, Q_COLS), BF16),
        compiler_params=_cparams(("parallel", "parallel", "arbitrary")),
        name="swa_attention",
    )(sinks, qkv, k4, k4, v4, v4, cos, sin, cos, sin, _rotate_half_matrix())


def _ssd_swa_mixer(x2, xb, batch, seq, w_in, b_qkv, conv_w, conv_b, dt_bias, a_log, d_skip, norm_w, sinks,
                   w_out, b_out):
    o1 = D_MODEL
    o2 = o1 + SSM_CONV_CH
    o3 = o2 + SSM_HEADS
    w_in_b = w_in.astype(BF16)
    u_z = _matmul(xb, w_in_b, n=o1)
    u_xbc = _matmul(xb, w_in_b, col_start=o1, n=SSM_CONV_CH)
    u_dt = _matmul(x2, w_in[:, o2:o3], precision=HIGHEST)
    qkv = _matmul(xb, w_in_b[:, o3:], b_qkv)
    y_ssm = _ssd_mixer(u_z, u_xbc, u_dt, conv_w, conv_b, dt_bias, a_log, d_skip, norm_w, batch, seq)
    y_att = _swa_attention(qkv, sinks, batch, seq)
    return _matmul((y_ssm, y_att), w_out.astype(BF16), b_out)


MOE_TILE = 512
MOE_GATHER_TILE = 256
ROUTER_TILE = 256


def _router_kernel(h_ref, wt_ref, b_ref, ids_ref, wts_ref):
    logits = lax.dot_general(wt_ref[...], h_ref[...], (((1,), (1,)), ((), ())),
                             precision=HIGHEST, preferred_element_type=F32) + b_ref[...]
    e = jnp.exp(logits - jnp.max(logits, 0, keepdims=True))
    probs = e / jnp.sum(e, 0, keepdims=True)
    p = [probs[i:i + 1, :] for i in range(N_EXPERTS)]

    keep, score = [], []
    for g in range(N_EXPERT_GROUPS):
        members = range(g * EXPERTS_PER_GROUP, (g + 1) * EXPERTS_PER_GROUP)
        s = jnp.zeros_like(p[0])
        for i in members:
            rank = jnp.zeros_like(p[0])
            for j in members:
                if j != i:
                    beats = (p[j] > p[i]) | ((p[j] == p[i]) & (j < i)) if j < i else (p[j] > p[i])
                    rank = rank + beats.astype(F32)
            keep.append(rank < TOP_K)
            s = s + jnp.where(keep[i], p[i], 0.0)
        score.append(s)
    chosen = []
    for g in range(N_EXPERT_GROUPS):
        c = jnp.ones_like(p[0]) > 0
        for g2 in range(N_EXPERT_GROUPS):
            if g2 < g:
                c = c & (score[g] > score[g2])
            elif g2 > g:
                c = c & (score[g] >= score[g2])
        chosen.append(c)
    denom = jnp.zeros_like(p[0])
    for g in range(N_EXPERT_GROUPS):
        denom = denom + jnp.where(chosen[g], score[g], 0.0)
    count = jnp.zeros_like(p[0])
    id0 = jnp.zeros_like(p[0])
    id1 = jnp.zeros_like(p[0])
    w0 = jnp.zeros_like(p[0])
    w1 = jnp.zeros_like(p[0])
    for i in range(N_EXPERTS):
        sel = keep[i] & chosen[i // EXPERTS_PER_GROUP]
        gate = p[i] / denom
        first = sel & (count == 0.0)
        second = sel & (count == 1.0)
        id0 = jnp.where(first, float(i), id0)
        w0 = jnp.where(first, gate, w0)
        id1 = jnp.where(second, float(i), id1)
        w1 = jnp.where(second, gate, w1)
        count = count + sel.astype(F32)
    ids_ref[0:1, :] = id0.astype(jnp.int32)
    ids_ref[1:2, :] = id1.astype(jnp.int32)
    wts_ref[0:1, :] = w0
    wts_ref[1:2, :] = w1


def _router(h, router_w, router_b):
    t, d = h.shape
    tm = _tile(t, ROUTER_TILE)
    return pl.pallas_call(
        _router_kernel,
        grid=(t // tm,),
        in_specs=[
            pl.BlockSpec((tm, d), lambda i: (i, 0)),
            pl.BlockSpec((N_EXPERTS, d), lambda i: (0, 0)),
            pl.BlockSpec((N_EXPERTS, 1), lambda i: (0, 0)),
        ],
        out_specs=[pl.BlockSpec((TOP_K, tm), lambda i: (0, i)), pl.BlockSpec((TOP_K, tm), lambda i: (0, i))],
        out_shape=[jax.ShapeDtypeStruct((TOP_K, t), jnp.int32), jax.ShapeDtypeStruct((TOP_K, t), F32)],
        compiler_params=_cparams(("parallel",)),
        name="moe_router",
    )(h, router_w.T, router_b.reshape(N_EXPERTS, 1))


def _route_metadata(ids, wts, tm):
    t = ids.shape[1]
    na = TOP_K * t
    eid = ids.T.reshape(na)
    onehot = (eid[:, None] == jnp.arange(N_EXPERTS, dtype=jnp.int32)[None, :]).astype(jnp.int32)
    csum = jnp.cumsum(onehot, axis=0)
    counts = csum[-1]
    rank = jnp.sum(csum * onehot, axis=1) - 1
    padded = (counts + tm - 1) // tm * tm
    pend = jnp.cumsum(padded)
    dest = (pend - padded)[eid] + rank
    n_tiles = na // tm + N_EXPERTS
    n_rows = n_tiles * tm
    row_assign = jnp.full((n_rows,), -1, jnp.int32).at[dest].set(jnp.arange(na, dtype=jnp.int32))
    assigned = jnp.maximum(row_assign, 0)
    row_token = assigned // TOP_K
    row_weight = jnp.where(row_assign >= 0, wts.T.reshape(na)[assigned], 0.0)
    tile_start = jnp.arange(n_tiles, dtype=jnp.int32) * tm
    tile_valid = (tile_start < pend[-1]).astype(jnp.int32)
    last_expert = jnp.max(jnp.where(counts > 0, jnp.arange(N_EXPERTS, dtype=jnp.int32), 0))
    tile_expert = jnp.minimum(jnp.searchsorted(pend, tile_start, side="right").astype(jnp.int32), last_expert)
    experts = jnp.arange(N_EXPERTS, dtype=jnp.int32)
    later = jnp.where((counts > 0)[None, :] & (experts[None, :] > experts[:, None]), experts[None, :], N_EXPERTS)
    next_nonempty = jnp.min(later, axis=1)
    next_nonempty = jnp.where(next_nonempty == N_EXPERTS, tile_expert[0], next_nonempty).astype(jnp.int32)
    prev_expert = jnp.concatenate([jnp.full((1,), -1, jnp.int32), tile_expert[:-1]])
    run_first = (tile_valid == 1) & (tile_expert != prev_expert)
    runs = jnp.stack([tile_expert, tile_valid, run_first.astype(jnp.int32), next_nonempty[tile_expert],
                      (tile_expert == last_expert).astype(jnp.int32)])
    return row_token, row_weight, dest, runs


def _row_copy(src_hbm, tiles_ref, sem, tile, sub, src_row):
    return pltpu.make_async_copy(src_hbm.at[src_row], tiles_ref.at[tile, :, sub, :], sem)


def _start_row_gather(src_hbm, tiles_ref, idx_ref, n, sem):
    def body(tile, carry):
        for sub in range(SUBLANES):
            _row_copy(src_hbm, tiles_ref, sem, tile, sub, idx_ref[0, 0, tile * SUBLANES + sub]).start(
                priority=sub % 2)
        return carry

    lax.fori_loop(0, n // SUBLANES, body, 0)


def _wait_row_gather(src_hbm, tiles_ref, n, sem):
    def body(tile, carry):
        for sub in range(SUBLANES):
            _row_copy(src_hbm, tiles_ref, sem, tile, sub, 0).wait()
        return carry

    lax.fori_loop(0, n // SUBLANES, body, 0)


def _gather_step(src_hbm, slots_ref, idx_ref, idx_next_ref, sems, n, *, more):
    i = pl.program_id(0)
    slot = i % 2

    @pl.when(i == 0)
    def _():
        _start_row_gather(src_hbm, slots_ref.at[0], idx_ref, n, sems.at[0])

    @pl.when(more)
    def _():
        _start_row_gather(src_hbm, slots_ref.at[1 - slot], idx_next_ref, n, sems.at[1 - slot])

    return slot


def _moe_gather_kernel(tv_ref, rt_ref, rt_next_ref, h_hbm, o_ref, xg_ref, sems, *, tm):
    i = pl.program_id(0)
    nxt = jnp.minimum(i + 1, pl.num_programs(0) - 1)
    valid = tv_ref[i] == 1
    slot = _gather_step(h_hbm, xg_ref, rt_ref, rt_next_ref, sems, tm, more=(nxt > i) & (tv_ref[nxt] == 1))

    @pl.when(valid)
    def _():
        _wait_row_gather(h_hbm, xg_ref.at[slot], tm, sems.at[slot])
        o_ref[...] = _tiles_to_rows(xg_ref.at[slot], 0, tm // SUBLANES).astype(o_ref.dtype)

    @pl.when(jnp.logical_not(valid))
    def _():
        o_ref[...] = jnp.zeros_like(o_ref)


RUN_EXPERT, RUN_VALID, RUN_FIRST, RUN_NEXT, RUN_LAST = range(5)


def _stream_expert_weights(runs_ref, w_hbm, stage_ref, wb_ref, sems, *, layer, tn):
    j, i = pl.program_id(0), pl.program_id(1)
    last_pass = j == pl.num_programs(0) - 1

    def copies(expert, col_block):
        col = pl.multiple_of(col_block * tn, tn)
        return [pltpu.make_async_copy(w.at[layer, expert, :, pl.ds(col, tn)], stage_ref.at[m], sems.at[m])
                for m, w in enumerate(w_hbm)]

    @pl.when((j == 0) & (i == 0))
    def _():
        for c in copies(runs_ref[RUN_EXPERT, 0], 0):
            c.start()

    @pl.when(runs_ref[RUN_FIRST, i] == 1)
    def _():
        for c in copies(runs_ref[RUN_EXPERT, i], j):
            c.wait()
        for m in range(len(w_hbm)):
            wb_ref[m] = stage_ref[m].astype(BF16)
        last_run = runs_ref[RUN_LAST, i] == 1

        @pl.when(jnp.logical_not(last_run & last_pass))
        def _():
            for c in copies(runs_ref[RUN_NEXT, i], jnp.where(last_run, j + 1, j)):
                c.start()


def _moe_up_kernel(runs_ref, x_ref, wg_hbm, wu_hbm, o_ref, stage_ref, wb_ref, sems, *, layer, tn):
    _stream_expert_weights(runs_ref, (wg_hbm, wu_hbm), stage_ref, wb_ref, sems, layer=layer, tn=tn)
    valid = runs_ref[RUN_VALID, pl.program_id(1)] == 1

    @pl.when(valid)
    def _():
        xt = x_ref[...]
        gate = jnp.dot(xt, wb_ref[0], preferred_element_type=F32)
        up = jnp.dot(xt, wb_ref[1], preferred_element_type=F32)
        o_ref[...] = (_silu(gate) * up).astype(o_ref.dtype)

    @pl.when(jnp.logical_not(valid))
    def _():
        o_ref[...] = jnp.zeros_like(o_ref)


def _moe_down_kernel(runs_ref, he_ref, wd_hbm, rw_ref, o_ref, stage_ref, wb_ref, sems, *, layer, tn):
    _stream_expert_weights(runs_ref, (wd_hbm,), stage_ref, wb_ref, sems, layer=layer, tn=tn)
    valid = runs_ref[RUN_VALID, pl.program_id(1)] == 1

    @pl.when(valid)
    def _():
        _to_token_major(o_ref, jnp.dot(he_ref[...], wb_ref[0], preferred_element_type=F32) * rw_ref[...])

    @pl.when(jnp.logical_not(valid))
    def _():
        o_ref[...] = jnp.zeros_like(o_ref)


def _moe_combine_ln_kernel(pos_ref, pos_next_ref, h_ref, y_hbm, g_ref, b_ref, o_ref, ob_ref, yg_ref, sems, *, tm):
    n = TOP_K * tm
    slot = _gather_step(y_hbm, yg_ref, pos_ref, pos_next_ref, sems, n,
                        more=pl.program_id(0) + 1 < pl.num_programs(0))
    _wait_row_gather(y_hbm, yg_ref.at[slot], n, sems.at[slot])
    per = tm // SUBLANES
    ffn = _tiles_to_rows(yg_ref.at[slot], 0, per) + _tiles_to_rows(yg_ref.at[slot], per, TOP_K * per)
    r = _layer_norm_rows(ALPHA * h_ref[...] + ffn, g_ref[...], b_ref[...])
    o_ref[...] = r
    ob_ref[...] = r.astype(BF16)


def _moe_block(h, h3, router_w, router_b, w_gate, w_up, w_down, layer, ln_g, ln_b, *, tm=MOE_TILE, tn_up=512,
               tn_down=2048, tm_out=128):
    t, d = h.shape
    slabs = d // LANES
    ids, wts = _router(h, router_w, router_b)
    row_token, row_weight, dest, runs = _route_metadata(ids, wts, tm)
    tile_valid = runs[RUN_VALID]
    n_rows = row_token.shape[0]
    n_tiles = n_rows // tm
    de = w_gate.shape[-1]
    tn_up, tn_down = _tile(de, tn_up), _tile(d, tn_down)

    def gather_scratch(n):
        return [pltpu.VMEM((2, n // SUBLANES, slabs, SUBLANES, LANES), F32), pltpu.SemaphoreType.DMA((2,))]

    tg = _tile(tm, MOE_GATHER_TILE)
    per = tm // tg
    n_gather = n_rows // tg
    row_token3 = row_token.reshape(n_gather, 1, tg)
    x_rows = pl.pallas_call(
        functools.partial(_moe_gather_kernel, tm=tg),
        grid_spec=pltpu.PrefetchScalarGridSpec(
            num_scalar_prefetch=1,
            grid=(n_gather,),
            in_specs=[
                pl.BlockSpec((1, 1, tg), lambda i, tv: (i, 0, 0), memory_space=pltpu.SMEM),
                pl.BlockSpec((1, 1, tg), lambda i, tv: (jnp.minimum(i + 1, n_gather - 1), 0, 0),
                             memory_space=pltpu.SMEM),
                pl.BlockSpec(memory_space=pl.ANY),
            ],
            out_specs=pl.BlockSpec((tg, d), lambda i, tv: (i, 0)),
            scratch_shapes=gather_scratch(tg),
        ),
        out_shape=jax.ShapeDtypeStruct((n_rows, d), BF16),
        compiler_params=_cparams(("arbitrary",)),
        name="moe_gather",
    )(jnp.repeat(tile_valid, per), row_token3, row_token3, h3)

    def weight_scratch(n_mats, k, tn):
        return [pltpu.VMEM((n_mats, k, tn), F32), pltpu.VMEM((n_mats, k, tn), BF16),
                pltpu.SemaphoreType.DMA((n_mats,))]

    he = pl.pallas_call(
        functools.partial(_moe_up_kernel, layer=layer, tn=tn_up),
        grid_spec=pltpu.PrefetchScalarGridSpec(
            num_scalar_prefetch=1,
            grid=(de // tn_up, n_tiles),
            in_specs=[
                pl.BlockSpec((tm, d), lambda j, i, runs: (i, 0)),
                pl.BlockSpec(memory_space=pl.ANY),
                pl.BlockSpec(memory_space=pl.ANY),
            ],
            out_specs=pl.BlockSpec((tm, tn_up), lambda j, i, runs: (i, j)),
            scratch_shapes=weight_scratch(2, d, tn_up),
        ),
        out_shape=jax.ShapeDtypeStruct((n_rows, de), BF16),
        compiler_params=_cparams(("arbitrary", "arbitrary")),
        name="moe_gate_up",
    )(runs, x_rows, w_gate, w_up)

    y_rows = pl.pallas_call(
        functools.partial(_moe_down_kernel, layer=layer, tn=tn_down),
        grid_spec=pltpu.PrefetchScalarGridSpec(
            num_scalar_prefetch=1,
            grid=(d // tn_down, n_tiles),
            in_specs=[
                pl.BlockSpec((tm, de), lambda j, i, runs: (i, 0)),
                pl.BlockSpec(memory_space=pl.ANY),
                pl.BlockSpec((tm, 1), lambda j, i, runs: (i, 0)),
            ],
            out_specs=pl.BlockSpec((tm, tn_down // LANES, LANES), lambda j, i, runs: (i, j, 0)),
            scratch_shapes=weight_scratch(1, de, tn_down),
        ),
        out_shape=jax.ShapeDtypeStruct((n_rows, slabs, LANES), F32),
        compiler_params=_cparams(("arbitrary", "arbitrary")),
        name="moe_down",
    )(runs, he, w_down, row_weight.reshape(n_rows, 1))

    tm_out = _tile(t, tm_out)
    n_out = t // tm_out
    pos = dest.reshape(n_out, tm_out, TOP_K).transpose(0, 2, 1).reshape(n_out, 1, TOP_K * tm_out)
    row = pl.BlockSpec((tm_out, d), lambda i: (i, 0))
    vec = pl.BlockSpec((1, d), lambda i: (0, 0))
    return pl.pallas_call(
        functools.partial(_moe_combine_ln_kernel, tm=tm_out),
        grid=(n_out,),
        in_specs=[
            pl.BlockSpec((1, 1, TOP_K * tm_out), lambda i: (i, 0, 0), memory_space=pltpu.SMEM),
            pl.BlockSpec((1, 1, TOP_K * tm_out), lambda i: (jnp.minimum(i + 1, n_out - 1), 0, 0),
                         memory_space=pltpu.SMEM),
            row,
            pl.BlockSpec(memory_space=pl.ANY),
            vec, vec,
        ],
        out_specs=[row, row],
        out_shape=[jax.ShapeDtypeStruct((t, d), F32), jax.ShapeDtypeStruct((t, d), BF16)],
        scratch_shapes=gather_scratch(TOP_K * tm_out),
        compiler_params=_cparams(("arbitrary",)),
        name="moe_combine_ln",
    )(pos, pos, h, y_rows, ln_g.reshape(1, d), ln_b.reshape(1, d))


def _shift_mix_kernel(x_ref, xp_ref, mix_ref, *rest, tm, seq):
    o_refs, pad_ref = rest[:-1], rest[-1]
    i = pl.program_id(0)
    starts_sequence = (i * tm) % seq == 0
    pad_ref[0:SUBLANES, :] = jnp.where(starts_sequence, 0.0, xp_ref[...])
    pad_ref[SUBLANES:SUBLANES + tm, :] = x_ref[...]
    xv = x_ref[...]
    xx = pad_ref[SUBLANES - 1:SUBLANES - 1 + tm, :] - xv
    for m, o_ref in enumerate(o_refs):
        o_ref[...] = (xv + xx * mix_ref[m:m + 1, :]).astype(o_ref.dtype)


def _shift_mix(x, mix, seq, *, tm=128):
    t, d = x.shape
    nm = mix.shape[0]
    tm = _tile(seq, tm)
    per = tm // SUBLANES
    return pl.pallas_call(
        functools.partial(_shift_mix_kernel, tm=tm, seq=seq),
        grid=(t // tm,),
        in_specs=[
            pl.BlockSpec((tm, d), lambda i: (i, 0)),
            pl.BlockSpec((SUBLANES, d), lambda i: (jnp.maximum(i * per - 1, 0), 0)),
            pl.BlockSpec((nm, d), lambda i: (0, 0)),
        ],
        out_specs=[pl.BlockSpec((tm, d), lambda i: (i, 0))] * nm,
        out_shape=[jax.ShapeDtypeStruct((t, d), BF16)] * nm,
        scratch_shapes=[pltpu.VMEM((SUBLANES + tm, d), F32)],
        compiler_params=_cparams(("parallel",)),
        name="rwkv_shift_mix",
    )(x, x, mix)


WKV_CHUNK = 64
WKV_HEADS = 4
WKV_WIDTH = WKV_HEADS * RWKV_HEAD
WKV_BLOCK = 512
NEUMANN_STEPS = 5


def _wkv_kernel(r_ref, k_ref, v_ref, wl_ref, ar_ref, g_ref, kk_ref, ka_ref, rk_ref, lnw_ref, lnb_ref,
                o_ref, state_ref, *, tb):
    lc, hw, wd = WKV_CHUNK, RWKV_HEAD, WKV_WIDTH
    nch = tb // lc

    @pl.when(pl.program_id(2) == 0)
    def _():
        state_ref[...] = jnp.zeros_like(state_ref)

    wrow = lax.broadcasted_iota(jnp.int32, (wd, wd), 0)
    wcol = lax.broadcasted_iota(jnp.int32, (wd, wd), 1)
    same_head = wrow // hw == wcol // hw
    ones_bd = same_head.astype(BF16)

    def block_diag(m):
        return jnp.where(same_head, jnp.concatenate([m] * WKV_HEADS, axis=0), 0.0).astype(BF16)

    def head_sum(m):
        hi = m.astype(BF16)
        lo = (m - hi.astype(F32)).astype(BF16)
        return jnp.dot(hi, ones_bd, preferred_element_type=F32) + jnp.dot(lo, ones_bd, preferred_element_type=F32)

    def mm(a, b):
        return jnp.dot(a.astype(BF16), b, preferred_element_type=F32)

    r = r_ref[...]
    k = k_ref[...]
    v = v_ref[...]
    w_log = -_softplus(-wl_ref[...]) - 0.5
    lw = -jnp.exp(w_log)
    a_sig = jax.nn.sigmoid(ar_ref[...])
    kx = k * kk_ref[...]
    kk = kx / jnp.maximum(jnp.sqrt(head_sum(kx * kx)), 1e-12)
    k2 = k * (1.0 + (a_sig - 1.0) * ka_ref[...])
    a_s = -kk
    b_s = kk * a_sig

    trow = lax.broadcasted_iota(jnp.int32, (tb, tb), 0)
    tcol = lax.broadcasted_iota(jnp.int32, (tb, tb), 1)
    same_chunk = trow // lc == tcol // lc
    cum = _dot_split3(lw, (same_chunk & (trow >= tcol)).astype(BF16), split_lhs=False)
    tot = jnp.concatenate([jnp.broadcast_to(cum[(c + 1) * lc - 1:(c + 1) * lc, :], (lc, wd)) for c in range(nch)],
                          axis=0)
    grow = jnp.exp(-cum)
    rt = r * jnp.exp(cum)
    at = a_s * jnp.exp(cum - lw)
    bt = b_s * grow
    kt = k2 * grow
    rest = jnp.exp(tot - cum)
    bh = b_s * rest
    kh = k2 * rest
    p_end = jnp.exp(tot)

    t_idx = lax.broadcasted_iota(jnp.int32, (lc, wd), 0)
    s_idx = lax.broadcasted_iota(jnp.int32, (lc, wd), 1) % hw
    strict = t_idx > s_idx
    incl = t_idx >= s_idx
    eye = (t_idx == s_idx).astype(F32)
    nt = (((1,), (1,)), ((), ()))

    tn = (((0,), (0,)), ((), ()))
    chunks = [slice(c * lc, (c + 1) * lc) for c in range(nch)]

    a_ab, a_rb, a_ak, a_rk = [], [], [], []
    for sl in chunks:
        ar = jnp.concatenate([at[sl], rt[sl]], axis=0).astype(BF16)
        xb = lax.dot_general(ar, block_diag(bt[sl]), nt, preferred_element_type=F32)
        xk = lax.dot_general(ar, block_diag(kt[sl]), nt, preferred_element_type=F32)
        a_ab.append(jnp.where(strict, xb[:lc], 0.0))
        a_rb.append(jnp.where(incl, xb[lc:], 0.0))
        a_ak.append(jnp.where(strict, xk[:lc], 0.0))
        a_rk.append(jnp.where(incl, xk[lc:], 0.0))
    inv = [eye + a for a in a_ab]
    pw = [mm(a, block_diag(a)) for a in a_ab]
    for step in range(1, NEUMANN_STEPS + 1):
        last = step == NEUMANN_STEPS
        lhs = inv if last else [jnp.concatenate([t, p], axis=0) for t, p in zip(inv, pw)]
        prod = [mm(x, block_diag(p)) for x, p in zip(lhs, pw)]
        inv = [t + q[:lc] for t, q in zip(inv, prod)]
        if not last:
            pw = [q[lc:] for q in prod]
    v_bd = [block_diag(v[sl]) for sl in chunks]
    ta = [mm(t, block_diag(at[sl])) for t, sl in zip(inv, chunks)]
    av = [mm(jnp.concatenate([ak, rk], axis=0), vb) for ak, rk, vb in zip(a_ak, a_rk, v_bd)]
    y0 = [x[lc:] for x in av]
    u0 = [mm(t, block_diag(x[:lc])) for t, x in zip(inv, av)]
    gain, add = [], []
    for c, sl in enumerate(chunks):
        bh_b = bh[sl].astype(BF16)
        gain.append(jnp.where(same_head, lax.dot_general(ta[c].astype(BF16), bh_b, tn, preferred_element_type=F32),
                              0.0).astype(BF16))
        uv = jnp.concatenate([u0[c], v[sl]], axis=0).astype(BF16)
        bk = jnp.concatenate([bh_b, kh[sl].astype(BF16)], axis=0)
        add.append(jnp.where(same_head, lax.dot_general(uv, bk, tn, preferred_element_type=F32), 0.0))

    states = [state_ref[...]]
    for c in range(nch):
        s0 = states[-1]
        states.append(s0 * p_end[c * lc:c * lc + 1, :] + mm(s0, gain[c]) + add[c])
    state_ref[...] = states[nch]

    ys = []
    for c, sl in enumerate(chunks):
        tr = jnp.concatenate([ta[c], rt[sl]], axis=0).astype(BF16)
        xs = lax.dot_general(tr, states[c].astype(BF16), nt, preferred_element_type=F32)
        u = xs[:lc] + u0[c]
        ys.append(xs[lc:] + mm(a_rb[c], block_diag(u)) + y0[c])

    y = jnp.concatenate(ys, axis=0)
    mu = head_sum(y) / hw
    dev = y - mu
    var = head_sum(dev * dev) / hw
    yn = dev * lax.rsqrt(var + RWKV_GN_EPS) * lnw_ref[...] + lnb_ref[...]
    bonus = head_sum(r * k2 * rk_ref[...]) * v
    o_ref[...] = ((yn + bonus) * g_ref[...]).astype(o_ref.dtype)


def _wkv(r, k, v, wl, araw, g, k_k, k_a, r_k, ln_w, ln_b, batch, seq):
    t, d = r.shape
    tb = _tile(seq, WKV_BLOCK)
    nt = seq // tb
    wd = WKV_WIDTH
    rows = pl.BlockSpec((tb, wd), lambda b, h, n: (b * nt + n, h))
    vec = pl.BlockSpec((1, wd), lambda b, h, n: (0, h))
    return pl.pallas_call(
        functools.partial(_wkv_kernel, tb=tb),
        grid=(batch, d // wd, nt),
        in_specs=[rows] * 6 + [vec] * 5,
        out_specs=rows,
        out_shape=jax.ShapeDtypeStruct((t, d), BF16),
        scratch_shapes=[pltpu.VMEM((wd, wd), F32)],
        compiler_params=_cparams(("parallel", "parallel", "arbitrary")),
        name="rwkv_wkv",
    )(r, k, v, wl, araw, g, k_k.reshape(1, d), k_a.reshape(1, d), r_k.reshape(1, d),
      ln_w.reshape(1, d), ln_b.reshape(1, d))


def _rwkv7_time_mix(x2, batch, seq, mix, w_r, w_k, w_v, w_o, w0, w1, w2, a0, a1, a2, g1, g2, k_k, k_a, r_k,
                    ln_w, ln_b):
    xm = _shift_mix(x2, mix, seq)
    bf = lambda w: w.astype(BF16)
    r = _matmul(xm[0], bf(w_r))
    k = _matmul(xm[2], bf(w_k))
    v = _matmul(xm[3], bf(w_v))
    wl = _low_rank(xm[1], bf(w1), bf(w2), w0, act="tanh")
    araw = _low_rank(xm[4], bf(a1), bf(a2), a0)
    g = _low_rank(xm[5], bf(g1), bf(g2), act="sigmoid")
    yg = _wkv(r, k, v, wl, araw, g, k_k, k_a, r_k, ln_w, ln_b, batch, seq)
    return _matmul(yg, bf(w_o))


def kernel(x, ab_w_in, ab_b_qkv, ssm_conv_w, ssm_conv_b, ssm_dt_bias, ssm_a_log, ssm_d, ssm_norm_w, attn_sinks,
           ab_w_out, ab_b_out, rwkv_mix, rwkv_w_r, rwkv_w_k, rwkv_w_v, rwkv_w_o, rwkv_w0, rwkv_w1, rwkv_w2,
           rwkv_a0, rwkv_a1, rwkv_a2, rwkv_g1, rwkv_g2, rwkv_k_k, rwkv_k_a, rwkv_r_k, rwkv_ln_w, rwkv_ln_b,
           ln_mix_g, ln_mix_b, ln_ffn_g, ln_ffn_b, router_w, router_b, moe_w_gate, moe_w_up, moe_w_down):
    batch, seq, d = x.shape
    x2 = x.reshape(batch * seq, d)
    xb = x2.astype(BF16)
    for layer in range(DEPTH):
        i = layer // 2
        if layer % 2 == 0:
            mix = _ssd_swa_mixer(x2, xb, batch, seq, ab_w_in[i], ab_b_qkv[i], ssm_conv_w[i], ssm_conv_b[i],
                                 ssm_dt_bias[i], ssm_a_log[i], ssm_d[i], ssm_norm_w[i], attn_sinks[i],
                                 ab_w_out[i], ab_b_out[i])
        else:
            mix = _rwkv7_time_mix(x2, batch, seq, rwkv_mix[i], rwkv_w_r[i], rwkv_w_k[i], rwkv_w_v[i], rwkv_w_o[i],
                                  rwkv_w0[i], rwkv_w1[i], rwkv_w2[i], rwkv_a0[i], rwkv_a1[i], rwkv_a2[i],
                                  rwkv_g1[i], rwkv_g2[i], rwkv_k_k[i], rwkv_k_a[i], rwkv_r_k[i].reshape(-1),
                                  rwkv_ln_w[i], rwkv_ln_b[i])
        h, h3 = _add_layer_norm(x2, mix, ln_mix_g[layer], ln_mix_b[layer])
        x2, xb = _moe_block(h, h3, router_w, router_b, moe_w_gate, moe_w_up, moe_w_down, layer,
                            ln_ffn_g[layer], ln_ffn_b[layer])
    return x2.reshape(batch, seq, d)
```

```python
import functools
import math

import jax
import jax.numpy as jnp
import numpy as np
from jax import lax
from jax.experimental import pallas as pl
from jax.experimental.pallas import tpu as pltpu

F32 = jnp.float32
BF16 = jnp.bfloat16
HIGHEST = lax.Precision.HIGHEST

D_MODEL = 4096
DEPTH = 2
SSM_HEAD_DIM = 64
SSM_HEADS = 64
SSM_GROUPS = 8
SSM_HEADS_PER_GROUP = 8
SSM_STATE = 128
SSM_CONV = 4
SSM_CHUNK = 128
SSM_GROUP_WIDTH = SSM_HEADS_PER_GROUP * SSM_HEAD_DIM
SSM_CONV_CH = D_MODEL + 2 * SSM_GROUPS * SSM_STATE
SSM_NORM_EPS = 1e-5
ATT_HEADS = 64
ATT_KV_HEADS = 8
ATT_HEAD_DIM = 64
ATT_REP = 8
ATT_WINDOW = 128
ROPE_DIM = 16
ROPE_THETA = 500000.0
Q_COLS = 4096
KV_COLS = 512
RWKV_HEAD = 64
RWKV_GN_EPS = 64e-5
N_EXPERTS = 16
N_EXPERT_GROUPS = 4
EXPERTS_PER_GROUP = 4
TOP_K = 2
D_EXPERT = 1536
ALPHA = (2 * DEPTH) ** 0.25
LN_EPS = 1e-5

VMEM_LIMIT_BYTES = 56 * 1024 * 1024
LANES = 128
SUBLANES = 8


def _cparams(semantics):
    return pltpu.CompilerParams(dimension_semantics=semantics, vmem_limit_bytes=VMEM_LIMIT_BYTES)


def _tile(dim, pref):
    if dim <= pref:
        return dim
    t = pref
    while dim % t:
        t //= 2
    return t


def _silu(v):
    return v * jax.nn.sigmoid(v)


def _softplus(v):
    return jnp.maximum(v, 0.0) + jnp.log1p(jnp.exp(-jnp.abs(v)))


def _split3(v):
    hi = v.astype(BF16)
    rest = v - hi.astype(F32)
    mid = rest.astype(BF16)
    lo = (rest - mid.astype(F32)).astype(BF16)
    return hi, mid, lo


def _dot_split3(v, m01, *, split_lhs):
    if split_lhs:
        return sum(jnp.dot(t, m01, preferred_element_type=F32) for t in _split3(v))
    return sum(jnp.dot(m01, t, preferred_element_type=F32) for t in _split3(v))


def _mm_kernel(*refs, act, precision, w_is_nk):
    a_refs, (w_ref, b_ref, o_ref) = refs[:-3], refs[-3:]
    r = b_ref[...]
    k0 = 0
    for a_ref in a_refs:
        k1 = k0 + a_ref.shape[1]
        if w_is_nk:
            part = lax.dot_general(a_ref[...], w_ref[:, k0:k1], (((1,), (1,)), ((), ())),
                                   preferred_element_type=F32, precision=precision)
        else:
            part = jnp.dot(a_ref[...], w_ref[k0:k1, :], preferred_element_type=F32, precision=precision)
        r = r + part
        k0 = k1
    if act == "tanh":
        r = jnp.tanh(r)
    elif act == "sigmoid":
        r = jax.nn.sigmoid(r)
    o_ref[...] = r.astype(o_ref.dtype)


MATMUL_VMEM_BUDGET = 44 * 1024 * 1024


def _matmul(a, w, bias=None, *, act=None, out_dtype=F32, precision=None, tm=1024, col_start=0, n=None,
            w_is_nk=False):
    panels = a if isinstance(a, (tuple, list)) else (a,)
    m = panels[0].shape[0]
    kdim = sum(p.shape[1] for p in panels)
    w_n, w_k = (w.shape if w_is_nk else w.shape[::-1])
    n = w_n if n is None else n
    tm = _tile(m, tm)
    out_bytes = jnp.dtype(out_dtype).itemsize
    for tn in (512, 256, LANES):
        tn = _tile(n, tn)
        need = 2 * (tm * kdim * panels[0].dtype.itemsize + kdim * tn * w.dtype.itemsize + tm * tn * out_bytes)
        if need <= MATMUL_VMEM_BUDGET:
            break
    assert need <= MATMUL_VMEM_BUDGET and col_start % tn == 0 and w_k == kdim
    col0 = col_start // tn
    if bias is None:
        bias = jnp.zeros((n,), F32)
    bias = bias.reshape(1, n).astype(F32)
    if w_is_nk:
        w_spec = pl.BlockSpec((tn, kdim), lambda i, j: (col0 + j, 0))
    else:
        w_spec = pl.BlockSpec((kdim, tn), lambda i, j: (0, col0 + j))
    return pl.pallas_call(
        functools.partial(_mm_kernel, act=act, precision=precision, w_is_nk=w_is_nk),
        grid=(m // tm, n // tn),
        in_specs=[pl.BlockSpec((tm, p.shape[1]), lambda i, j: (i, 0)) for p in panels] + [
            w_spec,
            pl.BlockSpec((1, tn), lambda i, j: (0, j)),
        ],
        out_specs=pl.BlockSpec((tm, tn), lambda i, j: (i, j)),
        out_shape=jax.ShapeDtypeStruct((m, n), out_dtype),
        compiler_params=_cparams(("parallel", "arbitrary")),
        name="matmul",
    )(*panels, w, bias)


def _low_rank_kernel(a_ref, w1_ref, w2_ref, b_ref, o_ref, *, act):
    t = jnp.dot(a_ref[...], w1_ref[...], preferred_element_type=F32)
    if act == "tanh":
        t = jnp.tanh(t)
    elif act == "sigmoid":
        t = jax.nn.sigmoid(t)
    o_ref[...] = jnp.dot(t.astype(BF16), w2_ref[...], preferred_element_type=F32) + b_ref[...]


def _low_rank(a, w1, w2, bias=None, *, act=None, tm=512):
    m, kdim = a.shape
    r, n = w2.shape
    tm = _tile(m, tm)
    if bias is None:
        bias = jnp.zeros((n,), F32)
    return pl.pallas_call(
        functools.partial(_low_rank_kernel, act=act),
        grid=(m // tm,),
        in_specs=[
            pl.BlockSpec((tm, kdim), lambda i: (i, 0)),
            pl.BlockSpec((kdim, r), lambda i: (0, 0)),
            pl.BlockSpec((r, n), lambda i: (0, 0)),
            pl.BlockSpec((1, n), lambda i: (0, 0)),
        ],
        out_specs=pl.BlockSpec((tm, n), lambda i: (i, 0)),
        out_shape=jax.ShapeDtypeStruct((m, n), F32),
        compiler_params=_cparams(("parallel",)),
        name="low_rank",
    )(a, w1, w2, bias.reshape(1, n).astype(F32))


def _layer_norm_rows(v, g, b):
    mu = jnp.mean(v, -1, keepdims=True)
    var = jnp.mean(jnp.square(v - mu), -1, keepdims=True)
    return (v - mu) * lax.rsqrt(var + LN_EPS) * g + b


def _to_token_major(o3_ref, v):
    for s in range(v.shape[1] // LANES):
        o3_ref[:, s, :] = v[:, s * LANES:(s + 1) * LANES].astype(o3_ref.dtype)


def _tiles_to_rows(tiles_ref, t0, t1):
    return jnp.concatenate([tiles_ref[t0:t1, s].reshape((t1 - t0) * SUBLANES, LANES)
                            for s in range(tiles_ref.shape[1])], axis=1)


def _add_ln_kernel(x_ref, y_ref, g_ref, b_ref, o_ref, o3_ref):
    r = _layer_norm_rows(ALPHA * x_ref[...] + y_ref[...], g_ref[...], b_ref[...])
    o_ref[...] = r
    _to_token_major(o3_ref, r)


def _add_layer_norm(x, y, g, b, *, tm=128):
    t, d = x.shape
    tm = _tile(t, tm)
    row = pl.BlockSpec((tm, d), lambda i: (i, 0))
    row3 = pl.BlockSpec((tm, d // LANES, LANES), lambda i: (i, 0, 0))
    vec = pl.BlockSpec((1, d), lambda i: (0, 0))
    return pl.pallas_call(
        _add_ln_kernel,
        grid=(t // tm,),
        in_specs=[row, row, vec, vec],
        out_specs=[row, row3],
        out_shape=[jax.ShapeDtypeStruct((t, d), F32), jax.ShapeDtypeStruct((t, d // LANES, LANES), F32)],
        compiler_params=_cparams(("parallel",)),
        name="add_layer_norm",
    )(x, y, g.reshape(1, d), b.reshape(1, d))


CONV_HALO = SUBLANES
SSD_CH = SSM_GROUP_WIDTH + 2 * SSM_STATE


def _ssd_kernel(z_ref, x_ref, b_ref, c_ref, wx_ref, wb_ref, wc_ref, bx_ref, bb_ref, bc_ref,
                dtr_ref, dbc_ref, dbr_ref, alc_ref, alr_ref, d_ref, nw_ref,
                o_ref, state_ref, pad_ref):
    ch, gw, ns, hd = SSM_CHUNK, SSM_GROUP_WIDTH, SSM_STATE, SSM_HEAD_DIM

    @pl.when(pl.program_id(2) == 0)
    def _():
        state_ref[...] = jnp.zeros_like(state_ref)
        pad_ref[0:CONV_HALO, :] = jnp.zeros((CONV_HALO, SSD_CH), F32)

    pad_ref[CONV_HALO:CONV_HALO + ch, 0:gw] = x_ref[...]
    pad_ref[CONV_HALO:CONV_HALO + ch, gw:gw + ns] = b_ref[...]
    pad_ref[CONV_HALO:CONV_HALO + ch, gw + ns:SSD_CH] = c_ref[...]
    w = jnp.concatenate([wx_ref[...], wb_ref[...], wc_ref[...]], axis=1)
    acc = jnp.concatenate([bx_ref[...], bb_ref[...], bc_ref[...]], axis=1)
    base = CONV_HALO - (SSM_CONV - 1)
    for k in range(SSM_CONV):
        acc = acc + w[k:k + 1, :] * pad_ref[base + k:base + k + ch, :]
    pad_ref[0:CONV_HALO, :] = pad_ref[ch:ch + CONV_HALO, :]
    xbc = _silu(acc)
    xc, bm, cm = xbc[:, :gw], xbc[:, gw:gw + ns], xbc[:, gw + ns:]

    row = lax.broadcasted_iota(jnp.int32, (ch, ch), 0)
    col = lax.broadcasted_iota(jnp.int32, (ch, ch), 1)
    causal = row >= col
    dt_raw_r = dtr_ref[...]
    eye = (row == col).astype(BF16)
    dt_raw_c = sum(lax.dot_general(eye, term, (((1,), (1,)), ((), ())), preferred_element_type=F32)
                   for term in _split3(dt_raw_r))
    dt_c = _softplus(dt_raw_c + dbc_ref[0])
    dt_r = _softplus(dt_raw_r + dbr_ref[...])
    a_c = -jnp.exp(alc_ref[0])
    a_r = -jnp.exp(alr_ref[...])
    acs_c = _dot_split3(dt_c * a_c, causal.astype(BF16), split_lhs=False)
    acs_r = _dot_split3(dt_r * a_r, (row <= col).astype(BF16), split_lhs=True)

    hrow = lax.broadcasted_iota(jnp.int32, (SSM_HEADS_PER_GROUP, gw), 0)
    hcol = lax.broadcasted_iota(jnp.int32, (SSM_HEADS_PER_GROUP, gw), 1)
    expand = (hcol // hd == hrow).astype(BF16)
    dt_x = _dot_split3(dt_c, expand, split_lhs=True)
    acs_x = _dot_split3(acs_c, expand, split_lhs=True)

    xdt = xc * dt_x
    xdt_b = xdt.astype(BF16)
    bm_b, cm_b = bm.astype(BF16), cm.astype(BF16)
    cb = lax.dot_general(cm_b, bm_b, (((1,), (1,)), ((), ())), preferred_element_type=F32)
    prev = state_ref[...]
    y_off = jnp.dot(cm_b, prev.astype(BF16), preferred_element_type=F32) * jnp.exp(acs_x)

    lane = lax.broadcasted_iota(jnp.int32, (ch, 2 * hd), 1)
    pairs = []
    for pr in range(SSM_HEADS_PER_GROUP // 2):
        xp = xdt_b[:, pr * 2 * hd:(pr + 1) * 2 * hd]
        halves = []
        for q in range(2):
            r = 2 * pr + q
            diff = acs_c[:, r:r + 1] - acs_r[r:r + 1, :]
            seg = jnp.where(causal, jnp.exp(jnp.where(causal, diff, 0.0)), 0.0)
            halves.append(jnp.dot((cb * seg).astype(BF16), xp, preferred_element_type=F32))
        pairs.append(jnp.where(lane < hd, halves[0], halves[1]))
    y = jnp.concatenate(pairs, axis=1) + y_off + d_ref[...] * xc

    acs_last = acs_x[ch - 1:ch, :]
    contrib = lax.dot_general(bm_b, (xdt * jnp.exp(acs_last - acs_x)).astype(BF16),
                              (((0,), (0,)), ((), ())), preferred_element_type=F32)
    state_ref[...] = prev * jnp.exp(acs_last) + contrib

    u = y * _silu(z_ref[...])
    u = u * lax.rsqrt(jnp.mean(u * u, -1, keepdims=True) + SSM_NORM_EPS)
    o_ref[...] = (u * nw_ref[...]).astype(o_ref.dtype)


def _ssd_mixer(u_z, u_xbc, u_dt, conv_w, conv_b, dt_bias, a_log, d_skip, norm_w, batch, seq):
    t = batch * seq
    nc = seq // SSM_CHUNK
    g, r, gw, ns = SSM_GROUPS, SSM_HEADS_PER_GROUP, SSM_GROUP_WIDTH, SSM_STATE
    xblocks = D_MODEL // ns
    conv_wt = conv_w.T
    conv_b2 = conv_b.reshape(1, SSM_CONV_CH)
    dt_row = u_dt.T
    rows = lambda b, gi, c: (b * nc + c, gi)
    in_specs = [
        pl.BlockSpec((SSM_CHUNK, gw), rows),
        pl.BlockSpec((SSM_CHUNK, gw), rows),
        pl.BlockSpec((SSM_CHUNK, ns), lambda b, gi, c: (b * nc + c, xblocks + gi)),
        pl.BlockSpec((SSM_CHUNK, ns), lambda b, gi, c: (b * nc + c, xblocks + g + gi)),
        pl.BlockSpec((SSM_CONV, gw), lambda b, gi, c: (0, gi)),
        pl.BlockSpec((SSM_CONV, ns), lambda b, gi, c: (0, xblocks + gi)),
        pl.BlockSpec((SSM_CONV, ns), lambda b, gi, c: (0, xblocks + g + gi)),
        pl.BlockSpec((1, gw), lambda b, gi, c: (0, gi)),
        pl.BlockSpec((1, ns), lambda b, gi, c: (0, xblocks + gi)),
        pl.BlockSpec((1, ns), lambda b, gi, c: (0, xblocks + g + gi)),
        pl.BlockSpec((r, SSM_CHUNK), lambda b, gi, c: (gi, b * nc + c)),
        pl.BlockSpec((1, 1, r), lambda b, gi, c: (gi, 0, 0)),
        pl.BlockSpec((r, 1), lambda b, gi, c: (gi, 0)),
        pl.BlockSpec((1, 1, r), lambda b, gi, c: (gi, 0, 0)),
        pl.BlockSpec((r, 1), lambda b, gi, c: (gi, 0)),
        pl.BlockSpec((1, gw), lambda b, gi, c: (0, gi)),
        pl.BlockSpec((1, gw), lambda b, gi, c: (0, gi)),
    ]
    return pl.pallas_call(
        _ssd_kernel,
        grid=(batch, g, nc),
        in_specs=in_specs,
        out_specs=pl.BlockSpec((SSM_CHUNK, gw), rows),
        out_shape=jax.ShapeDtypeStruct((t, D_MODEL), BF16),
        scratch_shapes=[pltpu.VMEM((ns, gw), F32), pltpu.VMEM((CONV_HALO + SSM_CHUNK, SSD_CH), F32)],
        compiler_params=_cparams(("parallel", "parallel", "arbitrary")),
        name="ssd_mixer",
    )(u_z, u_xbc, u_xbc, u_xbc, conv_wt, conv_wt, conv_wt, conv_b2, conv_b2, conv_b2,
      dt_row, dt_bias.reshape(g, 1, r), dt_bias.reshape(SSM_HEADS, 1),
      a_log.reshape(g, 1, r), a_log.reshape(SSM_HEADS, 1),
      jnp.repeat(d_skip, SSM_HEAD_DIM).reshape(1, D_MODEL), norm_w.reshape(1, D_MODEL))


def _rotary_tables(seq):
    half = ROPE_DIM // 2
    inv_freq = ROPE_THETA ** (-jnp.arange(half, dtype=F32) / half)
    ang = jnp.arange(seq, dtype=F32)[:, None] * inv_freq[None, :]
    ones = jnp.ones((seq, ATT_HEAD_DIM - ROPE_DIM), F32)
    cos = jnp.concatenate([jnp.cos(ang), jnp.cos(ang), ones], -1)
    sin = jnp.concatenate([jnp.sin(ang), jnp.sin(ang), 0.0 * ones], -1)
    return jnp.tile(cos, (1, ATT_REP)), jnp.tile(sin, (1, ATT_REP))


def _rotate_half_matrix():
    width = ATT_REP * ATT_HEAD_DIM
    half = ROPE_DIM // 2
    p = np.zeros((width, width), np.float32)
    for j in range(width):
        if j % ATT_HEAD_DIM < half:
            p[j + half, j] = -1.0
        elif j % ATT_HEAD_DIM < ROPE_DIM:
            p[j - half, j] = 1.0
    return jnp.asarray(p, BF16)


def _rotate(v, cos, sin, perm):
    hi = v.astype(BF16)
    lo = (v - hi.astype(F32)).astype(BF16)
    partner = jnp.dot(hi, perm, preferred_element_type=F32) + jnp.dot(lo, perm, preferred_element_type=F32)
    return v * cos + partner * sin


def _swa_kernel(sink_ref, q_ref, kc_ref, kp_ref, vc_ref, vp_ref, cosc_ref, sinc_ref, cosp_ref, sinp_ref,
                perm_ref, o_ref):
    w, hd = ATT_WINDOW, ATT_HEAD_DIM
    h = pl.program_id(1)
    n = pl.program_id(2)
    perm = perm_ref[...]
    perm_k = perm[:hd, :hd]
    q = _rotate(q_ref[...], cosc_ref[...], sinc_ref[...], perm)
    k_cur = _rotate(kc_ref[0, 0], cosc_ref[:, :hd], sinc_ref[:, :hd], perm_k)
    k_prev = _rotate(kp_ref[0, 0], cosp_ref[:, :hd], sinp_ref[:, :hd], perm_k)
    k_all = jnp.concatenate([k_prev, k_cur], axis=0).astype(BF16)
    v_all = jnp.concatenate([vp_ref[0, 0], vc_ref[0, 0]], axis=0).astype(BF16)
    qi = lax.broadcasted_iota(jnp.int32, (w, 2 * w), 0) + w
    ki = lax.broadcasted_iota(jnp.int32, (w, 2 * w), 1)
    rel = qi - ki
    first = jnp.where(n > 0, 0, w)
    bias = jnp.where((rel >= 0) & (rel < ATT_WINDOW) & (ki >= first), 0.0, -jnp.inf)
    heads = range(ATT_REP)
    qs = [(q[:, r * hd:(r + 1) * hd] * (hd ** -0.5)).astype(BF16) for r in heads]
    sinks = [sink_ref[h * ATT_REP + r] for r in heads]
    ss = [lax.dot_general(qr, k_all, (((1,), (1,)), ((), ())), preferred_element_type=F32) + bias for qr in qs]
    ms = [jnp.maximum(jnp.max(s, -1, keepdims=True), sink) for s, sink in zip(ss, sinks)]
    es = [jnp.exp(s - m) for s, m in zip(ss, ms)]
    dens = [jnp.sum(e, -1, keepdims=True) + jnp.exp(sink - m) for e, sink, m in zip(es, sinks, ms)]
    outs = [jnp.dot(e.astype(BF16), v_all, preferred_element_type=F32) / den for e, den in zip(es, dens)]
    o_ref[...] = jnp.concatenate(outs, axis=1).astype(o_ref.dtype)


def _swa_attention(qkv, sinks, batch, seq):
    t = batch * seq
    nb = seq // ATT_WINDOW
    hd, w = ATT_HEAD_DIM, ATT_WINDOW
    qw = ATT_REP * hd
    k4 = qkv[:, Q_COLS:Q_COLS + KV_COLS].reshape(batch, seq, ATT_KV_HEADS, hd).transpose(0, 2, 1, 3)
    v4 = qkv[:, Q_COLS + KV_COLS:].reshape(batch, seq, ATT_KV_HEADS, hd).transpose(0, 2, 1, 3)
    cos, sin = _rotary_tables(seq)
    cur = pl.BlockSpec((1, 1, w, hd), lambda b, h, n: (b, h, n, 0))
    prev = pl.BlockSpec((1, 1, w, hd), lambda b, h, n: (b, h, jnp.maximum(n - 1, 0), 0))
    tab_cur = pl.BlockSpec((w, qw), lambda b, h, n: (n, 0))
    tab_prev = pl.BlockSpec((w, qw), lambda b, h, n: (jnp.maximum(n - 1, 0), 0))
    return pl.pallas_call(
        _swa_kernel,
        grid=(batch, ATT_KV_HEADS, nb),
        in_specs=[
            pl.BlockSpec(memory_space=pltpu.SMEM),
            pl.BlockSpec((w, qw), lambda b, h, n: (b * nb + n, h)),
            cur, prev, cur, prev, tab_cur, tab_cur, tab_prev, tab_prev,
            pl.BlockSpec((qw, qw), lambda b, h, n: (0, 0)),
        ],
        out_specs=pl.BlockSpec((w, qw), lambda b, h, n: (b * nb + n, h)),
        out_shape=jax.ShapeDtypeStruct((t, Q_COLS), BF16),
        compiler_params=_cparams(("parallel", "parallel", "arbitrary")),
        name="swa_attention",
    )(sinks, qkv, k4, k4, v4, v4, cos, sin, cos, sin, _rotate_half_matrix())


def _ssd_swa_mixer(x2, xb, batch, seq, w_in, b_qkv, conv_w, conv_b, dt_bias, a_log, d_skip, norm_w, sinks,
                   w_out, b_out):
    o1 = D_MODEL
    o2 = o1 + SSM_CONV_CH
    o3 = o2 + SSM_HEADS
    w_t = jnp.swapaxes(w_in, 0, 1)
    w_t_b = w_t.astype(BF16)
    u_z = _matmul(xb, w_t_b, n=o1, w_is_nk=True)
    u_xbc = _matmul(xb, w_t_b, col_start=o1, n=SSM_CONV_CH, w_is_nk=True)
    u_dt = _matmul(x2, w_t[o2:o3], precision=HIGHEST, w_is_nk=True)
    qkv = _matmul(xb, w_t_b[o3:], b_qkv, w_is_nk=True)
    y_ssm = _ssd_mixer(u_z, u_xbc, u_dt, conv_w, conv_b, dt_bias, a_log, d_skip, norm_w, batch, seq)
    y_att = _swa_attention(qkv, sinks, batch, seq)
    return _matmul((y_ssm, y_att), w_out.astype(BF16), b_out)


MOE_TILE = 512
MOE_GATHER_TILE = 256
ROUTER_TILE = 256


def _router_kernel(h_ref, wt_ref, b_ref, ids_ref, wts_ref):
    logits = lax.dot_general(wt_ref[...], h_ref[...], (((1,), (1,)), ((), ())),
                             precision=HIGHEST, preferred_element_type=F32) + b_ref[...]
    e = jnp.exp(logits - jnp.max(logits, 0, keepdims=True))
    probs = e / jnp.sum(e, 0, keepdims=True)
    p = [probs[i:i + 1, :] for i in range(N_EXPERTS)]

    keep, score = [], []
    for g in range(N_EXPERT_GROUPS):
        members = range(g * EXPERTS_PER_GROUP, (g + 1) * EXPERTS_PER_GROUP)
        s = jnp.zeros_like(p[0])
        for i in members:
            rank = jnp.zeros_like(p[0])
            for j in members:
                if j != i:
                    beats = (p[j] > p[i]) | ((p[j] == p[i]) & (j < i)) if j < i else (p[j] > p[i])
                    rank = rank + beats.astype(F32)
            keep.append(rank < TOP_K)
            s = s + jnp.where(keep[i], p[i], 0.0)
        score.append(s)
    chosen = []
    for g in range(N_EXPERT_GROUPS):
        c = jnp.ones_like(p[0]) > 0
        for g2 in range(N_EXPERT_GROUPS):
            if g2 < g:
                c = c & (score[g] > score[g2])
            elif g2 > g:
                c = c & (score[g] >= score[g2])
        chosen.append(c)
    denom = jnp.zeros_like(p[0])
    for g in range(N_EXPERT_GROUPS):
        denom = denom + jnp.where(chosen[g], score[g], 0.0)
    count = jnp.zeros_like(p[0])
    id0 = jnp.zeros_like(p[0])
    id1 = jnp.zeros_like(p[0])
    w0 = jnp.zeros_like(p[0])
    w1 = jnp.zeros_like(p[0])
    for i in range(N_EXPERTS):
        sel = keep[i] & chosen[i // EXPERTS_PER_GROUP]
        gate = p[i] / denom
        first = sel & (count == 0.0)
        second = sel & (count == 1.0)
        id0 = jnp.where(first, float(i), id0)
        w0 = jnp.where(first, gate, w0)
        id1 = jnp.where(second, float(i), id1)
        w1 = jnp.where(second, gate, w1)
        count = count + sel.astype(F32)
    ids_ref[0:1, :] = id0.astype(jnp.int32)
    ids_ref[1:2, :] = id1.astype(jnp.int32)
    wts_ref[0:1, :] = w0
    wts_ref[1:2, :] = w1


def _router(h, router_w, router_b):
    t, d = h.shape
    tm = _tile(t, ROUTER_TILE)
    return pl.pallas_call(
        _router_kernel,
        grid=(t // tm,),
        in_specs=[
            pl.BlockSpec((tm, d), lambda i: (i, 0)),
            pl.BlockSpec((N_EXPERTS, d), lambda i: (0, 0)),
            pl.BlockSpec((N_EXPERTS, 1), lambda i: (0, 0)),
        ],
        out_specs=[pl.BlockSpec((TOP_K, tm), lambda i: (0, i)), pl.BlockSpec((TOP_K, tm), lambda i: (0, i))],
        out_shape=[jax.ShapeDtypeStruct((TOP_K, t), jnp.int32), jax.ShapeDtypeStruct((TOP_K, t), F32)],
        compiler_params=_cparams(("parallel",)),
        name="moe_router",
    )(h, router_w.T, router_b.reshape(N_EXPERTS, 1))


def _route_metadata(ids, wts, tm):
    t = ids.shape[1]
    na = TOP_K * t
    eid = ids.T.reshape(na)
    onehot = (eid[:, None] == jnp.arange(N_EXPERTS, dtype=jnp.int32)[None, :]).astype(jnp.int32)
    csum = jnp.cumsum(onehot, axis=0)
    counts = csum[-1]
    rank = jnp.sum(csum * onehot, axis=1) - 1
    padded = (counts + tm - 1) // tm * tm
    pend = jnp.cumsum(padded)
    dest = (pend - padded)[eid] + rank
    n_tiles = na // tm + N_EXPERTS
    n_rows = n_tiles * tm
    row_assign = jnp.full((n_rows,), -1, jnp.int32).at[dest].set(jnp.arange(na, dtype=jnp.int32))
    assigned = jnp.maximum(row_assign, 0)
    row_token = assigned // TOP_K
    row_weight = jnp.where(row_assign >= 0, wts.T.reshape(na)[assigned], 0.0)
    tile_start = jnp.arange(n_tiles, dtype=jnp.int32) * tm
    tile_valid = (tile_start < pend[-1]).astype(jnp.int32)
    last_expert = jnp.max(jnp.where(counts > 0, jnp.arange(N_EXPERTS, dtype=jnp.int32), 0))
    tile_expert = jnp.minimum(jnp.searchsorted(pend, tile_start, side="right").astype(jnp.int32), last_expert)
    experts = jnp.arange(N_EXPERTS, dtype=jnp.int32)
    later = jnp.where((counts > 0)[None, :] & (experts[None, :] > experts[:, None]), experts[None, :], N_EXPERTS)
    next_nonempty = jnp.min(later, axis=1)
    next_nonempty = jnp.where(next_nonempty == N_EXPERTS, tile_expert[0], next_nonempty).astype(jnp.int32)
    prev_expert = jnp.concatenate([jnp.full((1,), -1, jnp.int32), tile_expert[:-1]])
    run_first = (tile_valid == 1) & (tile_expert != prev_expert)
    runs = jnp.stack([tile_expert, tile_valid, run_first.astype(jnp.int32), next_nonempty[tile_expert],
                      (tile_expert == last_expert).astype(jnp.int32)])
    return row_token, row_weight, dest, runs


def _row_copy(src_hbm, tiles_ref, sem, tile, sub, src_row):
    return pltpu.make_async_copy(src_hbm.at[src_row], tiles_ref.at[tile, :, sub, :], sem)


def _start_row_gather(src_hbm, tiles_ref, idx_ref, n, sem):
    def body(tile, carry):
        for sub in range(SUBLANES):
            _row_copy(src_hbm, tiles_ref, sem, tile, sub, idx_ref[0, 0, tile * SUBLANES + sub]).start(
                priority=sub % 2)
        return carry

    lax.fori_loop(0, n // SUBLANES, body, 0)


def _wait_row_gather(src_hbm, tiles_ref, n, sem):
    def body(tile, carry):
        for sub in range(SUBLANES):
            _row_copy(src_hbm, tiles_ref, sem, tile, sub, 0).wait()
        return carry

    lax.fori_loop(0, n // SUBLANES, body, 0)


def _gather_step(src_hbm, slots_ref, idx_ref, idx_next_ref, sems, n, *, more):
    i = pl.program_id(0)
    slot = i % 2

    @pl.when(i == 0)
    def _():
        _start_row_gather(src_hbm, slots_ref.at[0], idx_ref, n, sems.at[0])

    @pl.when(more)
    def _():
        _start_row_gather(src_hbm, slots_ref.at[1 - slot], idx_next_ref, n, sems.at[1 - slot])

    return slot


def _moe_gather_kernel(tv_ref, rt_ref, rt_next_ref, h_hbm, o_ref, xg_ref, sems, *, tm):
    i = pl.program_id(0)
    nxt = jnp.minimum(i + 1, pl.num_programs(0) - 1)
    valid = tv_ref[i] == 1
    slot = _gather_step(h_hbm, xg_ref, rt_ref, rt_next_ref, sems, tm, more=(nxt > i) & (tv_ref[nxt] == 1))

    @pl.when(valid)
    def _():
        _wait_row_gather(h_hbm, xg_ref.at[slot], tm, sems.at[slot])
        o_ref[...] = _tiles_to_rows(xg_ref.at[slot], 0, tm // SUBLANES).astype(o_ref.dtype)

    @pl.when(jnp.logical_not(valid))
    def _():
        o_ref[...] = jnp.zeros_like(o_ref)


RUN_EXPERT, RUN_VALID, RUN_FIRST, RUN_NEXT, RUN_LAST = range(5)


def _stream_expert_weights(runs_ref, w_hbm, stage_ref, wb_ref, sems, *, layer, tn):
    j, i = pl.program_id(0), pl.program_id(1)
    last_pass = j == pl.num_programs(0) - 1

    def copies(expert, col_block):
        col = pl.multiple_of(col_block * tn, tn)
        return [pltpu.make_async_copy(w.at[layer, expert, :, pl.ds(col, tn)], stage_ref.at[m], sems.at[m])
                for m, w in enumerate(w_hbm)]

    @pl.when((j == 0) & (i == 0))
    def _():
        for c in copies(runs_ref[RUN_EXPERT, 0], 0):
            c.start()

    @pl.when(runs_ref[RUN_FIRST, i] == 1)
    def _():
        for c in copies(runs_ref[RUN_EXPERT, i], j):
            c.wait()
        for m in range(len(w_hbm)):
            wb_ref[m] = stage_ref[m].astype(BF16)
        last_run = runs_ref[RUN_LAST, i] == 1

        @pl.when(jnp.logical_not(last_run & last_pass))
        def _():
            for c in copies(runs_ref[RUN_NEXT, i], jnp.where(last_run, j + 1, j)):
                c.start()


def _moe_up_kernel(runs_ref, x_ref, wg_hbm, wu_hbm, o_ref, stage_ref, wb_ref, sems, *, layer, tn):
    _stream_expert_weights(runs_ref, (wg_hbm, wu_hbm), stage_ref, wb_ref, sems, layer=layer, tn=tn)
    valid = runs_ref[RUN_VALID, pl.program_id(1)] == 1

    @pl.when(valid)
    def _():
        xt = x_ref[...]
        gate = jnp.dot(xt, wb_ref[0], preferred_element_type=F32)
        up = jnp.dot(xt, wb_ref[1], preferred_element_type=F32)
        o_ref[...] = (_silu(gate) * up).astype(o_ref.dtype)

    @pl.when(jnp.logical_not(valid))
    def _():
        o_ref[...] = jnp.zeros_like(o_ref)


def _moe_down_kernel(runs_ref, he_ref, wd_hbm, rw_ref, o_ref, stage_ref, wb_ref, sems, *, layer, tn):
    _stream_expert_weights(runs_ref, (wd_hbm,), stage_ref, wb_ref, sems, layer=layer, tn=tn)
    valid = runs_ref[RUN_VALID, pl.program_id(1)] == 1

    @pl.when(valid)
    def _():
        _to_token_major(o_ref, jnp.dot(he_ref[...], wb_ref[0], preferred_element_type=F32) * rw_ref[...])

    @pl.when(jnp.logical_not(valid))
    def _():
        o_ref[...] = jnp.zeros_like(o_ref)


def _moe_combine_ln_kernel(pos_ref, pos_next_ref, h_ref, y_hbm, g_ref, b_ref, o_ref, ob_ref, yg_ref, sems, *, tm):
    n = TOP_K * tm
    slot = _gather_step(y_hbm, yg_ref, pos_ref, pos_next_ref, sems, n,
                        more=pl.program_id(0) + 1 < pl.num_programs(0))
    _wait_row_gather(y_hbm, yg_ref.at[slot], n, sems.at[slot])
    per = tm // SUBLANES
    ffn = _tiles_to_rows(yg_ref.at[slot], 0, per) + _tiles_to_rows(yg_ref.at[slot], per, TOP_K * per)
    r = _layer_norm_rows(ALPHA * h_ref[...] + ffn, g_ref[...], b_ref[...])
    o_ref[...] = r
    ob_ref[...] = r.astype(BF16)


def _moe_block(h, h3, router_w, router_b, w_gate, w_up, w_down, layer, ln_g, ln_b, *, tm=MOE_TILE, tn_up=512,
               tn_down=2048, tm_out=128):
    t, d = h.shape
    slabs = d // LANES
    ids, wts = _router(h, router_w, router_b)
    row_token, row_weight, dest, runs = _route_metadata(ids, wts, tm)
    tile_valid = runs[RUN_VALID]
    n_rows = row_token.shape[0]
    n_tiles = n_rows // tm
    de = w_gate.shape[-1]
    tn_up, tn_down = _tile(de, tn_up), _tile(d, tn_down)

    def gather_scratch(n):
        return [pltpu.VMEM((2, n // SUBLANES, slabs, SUBLANES, LANES), F32), pltpu.SemaphoreType.DMA((2,))]

    tg = _tile(tm, MOE_GATHER_TILE)
    per = tm // tg
    n_gather = n_rows // tg
    row_token3 = row_token.reshape(n_gather, 1, tg)
    x_rows = pl.pallas_call(
        functools.partial(_moe_gather_kernel, tm=tg),
        grid_spec=pltpu.PrefetchScalarGridSpec(
            num_scalar_prefetch=1,
            grid=(n_gather,),
            in_specs=[
                pl.BlockSpec((1, 1, tg), lambda i, tv: (i, 0, 0), memory_space=pltpu.SMEM),
                pl.BlockSpec((1, 1, tg), lambda i, tv: (jnp.minimum(i + 1, n_gather - 1), 0, 0),
                             memory_space=pltpu.SMEM),
                pl.BlockSpec(memory_space=pl.ANY),
            ],
            out_specs=pl.BlockSpec((tg, d), lambda i, tv: (i, 0)),
            scratch_shapes=gather_scratch(tg),
        ),
        out_shape=jax.ShapeDtypeStruct((n_rows, d), BF16),
        compiler_params=_cparams(("arbitrary",)),
        name="moe_gather",
    )(jnp.repeat(tile_valid, per), row_token3, row_token3, h3)

    def weight_scratch(n_mats, k, tn):
        return [pltpu.VMEM((n_mats, k, tn), F32), pltpu.VMEM((n_mats, k, tn), BF16),
                pltpu.SemaphoreType.DMA((n_mats,))]

    he = pl.pallas_call(
        functools.partial(_moe_up_kernel, layer=layer, tn=tn_up),
        grid_spec=pltpu.PrefetchScalarGridSpec(
            num_scalar_prefetch=1,
            grid=(de // tn_up, n_tiles),
            in_specs=[
                pl.BlockSpec((tm, d), lambda j, i, runs: (i, 0)),
                pl.BlockSpec(memory_space=pl.ANY),
                pl.BlockSpec(memory_space=pl.ANY),
            ],
            out_specs=pl.BlockSpec((tm, tn_up), lambda j, i, runs: (i, j)),
            scratch_shapes=weight_scratch(2, d, tn_up),
        ),
        out_shape=jax.ShapeDtypeStruct((n_rows, de), BF16),
        compiler_params=_cparams(("arbitrary", "arbitrary")),
        name="moe_gate_up",
    )(runs, x_rows, w_gate, w_up)

    y_rows = pl.pallas_call(
        functools.partial(_moe_down_kernel, layer=layer, tn=tn_down),
        grid_spec=pltpu.PrefetchScalarGridSpec(
            num_scalar_prefetch=1,
            grid=(d // tn_down, n_tiles),
            in_specs=[
                pl.BlockSpec((tm, de), lambda j, i, runs: (i, 0)),
                pl.BlockSpec(memory_space=pl.ANY),
                pl.BlockSpec((tm, 1), lambda j, i, runs: (i, 0)),
            ],
            out_specs=pl.BlockSpec((tm, tn_down // LANES, LANES), lambda j, i, runs: (i, j, 0)),
            scratch_shapes=weight_scratch(1, de, tn_down),
        ),
        out_shape=jax.ShapeDtypeStruct((n_rows, slabs, LANES), F32),
        compiler_params=_cparams(("arbitrary", "arbitrary")),
        name="moe_down",
    )(runs, he, w_down, row_weight.reshape(n_rows, 1))

    tm_out = _tile(t, tm_out)
    n_out = t // tm_out
    pos = dest.reshape(n_out, tm_out, TOP_K).transpose(0, 2, 1).reshape(n_out, 1, TOP_K * tm_out)
    row = pl.BlockSpec((tm_out, d), lambda i: (i, 0))
    vec = pl.BlockSpec((1, d), lambda i: (0, 0))
    return pl.pallas_call(
        functools.partial(_moe_combine_ln_kernel, tm=tm_out),
        grid=(n_out,),
        in_specs=[
            pl.BlockSpec((1, 1, TOP_K * tm_out), lambda i: (i, 0, 0), memory_space=pltpu.SMEM),
            pl.BlockSpec((1, 1, TOP_K * tm_out), lambda i: (jnp.minimum(i + 1, n_out - 1), 0, 0),
                         memory_space=pltpu.SMEM),
            row,
            pl.BlockSpec(memory_space=pl.ANY),
            vec, vec,
        ],
        out_specs=[row, row],
        out_shape=[jax.ShapeDtypeStruct((t, d), F32), jax.ShapeDtypeStruct((t, d), BF16)],
        scratch_shapes=gather_scratch(TOP_K * tm_out),
        compiler_params=_cparams(("arbitrary",)),
        name="moe_combine_ln",
    )(pos, pos, h, y_rows, ln_g.reshape(1, d), ln_b.reshape(1, d))


def _shift_mix_kernel(x_ref, xp_ref, mix_ref, *rest, tm, seq):
    o_refs, pad_ref = rest[:-1], rest[-1]
    i = pl.program_id(0)
    starts_sequence = (i * tm) % seq == 0
    pad_ref[0:SUBLANES, :] = jnp.where(starts_sequence, 0.0, xp_ref[...])
    pad_ref[SUBLANES:SUBLANES + tm, :] = x_ref[...]
    xv = x_ref[...]
    xx = pad_ref[SUBLANES - 1:SUBLANES - 1 + tm, :] - xv
    for m, o_ref in enumerate(o_refs):
        o_ref[...] = (xv + xx * mix_ref[m:m + 1, :]).astype(o_ref.dtype)


def _shift_mix(x, mix, seq, *, tm=128):
    t, d = x.shape
    nm = mix.shape[0]
    tm = _tile(seq, tm)
    per = tm // SUBLANES
    return pl.pallas_call(
        functools.partial(_shift_mix_kernel, tm=tm, seq=seq),
        grid=(t // tm,),
        in_specs=[
            pl.BlockSpec((tm, d), lambda i: (i, 0)),
            pl.BlockSpec((SUBLANES, d), lambda i: (jnp.maximum(i * per - 1, 0), 0)),
            pl.BlockSpec((nm, d), lambda i: (0, 0)),
        ],
        out_specs=[pl.BlockSpec((tm, d), lambda i: (i, 0))] * nm,
        out_shape=[jax.ShapeDtypeStruct((t, d), BF16)] * nm,
        scratch_shapes=[pltpu.VMEM((SUBLANES + tm, d), F32)],
        compiler_params=_cparams(("parallel",)),
        name="rwkv_shift_mix",
    )(x, x, mix)


WKV_CHUNK = 64
WKV_HEADS = 4
WKV_WIDTH = WKV_HEADS * RWKV_HEAD
WKV_BLOCK = 512
NEUMANN_STEPS = 5


def _wkv_kernel(r_ref, k_ref, v_ref, wl_ref, ar_ref, g_ref, kk_ref, ka_ref, rk_ref, lnw_ref, lnb_ref,
                o_ref, state_ref, *, tb):
    lc, hw, wd = WKV_CHUNK, RWKV_HEAD, WKV_WIDTH
    nch = tb // lc

    @pl.when(pl.program_id(2) == 0)
    def _():
        state_ref[...] = jnp.zeros_like(state_ref)

    wrow = lax.broadcasted_iota(jnp.int32, (wd, wd), 0)
    wcol = lax.broadcasted_iota(jnp.int32, (wd, wd), 1)
    same_head = wrow // hw == wcol // hw
    ones_bd = same_head.astype(BF16)

    def block_diag(m):
        return jnp.where(same_head, jnp.concatenate([m] * WKV_HEADS, axis=0), 0.0).astype(BF16)

    def head_sum(m):
        hi = m.astype(BF16)
        lo = (m - hi.astype(F32)).astype(BF16)
        return jnp.dot(hi, ones_bd, preferred_element_type=F32) + jnp.dot(lo, ones_bd, preferred_element_type=F32)

    def mm(a, b):
        return jnp.dot(a.astype(BF16), b, preferred_element_type=F32)

    r = r_ref[...]
    k = k_ref[...]
    v = v_ref[...]
    w_log = -_softplus(-wl_ref[...]) - 0.5
    lw = -jnp.exp(w_log)
    a_sig = jax.nn.sigmoid(ar_ref[...])
    kx = k * kk_ref[...]
    kk = kx / jnp.maximum(jnp.sqrt(head_sum(kx * kx)), 1e-12)
    k2 = k * (1.0 + (a_sig - 1.0) * ka_ref[...])
    a_s = -kk
    b_s = kk * a_sig

    trow = lax.broadcasted_iota(jnp.int32, (tb, tb), 0)
    tcol = lax.broadcasted_iota(jnp.int32, (tb, tb), 1)
    same_chunk = trow // lc == tcol // lc
    cum = _dot_split3(lw, (same_chunk & (trow >= tcol)).astype(BF16), split_lhs=False)
    tot = jnp.concatenate([jnp.broadcast_to(cum[(c + 1) * lc - 1:(c + 1) * lc, :], (lc, wd)) for c in range(nch)],
                          axis=0)
    grow = jnp.exp(-cum)
    rt = r * jnp.exp(cum)
    at = a_s * jnp.exp(cum - lw)
    bt = b_s * grow
    kt = k2 * grow
    rest = jnp.exp(tot - cum)
    bh = b_s * rest
    kh = k2 * rest
    p_end = jnp.exp(tot)

    t_idx = lax.broadcasted_iota(jnp.int32, (lc, wd), 0)
    s_idx = lax.broadcasted_iota(jnp.int32, (lc, wd), 1) % hw
    strict = t_idx > s_idx
    incl = t_idx >= s_idx
    eye = (t_idx == s_idx).astype(F32)
    nt = (((1,), (1,)), ((), ()))

    tn = (((0,), (0,)), ((), ()))
    chunks = [slice(c * lc, (c + 1) * lc) for c in range(nch)]

    a_ab, a_rb, a_ak, a_rk = [], [], [], []
    for sl in chunks:
        ar = jnp.concatenate([at[sl], rt[sl]], axis=0).astype(BF16)
        xb = lax.dot_general(ar, block_diag(bt[sl]), nt, preferred_element_type=F32)
        xk = lax.dot_general(ar, block_diag(kt[sl]), nt, preferred_element_type=F32)
        a_ab.append(jnp.where(strict, xb[:lc], 0.0))
        a_rb.append(jnp.where(incl, xb[lc:], 0.0))
        a_ak.append(jnp.where(strict, xk[:lc], 0.0))
        a_rk.append(jnp.where(incl, xk[lc:], 0.0))
    inv = [eye + a for a in a_ab]
    pw = [mm(a, block_diag(a)) for a in a_ab]
    for step in range(1, NEUMANN_STEPS + 1):
        last = step == NEUMANN_STEPS
        lhs = inv if last else [jnp.concatenate([t, p], axis=0) for t, p in zip(inv, pw)]
        prod = [mm(x, block_diag(p)) for x, p in zip(lhs, pw)]
        inv = [t + q[:lc] for t, q in zip(inv, prod)]
        if not last:
            pw = [q[lc:] for q in prod]
    v_bd = [block_diag(v[sl]) for sl in chunks]
    ta = [mm(t, block_diag(at[sl])) for t, sl in zip(inv, chunks)]
    av = [mm(jnp.concatenate([ak, rk], axis=0), vb) for ak, rk, vb in zip(a_ak, a_rk, v_bd)]
    y0 = [x[lc:] for x in av]
    u0 = [mm(t, block_diag(x[:lc])) for t, x in zip(inv, av)]
    gain, add = [], []
    for c, sl in enumerate(chunks):
        bh_b = bh[sl].astype(BF16)
        gain.append(jnp.where(same_head, lax.dot_general(ta[c].astype(BF16), bh_b, tn, preferred_element_type=F32),
                              0.0).astype(BF16))
        uv = jnp.concatenate([u0[c], v[sl]], axis=0).astype(BF16)
        bk = jnp.concatenate([bh_b, kh[sl].astype(BF16)], axis=0)
        add.append(jnp.where(same_head, lax.dot_general(uv, bk, tn, preferred_element_type=F32), 0.0))

    states = [state_ref[...]]
    for c in range(nch):
        s0 = states[-1]
        states.append(s0 * p_end[c * lc:c * lc + 1, :] + mm(s0, gain[c]) + add[c])
    state_ref[...] = states[nch]

    ys = []
    for c, sl in enumerate(chunks):
        tr = jnp.concatenate([ta[c], rt[sl]], axis=0).astype(BF16)
        xs = lax.dot_general(tr, states[c].astype(BF16), nt, preferred_element_type=F32)
        u = xs[:lc] + u0[c]
        ys.append(xs[lc:] + mm(a_rb[c], block_diag(u)) + y0[c])

    y = jnp.concatenate(ys, axis=0)
    mu = head_sum(y) / hw
    dev = y - mu
    var = head_sum(dev * dev) / hw
    yn = dev * lax.rsqrt(var + RWKV_GN_EPS) * lnw_ref[...] + lnb_ref[...]
    bonus = head_sum(r * k2 * rk_ref[...]) * v
    o_ref[...] = ((yn + bonus) * g_ref[...]).astype(o_ref.dtype)


def _wkv(r, k, v, wl, araw, g, k_k, k_a, r_k, ln_w, ln_b, batch, seq):
    t, d = r.shape
    tb = _tile(seq, WKV_BLOCK)
    nt = seq // tb
    wd = WKV_WIDTH
    rows = pl.BlockSpec((tb, wd), lambda b, h, n: (b * nt + n, h))
    vec = pl.BlockSpec((1, wd), lambda b, h, n: (0, h))
    return pl.pallas_call(
        functools.partial(_wkv_kernel, tb=tb),
        grid=(batch, d // wd, nt),
        in_specs=[rows] * 6 + [vec] * 5,
        out_specs=rows,
        out_shape=jax.ShapeDtypeStruct((t, d), BF16),
        scratch_shapes=[pltpu.VMEM((wd, wd), F32)],
        compiler_params=_cparams(("parallel", "parallel", "arbitrary")),
        name="rwkv_wkv",
    )(r, k, v, wl, araw, g, k_k.reshape(1, d), k_a.reshape(1, d), r_k.reshape(1, d),
      ln_w.reshape(1, d), ln_b.reshape(1, d))


def _rwkv7_time_mix(x2, batch, seq, mix, w_r, w_k, w_v, w_o, w0, w1, w2, a0, a1, a2, g1, g2, k_k, k_a, r_k,
                    ln_w, ln_b):
    xm = _shift_mix(x2, mix, seq)
    bf = lambda w: w.astype(BF16)
    r = _matmul(xm[0], bf(w_r))
    k = _matmul(xm[2], bf(w_k))
    v = _matmul(xm[3], bf(w_v))
    wl = _low_rank(xm[1], bf(w1), bf(w2), w0, act="tanh")
    araw = _low_rank(xm[4], bf(a1), bf(a2), a0)
    g = _low_rank(xm[5], bf(g1), bf(g2), act="sigmoid")
    yg = _wkv(r, k, v, wl, araw, g, k_k, k_a, r_k, ln_w, ln_b, batch, seq)
    return _matmul(yg, bf(w_o))


def kernel(x, ab_w_in, ab_b_qkv, ssm_conv_w, ssm_conv_b, ssm_dt_bias, ssm_a_log, ssm_d, ssm_norm_w, attn_sinks,
           ab_w_out, ab_b_out, rwkv_mix, rwkv_w_r, rwkv_w_k, rwkv_w_v, rwkv_w_o, rwkv_w0, rwkv_w1, rwkv_w2,
           rwkv_a0, rwkv_a1, rwkv_a2, rwkv_g1, rwkv_g2, rwkv_k_k, rwkv_k_a, rwkv_r_k, rwkv_ln_w, rwkv_ln_b,
           ln_mix_g, ln_mix_b, ln_ffn_g, ln_ffn_b, router_w, router_b, moe_w_gate, moe_w_up, moe_w_down):
    batch, seq, d = x.shape
    x2 = x.reshape(batch * seq, d)
    xb = x2.astype(BF16)
    for layer in range(DEPTH):
        i = layer // 2
        if layer % 2 == 0:
            mix = _ssd_swa_mixer(x2, xb, batch, seq, ab_w_in[i], ab_b_qkv[i], ssm_conv_w[i], ssm_conv_b[i],
                                 ssm_dt_bias[i], ssm_a_log[i], ssm_d[i], ssm_norm_w[i], attn_sinks[i],
                                 ab_w_out[i], ab_b_out[i])
        else:
            mix = _rwkv7_time_mix(x2, batch, seq, rwkv_mix[i], rwkv_w_r[i], rwkv_w_k[i], rwkv_w_v[i], rwkv_w_o[i],
                                  rwkv_w0[i], rwkv_w1[i], rwkv_w2[i], rwkv_a0[i], rwkv_a1[i], rwkv_a2[i],
                                  rwkv_g1[i], rwkv_g2[i], rwkv_k_k[i], rwkv_k_a[i], rwkv_r_k[i].reshape(-1),
                                  rwkv_ln_w[i], rwkv_ln_b[i])
        h, h3 = _add_layer_norm(x2, mix, ln_mix_g[layer], ln_mix_b[layer])
        x2, xb = _moe_block(h, h3, router_w, router_b, moe_w_gate, moe_w_up, moe_w_down, layer,
                            ln_ffn_g[layer], ln_ffn_b[layer])
    return x2.reshape(batch, seq, d)
```

```python
import functools
import math

import jax
import jax.numpy as jnp
import numpy as np
from jax import lax
from jax.experimental import pallas as pl
from jax.experimental.pallas import tpu as pltpu

F32 = jnp.float32
BF16 = jnp.bfloat16

D_MODEL = 4096
DEPTH = 2
SSM_HEAD_DIM = 64
SSM_HEADS = 64
SSM_GROUPS = 8
SSM_HEADS_PER_GROUP = 8
SSM_STATE = 128
SSM_CONV = 4
SSM_CHUNK = 128
SSM_GROUP_WIDTH = SSM_HEADS_PER_GROUP * SSM_HEAD_DIM
SSM_CONV_CH = D_MODEL + 2 * SSM_GROUPS * SSM_STATE
SSM_NORM_EPS = 1e-5
ATT_HEADS = 64
ATT_KV_HEADS = 8
ATT_HEAD_DIM = 64
ATT_REP = 8
ATT_WINDOW = 128
ROPE_DIM = 16
ROPE_THETA = 500000.0
Q_COLS = 4096
KV_COLS = 512
RWKV_HEAD = 64
RWKV_GN_EPS = 64e-5
N_EXPERTS = 16
N_EXPERT_GROUPS = 4
EXPERTS_PER_GROUP = 4
TOP_K = 2
D_EXPERT = 1536
ALPHA = (2 * DEPTH) ** 0.25
LN_EPS = 1e-5

VMEM_LIMIT_BYTES = 56 * 1024 * 1024
LANES = 128
SUBLANES = 8


def _cparams(semantics):
    return pltpu.CompilerParams(dimension_semantics=semantics, vmem_limit_bytes=VMEM_LIMIT_BYTES)


def _tile(dim, pref):
    if dim <= pref:
        return dim
    t = pref
    while dim % t:
        t //= 2
    return t


def _silu(v):
    return v * jax.nn.sigmoid(v)


def _softplus(v):
    return jnp.maximum(v, 0.0) + jnp.log1p(jnp.exp(-jnp.abs(v)))


def _split3(v):
    hi = v.astype(BF16)
    rest = v - hi.astype(F32)
    mid = rest.astype(BF16)
    lo = (rest - mid.astype(F32)).astype(BF16)
    return hi, mid, lo


NT_DIMS = (((1,), (1,)), ((), ()))
NN_DIMS = (((1,), (0,)), ((), ()))


def _dot_bf16x3(a, b, dims):
    a_hi, b_hi = a.astype(BF16), b.astype(BF16)
    a_lo = (a - a_hi.astype(F32)).astype(BF16)
    b_lo = (b - b_hi.astype(F32)).astype(BF16)

    def dg(u, v):
        return lax.dot_general(u, v, dims, preferred_element_type=F32)

    return dg(a_hi, b_hi) + (dg(a_hi, b_lo) + dg(a_lo, b_hi))


def _dot_split3(v, m01, *, split_lhs):
    if split_lhs:
        return sum(jnp.dot(t, m01, preferred_element_type=F32) for t in _split3(v))
    return sum(jnp.dot(m01, t, preferred_element_type=F32) for t in _split3(v))


def _mm_kernel(*refs, act, bf16x3, w_is_nk):
    a_refs, (w_ref, b_ref, o_ref) = refs[:-3], refs[-3:]
    dims = NT_DIMS if w_is_nk else NN_DIMS
    r = b_ref[...]
    k0 = 0
    for a_ref in a_refs:
        k1 = k0 + a_ref.shape[1]
        w_part = w_ref[:, k0:k1] if w_is_nk else w_ref[k0:k1, :]
        if bf16x3:
            r = r + _dot_bf16x3(a_ref[...], w_part, dims)
        else:
            r = r + lax.dot_general(a_ref[...], w_part, dims, preferred_element_type=F32)
        k0 = k1
    if act == "tanh":
        r = jnp.tanh(r)
    elif act == "sigmoid":
        r = jax.nn.sigmoid(r)
    o_ref[...] = r.astype(o_ref.dtype)


MATMUL_VMEM_BUDGET = 44 * 1024 * 1024


def _matmul(a, w, bias=None, *, act=None, out_dtype=F32, bf16x3=False, tm=1024, col_start=0, n=None,
            w_is_nk=False):
    panels = a if isinstance(a, (tuple, list)) else (a,)
    m = panels[0].shape[0]
    kdim = sum(p.shape[1] for p in panels)
    w_n, w_k = (w.shape if w_is_nk else w.shape[::-1])
    n = w_n if n is None else n
    tm = _tile(m, tm)
    out_bytes = jnp.dtype(out_dtype).itemsize
    for tn in (512, 256, LANES):
        tn = _tile(n, tn)
        need = 2 * (tm * kdim * panels[0].dtype.itemsize + kdim * tn * w.dtype.itemsize + tm * tn * out_bytes)
        if need <= MATMUL_VMEM_BUDGET:
            break
    assert need <= MATMUL_VMEM_BUDGET and col_start % tn == 0 and w_k == kdim
    col0 = col_start // tn
    if bias is None:
        bias = jnp.zeros((n,), F32)
    bias = bias.reshape(1, n).astype(F32)
    if w_is_nk:
        w_spec = pl.BlockSpec((tn, kdim), lambda i, j: (col0 + j, 0))
    else:
        w_spec = pl.BlockSpec((kdim, tn), lambda i, j: (0, col0 + j))
    return pl.pallas_call(
        functools.partial(_mm_kernel, act=act, bf16x3=bf16x3, w_is_nk=w_is_nk),
        grid=(m // tm, n // tn),
        in_specs=[pl.BlockSpec((tm, p.shape[1]), lambda i, j: (i, 0)) for p in panels] + [
            w_spec,
            pl.BlockSpec((1, tn), lambda i, j: (0, j)),
        ],
        out_specs=pl.BlockSpec((tm, tn), lambda i, j: (i, j)),
        out_shape=jax.ShapeDtypeStruct((m, n), out_dtype),
        compiler_params=_cparams(("parallel", "arbitrary")),
        name="matmul",
    )(*panels, w, bias)


def _low_rank_kernel(a_ref, w1_ref, w2_ref, b_ref, o_ref, *, act):
    t = jnp.dot(a_ref[...], w1_ref[...], preferred_element_type=F32)
    if act == "tanh":
        t = jnp.tanh(t)
    elif act == "sigmoid":
        t = jax.nn.sigmoid(t)
    o_ref[...] = jnp.dot(t.astype(BF16), w2_ref[...], preferred_element_type=F32) + b_ref[...]


def _low_rank(a, w1, w2, bias=None, *, act=None, tm=512):
    m, kdim = a.shape
    r, n = w2.shape
    tm = _tile(m, tm)
    if bias is None:
        bias = jnp.zeros((n,), F32)
    return pl.pallas_call(
        functools.partial(_low_rank_kernel, act=act),
        grid=(m // tm,),
        in_specs=[
            pl.BlockSpec((tm, kdim), lambda i: (i, 0)),
            pl.BlockSpec((kdim, r), lambda i: (0, 0)),
            pl.BlockSpec((r, n), lambda i: (0, 0)),
            pl.BlockSpec((1, n), lambda i: (0, 0)),
        ],
        out_specs=pl.BlockSpec((tm, n), lambda i: (i, 0)),
        out_shape=jax.ShapeDtypeStruct((m, n), F32),
        compiler_params=_cparams(("parallel",)),
        name="low_rank",
    )(a, w1, w2, bias.reshape(1, n).astype(F32))


def _layer_norm_rows(v, g, b):
    mu = jnp.mean(v, -1, keepdims=True)
    var = jnp.mean(jnp.square(v - mu), -1, keepdims=True)
    return (v - mu) * lax.rsqrt(var + LN_EPS) * g + b


def _to_token_major(o3_ref, v):
    for s in range(v.shape[1] // LANES):
        o3_ref[:, s, :] = v[:, s * LANES:(s + 1) * LANES].astype(o3_ref.dtype)


def _tiles_to_rows(tiles_ref, t0, t1):
    return jnp.concatenate([tiles_ref[t0:t1, s].reshape((t1 - t0) * SUBLANES, LANES)
                            for s in range(tiles_ref.shape[1])], axis=1)


def _add_ln_kernel(x_ref, y_ref, g_ref, b_ref, o_ref, o3_ref):
    r = _layer_norm_rows(ALPHA * x_ref[...] + y_ref[...], g_ref[...], b_ref[...])
    o_ref[...] = r
    _to_token_major(o3_ref, r)


def _add_layer_norm(x, y, g, b, *, tm=128):
    t, d = x.shape
    tm = _tile(t, tm)
    row = pl.BlockSpec((tm, d), lambda i: (i, 0))
    row3 = pl.BlockSpec((tm, d // LANES, LANES), lambda i: (i, 0, 0))
    vec = pl.BlockSpec((1, d), lambda i: (0, 0))
    return pl.pallas_call(
        _add_ln_kernel,
        grid=(t // tm,),
        in_specs=[row, row, vec, vec],
        out_specs=[row, row3],
        out_shape=[jax.ShapeDtypeStruct((t, d), F32), jax.ShapeDtypeStruct((t, d // LANES, LANES), F32)],
        compiler_params=_cparams(("parallel",)),
        name="add_layer_norm",
    )(x, y, g.reshape(1, d), b.reshape(1, d))


CONV_HALO = SUBLANES
SSD_CH = SSM_GROUP_WIDTH + 2 * SSM_STATE


def _ssd_kernel(z_ref, x_ref, b_ref, c_ref, wx_ref, wb_ref, wc_ref, bx_ref, bb_ref, bc_ref,
                dtr_ref, dbc_ref, dbr_ref, alc_ref, alr_ref, d_ref, nw_ref,
                o_ref, state_ref, pad_ref):
    ch, gw, ns, hd = SSM_CHUNK, SSM_GROUP_WIDTH, SSM_STATE, SSM_HEAD_DIM

    @pl.when(pl.program_id(2) == 0)
    def _():
        state_ref[...] = jnp.zeros_like(state_ref)
        pad_ref[0:CONV_HALO, :] = jnp.zeros((CONV_HALO, SSD_CH), F32)

    pad_ref[CONV_HALO:CONV_HALO + ch, 0:gw] = x_ref[...]
    pad_ref[CONV_HALO:CONV_HALO + ch, gw:gw + ns] = b_ref[...]
    pad_ref[CONV_HALO:CONV_HALO + ch, gw + ns:SSD_CH] = c_ref[...]
    w = jnp.concatenate([wx_ref[...], wb_ref[...], wc_ref[...]], axis=1)
    acc = jnp.concatenate([bx_ref[...], bb_ref[...], bc_ref[...]], axis=1)
    base = CONV_HALO - (SSM_CONV - 1)
    for k in range(SSM_CONV):
        acc = acc + w[k:k + 1, :] * pad_ref[base + k:base + k + ch, :]
    pad_ref[0:CONV_HALO, :] = pad_ref[ch:ch + CONV_HALO, :]
    xbc = _silu(acc)
    xc, bm, cm = xbc[:, :gw], xbc[:, gw:gw + ns], xbc[:, gw + ns:]

    row = lax.broadcasted_iota(jnp.int32, (ch, ch), 0)
    col = lax.broadcasted_iota(jnp.int32, (ch, ch), 1)
    causal = row >= col
    dt_raw_r = dtr_ref[...]
    eye = (row == col).astype(BF16)
    dt_raw_c = sum(lax.dot_general(eye, term, (((1,), (1,)), ((), ())), preferred_element_type=F32)
                   for term in _split3(dt_raw_r))
    dt_c = _softplus(dt_raw_c + dbc_ref[0])
    dt_r = _softplus(dt_raw_r + dbr_ref[...])
    a_c = -jnp.exp(alc_ref[0])
    a_r = -jnp.exp(alr_ref[...])
    acs_c = _dot_split3(dt_c * a_c, causal.astype(BF16), split_lhs=False)
    acs_r = _dot_split3(dt_r * a_r, (row <= col).astype(BF16), split_lhs=True)

    hrow = lax.broadcasted_iota(jnp.int32, (SSM_HEADS_PER_GROUP, gw), 0)
    hcol = lax.broadcasted_iota(jnp.int32, (SSM_HEADS_PER_GROUP, gw), 1)
    expand = (hcol // hd == hrow).astype(BF16)
    dt_x = _dot_split3(dt_c, expand, split_lhs=True)
    acs_x = _dot_split3(acs_c, expand, split_lhs=True)

    xdt = xc * dt_x
    xdt_b = xdt.astype(BF16)
    bm_b, cm_b = bm.astype(BF16), cm.astype(BF16)
    cb = lax.dot_general(cm_b, bm_b, (((1,), (1,)), ((), ())), preferred_element_type=F32)
    prev = state_ref[...]
    y_off = jnp.dot(cm_b, prev.astype(BF16), preferred_element_type=F32) * jnp.exp(acs_x)

    lane = lax.broadcasted_iota(jnp.int32, (ch, 2 * hd), 1)
    pairs = []
    for pr in range(SSM_HEADS_PER_GROUP // 2):
        xp = xdt_b[:, pr * 2 * hd:(pr + 1) * 2 * hd]
        halves = []
        for q in range(2):
            r = 2 * pr + q
            diff = acs_c[:, r:r + 1] - acs_r[r:r + 1, :]
            seg = jnp.where(causal, jnp.exp(jnp.where(causal, diff, 0.0)), 0.0)
            halves.append(jnp.dot((cb * seg).astype(BF16), xp, preferred_element_type=F32))
        pairs.append(jnp.where(lane < hd, halves[0], halves[1]))
    y = jnp.concatenate(pairs, axis=1) + y_off + d_ref[...] * xc

    acs_last = acs_x[ch - 1:ch, :]
    contrib = lax.dot_general(bm_b, (xdt * jnp.exp(acs_last - acs_x)).astype(BF16),
                              (((0,), (0,)), ((), ())), preferred_element_type=F32)
    state_ref[...] = prev * jnp.exp(acs_last) + contrib

    u = y * _silu(z_ref[...])
    u = u * lax.rsqrt(jnp.mean(u * u, -1, keepdims=True) + SSM_NORM_EPS)
    o_ref[...] = (u * nw_ref[...]).astype(o_ref.dtype)


def _ssd_mixer(u_z, u_xbc, u_dt, conv_w, conv_b, dt_bias, a_log, d_skip, norm_w, batch, seq):
    t = batch * seq
    nc = seq // SSM_CHUNK
    g, r, gw, ns = SSM_GROUPS, SSM_HEADS_PER_GROUP, SSM_GROUP_WIDTH, SSM_STATE
    xblocks = D_MODEL // ns
    conv_wt = conv_w.T
    conv_b2 = conv_b.reshape(1, SSM_CONV_CH)
    dt_row = u_dt.T
    rows = lambda b, gi, c: (b * nc + c, gi)
    in_specs = [
        pl.BlockSpec((SSM_CHUNK, gw), rows),
        pl.BlockSpec((SSM_CHUNK, gw), rows),
        pl.BlockSpec((SSM_CHUNK, ns), lambda b, gi, c: (b * nc + c, xblocks + gi)),
        pl.BlockSpec((SSM_CHUNK, ns), lambda b, gi, c: (b * nc + c, xblocks + g + gi)),
        pl.BlockSpec((SSM_CONV, gw), lambda b, gi, c: (0, gi)),
        pl.BlockSpec((SSM_CONV, ns), lambda b, gi, c: (0, xblocks + gi)),
        pl.BlockSpec((SSM_CONV, ns), lambda b, gi, c: (0, xblocks + g + gi)),
        pl.BlockSpec((1, gw), lambda b, gi, c: (0, gi)),
        pl.BlockSpec((1, ns), lambda b, gi, c: (0, xblocks + gi)),
        pl.BlockSpec((1, ns), lambda b, gi, c: (0, xblocks + g + gi)),
        pl.BlockSpec((r, SSM_CHUNK), lambda b, gi, c: (gi, b * nc + c)),
        pl.BlockSpec((1, 1, r), lambda b, gi, c: (gi, 0, 0)),
        pl.BlockSpec((r, 1), lambda b, gi, c: (gi, 0)),
        pl.BlockSpec((1, 1, r), lambda b, gi, c: (gi, 0, 0)),
        pl.BlockSpec((r, 1), lambda b, gi, c: (gi, 0)),
        pl.BlockSpec((1, gw), lambda b, gi, c: (0, gi)),
        pl.BlockSpec((1, gw), lambda b, gi, c: (0, gi)),
    ]
    return pl.pallas_call(
        _ssd_kernel,
        grid=(batch, g, nc),
        in_specs=in_specs,
        out_specs=pl.BlockSpec((SSM_CHUNK, gw), rows),
        out_shape=jax.ShapeDtypeStruct((t, D_MODEL), BF16),
        scratch_shapes=[pltpu.VMEM((ns, gw), F32), pltpu.VMEM((CONV_HALO + SSM_CHUNK, SSD_CH), F32)],
        compiler_params=_cparams(("parallel", "parallel", "arbitrary")),
        name="ssd_mixer",
    )(u_z, u_xbc, u_xbc, u_xbc, conv_wt, conv_wt, conv_wt, conv_b2, conv_b2, conv_b2,
      dt_row, dt_bias.reshape(g, 1, r), dt_bias.reshape(SSM_HEADS, 1),
      a_log.reshape(g, 1, r), a_log.reshape(SSM_HEADS, 1),
      jnp.repeat(d_skip, SSM_HEAD_DIM).reshape(1, D_MODEL), norm_w.reshape(1, D_MODEL))


def _rotary_tables(seq):
    half = ROPE_DIM // 2
    inv_freq = ROPE_THETA ** (-jnp.arange(half, dtype=F32) / half)
    ang = jnp.arange(seq, dtype=F32)[:, None] * inv_freq[None, :]
    ones = jnp.ones((seq, ATT_HEAD_DIM - ROPE_DIM), F32)
    cos = jnp.concatenate([jnp.cos(ang), jnp.cos(ang), ones], -1)
    sin = jnp.concatenate([jnp.sin(ang), jnp.sin(ang), 0.0 * ones], -1)
    return jnp.tile(cos, (1, ATT_REP)), jnp.tile(sin, (1, ATT_REP))


def _rotate_half_matrix():
    width = ATT_REP * ATT_HEAD_DIM
    half = ROPE_DIM // 2
    p = np.zeros((width, width), np.float32)
    for j in range(width):
        if j % ATT_HEAD_DIM < half:
            p[j + half, j] = -1.0
        elif j % ATT_HEAD_DIM < ROPE_DIM:
            p[j - half, j] = 1.0
    return jnp.asarray(p, BF16)


def _rotate(v, cos, sin, perm):
    hi = v.astype(BF16)
    lo = (v - hi.astype(F32)).astype(BF16)
    partner = jnp.dot(hi, perm, preferred_element_type=F32) + jnp.dot(lo, perm, preferred_element_type=F32)
    return v * cos + partner * sin


def _swa_kernel(sink_ref, q_ref, kc_ref, kp_ref, vc_ref, vp_ref, cosc_ref, sinc_ref, cosp_ref, sinp_ref,
                perm_ref, o_ref):
    w, hd = ATT_WINDOW, ATT_HEAD_DIM
    h = pl.program_id(1)
    n = pl.program_id(2)
    perm = perm_ref[...]
    perm_k = perm[:hd, :hd]
    q = _rotate(q_ref[...], cosc_ref[...], sinc_ref[...], perm)
    k_cur = _rotate(kc_ref[0, 0], cosc_ref[:, :hd], sinc_ref[:, :hd], perm_k)
    k_prev = _rotate(kp_ref[0, 0], cosp_ref[:, :hd], sinp_ref[:, :hd], perm_k)
    k_all = jnp.concatenate([k_prev, k_cur], axis=0).astype(BF16)
    v_all = jnp.concatenate([vp_ref[0, 0], vc_ref[0, 0]], axis=0).astype(BF16)
    qi = lax.broadcasted_iota(jnp.int32, (w, 2 * w), 0) + w
    ki = lax.broadcasted_iota(jnp.int32, (w, 2 * w), 1)
    rel = qi - ki
    first = jnp.where(n > 0, 0, w)
    bias = jnp.where((rel >= 0) & (rel < ATT_WINDOW) & (ki >= first), 0.0, -jnp.inf)
    heads = range(ATT_REP)
    qs = [(q[:, r * hd:(r + 1) * hd] * (hd ** -0.5)).astype(BF16) for r in heads]
    sinks = [sink_ref[h * ATT_REP + r] for r in heads]
    ss = [lax.dot_general(qr, k_all, (((1,), (1,)), ((), ())), preferred_element_type=F32) + bias for qr in qs]
    ms = [jnp.maximum(jnp.max(s, -1, keepdims=True), sink) for s, sink in zip(ss, sinks)]
    es = [jnp.exp(s - m) for s, m in zip(ss, ms)]
    dens = [jnp.sum(e, -1, keepdims=True) + jnp.exp(sink - m) for e, sink, m in zip(es, sinks, ms)]
    outs = [jnp.dot(e.astype(BF16), v_all, preferred_element_type=F32) / den for e, den in zip(es, dens)]
    o_ref[...] = jnp.concatenate(outs, axis=1).astype(o_ref.dtype)


def _swa_attention(qkv, sinks, batch, seq):
    t = batch * seq
    nb = seq // ATT_WINDOW
    hd, w = ATT_HEAD_DIM, ATT_WINDOW
    qw = ATT_REP * hd
    k4 = qkv[:, Q_COLS:Q_COLS + KV_COLS].reshape(batch, seq, ATT_KV_HEADS, hd).transpose(0, 2, 1, 3)
    v4 = qkv[:, Q_COLS + KV_COLS:].reshape(batch, seq, ATT_KV_HEADS, hd).transpose(0, 2, 1, 3)
    cos, sin = _rotary_tables(seq)
    cur = pl.BlockSpec((1, 1, w, hd), lambda b, h, n: (b, h, n, 0))
    prev = pl.BlockSpec((1, 1, w, hd), lambda b, h, n: (b, h, jnp.maximum(n - 1, 0), 0))
    tab_cur = pl.BlockSpec((w, qw), lambda b, h, n: (n, 0))
    tab_prev = pl.BlockSpec((w, qw), lambda b, h, n: (jnp.maximum(n - 1, 0), 0))
    return pl.pallas_call(
        _swa_kernel,
        grid=(batch, ATT_KV_HEADS, nb),
        in_specs=[
            pl.BlockSpec(memory_space=pltpu.SMEM),
            pl.BlockSpec((w, qw), lambda b, h, n: (b * nb + n, h)),
            cur, prev, cur, prev, tab_cur, tab_cur, tab_prev, tab_prev,
            pl.BlockSpec((qw, qw), lambda b, h, n: (0, 0)),
        ],
        out_specs=pl.BlockSpec((w, qw), lambda b, h, n: (b * nb + n, h)),
        out_shape=jax.ShapeDtypeStruct((t, Q_COLS), BF16),
        compiler_params=_cparams(("parallel", "parallel", "arbitrary")),
        name="swa_attention",
    )(sinks, qkv, k4, k4, v4, v4, cos, sin, cos, sin, _rotate_half_matrix())


def _ssd_swa_mixer(x2, xb, batch, seq, w_in, b_qkv, conv_w, conv_b, dt_bias, a_log, d_skip, norm_w, sinks,
                   w_out, b_out):
    o1 = D_MODEL
    o2 = o1 + SSM_CONV_CH
    o3 = o2 + SSM_HEADS
    w_t = jnp.swapaxes(w_in, 0, 1)
    w_t_b = w_t.astype(BF16)
    u_z = _matmul(xb, w_t_b, n=o1, w_is_nk=True)
    u_xbc = _matmul(xb, w_t_b, col_start=o1, n=SSM_CONV_CH, w_is_nk=True)
    u_dt = _matmul(x2, w_t[o2:o3], bf16x3=True, w_is_nk=True)
    qkv = _matmul(xb, w_t_b[o3:], b_qkv, w_is_nk=True)
    y_ssm = _ssd_mixer(u_z, u_xbc, u_dt, conv_w, conv_b, dt_bias, a_log, d_skip, norm_w, batch, seq)
    y_att = _swa_attention(qkv, sinks, batch, seq)
    return _matmul((y_ssm, y_att), w_out.astype(BF16), b_out)


MOE_TILE = 256
MOE_GATHER_TILE = 256
ROUTER_TILE = 256


def _router_kernel(h_ref, wt_ref, b_ref, ids_ref, wts_ref):
    logits = _dot_bf16x3(wt_ref[...], h_ref[...], NT_DIMS) + b_ref[...]
    e = jnp.exp(logits - jnp.max(logits, 0, keepdims=True))
    probs = e / jnp.sum(e, 0, keepdims=True)
    p = [probs[i:i + 1, :] for i in range(N_EXPERTS)]

    keep, score = [], []
    for g in range(N_EXPERT_GROUPS):
        members = range(g * EXPERTS_PER_GROUP, (g + 1) * EXPERTS_PER_GROUP)
        s = jnp.zeros_like(p[0])
        for i in members:
            rank = jnp.zeros_like(p[0])
            for j in members:
                if j != i:
                    beats = (p[j] > p[i]) | ((p[j] == p[i]) & (j < i)) if j < i else (p[j] > p[i])
                    rank = rank + beats.astype(F32)
            keep.append(rank < TOP_K)
            s = s + jnp.where(keep[i], p[i], 0.0)
        score.append(s)
    chosen = []
    for g in range(N_EXPERT_GROUPS):
        c = jnp.ones_like(p[0]) > 0
        for g2 in range(N_EXPERT_GROUPS):
            if g2 < g:
                c = c & (score[g] > score[g2])
            elif g2 > g:
                c = c & (score[g] >= score[g2])
        chosen.append(c)
    denom = jnp.zeros_like(p[0])
    for g in range(N_EXPERT_GROUPS):
        denom = denom + jnp.where(chosen[g], score[g], 0.0)
    count = jnp.zeros_like(p[0])
    id0 = jnp.zeros_like(p[0])
    id1 = jnp.zeros_like(p[0])
    w0 = jnp.zeros_like(p[0])
    w1 = jnp.zeros_like(p[0])
    for i in range(N_EXPERTS):
        sel = keep[i] & chosen[i // EXPERTS_PER_GROUP]
        gate = p[i] / denom
        first = sel & (count == 0.0)
        second = sel & (count == 1.0)
        id0 = jnp.where(first, float(i), id0)
        w0 = jnp.where(first, gate, w0)
        id1 = jnp.where(second, float(i), id1)
        w1 = jnp.where(second, gate, w1)
        count = count + sel.astype(F32)
    ids_ref[0:1, :] = id0.astype(jnp.int32)
    ids_ref[1:2, :] = id1.astype(jnp.int32)
    wts_ref[0:1, :] = w0
    wts_ref[1:2, :] = w1


def _router(h, router_w, router_b):
    t, d = h.shape
    tm = _tile(t, ROUTER_TILE)
    return pl.pallas_call(
        _router_kernel,
        grid=(t // tm,),
        in_specs=[
            pl.BlockSpec((tm, d), lambda i: (i, 0)),
            pl.BlockSpec((N_EXPERTS, d), lambda i: (0, 0)),
            pl.BlockSpec((N_EXPERTS, 1), lambda i: (0, 0)),
        ],
        out_specs=[pl.BlockSpec((TOP_K, tm), lambda i: (0, i)), pl.BlockSpec((TOP_K, tm), lambda i: (0, i))],
        out_shape=[jax.ShapeDtypeStruct((TOP_K, t), jnp.int32), jax.ShapeDtypeStruct((TOP_K, t), F32)],
        compiler_params=_cparams(("parallel",)),
        name="moe_router",
    )(h, router_w.T, router_b.reshape(N_EXPERTS, 1))


def _route_metadata(ids, wts, tm):
    t = ids.shape[1]
    na = TOP_K * t
    eid = ids.T.reshape(na)
    onehot = (eid[:, None] == jnp.arange(N_EXPERTS, dtype=jnp.int32)[None, :]).astype(jnp.int32)
    csum = jnp.cumsum(onehot, axis=0)
    counts = csum[-1]
    rank = jnp.sum(csum * onehot, axis=1) - 1
    padded = (counts + tm - 1) // tm * tm
    pend = jnp.cumsum(padded)
    dest = (pend - padded)[eid] + rank
    n_tiles = na // tm + N_EXPERTS
    n_rows = n_tiles * tm
    row_assign = jnp.full((n_rows,), -1, jnp.int32).at[dest].set(jnp.arange(na, dtype=jnp.int32))
    assigned = jnp.maximum(row_assign, 0)
    row_token = assigned // TOP_K
    row_weight = jnp.where(row_assign >= 0, wts.T.reshape(na)[assigned], 0.0)
    tile_start = jnp.arange(n_tiles, dtype=jnp.int32) * tm
    tile_valid = (tile_start < pend[-1]).astype(jnp.int32)
    last_expert = jnp.max(jnp.where(counts > 0, jnp.arange(N_EXPERTS, dtype=jnp.int32), 0))
    tile_expert = jnp.minimum(jnp.searchsorted(pend, tile_start, side="right").astype(jnp.int32), last_expert)
    experts = jnp.arange(N_EXPERTS, dtype=jnp.int32)
    later = jnp.where((counts > 0)[None, :] & (experts[None, :] > experts[:, None]), experts[None, :], N_EXPERTS)
    next_nonempty = jnp.min(later, axis=1)
    next_nonempty = jnp.where(next_nonempty == N_EXPERTS, tile_expert[0], next_nonempty).astype(jnp.int32)
    prev_expert = jnp.concatenate([jnp.full((1,), -1, jnp.int32), tile_expert[:-1]])
    run_first = (tile_valid == 1) & (tile_expert != prev_expert)
    runs = jnp.stack([tile_expert, tile_valid, run_first.astype(jnp.int32), next_nonempty[tile_expert],
                      (tile_expert == last_expert).astype(jnp.int32)])
    return row_token, row_weight, dest, runs


def _row_copy(src_hbm, tiles_ref, sem, tile, sub, src_row):
    return pltpu.make_async_copy(src_hbm.at[src_row], tiles_ref.at[tile, :, sub, :], sem)


def _start_row_gather(src_hbm, tiles_ref, idx_ref, n, sem):
    def body(tile, carry):
        for sub in range(SUBLANES):
            _row_copy(src_hbm, tiles_ref, sem, tile, sub, idx_ref[0, 0, tile * SUBLANES + sub]).start(
                priority=sub % 2)
        return carry

    lax.fori_loop(0, n // SUBLANES, body, 0)


def _wait_row_gather(src_hbm, tiles_ref, n, sem):
    def body(tile, carry):
        for sub in range(SUBLANES):
            _row_copy(src_hbm, tiles_ref, sem, tile, sub, 0).wait()
        return carry

    lax.fori_loop(0, n // SUBLANES, body, 0)


def _gather_step(src_hbm, slots_ref, idx_ref, idx_next_ref, sems, n, *, more):
    i = pl.program_id(0)
    slot = i % 2

    @pl.when(i == 0)
    def _():
        _start_row_gather(src_hbm, slots_ref.at[0], idx_ref, n, sems.at[0])

    @pl.when(more)
    def _():
        _start_row_gather(src_hbm, slots_ref.at[1 - slot], idx_next_ref, n, sems.at[1 - slot])

    return slot


def _moe_gather_kernel(tv_ref, rt_ref, rt_next_ref, h_hbm, o_ref, xg_ref, sems, *, tm):
    i = pl.program_id(0)
    nxt = jnp.minimum(i + 1, pl.num_programs(0) - 1)
    valid = tv_ref[i] == 1
    slot = _gather_step(h_hbm, xg_ref, rt_ref, rt_next_ref, sems, tm, more=(nxt > i) & (tv_ref[nxt] == 1))

    @pl.when(valid)
    def _():
        _wait_row_gather(h_hbm, xg_ref.at[slot], tm, sems.at[slot])
        o_ref[...] = _tiles_to_rows(xg_ref.at[slot], 0, tm // SUBLANES).astype(o_ref.dtype)

    @pl.when(jnp.logical_not(valid))
    def _():
        o_ref[...] = jnp.zeros_like(o_ref)


RUN_EXPERT, RUN_VALID, RUN_FIRST, RUN_NEXT, RUN_LAST = range(5)


def _stream_expert_weights(runs_ref, w_hbm, stage_ref, wb_ref, sems, *, layer, tn):
    j, i = pl.program_id(0), pl.program_id(1)
    last_pass = j == pl.num_programs(0) - 1

    def copies(expert, col_block):
        col = pl.multiple_of(col_block * tn, tn)
        return [pltpu.make_async_copy(w.at[layer, expert, :, pl.ds(col, tn)], stage_ref.at[m], sems.at[m])
                for m, w in enumerate(w_hbm)]

    @pl.when((j == 0) & (i == 0))
    def _():
        for c in copies(runs_ref[RUN_EXPERT, 0], 0):
            c.start()

    @pl.when(runs_ref[RUN_FIRST, i] == 1)
    def _():
        for c in copies(runs_ref[RUN_EXPERT, i], j):
            c.wait()
        for m in range(len(w_hbm)):
            wb_ref[m] = stage_ref[m].astype(BF16)
        last_run = runs_ref[RUN_LAST, i] == 1

        @pl.when(jnp.logical_not(last_run & last_pass))
        def _():
            for c in copies(runs_ref[RUN_NEXT, i], jnp.where(last_run, j + 1, j)):
                c.start()


def _moe_up_kernel(runs_ref, x_ref, wg_hbm, wu_hbm, o_ref, stage_ref, wb_ref, sems, *, layer, tn):
    _stream_expert_weights(runs_ref, (wg_hbm, wu_hbm), stage_ref, wb_ref, sems, layer=layer, tn=tn)
    valid = runs_ref[RUN_VALID, pl.program_id(1)] == 1

    @pl.when(valid)
    def _():
        xt = x_ref[...]
        gate = jnp.dot(xt, wb_ref[0], preferred_element_type=F32)
        up = jnp.dot(xt, wb_ref[1], preferred_element_type=F32)
        o_ref[...] = (_silu(gate) * up).astype(o_ref.dtype)

    @pl.when(jnp.logical_not(valid))
    def _():
        o_ref[...] = jnp.zeros_like(o_ref)


def _moe_down_kernel(runs_ref, he_ref, wd_hbm, rw_ref, o_ref, stage_ref, wb_ref, sems, *, layer, tn):
    _stream_expert_weights(runs_ref, (wd_hbm,), stage_ref, wb_ref, sems, layer=layer, tn=tn)
    valid = runs_ref[RUN_VALID, pl.program_id(1)] == 1

    @pl.when(valid)
    def _():
        _to_token_major(o_ref, jnp.dot(he_ref[...], wb_ref[0], preferred_element_type=F32) * rw_ref[...])

    @pl.when(jnp.logical_not(valid))
    def _():
        o_ref[...] = jnp.zeros_like(o_ref)


def _moe_combine_ln_kernel(pos_ref, pos_next_ref, h_ref, y_hbm, g_ref, b_ref, o_ref, ob_ref, yg_ref, sems, *, tm):
    n = TOP_K * tm
    slot = _gather_step(y_hbm, yg_ref, pos_ref, pos_next_ref, sems, n,
                        more=pl.program_id(0) + 1 < pl.num_programs(0))
    _wait_row_gather(y_hbm, yg_ref.at[slot], n, sems.at[slot])
    per = tm // SUBLANES
    ffn = _tiles_to_rows(yg_ref.at[slot], 0, per) + _tiles_to_rows(yg_ref.at[slot], per, TOP_K * per)
    r = _layer_norm_rows(ALPHA * h_ref[...] + ffn, g_ref[...], b_ref[...])
    o_ref[...] = r
    ob_ref[...] = r.astype(BF16)


def _moe_block(h, h3, router_w, router_b, w_gate, w_up, w_down, layer, ln_g, ln_b, *, tm=MOE_TILE, tn_up=512,
               tn_down=2048, tm_out=128):
    t, d = h.shape
    slabs = d // LANES
    ids, wts = _router(h, router_w, router_b)
    row_token, row_weight, dest, runs = _route_metadata(ids, wts, tm)
    tile_valid = runs[RUN_VALID]
    n_rows = row_token.shape[0]
    n_tiles = n_rows // tm
    de = w_gate.shape[-1]
    tn_up, tn_down = _tile(de, tn_up), _tile(d, tn_down)

    def gather_scratch(n):
        return [pltpu.VMEM((2, n // SUBLANES, slabs, SUBLANES, LANES), F32), pltpu.SemaphoreType.DMA((2,))]

    tg = _tile(tm, MOE_GATHER_TILE)
    per = tm // tg
    n_gather = n_rows // tg
    row_token3 = row_token.reshape(n_gather, 1, tg)
    x_rows = pl.pallas_call(
        functools.partial(_moe_gather_kernel, tm=tg),
        grid_spec=pltpu.PrefetchScalarGridSpec(
            num_scalar_prefetch=1,
            grid=(n_gather,),
            in_specs=[
                pl.BlockSpec((1, 1, tg), lambda i, tv: (i, 0, 0), memory_space=pltpu.SMEM),
                pl.BlockSpec((1, 1, tg), lambda i, tv: (jnp.minimum(i + 1, n_gather - 1), 0, 0),
                             memory_space=pltpu.SMEM),
                pl.BlockSpec(memory_space=pl.ANY),
            ],
            out_specs=pl.BlockSpec((tg, d), lambda i, tv: (i, 0)),
            scratch_shapes=gather_scratch(tg),
        ),
        out_shape=jax.ShapeDtypeStruct((n_rows, d), BF16),
        compiler_params=_cparams(("arbitrary",)),
        name="moe_gather",
    )(jnp.repeat(tile_valid, per), row_token3, row_token3, h3)

    def weight_scratch(n_mats, k, tn):
        return [pltpu.VMEM((n_mats, k, tn), F32), pltpu.VMEM((n_mats, k, tn), BF16),
                pltpu.SemaphoreType.DMA((n_mats,))]

    he = pl.pallas_call(
        functools.partial(_moe_up_kernel, layer=layer, tn=tn_up),
        grid_spec=pltpu.PrefetchScalarGridSpec(
            num_scalar_prefetch=1,
            grid=(de // tn_up, n_tiles),
            in_specs=[
                pl.BlockSpec((tm, d), lambda j, i, runs: (i, 0)),
                pl.BlockSpec(memory_space=pl.ANY),
                pl.BlockSpec(memory_space=pl.ANY),
            ],
            out_specs=pl.BlockSpec((tm, tn_up), lambda j, i, runs: (i, j)),
            scratch_shapes=weight_scratch(2, d, tn_up),
        ),
        out_shape=jax.ShapeDtypeStruct((n_rows, de), BF16),
        compiler_params=_cparams(("arbitrary", "arbitrary")),
        name="moe_gate_up",
    )(runs, x_rows, w_gate, w_up)

    y_rows = pl.pallas_call(
        functools.partial(_moe_down_kernel, layer=layer, tn=tn_down),
        grid_spec=pltpu.PrefetchScalarGridSpec(
            num_scalar_prefetch=1,
            grid=(d // tn_down, n_tiles),
            in_specs=[
                pl.BlockSpec((tm, de), lambda j, i, runs: (i, 0)),
                pl.BlockSpec(memory_space=pl.ANY),
                pl.BlockSpec((tm, 1), lambda j, i, runs: (i, 0)),
            ],
            out_specs=pl.BlockSpec((tm, tn_down // LANES, LANES), lambda j, i, runs: (i, j, 0)),
            scratch_shapes=weight_scratch(1, de, tn_down),
        ),
        out_shape=jax.ShapeDtypeStruct((n_rows, slabs, LANES), F32),
        compiler_params=_cparams(("arbitrary", "arbitrary")),
        name="moe_down",
    )(runs, he, w_down, row_weight.reshape(n_rows, 1))

    tm_out = _tile(t, tm_out)
    n_out = t // tm_out
    pos = dest.reshape(n_out, tm_out, TOP_K).transpose(0, 2, 1).reshape(n_out, 1, TOP_K * tm_out)
    row = pl.BlockSpec((tm_out, d), lambda i: (i, 0))
    vec = pl.BlockSpec((1, d), lambda i: (0, 0))
    return pl.pallas_call(
        functools.partial(_moe_combine_ln_kernel, tm=tm_out),
        grid=(n_out,),
        in_specs=[
            pl.BlockSpec((1, 1, TOP_K * tm_out), lambda i: (i, 0, 0), memory_space=pltpu.SMEM),
            pl.BlockSpec((1, 1, TOP_K * tm_out), lambda i: (jnp.minimum(i + 1, n_out - 1), 0, 0),
                         memory_space=pltpu.SMEM),
            row,
            pl.BlockSpec(memory_space=pl.ANY),
            vec, vec,
        ],
        out_specs=[row, row],
        out_shape=[jax.ShapeDtypeStruct((t, d), F32), jax.ShapeDtypeStruct((t, d), BF16)],
        scratch_shapes=gather_scratch(TOP_K * tm_out),
        compiler_params=_cparams(("arbitrary",)),
        name="moe_combine_ln",
    )(pos, pos, h, y_rows, ln_g.reshape(1, d), ln_b.reshape(1, d))


def _shift_mix_kernel(x_ref, xp_ref, mix_ref, *rest, tm, seq):
    o_refs, pad_ref = rest[:-1], rest[-1]
    i = pl.program_id(0)
    starts_sequence = (i * tm) % seq == 0
    pad_ref[0:SUBLANES, :] = jnp.where(starts_sequence, 0.0, xp_ref[...])
    pad_ref[SUBLANES:SUBLANES + tm, :] = x_ref[...]
    xv = x_ref[...]
    xx = pad_ref[SUBLANES - 1:SUBLANES - 1 + tm, :] - xv
    for m, o_ref in enumerate(o_refs):
        o_ref[...] = (xv + xx * mix_ref[m:m + 1, :]).astype(o_ref.dtype)


def _shift_mix(x, mix, seq, *, tm=128):
    t, d = x.shape
    nm = mix.shape[0]
    tm = _tile(seq, tm)
    per = tm // SUBLANES
    return pl.pallas_call(
        functools.partial(_shift_mix_kernel, tm=tm, seq=seq),
        grid=(t // tm,),
        in_specs=[
            pl.BlockSpec((tm, d), lambda i: (i, 0)),
            pl.BlockSpec((SUBLANES, d), lambda i: (jnp.maximum(i * per - 1, 0), 0)),
            pl.BlockSpec((nm, d), lambda i: (0, 0)),
        ],
        out_specs=[pl.BlockSpec((tm, d), lambda i: (i, 0))] * nm,
        out_shape=[jax.ShapeDtypeStruct((t, d), BF16)] * nm,
        scratch_shapes=[pltpu.VMEM((SUBLANES + tm, d), F32)],
        compiler_params=_cparams(("parallel",)),
        name="rwkv_shift_mix",
    )(x, x, mix)


WKV_CHUNK = 64
WKV_HEADS = 4
WKV_WIDTH = WKV_HEADS * RWKV_HEAD
WKV_BLOCK = 512
NEUMANN_STEPS = 5


def _wkv_kernel(r_ref, k_ref, v_ref, wl_ref, ar_ref, g_ref, kk_ref, ka_ref, rk_ref, lnw_ref, lnb_ref,
                o_ref, state_ref, *, tb):
    lc, hw, wd = WKV_CHUNK, RWKV_HEAD, WKV_WIDTH
    nch = tb // lc

    @pl.when(pl.program_id(2) == 0)
    def _():
        state_ref[...] = jnp.zeros_like(state_ref)

    wrow = lax.broadcasted_iota(jnp.int32, (wd, wd), 0)
    wcol = lax.broadcasted_iota(jnp.int32, (wd, wd), 1)
    same_head = wrow // hw == wcol // hw
    ones_bd = same_head.astype(BF16)

    def block_diag(m):
        return jnp.where(same_head, jnp.concatenate([m] * WKV_HEADS, axis=0), 0.0).astype(BF16)

    def head_sum(m):
        hi = m.astype(BF16)
        lo = (m - hi.astype(F32)).astype(BF16)
        return jnp.dot(hi, ones_bd, preferred_element_type=F32) + jnp.dot(lo, ones_bd, preferred_element_type=F32)

    def mm(a, b):
        return jnp.dot(a.astype(BF16), b, preferred_element_type=F32)

    r = r_ref[...]
    k = k_ref[...]
    v = v_ref[...]
    w_log = -_softplus(-wl_ref[...]) - 0.5
    lw = -jnp.exp(w_log)
    a_sig = jax.nn.sigmoid(ar_ref[...])
    kx = k * kk_ref[...]
    kk = kx / jnp.maximum(jnp.sqrt(head_sum(kx * kx)), 1e-12)
    k2 = k * (1.0 + (a_sig - 1.0) * ka_ref[...])
    a_s = -kk
    b_s = kk * a_sig

    trow = lax.broadcasted_iota(jnp.int32, (tb, tb), 0)
    tcol = lax.broadcasted_iota(jnp.int32, (tb, tb), 1)
    same_chunk = trow // lc == tcol // lc
    cum = _dot_split3(lw, (same_chunk & (trow >= tcol)).astype(BF16), split_lhs=False)
    tot = jnp.concatenate([jnp.broadcast_to(cum[(c + 1) * lc - 1:(c + 1) * lc, :], (lc, wd)) for c in range(nch)],
                          axis=0)
    grow = jnp.exp(-cum)
    rt = r * jnp.exp(cum)
    at = a_s * jnp.exp(cum - lw)
    bt = b_s * grow
    kt = k2 * grow
    rest = jnp.exp(tot - cum)
    bh = b_s * rest
    kh = k2 * rest
    p_end = jnp.exp(tot)

    t_idx = lax.broadcasted_iota(jnp.int32, (lc, wd), 0)
    s_idx = lax.broadcasted_iota(jnp.int32, (lc, wd), 1) % hw
    strict = t_idx > s_idx
    incl = t_idx >= s_idx
    eye = (t_idx == s_idx).astype(F32)
    nt = (((1,), (1,)), ((), ()))

    tn = (((0,), (0,)), ((), ()))
    chunks = [slice(c * lc, (c + 1) * lc) for c in range(nch)]

    a_ab, a_rb, a_ak, a_rk = [], [], [], []
    for sl in chunks:
        ar = jnp.concatenate([at[sl], rt[sl]], axis=0).astype(BF16)
        xb = lax.dot_general(ar, block_diag(bt[sl]), nt, preferred_element_type=F32)
        xk = lax.dot_general(ar, block_diag(kt[sl]), nt, preferred_element_type=F32)
        a_ab.append(jnp.where(strict, xb[:lc], 0.0))
        a_rb.append(jnp.where(incl, xb[lc:], 0.0))
        a_ak.append(jnp.where(strict, xk[:lc], 0.0))
        a_rk.append(jnp.where(incl, xk[lc:], 0.0))
    inv = [eye + a for a in a_ab]
    pw = [mm(a, block_diag(a)) for a in a_ab]
    for step in range(1, NEUMANN_STEPS + 1):
        last = step == NEUMANN_STEPS
        lhs = inv if last else [jnp.concatenate([t, p], axis=0) for t, p in zip(inv, pw)]
        prod = [mm(x, block_diag(p)) for x, p in zip(lhs, pw)]
        inv = [t + q[:lc] for t, q in zip(inv, prod)]
        if not last:
            pw = [q[lc:] for q in prod]
    v_bd = [block_diag(v[sl]) for sl in chunks]
    ta = [mm(t, block_diag(at[sl])) for t, sl in zip(inv, chunks)]
    av = [mm(jnp.concatenate([ak, rk], axis=0), vb) for ak, rk, vb in zip(a_ak, a_rk, v_bd)]
    y0 = [x[lc:] for x in av]
    u0 = [mm(t, block_diag(x[:lc])) for t, x in zip(inv, av)]
    gain, add = [], []
    for c, sl in enumerate(chunks):
        bh_b = bh[sl].astype(BF16)
        gain.append(jnp.where(same_head, lax.dot_general(ta[c].astype(BF16), bh_b, tn, preferred_element_type=F32),
                              0.0).astype(BF16))
        uv = jnp.concatenate([u0[c], v[sl]], axis=0).astype(BF16)
        bk = jnp.concatenate([bh_b, kh[sl].astype(BF16)], axis=0)
        add.append(jnp.where(same_head, lax.dot_general(uv, bk, tn, preferred_element_type=F32), 0.0))

    states = [state_ref[...]]
    for c in range(nch):
        s0 = states[-1]
        states.append(s0 * p_end[c * lc:c * lc + 1, :] + mm(s0, gain[c]) + add[c])
    state_ref[...] = states[nch]

    ys = []
    for c, sl in enumerate(chunks):
        tr = jnp.concatenate([ta[c], rt[sl]], axis=0).astype(BF16)
        xs = lax.dot_general(tr, states[c].astype(BF16), nt, preferred_element_type=F32)
        u = xs[:lc] + u0[c]
        ys.append(xs[lc:] + mm(a_rb[c], block_diag(u)) + y0[c])

    y = jnp.concatenate(ys, axis=0)
    mu = head_sum(y) / hw
    dev = y - mu
    var = head_sum(dev * dev) / hw
    yn = dev * lax.rsqrt(var + RWKV_GN_EPS) * lnw_ref[...] + lnb_ref[...]
    bonus = head_sum(r * k2 * rk_ref[...]) * v
    o_ref[...] = ((yn + bonus) * g_ref[...]).astype(o_ref.dtype)


def _wkv(r, k, v, wl, araw, g, k_k, k_a, r_k, ln_w, ln_b, batch, seq):
    t, d = r.shape
    tb = _tile(seq, WKV_BLOCK)
    nt = seq // tb
    wd = WKV_WIDTH
    rows = pl.BlockSpec((tb, wd), lambda b, h, n: (b * nt + n, h))
    vec = pl.BlockSpec((1, wd), lambda b, h, n: (0, h))
    return pl.pallas_call(
        functools.partial(_wkv_kernel, tb=tb),
        grid=(batch, d // wd, nt),
        in_specs=[rows] * 6 + [vec] * 5,
        out_specs=rows,
        out_shape=jax.ShapeDtypeStruct((t, d), BF16),
        scratch_shapes=[pltpu.VMEM((wd, wd), F32)],
        compiler_params=_cparams(("parallel", "parallel", "arbitrary")),
        name="rwkv_wkv",
    )(r, k, v, wl, araw, g, k_k.reshape(1, d), k_a.reshape(1, d), r_k.reshape(1, d),
      ln_w.reshape(1, d), ln_b.reshape(1, d))


def _rwkv7_time_mix(x2, batch, seq, mix, w_r, w_k, w_v, w_o, w0, w1, w2, a0, a1, a2, g1, g2, k_k, k_a, r_k,
                    ln_w, ln_b):
    xm = _shift_mix(x2, mix, seq)
    bf = lambda w: w.astype(BF16)
    r = _matmul(xm[0], bf(w_r))
    k = _matmul(xm[2], bf(w_k))
    v = _matmul(xm[3], bf(w_v))
    wl = _low_rank(xm[1], bf(w1), bf(w2), w0, act="tanh")
    araw = _low_rank(xm[4], bf(a1), bf(a2), a0)
    g = _low_rank(xm[5], bf(g1), bf(g2), act="sigmoid")
    yg = _wkv(r, k, v, wl, araw, g, k_k, k_a, r_k, ln_w, ln_b, batch, seq)
    return _matmul(yg, bf(w_o))


def kernel(x, ab_w_in, ab_b_qkv, ssm_conv_w, ssm_conv_b, ssm_dt_bias, ssm_a_log, ssm_d, ssm_norm_w, attn_sinks,
           ab_w_out, ab_b_out, rwkv_mix, rwkv_w_r, rwkv_w_k, rwkv_w_v, rwkv_w_o, rwkv_w0, rwkv_w1, rwkv_w2,
           rwkv_a0, rwkv_a1, rwkv_a2, rwkv_g1, rwkv_g2, rwkv_k_k, rwkv_k_a, rwkv_r_k, rwkv_ln_w, rwkv_ln_b,
           ln_mix_g, ln_mix_b, ln_ffn_g, ln_ffn_b, router_w, router_b, moe_w_gate, moe_w_up, moe_w_down):
    batch, seq, d = x.shape
    x2 = x.reshape(batch * seq, d)
    xb = x2.astype(BF16)
    for layer in range(DEPTH):
        i = layer // 2
        if layer % 2 == 0:
            mix = _ssd_swa_mixer(x2, xb, batch, seq, ab_w_in[i], ab_b_qkv[i], ssm_conv_w[i], ssm_conv_b[i],
                                 ssm_dt_bias[i], ssm_a_log[i], ssm_d[i], ssm_norm_w[i], attn_sinks[i],
                                 ab_w_out[i], ab_b_out[i])
        else:
            mix = _rwkv7_time_mix(x2, batch, seq, rwkv_mix[i], rwkv_w_r[i], rwkv_w_k[i], rwkv_w_v[i], rwkv_w_o[i],
                                  rwkv_w0[i], rwkv_w1[i], rwkv_w2[i], rwkv_a0[i], rwkv_a1[i], rwkv_a2[i],
                                  rwkv_g1[i], rwkv_g2[i], rwkv_k_k[i], rwkv_k_a[i], rwkv_r_k[i].reshape(-1),
                                  rwkv_ln_w[i], rwkv_ln_b[i])
        h, h3 = _add_layer_norm(x2, mix, ln_mix_g[layer], ln_mix_b[layer])
        x2, xb = _moe_block(h, h3, router_w, router_b, moe_w_gate, moe_w_up, moe_w_down, layer,
                            ln_ffn_g[layer], ln_ffn_b[layer])
    return x2.reshape(batch, seq, d)
```

```python
import functools
import math

import jax
import jax.numpy as jnp
import numpy as np
from jax import lax
from jax.experimental import pallas as pl
from jax.experimental.pallas import tpu as pltpu

F32 = jnp.float32
BF16 = jnp.bfloat16

D_MODEL = 4096
DEPTH = 2
SSM_HEAD_DIM = 64
SSM_HEADS = 64
SSM_GROUPS = 8
SSM_HEADS_PER_GROUP = 8
SSM_STATE = 128
SSM_CONV = 4
SSM_CHUNK = 128
SSM_GROUP_WIDTH = SSM_HEADS_PER_GROUP * SSM_HEAD_DIM
SSM_CONV_CH = D_MODEL + 2 * SSM_GROUPS * SSM_STATE
SSM_NORM_EPS = 1e-5
ATT_HEADS = 64
ATT_KV_HEADS = 8
ATT_HEAD_DIM = 64
ATT_REP = 8
ATT_WINDOW = 128
ROPE_DIM = 16
ROPE_THETA = 500000.0
Q_COLS = 4096
KV_COLS = 512
RWKV_HEAD = 64
RWKV_GN_EPS = 64e-5
N_EXPERTS = 16
N_EXPERT_GROUPS = 4
EXPERTS_PER_GROUP = 4
TOP_K = 2
D_EXPERT = 1536
ALPHA = (2 * DEPTH) ** 0.25
LN_EPS = 1e-5

VMEM_LIMIT_BYTES = 56 * 1024 * 1024
LANES = 128
SUBLANES = 8


def _cparams(semantics):
    return pltpu.CompilerParams(dimension_semantics=semantics, vmem_limit_bytes=VMEM_LIMIT_BYTES)


def _tile(dim, pref):
    if dim <= pref:
        return dim
    t = pref
    while dim % t:
        t //= 2
    return t


def _silu(v):
    return v * jax.nn.sigmoid(v)


def _softplus(v):
    return jnp.maximum(v, 0.0) + jnp.log1p(jnp.exp(-jnp.abs(v)))


def _split3(v):
    hi = v.astype(BF16)
    rest = v - hi.astype(F32)
    mid = rest.astype(BF16)
    lo = (rest - mid.astype(F32)).astype(BF16)
    return hi, mid, lo


NT_DIMS = (((1,), (1,)), ((), ()))
NN_DIMS = (((1,), (0,)), ((), ()))


def _dot_bf16x3(a, b, dims):
    a_hi, b_hi = a.astype(BF16), b.astype(BF16)
    a_lo = (a - a_hi.astype(F32)).astype(BF16)
    b_lo = (b - b_hi.astype(F32)).astype(BF16)

    def dg(u, v):
        return lax.dot_general(u, v, dims, preferred_element_type=F32)

    return dg(a_hi, b_hi) + (dg(a_hi, b_lo) + dg(a_lo, b_hi))


def _dot_split3(v, m01, *, split_lhs):
    if split_lhs:
        return sum(jnp.dot(t, m01, preferred_element_type=F32) for t in _split3(v))
    return sum(jnp.dot(m01, t, preferred_element_type=F32) for t in _split3(v))


def _mm_kernel(*refs, act, bf16x3, w_is_nk):
    a_refs, (w_ref, b_ref, o_ref) = refs[:-3], refs[-3:]
    dims = NT_DIMS if w_is_nk else NN_DIMS
    r = b_ref[...]
    k0 = 0
    for a_ref in a_refs:
        k1 = k0 + a_ref.shape[1]
        w_part = w_ref[:, k0:k1] if w_is_nk else w_ref[k0:k1, :]
        if bf16x3:
            r = r + _dot_bf16x3(a_ref[...], w_part, dims)
        else:
            r = r + lax.dot_general(a_ref[...], w_part, dims, preferred_element_type=F32)
        k0 = k1
    if act == "tanh":
        r = jnp.tanh(r)
    elif act == "sigmoid":
        r = jax.nn.sigmoid(r)
    o_ref[...] = r.astype(o_ref.dtype)


MATMUL_VMEM_BUDGET = 44 * 1024 * 1024


def _matmul(a, w, bias=None, *, act=None, out_dtype=F32, bf16x3=False, tm=1024, col_start=0, n=None,
            w_is_nk=False):
    panels = a if isinstance(a, (tuple, list)) else (a,)
    m = panels[0].shape[0]
    kdim = sum(p.shape[1] for p in panels)
    w_n, w_k = (w.shape if w_is_nk else w.shape[::-1])
    n = w_n if n is None else n
    tm = _tile(m, tm)
    out_bytes = jnp.dtype(out_dtype).itemsize
    for tn in (512, 256, LANES):
        tn = _tile(n, tn)
        need = 2 * (tm * kdim * panels[0].dtype.itemsize + kdim * tn * w.dtype.itemsize + tm * tn * out_bytes)
        if need <= MATMUL_VMEM_BUDGET:
            break
    assert need <= MATMUL_VMEM_BUDGET and col_start % tn == 0 and w_k == kdim
    col0 = col_start // tn
    if bias is None:
        bias = jnp.zeros((n,), F32)
    bias = bias.reshape(1, n).astype(F32)
    if w_is_nk:
        w_spec = pl.BlockSpec((tn, kdim), lambda i, j: (col0 + j, 0))
    else:
        w_spec = pl.BlockSpec((kdim, tn), lambda i, j: (0, col0 + j))
    return pl.pallas_call(
        functools.partial(_mm_kernel, act=act, bf16x3=bf16x3, w_is_nk=w_is_nk),
        grid=(m // tm, n // tn),
        in_specs=[pl.BlockSpec((tm, p.shape[1]), lambda i, j: (i, 0)) for p in panels] + [
            w_spec,
            pl.BlockSpec((1, tn), lambda i, j: (0, j)),
        ],
        out_specs=pl.BlockSpec((tm, tn), lambda i, j: (i, j)),
        out_shape=jax.ShapeDtypeStruct((m, n), out_dtype),
        compiler_params=_cparams(("parallel", "arbitrary")),
        name="matmul",
    )(*panels, w, bias)


def _low_rank_kernel(a_ref, w1_ref, w2_ref, b_ref, o_ref, *, act):
    t = jnp.dot(a_ref[...], w1_ref[...], preferred_element_type=F32)
    if act == "tanh":
        t = jnp.tanh(t)
    elif act == "sigmoid":
        t = jax.nn.sigmoid(t)
    o_ref[...] = jnp.dot(t.astype(BF16), w2_ref[...], preferred_element_type=F32) + b_ref[...]


def _low_rank(a, w1, w2, bias=None, *, act=None, tm=512):
    m, kdim = a.shape
    r, n = w2.shape
    tm = _tile(m, tm)
    if bias is None:
        bias = jnp.zeros((n,), F32)
    return pl.pallas_call(
        functools.partial(_low_rank_kernel, act=act),
        grid=(m // tm,),
        in_specs=[
            pl.BlockSpec((tm, kdim), lambda i: (i, 0)),
            pl.BlockSpec((kdim, r), lambda i: (0, 0)),
            pl.BlockSpec((r, n), lambda i: (0, 0)),
            pl.BlockSpec((1, n), lambda i: (0, 0)),
        ],
        out_specs=pl.BlockSpec((tm, n), lambda i: (i, 0)),
        out_shape=jax.ShapeDtypeStruct((m, n), F32),
        compiler_params=_cparams(("parallel",)),
        name="low_rank",
    )(a, w1, w2, bias.reshape(1, n).astype(F32))


def _layer_norm_rows(v, g, b):
    mu = jnp.mean(v, -1, keepdims=True)
    var = jnp.mean(jnp.square(v - mu), -1, keepdims=True)
    return (v - mu) * lax.rsqrt(var + LN_EPS) * g + b


def _to_token_major(o3_ref, v):
    for s in range(v.shape[1] // LANES):
        o3_ref[:, s, :] = v[:, s * LANES:(s + 1) * LANES].astype(o3_ref.dtype)


def _tiles_to_rows(tiles_ref, t0, t1):
    return jnp.concatenate([tiles_ref[t0:t1, s].reshape((t1 - t0) * SUBLANES, LANES)
                            for s in range(tiles_ref.shape[1])], axis=1)


def _add_ln_kernel(x_ref, y_ref, g_ref, b_ref, o_ref, o3_ref):
    r = _layer_norm_rows(ALPHA * x_ref[...] + y_ref[...], g_ref[...], b_ref[...])
    o_ref[...] = r
    _to_token_major(o3_ref, r)


def _add_layer_norm(x, y, g, b, *, tm=256):
    t, d = x.shape
    tm = _tile(t, tm)
    row = pl.BlockSpec((tm, d), lambda i: (i, 0))
    row3 = pl.BlockSpec((tm, d // LANES, LANES), lambda i: (i, 0, 0))
    vec = pl.BlockSpec((1, d), lambda i: (0, 0))
    return pl.pallas_call(
        _add_ln_kernel,
        grid=(t // tm,),
        in_specs=[row, row, vec, vec],
        out_specs=[row, row3],
        out_shape=[jax.ShapeDtypeStruct((t, d), F32), jax.ShapeDtypeStruct((t, d // LANES, LANES), F32)],
        compiler_params=_cparams(("parallel",)),
        name="add_layer_norm",
    )(x, y, g.reshape(1, d), b.reshape(1, d))


CONV_HALO = SUBLANES
SSD_CH = SSM_GROUP_WIDTH + 2 * SSM_STATE


def _ssd_kernel(z_ref, x_ref, b_ref, c_ref, wx_ref, wb_ref, wc_ref, bx_ref, bb_ref, bc_ref,
                dtr_ref, dbc_ref, dbr_ref, alc_ref, alr_ref, d_ref, nw_ref,
                o_ref, state_ref, pad_ref):
    ch, gw, ns, hd = SSM_CHUNK, SSM_GROUP_WIDTH, SSM_STATE, SSM_HEAD_DIM

    @pl.when(pl.program_id(2) == 0)
    def _():
        state_ref[...] = jnp.zeros_like(state_ref)
        pad_ref[0:CONV_HALO, :] = jnp.zeros((CONV_HALO, SSD_CH), F32)

    pad_ref[CONV_HALO:CONV_HALO + ch, 0:gw] = x_ref[...]
    pad_ref[CONV_HALO:CONV_HALO + ch, gw:gw + ns] = b_ref[...]
    pad_ref[CONV_HALO:CONV_HALO + ch, gw + ns:SSD_CH] = c_ref[...]
    w = jnp.concatenate([wx_ref[...], wb_ref[...], wc_ref[...]], axis=1)
    acc = jnp.concatenate([bx_ref[...], bb_ref[...], bc_ref[...]], axis=1)
    base = CONV_HALO - (SSM_CONV - 1)
    for k in range(SSM_CONV):
        acc = acc + w[k:k + 1, :] * pad_ref[base + k:base + k + ch, :]
    pad_ref[0:CONV_HALO, :] = pad_ref[ch:ch + CONV_HALO, :]
    xbc = _silu(acc)
    xc, bm, cm = xbc[:, :gw], xbc[:, gw:gw + ns], xbc[:, gw + ns:]

    row = lax.broadcasted_iota(jnp.int32, (ch, ch), 0)
    col = lax.broadcasted_iota(jnp.int32, (ch, ch), 1)
    causal = row >= col
    dt_raw_r = dtr_ref[...]
    eye = (row == col).astype(BF16)
    dt_raw_c = sum(lax.dot_general(eye, term, (((1,), (1,)), ((), ())), preferred_element_type=F32)
                   for term in _split3(dt_raw_r))
    dt_c = _softplus(dt_raw_c + dbc_ref[0])
    dt_r = _softplus(dt_raw_r + dbr_ref[...])
    a_c = -jnp.exp(alc_ref[0])
    a_r = -jnp.exp(alr_ref[...])
    acs_c = _dot_split3(dt_c * a_c, causal.astype(BF16), split_lhs=False)
    acs_r = _dot_split3(dt_r * a_r, (row <= col).astype(BF16), split_lhs=True)

    hrow = lax.broadcasted_iota(jnp.int32, (SSM_HEADS_PER_GROUP, gw), 0)
    hcol = lax.broadcasted_iota(jnp.int32, (SSM_HEADS_PER_GROUP, gw), 1)
    expand = (hcol // hd == hrow).astype(BF16)
    dt_x = _dot_split3(dt_c, expand, split_lhs=True)
    acs_x = _dot_split3(acs_c, expand, split_lhs=True)

    xdt = xc * dt_x
    xdt_b = xdt.astype(BF16)
    bm_b, cm_b = bm.astype(BF16), cm.astype(BF16)
    cb = lax.dot_general(cm_b, bm_b, (((1,), (1,)), ((), ())), preferred_element_type=F32)
    prev = state_ref[...]
    y_off = jnp.dot(cm_b, prev.astype(BF16), preferred_element_type=F32) * jnp.exp(acs_x)

    lane = lax.broadcasted_iota(jnp.int32, (ch, 2 * hd), 1)
    pairs = []
    for pr in range(SSM_HEADS_PER_GROUP // 2):
        xp = xdt_b[:, pr * 2 * hd:(pr + 1) * 2 * hd]
        halves = []
        for q in range(2):
            r = 2 * pr + q
            diff = acs_c[:, r:r + 1] - acs_r[r:r + 1, :]
            seg = jnp.where(causal, jnp.exp(jnp.where(causal, diff, 0.0)), 0.0)
            halves.append(jnp.dot((cb * seg).astype(BF16), xp, preferred_element_type=F32))
        pairs.append(jnp.where(lane < hd, halves[0], halves[1]))
    y = jnp.concatenate(pairs, axis=1) + y_off + d_ref[...] * xc

    acs_last = acs_x[ch - 1:ch, :]
    contrib = lax.dot_general(bm_b, (xdt * jnp.exp(acs_last - acs_x)).astype(BF16),
                              (((0,), (0,)), ((), ())), preferred_element_type=F32)
    state_ref[...] = prev * jnp.exp(acs_last) + contrib

    u = y * _silu(z_ref[...])
    u = u * lax.rsqrt(jnp.mean(u * u, -1, keepdims=True) + SSM_NORM_EPS)
    o_ref[...] = (u * nw_ref[...]).astype(o_ref.dtype)


def _ssd_mixer(u_z, u_xbc, u_dt, conv_w, conv_b, dt_bias, a_log, d_skip, norm_w, batch, seq):
    t = batch * seq
    nc = seq // SSM_CHUNK
    g, r, gw, ns = SSM_GROUPS, SSM_HEADS_PER_GROUP, SSM_GROUP_WIDTH, SSM_STATE
    xblocks = D_MODEL // ns
    conv_wt = conv_w.T
    conv_b2 = conv_b.reshape(1, SSM_CONV_CH)
    dt_row = u_dt.T
    rows = lambda b, gi, c: (b * nc + c, gi)
    in_specs = [
        pl.BlockSpec((SSM_CHUNK, gw), rows),
        pl.BlockSpec((SSM_CHUNK, gw), rows),
        pl.BlockSpec((SSM_CHUNK, ns), lambda b, gi, c: (b * nc + c, xblocks + gi)),
        pl.BlockSpec((SSM_CHUNK, ns), lambda b, gi, c: (b * nc + c, xblocks + g + gi)),
        pl.BlockSpec((SSM_CONV, gw), lambda b, gi, c: (0, gi)),
        pl.BlockSpec((SSM_CONV, ns), lambda b, gi, c: (0, xblocks + gi)),
        pl.BlockSpec((SSM_CONV, ns), lambda b, gi, c: (0, xblocks + g + gi)),
        pl.BlockSpec((1, gw), lambda b, gi, c: (0, gi)),
        pl.BlockSpec((1, ns), lambda b, gi, c: (0, xblocks + gi)),
        pl.BlockSpec((1, ns), lambda b, gi, c: (0, xblocks + g + gi)),
        pl.BlockSpec((r, SSM_CHUNK), lambda b, gi, c: (gi, b * nc + c)),
        pl.BlockSpec((1, 1, r), lambda b, gi, c: (gi, 0, 0)),
        pl.BlockSpec((r, 1), lambda b, gi, c: (gi, 0)),
        pl.BlockSpec((1, 1, r), lambda b, gi, c: (gi, 0, 0)),
        pl.BlockSpec((r, 1), lambda b, gi, c: (gi, 0)),
        pl.BlockSpec((1, gw), lambda b, gi, c: (0, gi)),
        pl.BlockSpec((1, gw), lambda b, gi, c: (0, gi)),
    ]
    return pl.pallas_call(
        _ssd_kernel,
        grid=(batch, g, nc),
        in_specs=in_specs,
        out_specs=pl.BlockSpec((SSM_CHUNK, gw), rows),
        out_shape=jax.ShapeDtypeStruct((t, D_MODEL), BF16),
        scratch_shapes=[pltpu.VMEM((ns, gw), F32), pltpu.VMEM((CONV_HALO + SSM_CHUNK, SSD_CH), F32)],
        compiler_params=_cparams(("parallel", "parallel", "arbitrary")),
        name="ssd_mixer",
    )(u_z, u_xbc, u_xbc, u_xbc, conv_wt, conv_wt, conv_wt, conv_b2, conv_b2, conv_b2,
      dt_row, dt_bias.reshape(g, 1, r), dt_bias.reshape(SSM_HEADS, 1),
      a_log.reshape(g, 1, r), a_log.reshape(SSM_HEADS, 1),
      jnp.repeat(d_skip, SSM_HEAD_DIM).reshape(1, D_MODEL), norm_w.reshape(1, D_MODEL))


KV_PAIR = 2

def _rotary_tables(seq):
    half = ROPE_DIM // 2
    inv_freq = ROPE_THETA ** (-jnp.arange(half, dtype=F32) / half)
    ang = jnp.arange(seq, dtype=F32)[:, None] * inv_freq[None, :]
    ones = jnp.ones((seq, ATT_HEAD_DIM - ROPE_DIM), F32)
    cos = jnp.concatenate([jnp.cos(ang), jnp.cos(ang), ones], -1)
    sin = jnp.concatenate([jnp.sin(ang), jnp.sin(ang), 0.0 * ones], -1)
    return jnp.tile(cos, (1, ATT_REP)), jnp.tile(sin, (1, ATT_REP))


def _rotate_half_matrix():
    width = ATT_REP * ATT_HEAD_DIM
    half = ROPE_DIM // 2
    p = np.zeros((width, width), np.float32)
    for j in range(width):
        if j % ATT_HEAD_DIM < half:
            p[j + half, j] = -1.0
        elif j % ATT_HEAD_DIM < ROPE_DIM:
            p[j - half, j] = 1.0
    return jnp.asarray(p, BF16)


def _rotate(v, cos, sin, perm):
    hi = v.astype(BF16)
    lo = (v - hi.astype(F32)).astype(BF16)
    partner = jnp.dot(hi, perm, preferred_element_type=F32) + jnp.dot(lo, perm, preferred_element_type=F32)
    return v * cos + partner * sin


def _swa_kernel(sink_ref, q_ref, kc_ref, kp_ref, vc_ref, vp_ref, cosc_ref, sinc_ref, cosp_ref, sinp_ref,
                perm_ref, o_ref):
    w, hd = ATT_WINDOW, ATT_HEAD_DIM
    qw, kw = ATT_REP * hd, KV_PAIR * hd
    pair = pl.program_id(1)
    n = pl.program_id(2)
    perm = perm_ref[...]
    perm_k = perm[:kw, :kw]
    cos_c, sin_c = cosc_ref[...], sinc_ref[...]
    q = jnp.concatenate([_rotate(q_ref[:, p * qw:(p + 1) * qw], cos_c, sin_c, perm) for p in range(KV_PAIR)],
                        axis=1)
    k_cur = _rotate(kc_ref[...], cos_c[:, :kw], sin_c[:, :kw], perm_k)
    k_prev = _rotate(kp_ref[...], cosp_ref[:, :kw], sinp_ref[:, :kw], perm_k)
    k_pair = jnp.concatenate([k_prev, k_cur], axis=0).astype(BF16)
    v_pair = jnp.concatenate([vp_ref[...], vc_ref[...]], axis=0).astype(BF16)
    k_all = [k_pair[:, p * hd:(p + 1) * hd] for p in range(KV_PAIR)]
    v_all = [v_pair[:, p * hd:(p + 1) * hd] for p in range(KV_PAIR)]
    qi = lax.broadcasted_iota(jnp.int32, (w, 2 * w), 0) + w
    ki = lax.broadcasted_iota(jnp.int32, (w, 2 * w), 1)
    rel = qi - ki
    first = jnp.where(n > 0, 0, w)
    bias = jnp.where((rel >= 0) & (rel < ATT_WINDOW) & (ki >= first), 0.0, -jnp.inf)
    heads = range(KV_PAIR * ATT_REP)
    qs = [(q[:, r * hd:(r + 1) * hd] * (hd ** -0.5)).astype(BF16) for r in heads]
    sinks = [sink_ref[pair * KV_PAIR * ATT_REP + r] for r in heads]
    ss = [lax.dot_general(qr, k_all[r // ATT_REP], NT_DIMS, preferred_element_type=F32) + bias
          for r, qr in zip(heads, qs)]
    ms = [jnp.maximum(jnp.max(s, -1, keepdims=True), sink) for s, sink in zip(ss, sinks)]
    es = [jnp.exp(s - m) for s, m in zip(ss, ms)]
    dens = [jnp.sum(e, -1, keepdims=True) + jnp.exp(sink - m) for e, sink, m in zip(es, sinks, ms)]
    outs = [jnp.dot(e.astype(BF16), v_all[r // ATT_REP], preferred_element_type=F32) / den
            for r, e, den in zip(heads, es, dens)]
    o_ref[...] = jnp.concatenate(outs, axis=1).astype(o_ref.dtype)


def _swa_attention(qkv, sinks, batch, seq):
    t = batch * seq
    nb = seq // ATT_WINDOW
    hd, w = ATT_HEAD_DIM, ATT_WINDOW
    qw, kw = ATT_REP * hd, KV_PAIR * hd
    k_block0 = Q_COLS // kw
    v_block0 = (Q_COLS + KV_COLS) // kw
    cos, sin = _rotary_tables(seq)

    def kv_spec(block0, back):
        return pl.BlockSpec((w, kw), lambda b, p, n: (b * nb + jnp.maximum(n - back, 0), block0 + p))

    tab_cur = pl.BlockSpec((w, qw), lambda b, p, n: (n, 0))
    tab_prev = pl.BlockSpec((w, qw), lambda b, p, n: (jnp.maximum(n - 1, 0), 0))
    return pl.pallas_call(
        _swa_kernel,
        grid=(batch, ATT_KV_HEADS // KV_PAIR, nb),
        in_specs=[
            pl.BlockSpec(memory_space=pltpu.SMEM),
            pl.BlockSpec((w, KV_PAIR * qw), lambda b, p, n: (b * nb + n, p)),
            kv_spec(k_block0, 0), kv_spec(k_block0, 1), kv_spec(v_block0, 0), kv_spec(v_block0, 1),
            tab_cur, tab_cur, tab_prev, tab_prev,
            pl.BlockSpec((qw, qw), lambda b, p, n: (0, 0)),
        ],
        out_specs=pl.BlockSpec((w, KV_PAIR * qw), lambda b, p, n: (b * nb + n, p)),
        out_shape=jax.ShapeDtypeStruct((t, Q_COLS), BF16),
        compiler_params=_cparams(("parallel", "parallel", "arbitrary")),
        name="swa_attention",
    )(sinks, qkv, qkv, qkv, qkv, qkv, cos, sin, cos, sin, _rotate_half_matrix())


def _ssd_swa_mixer(x2, xb, batch, seq, w_in, b_qkv, conv_w, conv_b, dt_bias, a_log, d_skip, norm_w, sinks,
                   w_out, b_out):
    o1 = D_MODEL
    o2 = o1 + SSM_CONV_CH
    o3 = o2 + SSM_HEADS
    w_t = jnp.swapaxes(w_in, 0, 1)
    w_t_b = w_t.astype(BF16)
    u_z = _matmul(xb, w_t_b, n=o1, w_is_nk=True)
    u_xbc = _matmul(xb, w_t_b, col_start=o1, n=SSM_CONV_CH, w_is_nk=True)
    u_dt = _matmul(x2, w_t[o2:o3], bf16x3=True, w_is_nk=True)
    qkv = _matmul(xb, w_t_b[o3:], b_qkv, w_is_nk=True)
    y_ssm = _ssd_mixer(u_z, u_xbc, u_dt, conv_w, conv_b, dt_bias, a_log, d_skip, norm_w, batch, seq)
    y_att = _swa_attention(qkv, sinks, batch, seq)
    return _matmul((y_ssm, y_att), w_out.astype(BF16), b_out)


MOE_TILE = 256
MOE_GATHER_TILE = 256
ROUTER_TILE = 256


def _router_kernel(h_ref, wt_ref, b_ref, ids_ref, wts_ref):
    logits = _dot_bf16x3(wt_ref[...], h_ref[...], NT_DIMS) + b_ref[...]
    e = jnp.exp(logits - jnp.max(logits, 0, keepdims=True))
    probs = e / jnp.sum(e, 0, keepdims=True)
    p = [probs[i:i + 1, :] for i in range(N_EXPERTS)]

    keep, score = [], []
    for g in range(N_EXPERT_GROUPS):
        members = range(g * EXPERTS_PER_GROUP, (g + 1) * EXPERTS_PER_GROUP)
        s = jnp.zeros_like(p[0])
        for i in members:
            rank = jnp.zeros_like(p[0])
            for j in members:
                if j != i:
                    beats = (p[j] > p[i]) | ((p[j] == p[i]) & (j < i)) if j < i else (p[j] > p[i])
                    rank = rank + beats.astype(F32)
            keep.append(rank < TOP_K)
            s = s + jnp.where(keep[i], p[i], 0.0)
        score.append(s)
    chosen = []
    for g in range(N_EXPERT_GROUPS):
        c = jnp.ones_like(p[0]) > 0
        for g2 in range(N_EXPERT_GROUPS):
            if g2 < g:
                c = c & (score[g] > score[g2])
            elif g2 > g:
                c = c & (score[g] >= score[g2])
        chosen.append(c)
    denom = jnp.zeros_like(p[0])
    for g in range(N_EXPERT_GROUPS):
        denom = denom + jnp.where(chosen[g], score[g], 0.0)
    count = jnp.zeros_like(p[0])
    id0 = jnp.zeros_like(p[0])
    id1 = jnp.zeros_like(p[0])
    w0 = jnp.zeros_like(p[0])
    w1 = jnp.zeros_like(p[0])
    for i in range(N_EXPERTS):
        sel = keep[i] & chosen[i // EXPERTS_PER_GROUP]
        gate = p[i] / denom
        first = sel & (count == 0.0)
        second = sel & (count == 1.0)
        id0 = jnp.where(first, float(i), id0)
        w0 = jnp.where(first, gate, w0)
        id1 = jnp.where(second, float(i), id1)
        w1 = jnp.where(second, gate, w1)
        count = count + sel.astype(F32)
    ids_ref[0:1, :] = id0.astype(jnp.int32)
    ids_ref[1:2, :] = id1.astype(jnp.int32)
    wts_ref[0:1, :] = w0
    wts_ref[1:2, :] = w1


def _router(h, router_w, router_b):
    t, d = h.shape
    tm = _tile(t, ROUTER_TILE)
    return pl.pallas_call(
        _router_kernel,
        grid=(t // tm,),
        in_specs=[
            pl.BlockSpec((tm, d), lambda i: (i, 0)),
            pl.BlockSpec((N_EXPERTS, d), lambda i: (0, 0)),
            pl.BlockSpec((N_EXPERTS, 1), lambda i: (0, 0)),
        ],
        out_specs=[pl.BlockSpec((TOP_K, tm), lambda i: (0, i)), pl.BlockSpec((TOP_K, tm), lambda i: (0, i))],
        out_shape=[jax.ShapeDtypeStruct((TOP_K, t), jnp.int32), jax.ShapeDtypeStruct((TOP_K, t), F32)],
        compiler_params=_cparams(("parallel",)),
        name="moe_router",
    )(h, router_w.T, router_b.reshape(N_EXPERTS, 1))


def _route_metadata(ids, wts, tm):
    t = ids.shape[1]
    na = TOP_K * t
    eid = ids.T.reshape(na)
    onehot = (eid[:, None] == jnp.arange(N_EXPERTS, dtype=jnp.int32)[None, :]).astype(jnp.int32)
    csum = jnp.cumsum(onehot, axis=0)
    counts = csum[-1]
    rank = jnp.sum(csum * onehot, axis=1) - 1
    padded = (counts + tm - 1) // tm * tm
    pend = jnp.cumsum(padded)
    dest = (pend - padded)[eid] + rank
    n_tiles = na // tm + N_EXPERTS
    n_rows = n_tiles * tm
    row_assign = jnp.full((n_rows,), -1, jnp.int32).at[dest].set(jnp.arange(na, dtype=jnp.int32))
    assigned = jnp.maximum(row_assign, 0)
    row_token = assigned // TOP_K
    row_weight = jnp.where(row_assign >= 0, wts.T.reshape(na)[assigned], 0.0)
    tile_start = jnp.arange(n_tiles, dtype=jnp.int32) * tm
    tile_valid = (tile_start < pend[-1]).astype(jnp.int32)
    last_expert = jnp.max(jnp.where(counts > 0, jnp.arange(N_EXPERTS, dtype=jnp.int32), 0))
    tile_expert = jnp.minimum(jnp.searchsorted(pend, tile_start, side="right").astype(jnp.int32), last_expert)
    experts = jnp.arange(N_EXPERTS, dtype=jnp.int32)
    later = jnp.where((counts > 0)[None, :] & (experts[None, :] > experts[:, None]), experts[None, :], N_EXPERTS)
    next_nonempty = jnp.min(later, axis=1)
    next_nonempty = jnp.where(next_nonempty == N_EXPERTS, tile_expert[0], next_nonempty).astype(jnp.int32)
    prev_expert = jnp.concatenate([jnp.full((1,), -1, jnp.int32), tile_expert[:-1]])
    run_first = (tile_valid == 1) & (tile_expert != prev_expert)
    runs = jnp.stack([tile_expert, tile_valid, run_first.astype(jnp.int32), next_nonempty[tile_expert],
                      (tile_expert == last_expert).astype(jnp.int32)])
    return row_token, row_weight, dest, runs


def _row_copy(src_hbm, tiles_ref, sem, tile, sub, src_row):
    return pltpu.make_async_copy(src_hbm.at[src_row], tiles_ref.at[tile, :, sub, :], sem)


def _start_row_gather(src_hbm, tiles_ref, idx_ref, n, sem):
    def body(tile, carry):
        for sub in range(SUBLANES):
            _row_copy(src_hbm, tiles_ref, sem, tile, sub, idx_ref[0, 0, tile * SUBLANES + sub]).start(
                priority=sub % 2)
        return carry

    lax.fori_loop(0, n // SUBLANES, body, 0)


def _wait_row_gather(src_hbm, tiles_ref, n, sem):
    def body(tile, carry):
        for sub in range(SUBLANES):
            _row_copy(src_hbm, tiles_ref, sem, tile, sub, 0).wait()
        return carry

    lax.fori_loop(0, n // SUBLANES, body, 0)


def _gather_step(src_hbm, slots_ref, idx_ref, idx_next_ref, sems, n, *, more):
    i = pl.program_id(0)
    slot = i % 2

    @pl.when(i == 0)
    def _():
        _start_row_gather(src_hbm, slots_ref.at[0], idx_ref, n, sems.at[0])

    @pl.when(more)
    def _():
        _start_row_gather(src_hbm, slots_ref.at[1 - slot], idx_next_ref, n, sems.at[1 - slot])

    return slot


def _moe_gather_kernel(tv_ref, rt_ref, rt_next_ref, h_hbm, o_ref, xg_ref, sems, *, tm):
    i = pl.program_id(0)
    nxt = jnp.minimum(i + 1, pl.num_programs(0) - 1)
    valid = tv_ref[i] == 1
    slot = _gather_step(h_hbm, xg_ref, rt_ref, rt_next_ref, sems, tm, more=(nxt > i) & (tv_ref[nxt] == 1))

    @pl.when(valid)
    def _():
        _wait_row_gather(h_hbm, xg_ref.at[slot], tm, sems.at[slot])
        o_ref[...] = _tiles_to_rows(xg_ref.at[slot], 0, tm // SUBLANES).astype(o_ref.dtype)

    @pl.when(jnp.logical_not(valid))
    def _():
        o_ref[...] = jnp.zeros_like(o_ref)


RUN_EXPERT, RUN_VALID, RUN_FIRST, RUN_NEXT, RUN_LAST = range(5)


def _stream_expert_weights(runs_ref, w_hbm, stage_ref, wb_ref, sems, *, layer, tn):
    j, i = pl.program_id(0), pl.program_id(1)
    last_pass = j == pl.num_programs(0) - 1

    def copies(expert, col_block):
        col = pl.multiple_of(col_block * tn, tn)
        return [pltpu.make_async_copy(w.at[layer, expert, :, pl.ds(col, tn)], stage_ref.at[m], sems.at[m])
                for m, w in enumerate(w_hbm)]

    @pl.when((j == 0) & (i == 0))
    def _():
        for c in copies(runs_ref[RUN_EXPERT, 0], 0):
            c.start()

    @pl.when(runs_ref[RUN_FIRST, i] == 1)
    def _():
        for c in copies(runs_ref[RUN_EXPERT, i], j):
            c.wait()
        for m in range(len(w_hbm)):
            wb_ref[m] = stage_ref[m].astype(BF16)
        last_run = runs_ref[RUN_LAST, i] == 1

        @pl.when(jnp.logical_not(last_run & last_pass))
        def _():
            for c in copies(runs_ref[RUN_NEXT, i], jnp.where(last_run, j + 1, j)):
                c.start()


def _moe_up_kernel(runs_ref, x_ref, wg_hbm, wu_hbm, o_ref, stage_ref, wb_ref, sems, *, layer, tn):
    _stream_expert_weights(runs_ref, (wg_hbm, wu_hbm), stage_ref, wb_ref, sems, layer=layer, tn=tn)
    valid = runs_ref[RUN_VALID, pl.program_id(1)] == 1

    @pl.when(valid)
    def _():
        xt = x_ref[...]
        gate = jnp.dot(xt, wb_ref[0], preferred_element_type=F32)
        up = jnp.dot(xt, wb_ref[1], preferred_element_type=F32)
        o_ref[...] = (_silu(gate) * up).astype(o_ref.dtype)

    @pl.when(jnp.logical_not(valid))
    def _():
        o_ref[...] = jnp.zeros_like(o_ref)


def _moe_down_kernel(runs_ref, he_ref, wd_hbm, rw_ref, o_ref, stage_ref, wb_ref, sems, *, layer, tn):
    _stream_expert_weights(runs_ref, (wd_hbm,), stage_ref, wb_ref, sems, layer=layer, tn=tn)
    valid = runs_ref[RUN_VALID, pl.program_id(1)] == 1

    @pl.when(valid)
    def _():
        _to_token_major(o_ref, jnp.dot(he_ref[...], wb_ref[0], preferred_element_type=F32) * rw_ref[...])

    @pl.when(jnp.logical_not(valid))
    def _():
        o_ref[...] = jnp.zeros_like(o_ref)


def _moe_combine_ln_kernel(pos_ref, pos_next_ref, h_ref, y_hbm, g_ref, b_ref, o_ref, ob_ref, yg_ref, sems, *, tm):
    n = TOP_K * tm
    slot = _gather_step(y_hbm, yg_ref, pos_ref, pos_next_ref, sems, n,
                        more=pl.program_id(0) + 1 < pl.num_programs(0))
    _wait_row_gather(y_hbm, yg_ref.at[slot], n, sems.at[slot])
    per = tm // SUBLANES
    ffn = _tiles_to_rows(yg_ref.at[slot], 0, per) + _tiles_to_rows(yg_ref.at[slot], per, TOP_K * per)
    r = _layer_norm_rows(ALPHA * h_ref[...] + ffn, g_ref[...], b_ref[...])
    o_ref[...] = r
    ob_ref[...] = r.astype(BF16)


def _moe_block(h, h3, router_w, router_b, w_gate, w_up, w_down, layer, ln_g, ln_b, *, tm=MOE_TILE, tn_up=512,
               tn_down=2048, tm_out=128):
    t, d = h.shape
    slabs = d // LANES
    ids, wts = _router(h, router_w, router_b)
    row_token, row_weight, dest, runs = _route_metadata(ids, wts, tm)
    tile_valid = runs[RUN_VALID]
    n_rows = row_token.shape[0]
    n_tiles = n_rows // tm
    de = w_gate.shape[-1]
    tn_up, tn_down = _tile(de, tn_up), _tile(d, tn_down)

    def gather_scratch(n):
        return [pltpu.VMEM((2, n // SUBLANES, slabs, SUBLANES, LANES), F32), pltpu.SemaphoreType.DMA((2,))]

    tg = _tile(tm, MOE_GATHER_TILE)
    per = tm // tg
    n_gather = n_rows // tg
    row_token3 = row_token.reshape(n_gather, 1, tg)
    x_rows = pl.pallas_call(
        functools.partial(_moe_gather_kernel, tm=tg),
        grid_spec=pltpu.PrefetchScalarGridSpec(
            num_scalar_prefetch=1,
            grid=(n_gather,),
            in_specs=[
                pl.BlockSpec((1, 1, tg), lambda i, tv: (i, 0, 0), memory_space=pltpu.SMEM),
                pl.BlockSpec((1, 1, tg), lambda i, tv: (jnp.minimum(i + 1, n_gather - 1), 0, 0),
                             memory_space=pltpu.SMEM),
                pl.BlockSpec(memory_space=pl.ANY),
            ],
            out_specs=pl.BlockSpec((tg, d), lambda i, tv: (i, 0)),
            scratch_shapes=gather_scratch(tg),
        ),
        out_shape=jax.ShapeDtypeStruct((n_rows, d), BF16),
        compiler_params=_cparams(("arbitrary",)),
        name="moe_gather",
    )(jnp.repeat(tile_valid, per), row_token3, row_token3, h3)

    def weight_scratch(n_mats, k, tn):
        return [pltpu.VMEM((n_mats, k, tn), F32), pltpu.VMEM((n_mats, k, tn), BF16),
                pltpu.SemaphoreType.DMA((n_mats,))]

    he = pl.pallas_call(
        functools.partial(_moe_up_kernel, layer=layer, tn=tn_up),
        grid_spec=pltpu.PrefetchScalarGridSpec(
            num_scalar_prefetch=1,
            grid=(de // tn_up, n_tiles),
            in_specs=[
                pl.BlockSpec((tm, d), lambda j, i, runs: (i, 0)),
                pl.BlockSpec(memory_space=pl.ANY),
                pl.BlockSpec(memory_space=pl.ANY),
            ],
            out_specs=pl.BlockSpec((tm, tn_up), lambda j, i, runs: (i, j)),
            scratch_shapes=weight_scratch(2, d, tn_up),
        ),
        out_shape=jax.ShapeDtypeStruct((n_rows, de), BF16),
        compiler_params=_cparams(("arbitrary", "arbitrary")),
        name="moe_gate_up",
    )(runs, x_rows, w_gate, w_up)

    y_rows = pl.pallas_call(
        functools.partial(_moe_down_kernel, layer=layer, tn=tn_down),
        grid_spec=pltpu.PrefetchScalarGridSpec(
            num_scalar_prefetch=1,
            grid=(d // tn_down, n_tiles),
            in_specs=[
                pl.BlockSpec((tm, de), lambda j, i, runs: (i, 0)),
                pl.BlockSpec(memory_space=pl.ANY),
                pl.BlockSpec((tm, 1), lambda j, i, runs: (i, 0)),
            ],
            out_specs=pl.BlockSpec((tm, tn_down // LANES, LANES), lambda j, i, runs: (i, j, 0)),
            scratch_shapes=weight_scratch(1, de, tn_down),
        ),
        out_shape=jax.ShapeDtypeStruct((n_rows, slabs, LANES), F32),
        compiler_params=_cparams(("arbitrary", "arbitrary")),
        name="moe_down",
    )(runs, he, w_down, row_weight.reshape(n_rows, 1))

    tm_out = _tile(t, tm_out)
    n_out = t // tm_out
    pos = dest.reshape(n_out, tm_out, TOP_K).transpose(0, 2, 1).reshape(n_out, 1, TOP_K * tm_out)
    row = pl.BlockSpec((tm_out, d), lambda i: (i, 0))
    vec = pl.BlockSpec((1, d), lambda i: (0, 0))
    return pl.pallas_call(
        functools.partial(_moe_combine_ln_kernel, tm=tm_out),
        grid=(n_out,),
        in_specs=[
            pl.BlockSpec((1, 1, TOP_K * tm_out), lambda i: (i, 0, 0), memory_space=pltpu.SMEM),
            pl.BlockSpec((1, 1, TOP_K * tm_out), lambda i: (jnp.minimum(i + 1, n_out - 1), 0, 0),
                         memory_space=pltpu.SMEM),
            row,
            pl.BlockSpec(memory_space=pl.ANY),
            vec, vec,
        ],
        out_specs=[row, row],
        out_shape=[jax.ShapeDtypeStruct((t, d), F32), jax.ShapeDtypeStruct((t, d), BF16)],
        scratch_shapes=gather_scratch(TOP_K * tm_out),
        compiler_params=_cparams(("arbitrary",)),
        name="moe_combine_ln",
    )(pos, pos, h, y_rows, ln_g.reshape(1, d), ln_b.reshape(1, d))


def _shift_mix_kernel(x_ref, xp_ref, mix_ref, *rest, tm, seq):
    o_refs, pad_ref = rest[:-1], rest[-1]
    i = pl.program_id(0)
    starts_sequence = (i * tm) % seq == 0
    pad_ref[0:SUBLANES, :] = jnp.where(starts_sequence, 0.0, xp_ref[...])
    pad_ref[SUBLANES:SUBLANES + tm, :] = x_ref[...]
    xv = x_ref[...]
    xx = pad_ref[SUBLANES - 1:SUBLANES - 1 + tm, :] - xv
    for m, o_ref in enumerate(o_refs):
        o_ref[...] = (xv + xx * mix_ref[m:m + 1, :]).astype(o_ref.dtype)


def _shift_mix(x, mix, seq, *, tm=128):
    t, d = x.shape
    nm = mix.shape[0]
    tm = _tile(seq, tm)
    per = tm // SUBLANES
    return pl.pallas_call(
        functools.partial(_shift_mix_kernel, tm=tm, seq=seq),
        grid=(t // tm,),
        in_specs=[
            pl.BlockSpec((tm, d), lambda i: (i, 0)),
            pl.BlockSpec((SUBLANES, d), lambda i: (jnp.maximum(i * per - 1, 0), 0)),
            pl.BlockSpec((nm, d), lambda i: (0, 0)),
        ],
        out_specs=[pl.BlockSpec((tm, d), lambda i: (i, 0))] * nm,
        out_shape=[jax.ShapeDtypeStruct((t, d), BF16)] * nm,
        scratch_shapes=[pltpu.VMEM((SUBLANES + tm, d), F32)],
        compiler_params=_cparams(("parallel",)),
        name="rwkv_shift_mix",
    )(x, x, mix)


WKV_CHUNK = 64
WKV_HEADS = 4
WKV_WIDTH = WKV_HEADS * RWKV_HEAD
WKV_BLOCK = 512
NEUMANN_STEPS = 5


def _wkv_kernel(r_ref, k_ref, v_ref, wl_ref, ar_ref, g_ref, kk_ref, ka_ref, rk_ref, lnw_ref, lnb_ref,
                o_ref, state_ref, *, tb):
    lc, hw, wd = WKV_CHUNK, RWKV_HEAD, WKV_WIDTH
    nch = tb // lc

    @pl.when(pl.program_id(2) == 0)
    def _():
        state_ref[...] = jnp.zeros_like(state_ref)

    wrow = lax.broadcasted_iota(jnp.int32, (wd, wd), 0)
    wcol = lax.broadcasted_iota(jnp.int32, (wd, wd), 1)
    same_head = wrow // hw == wcol // hw
    ones_bd = same_head.astype(BF16)

    def block_diag(m):
        return jnp.where(same_head, jnp.concatenate([m] * WKV_HEADS, axis=0), 0.0).astype(BF16)

    def head_sum(m):
        hi = m.astype(BF16)
        lo = (m - hi.astype(F32)).astype(BF16)
        return jnp.dot(hi, ones_bd, preferred_element_type=F32) + jnp.dot(lo, ones_bd, preferred_element_type=F32)

    def mm(a, b):
        return jnp.dot(a.astype(BF16), b, preferred_element_type=F32)

    r = r_ref[...]
    k = k_ref[...]
    v = v_ref[...]
    w_log = -_softplus(-wl_ref[...]) - 0.5
    lw = -jnp.exp(w_log)
    a_sig = jax.nn.sigmoid(ar_ref[...])
    kx = k * kk_ref[...]
    kk = kx / jnp.maximum(jnp.sqrt(head_sum(kx * kx)), 1e-12)
    k2 = k * (1.0 + (a_sig - 1.0) * ka_ref[...])
    a_s = -kk
    b_s = kk * a_sig

    trow = lax.broadcasted_iota(jnp.int32, (tb, tb), 0)
    tcol = lax.broadcasted_iota(jnp.int32, (tb, tb), 1)
    same_chunk = trow // lc == tcol // lc
    cum = _dot_split3(lw, (same_chunk & (trow >= tcol)).astype(BF16), split_lhs=False)
    tot = jnp.concatenate([jnp.broadcast_to(cum[(c + 1) * lc - 1:(c + 1) * lc, :], (lc, wd)) for c in range(nch)],
                          axis=0)
    grow = jnp.exp(-cum)
    rt = r * jnp.exp(cum)
    at = a_s * jnp.exp(cum - lw)
    bt = b_s * grow
    kt = k2 * grow
    rest = jnp.exp(tot - cum)
    bh = b_s * rest
    kh = k2 * rest
    p_end = jnp.exp(tot)

    t_idx = lax.broadcasted_iota(jnp.int32, (lc, wd), 0)
    s_idx = lax.broadcasted_iota(jnp.int32, (lc, wd), 1) % hw
    strict = t_idx > s_idx
    incl = t_idx >= s_idx
    eye = (t_idx == s_idx).astype(F32)
    nt = (((1,), (1,)), ((), ()))

    tn = (((0,), (0,)), ((), ()))
    chunks = [slice(c * lc, (c + 1) * lc) for c in range(nch)]

    a_ab, a_rb, a_ak, a_rk = [], [], [], []
    for sl in chunks:
        ar = jnp.concatenate([at[sl], rt[sl]], axis=0).astype(BF16)
        xb = lax.dot_general(ar, block_diag(bt[sl]), nt, preferred_element_type=F32)
        xk = lax.dot_general(ar, block_diag(kt[sl]), nt, preferred_element_type=F32)
        a_ab.append(jnp.where(strict, xb[:lc], 0.0))
        a_rb.append(jnp.where(incl, xb[lc:], 0.0))
        a_ak.append(jnp.where(strict, xk[:lc], 0.0))
        a_rk.append(jnp.where(incl, xk[lc:], 0.0))
    inv = [eye + a for a in a_ab]
    pw = [mm(a, block_diag(a)) for a in a_ab]
    for step in range(1, NEUMANN_STEPS + 1):
        last = step == NEUMANN_STEPS
        lhs = inv if last else [jnp.concatenate([t, p], axis=0) for t, p in zip(inv, pw)]
        prod = [mm(x, block_diag(p)) for x, p in zip(lhs, pw)]
        inv = [t + q[:lc] for t, q in zip(inv, prod)]
        if not last:
            pw = [q[lc:] for q in prod]
    v_bd = [block_diag(v[sl]) for sl in chunks]
    ta = [mm(t, block_diag(at[sl])) for t, sl in zip(inv, chunks)]
    av = [mm(jnp.concatenate([ak, rk], axis=0), vb) for ak, rk, vb in zip(a_ak, a_rk, v_bd)]
    y0 = [x[lc:] for x in av]
    u0 = [mm(t, block_diag(x[:lc])) for t, x in zip(inv, av)]
    gain, add = [], []
    for c, sl in enumerate(chunks):
        bh_b = bh[sl].astype(BF16)
        gain.append(jnp.where(same_head, lax.dot_general(ta[c].astype(BF16), bh_b, tn, preferred_element_type=F32),
                              0.0).astype(BF16))
        uv = jnp.concatenate([u0[c], v[sl]], axis=0).astype(BF16)
        bk = jnp.concatenate([bh_b, kh[sl].astype(BF16)], axis=0)
        add.append(jnp.where(same_head, lax.dot_general(uv, bk, tn, preferred_element_type=F32), 0.0))

    states = [state_ref[...]]
    for c in range(nch):
        s0 = states[-1]
        states.append(s0 * p_end[c * lc:c * lc + 1, :] + mm(s0, gain[c]) + add[c])
    state_ref[...] = states[nch]

    ys = []
    for c, sl in enumerate(chunks):
        tr = jnp.concatenate([ta[c], rt[sl]], axis=0).astype(BF16)
        xs = lax.dot_general(tr, states[c].astype(BF16), nt, preferred_element_type=F32)
        u = xs[:lc] + u0[c]
        ys.append(xs[lc:] + mm(a_rb[c], block_diag(u)) + y0[c])

    y = jnp.concatenate(ys, axis=0)
    mu = head_sum(y) / hw
    dev = y - mu
    var = head_sum(dev * dev) / hw
    yn = dev * lax.rsqrt(var + RWKV_GN_EPS) * lnw_ref[...] + lnb_ref[...]
    bonus = head_sum(r * k2 * rk_ref[...]) * v
    o_ref[...] = ((yn + bonus) * g_ref[...]).astype(o_ref.dtype)


def _wkv(r, k, v, wl, araw, g, k_k, k_a, r_k, ln_w, ln_b, batch, seq):
    t, d = r.shape
    tb = _tile(seq, WKV_BLOCK)
    nt = seq // tb
    wd = WKV_WIDTH
    rows = pl.BlockSpec((tb, wd), lambda b, h, n: (b * nt + n, h))
    vec = pl.BlockSpec((1, wd), lambda b, h, n: (0, h))
    return pl.pallas_call(
        functools.partial(_wkv_kernel, tb=tb),
        grid=(batch, d // wd, nt),
        in_specs=[rows] * 6 + [vec] * 5,
        out_specs=rows,
        out_shape=jax.ShapeDtypeStruct((t, d), BF16),
        scratch_shapes=[pltpu.VMEM((wd, wd), F32)],
        compiler_params=_cparams(("parallel", "parallel", "arbitrary")),
        name="rwkv_wkv",
    )(r, k, v, wl, araw, g, k_k.reshape(1, d), k_a.reshape(1, d), r_k.reshape(1, d),
      ln_w.reshape(1, d), ln_b.reshape(1, d))


def _rwkv7_time_mix(x2, batch, seq, mix, w_r, w_k, w_v, w_o, w0, w1, w2, a0, a1, a2, g1, g2, k_k, k_a, r_k,
                    ln_w, ln_b):
    xm = _shift_mix(x2, mix, seq)
    bf = lambda w: w.astype(BF16)
    r = _matmul(xm[0], bf(w_r))
    k = _matmul(xm[2], bf(w_k))
    v = _matmul(xm[3], bf(w_v))
    wl = _low_rank(xm[1], bf(w1), bf(w2), w0, act="tanh")
    araw = _low_rank(xm[4], bf(a1), bf(a2), a0)
    g = _low_rank(xm[5], bf(g1), bf(g2), act="sigmoid")
    yg = _wkv(r, k, v, wl, araw, g, k_k, k_a, r_k, ln_w, ln_b, batch, seq)
    return _matmul(yg, bf(w_o))


def kernel(x, ab_w_in, ab_b_qkv, ssm_conv_w, ssm_conv_b, ssm_dt_bias, ssm_a_log, ssm_d, ssm_norm_w, attn_sinks,
           ab_w_out, ab_b_out, rwkv_mix, rwkv_w_r, rwkv_w_k, rwkv_w_v, rwkv_w_o, rwkv_w0, rwkv_w1, rwkv_w2,
           rwkv_a0, rwkv_a1, rwkv_a2, rwkv_g1, rwkv_g2, rwkv_k_k, rwkv_k_a, rwkv_r_k, rwkv_ln_w, rwkv_ln_b,
           ln_mix_g, ln_mix_b, ln_ffn_g, ln_ffn_b, router_w, router_b, moe_w_gate, moe_w_up, moe_w_down):
    batch, seq, d = x.shape
    x2 = x.reshape(batch * seq, d)
    xb = x2.astype(BF16)
    for layer in range(DEPTH):
        i = layer // 2
        if layer % 2 == 0:
            mix = _ssd_swa_mixer(x2, xb, batch, seq, ab_w_in[i], ab_b_qkv[i], ssm_conv_w[i], ssm_conv_b[i],
                                 ssm_dt_bias[i], ssm_a_log[i], ssm_d[i], ssm_norm_w[i], attn_sinks[i],
                                 ab_w_out[i], ab_b_out[i])
        else:
            mix = _rwkv7_time_mix(x2, batch, seq, rwkv_mix[i], rwkv_w_r[i], rwkv_w_k[i], rwkv_w_v[i], rwkv_w_o[i],
                                  rwkv_w0[i], rwkv_w1[i], rwkv_w2[i], rwkv_a0[i], rwkv_a1[i], rwkv_a2[i],
                                  rwkv_g1[i], rwkv_g2[i], rwkv_k_k[i], rwkv_k_a[i], rwkv_r_k[i].reshape(-1),
                                  rwkv_ln_w[i], rwkv_ln_b[i])
        h, h3 = _add_layer_norm(x2, mix, ln_mix_g[layer], ln_mix_b[layer])
        x2, xb = _moe_block(h, h3, router_w, router_b, moe_w_gate, moe_w_up, moe_w_down, layer,
                            ln_ffn_g[layer], ln_ffn_b[layer])
    return x2.reshape(batch, seq, d)
```

```python
import functools
import math

import jax
import jax.numpy as jnp
import numpy as np
from jax import lax
from jax.experimental import pallas as pl
from jax.experimental.pallas import tpu as pltpu

F32 = jnp.float32
BF16 = jnp.bfloat16

D_MODEL = 4096
DEPTH = 2
SSM_HEAD_DIM = 64
SSM_HEADS = 64
SSM_GROUPS = 8
SSM_HEADS_PER_GROUP = 8
SSM_STATE = 128
SSM_CONV = 4
SSM_CHUNK = 128
SSM_GROUP_WIDTH = SSM_HEADS_PER_GROUP * SSM_HEAD_DIM
SSM_CONV_CH = D_MODEL + 2 * SSM_GROUPS * SSM_STATE
SSM_NORM_EPS = 1e-5
ATT_HEADS = 64
ATT_KV_HEADS = 8
ATT_HEAD_DIM = 64
ATT_REP = 8
ATT_WINDOW = 128
ROPE_DIM = 16
ROPE_THETA = 500000.0
Q_COLS = 4096
KV_COLS = 512
RWKV_HEAD = 64
RWKV_GN_EPS = 64e-5
N_EXPERTS = 16
N_EXPERT_GROUPS = 4
EXPERTS_PER_GROUP = 4
TOP_K = 2
D_EXPERT = 1536
ALPHA = (2 * DEPTH) ** 0.25
LN_EPS = 1e-5

VMEM_LIMIT_BYTES = 56 * 1024 * 1024
LANES = 128
SUBLANES = 8


def _cparams(semantics):
    return pltpu.CompilerParams(dimension_semantics=semantics, vmem_limit_bytes=VMEM_LIMIT_BYTES)


def _tile(dim, pref):
    if dim <= pref:
        return dim
    t = pref
    while dim % t:
        t //= 2
    return t


def _silu(v):
    return v * jax.nn.sigmoid(v)


def _softplus(v):
    return jnp.maximum(v, 0.0) + jnp.log1p(jnp.exp(-jnp.abs(v)))


def _split3(v):
    hi = v.astype(BF16)
    rest = v - hi.astype(F32)
    mid = rest.astype(BF16)
    lo = (rest - mid.astype(F32)).astype(BF16)
    return hi, mid, lo


NT_DIMS = (((1,), (1,)), ((), ()))
NN_DIMS = (((1,), (0,)), ((), ()))


def _dot_bf16x3(a, b, dims):
    a_hi, b_hi = a.astype(BF16), b.astype(BF16)
    a_lo = (a - a_hi.astype(F32)).astype(BF16)
    b_lo = (b - b_hi.astype(F32)).astype(BF16)

    def dg(u, v):
        return lax.dot_general(u, v, dims, preferred_element_type=F32)

    return dg(a_hi, b_hi) + (dg(a_hi, b_lo) + dg(a_lo, b_hi))


def _dot_split3(v, m01, *, split_lhs):
    if split_lhs:
        return sum(jnp.dot(t, m01, preferred_element_type=F32) for t in _split3(v))
    return sum(jnp.dot(m01, t, preferred_element_type=F32) for t in _split3(v))


def _mm_kernel(*refs, act, bf16x3, w_is_nk):
    a_refs, (w_ref, b_ref, o_ref) = refs[:-3], refs[-3:]
    dims = NT_DIMS if w_is_nk else NN_DIMS
    r = b_ref[...]
    k0 = 0
    for a_ref in a_refs:
        k1 = k0 + a_ref.shape[1]
        w_part = w_ref[:, k0:k1] if w_is_nk else w_ref[k0:k1, :]
        if bf16x3:
            r = r + _dot_bf16x3(a_ref[...], w_part, dims)
        else:
            r = r + lax.dot_general(a_ref[...], w_part, dims, preferred_element_type=F32)
        k0 = k1
    if act == "tanh":
        r = jnp.tanh(r)
    elif act == "sigmoid":
        r = jax.nn.sigmoid(r)
    o_ref[...] = r.astype(o_ref.dtype)


MATMUL_VMEM_BUDGET = 44 * 1024 * 1024


def _matmul(a, w, bias=None, *, act=None, out_dtype=F32, bf16x3=False, tm=1024, col_start=0, n=None,
            w_is_nk=False):
    panels = a if isinstance(a, (tuple, list)) else (a,)
    m = panels[0].shape[0]
    kdim = sum(p.shape[1] for p in panels)
    w_n, w_k = (w.shape if w_is_nk else w.shape[::-1])
    n = w_n if n is None else n
    tm = _tile(m, tm)
    out_bytes = jnp.dtype(out_dtype).itemsize
    for tn in (512, 256, LANES):
        tn = _tile(n, tn)
        need = 2 * (tm * kdim * panels[0].dtype.itemsize + kdim * tn * w.dtype.itemsize + tm * tn * out_bytes)
        if need <= MATMUL_VMEM_BUDGET:
            break
    assert need <= MATMUL_VMEM_BUDGET and col_start % tn == 0 and w_k == kdim
    col0 = col_start // tn
    if bias is None:
        bias = jnp.zeros((n,), F32)
    bias = bias.reshape(1, n).astype(F32)
    if w_is_nk:
        w_spec = pl.BlockSpec((tn, kdim), lambda i, j: (col0 + j, 0))
    else:
        w_spec = pl.BlockSpec((kdim, tn), lambda i, j: (0, col0 + j))
    return pl.pallas_call(
        functools.partial(_mm_kernel, act=act, bf16x3=bf16x3, w_is_nk=w_is_nk),
        grid=(m // tm, n // tn),
        in_specs=[pl.BlockSpec((tm, p.shape[1]), lambda i, j: (i, 0)) for p in panels] + [
            w_spec,
            pl.BlockSpec((1, tn), lambda i, j: (0, j)),
        ],
        out_specs=pl.BlockSpec((tm, tn), lambda i, j: (i, j)),
        out_shape=jax.ShapeDtypeStruct((m, n), out_dtype),
        compiler_params=_cparams(("parallel", "arbitrary")),
        name="matmul",
    )(*panels, w, bias)


def _low_rank_kernel(a_ref, w1_ref, w2_ref, b_ref, o_ref, *, act):
    t = jnp.dot(a_ref[...], w1_ref[...], preferred_element_type=F32)
    if act == "tanh":
        t = jnp.tanh(t)
    elif act == "sigmoid":
        t = jax.nn.sigmoid(t)
    o_ref[...] = jnp.dot(t.astype(BF16), w2_ref[...], preferred_element_type=F32) + b_ref[...]


def _low_rank(a, w1, w2, bias=None, *, act=None, tm=512):
    m, kdim = a.shape
    r, n = w2.shape
    tm = _tile(m, tm)
    if bias is None:
        bias = jnp.zeros((n,), F32)
    return pl.pallas_call(
        functools.partial(_low_rank_kernel, act=act),
        grid=(m // tm,),
        in_specs=[
            pl.BlockSpec((tm, kdim), lambda i: (i, 0)),
            pl.BlockSpec((kdim, r), lambda i: (0, 0)),
            pl.BlockSpec((r, n), lambda i: (0, 0)),
            pl.BlockSpec((1, n), lambda i: (0, 0)),
        ],
        out_specs=pl.BlockSpec((tm, n), lambda i: (i, 0)),
        out_shape=jax.ShapeDtypeStruct((m, n), F32),
        compiler_params=_cparams(("parallel",)),
        name="low_rank",
    )(a, w1, w2, bias.reshape(1, n).astype(F32))


def _layer_norm_rows(v, g, b):
    mu = jnp.mean(v, -1, keepdims=True)
    var = jnp.mean(jnp.square(v - mu), -1, keepdims=True)
    return (v - mu) * lax.rsqrt(var + LN_EPS) * g + b


def _to_token_major(o3_ref, v):
    for s in range(v.shape[1] // LANES):
        o3_ref[:, s, :] = v[:, s * LANES:(s + 1) * LANES].astype(o3_ref.dtype)


def _tiles_to_rows(tiles_ref, t0, t1):
    return jnp.concatenate([tiles_ref[t0:t1, s].reshape((t1 - t0) * SUBLANES, LANES)
                            for s in range(tiles_ref.shape[1])], axis=1)


def _add_ln_kernel(x_ref, y_ref, g_ref, b_ref, o_ref, o3_ref):
    r = _layer_norm_rows(ALPHA * x_ref[...] + y_ref[...], g_ref[...], b_ref[...])
    o_ref[...] = r
    _to_token_major(o3_ref, r)


def _add_layer_norm(x, y, g, b, *, tm=256):
    t, d = x.shape
    tm = _tile(t, tm)
    row = pl.BlockSpec((tm, d), lambda i: (i, 0))
    row3 = pl.BlockSpec((tm, d // LANES, LANES), lambda i: (i, 0, 0))
    vec = pl.BlockSpec((1, d), lambda i: (0, 0))
    return pl.pallas_call(
        _add_ln_kernel,
        grid=(t // tm,),
        in_specs=[row, row, vec, vec],
        out_specs=[row, row3],
        out_shape=[jax.ShapeDtypeStruct((t, d), F32), jax.ShapeDtypeStruct((t, d // LANES, LANES), F32)],
        compiler_params=_cparams(("parallel",)),
        name="add_layer_norm",
    )(x, y, g.reshape(1, d), b.reshape(1, d))


CONV_HALO = SUBLANES
SSD_CH = SSM_GROUP_WIDTH + 2 * SSM_STATE


def _ssd_kernel(z_ref, x_ref, b_ref, c_ref, wx_ref, wb_ref, wc_ref, bx_ref, bb_ref, bc_ref,
                dtr_ref, dbc_ref, dbr_ref, alc_ref, alr_ref, d_ref, nw_ref,
                o_ref, state_ref, pad_ref):
    ch, gw, ns, hd = SSM_CHUNK, SSM_GROUP_WIDTH, SSM_STATE, SSM_HEAD_DIM

    @pl.when(pl.program_id(2) == 0)
    def _():
        state_ref[...] = jnp.zeros_like(state_ref)
        pad_ref[0:CONV_HALO, :] = jnp.zeros((CONV_HALO, SSD_CH), F32)

    pad_ref[CONV_HALO:CONV_HALO + ch, 0:gw] = x_ref[...]
    pad_ref[CONV_HALO:CONV_HALO + ch, gw:gw + ns] = b_ref[...]
    pad_ref[CONV_HALO:CONV_HALO + ch, gw + ns:SSD_CH] = c_ref[...]
    w = jnp.concatenate([wx_ref[...], wb_ref[...], wc_ref[...]], axis=1)
    acc = jnp.concatenate([bx_ref[...], bb_ref[...], bc_ref[...]], axis=1)
    base = CONV_HALO - (SSM_CONV - 1)
    for k in range(SSM_CONV):
        acc = acc + w[k:k + 1, :] * pad_ref[base + k:base + k + ch, :]
    pad_ref[0:CONV_HALO, :] = pad_ref[ch:ch + CONV_HALO, :]
    xbc = _silu(acc)
    xc, bm, cm = xbc[:, :gw], xbc[:, gw:gw + ns], xbc[:, gw + ns:]

    row = lax.broadcasted_iota(jnp.int32, (ch, ch), 0)
    col = lax.broadcasted_iota(jnp.int32, (ch, ch), 1)
    causal = row >= col
    dt_raw_r = dtr_ref[...]
    eye = (row == col).astype(BF16)
    dt_raw_c = sum(lax.dot_general(eye, term, (((1,), (1,)), ((), ())), preferred_element_type=F32)
                   for term in _split3(dt_raw_r))
    dt_c = _softplus(dt_raw_c + dbc_ref[0])
    dt_r = _softplus(dt_raw_r + dbr_ref[...])
    a_c = -jnp.exp(alc_ref[0])
    a_r = -jnp.exp(alr_ref[...])
    acs_c = _dot_split3(dt_c * a_c, causal.astype(BF16), split_lhs=False)
    acs_r = _dot_split3(dt_r * a_r, (row <= col).astype(BF16), split_lhs=True)

    hrow = lax.broadcasted_iota(jnp.int32, (SSM_HEADS_PER_GROUP, gw), 0)
    hcol = lax.broadcasted_iota(jnp.int32, (SSM_HEADS_PER_GROUP, gw), 1)
    expand = (hcol // hd == hrow).astype(BF16)
    dt_x = _dot_split3(dt_c, expand, split_lhs=True)
    acs_x = _dot_split3(acs_c, expand, split_lhs=True)

    xdt = xc * dt_x
    xdt_b = xdt.astype(BF16)
    bm_b, cm_b = bm.astype(BF16), cm.astype(BF16)
    cb = lax.dot_general(cm_b, bm_b, (((1,), (1,)), ((), ())), preferred_element_type=F32)
    prev = state_ref[...]
    y_off = jnp.dot(cm_b, prev.astype(BF16), preferred_element_type=F32) * jnp.exp(acs_x)

    lane = lax.broadcasted_iota(jnp.int32, (ch, 2 * hd), 1)
    pairs = []
    for pr in range(SSM_HEADS_PER_GROUP // 2):
        xp = xdt_b[:, pr * 2 * hd:(pr + 1) * 2 * hd]
        halves = []
        for q in range(2):
            r = 2 * pr + q
            diff = acs_c[:, r:r + 1] - acs_r[r:r + 1, :]
            seg = jnp.where(causal, jnp.exp(jnp.where(causal, diff, 0.0)), 0.0)
            halves.append(jnp.dot((cb * seg).astype(BF16), xp, preferred_element_type=F32))
        pairs.append(jnp.where(lane < hd, halves[0], halves[1]))
    y = jnp.concatenate(pairs, axis=1) + y_off + d_ref[...] * xc

    acs_last = acs_x[ch - 1:ch, :]
    contrib = lax.dot_general(bm_b, (xdt * jnp.exp(acs_last - acs_x)).astype(BF16),
                              (((0,), (0,)), ((), ())), preferred_element_type=F32)
    state_ref[...] = prev * jnp.exp(acs_last) + contrib

    u = y * _silu(z_ref[...])
    u = u * lax.rsqrt(jnp.mean(u * u, -1, keepdims=True) + SSM_NORM_EPS)
    o_ref[...] = (u * nw_ref[...]).astype(o_ref.dtype)


def _ssd_mixer(u_z, u_xbc, u_dt, conv_w, conv_b, dt_bias, a_log, d_skip, norm_w, batch, seq):
    t = batch * seq
    nc = seq // SSM_CHUNK
    g, r, gw, ns = SSM_GROUPS, SSM_HEADS_PER_GROUP, SSM_GROUP_WIDTH, SSM_STATE
    xblocks = D_MODEL // ns
    conv_wt = conv_w.T
    conv_b2 = conv_b.reshape(1, SSM_CONV_CH)
    dt_row = u_dt.T
    rows = lambda b, gi, c: (b * nc + c, gi)
    in_specs = [
        pl.BlockSpec((SSM_CHUNK, gw), rows),
        pl.BlockSpec((SSM_CHUNK, gw), rows),
        pl.BlockSpec((SSM_CHUNK, ns), lambda b, gi, c: (b * nc + c, xblocks + gi)),
        pl.BlockSpec((SSM_CHUNK, ns), lambda b, gi, c: (b * nc + c, xblocks + g + gi)),
        pl.BlockSpec((SSM_CONV, gw), lambda b, gi, c: (0, gi)),
        pl.BlockSpec((SSM_CONV, ns), lambda b, gi, c: (0, xblocks + gi)),
        pl.BlockSpec((SSM_CONV, ns), lambda b, gi, c: (0, xblocks + g + gi)),
        pl.BlockSpec((1, gw), lambda b, gi, c: (0, gi)),
        pl.BlockSpec((1, ns), lambda b, gi, c: (0, xblocks + gi)),
        pl.BlockSpec((1, ns), lambda b, gi, c: (0, xblocks + g + gi)),
        pl.BlockSpec((r, SSM_CHUNK), lambda b, gi, c: (gi, b * nc + c)),
        pl.BlockSpec((1, 1, r), lambda b, gi, c: (gi, 0, 0)),
        pl.BlockSpec((r, 1), lambda b, gi, c: (gi, 0)),
        pl.BlockSpec((1, 1, r), lambda b, gi, c: (gi, 0, 0)),
        pl.BlockSpec((r, 1), lambda b, gi, c: (gi, 0)),
        pl.BlockSpec((1, gw), lambda b, gi, c: (0, gi)),
        pl.BlockSpec((1, gw), lambda b, gi, c: (0, gi)),
    ]
    return pl.pallas_call(
        _ssd_kernel,
        grid=(batch, g, nc),
        in_specs=in_specs,
        out_specs=pl.BlockSpec((SSM_CHUNK, gw), rows),
        out_shape=jax.ShapeDtypeStruct((t, D_MODEL), BF16),
        scratch_shapes=[pltpu.VMEM((ns, gw), F32), pltpu.VMEM((CONV_HALO + SSM_CHUNK, SSD_CH), F32)],
        compiler_params=_cparams(("parallel", "parallel", "arbitrary")),
        name="ssd_mixer",
    )(u_z, u_xbc, u_xbc, u_xbc, conv_wt, conv_wt, conv_wt, conv_b2, conv_b2, conv_b2,
      dt_row, dt_bias.reshape(g, 1, r), dt_bias.reshape(SSM_HEADS, 1),
      a_log.reshape(g, 1, r), a_log.reshape(SSM_HEADS, 1),
      jnp.repeat(d_skip, SSM_HEAD_DIM).reshape(1, D_MODEL), norm_w.reshape(1, D_MODEL))


KV_PAIR = 4

def _rotary_tables(seq):
    half = ROPE_DIM // 2
    inv_freq = ROPE_THETA ** (-jnp.arange(half, dtype=F32) / half)
    ang = jnp.arange(seq, dtype=F32)[:, None] * inv_freq[None, :]
    ones = jnp.ones((seq, ATT_HEAD_DIM - ROPE_DIM), F32)
    cos = jnp.concatenate([jnp.cos(ang), jnp.cos(ang), ones], -1)
    sin = jnp.concatenate([jnp.sin(ang), jnp.sin(ang), 0.0 * ones], -1)
    return jnp.tile(cos, (1, ATT_REP)), jnp.tile(sin, (1, ATT_REP))


def _rotate_half_matrix():
    width = ATT_REP * ATT_HEAD_DIM
    half = ROPE_DIM // 2
    p = np.zeros((width, width), np.float32)
    for j in range(width):
        if j % ATT_HEAD_DIM < half:
            p[j + half, j] = -1.0
        elif j % ATT_HEAD_DIM < ROPE_DIM:
            p[j - half, j] = 1.0
    return jnp.asarray(p, BF16)


def _rotate(v, cos, sin, perm):
    hi = v.astype(BF16)
    lo = (v - hi.astype(F32)).astype(BF16)
    partner = jnp.dot(hi, perm, preferred_element_type=F32) + jnp.dot(lo, perm, preferred_element_type=F32)
    return v * cos + partner * sin


def _swa_kernel(sink_ref, q_ref, kc_ref, kp_ref, vc_ref, vp_ref, cosc_ref, sinc_ref, cosp_ref, sinp_ref,
                perm_ref, o_ref):
    w, hd = ATT_WINDOW, ATT_HEAD_DIM
    qw, kw = ATT_REP * hd, KV_PAIR * hd
    pair = pl.program_id(1)
    n = pl.program_id(2)
    perm = perm_ref[...]
    perm_k = perm[:kw, :kw]
    cos_c, sin_c = cosc_ref[...], sinc_ref[...]
    q = jnp.concatenate([_rotate(q_ref[:, p * qw:(p + 1) * qw], cos_c, sin_c, perm) for p in range(KV_PAIR)],
                        axis=1)
    k_cur = _rotate(kc_ref[...], cos_c[:, :kw], sin_c[:, :kw], perm_k)
    k_prev = _rotate(kp_ref[...], cosp_ref[:, :kw], sinp_ref[:, :kw], perm_k)
    k_pair = jnp.concatenate([k_prev, k_cur], axis=0).astype(BF16)
    v_pair = jnp.concatenate([vp_ref[...], vc_ref[...]], axis=0).astype(BF16)
    k_all = [k_pair[:, p * hd:(p + 1) * hd] for p in range(KV_PAIR)]
    v_all = [v_pair[:, p * hd:(p + 1) * hd] for p in range(KV_PAIR)]
    qi = lax.broadcasted_iota(jnp.int32, (w, 2 * w), 0) + w
    ki = lax.broadcasted_iota(jnp.int32, (w, 2 * w), 1)
    rel = qi - ki
    first = jnp.where(n > 0, 0, w)
    bias = jnp.where((rel >= 0) & (rel < ATT_WINDOW) & (ki >= first), 0.0, -jnp.inf)
    heads = range(KV_PAIR * ATT_REP)
    qs = [(q[:, r * hd:(r + 1) * hd] * (hd ** -0.5)).astype(BF16) for r in heads]
    sinks = [sink_ref[pair * KV_PAIR * ATT_REP + r] for r in heads]
    ss = [lax.dot_general(qr, k_all[r // ATT_REP], NT_DIMS, preferred_element_type=F32) + bias
          for r, qr in zip(heads, qs)]
    ms = [jnp.maximum(jnp.max(s, -1, keepdims=True), sink) for s, sink in zip(ss, sinks)]
    es = [jnp.exp(s - m) for s, m in zip(ss, ms)]
    dens = [jnp.sum(e, -1, keepdims=True) + jnp.exp(sink - m) for e, sink, m in zip(es, sinks, ms)]
    outs = [jnp.dot(e.astype(BF16), v_all[r // ATT_REP], preferred_element_type=F32) / den
            for r, e, den in zip(heads, es, dens)]
    o_ref[...] = jnp.concatenate(outs, axis=1).astype(o_ref.dtype)


def _swa_attention(qkv, sinks, batch, seq):
    t = batch * seq
    nb = seq // ATT_WINDOW
    hd, w = ATT_HEAD_DIM, ATT_WINDOW
    qw, kw = ATT_REP * hd, KV_PAIR * hd
    k_block0 = Q_COLS // kw
    v_block0 = (Q_COLS + KV_COLS) // kw
    cos, sin = _rotary_tables(seq)

    def kv_spec(block0, back):
        return pl.BlockSpec((w, kw), lambda b, p, n: (b * nb + jnp.maximum(n - back, 0), block0 + p))

    tab_cur = pl.BlockSpec((w, qw), lambda b, p, n: (n, 0))
    tab_prev = pl.BlockSpec((w, qw), lambda b, p, n: (jnp.maximum(n - 1, 0), 0))
    return pl.pallas_call(
        _swa_kernel,
        grid=(batch, ATT_KV_HEADS // KV_PAIR, nb),
        in_specs=[
            pl.BlockSpec(memory_space=pltpu.SMEM),
            pl.BlockSpec((w, KV_PAIR * qw), lambda b, p, n: (b * nb + n, p)),
            kv_spec(k_block0, 0), kv_spec(k_block0, 1), kv_spec(v_block0, 0), kv_spec(v_block0, 1),
            tab_cur, tab_cur, tab_prev, tab_prev,
            pl.BlockSpec((qw, qw), lambda b, p, n: (0, 0)),
        ],
        out_specs=pl.BlockSpec((w, KV_PAIR * qw), lambda b, p, n: (b * nb + n, p)),
        out_shape=jax.ShapeDtypeStruct((t, Q_COLS), BF16),
        compiler_params=_cparams(("parallel", "parallel", "arbitrary")),
        name="swa_attention",
    )(sinks, qkv, qkv, qkv, qkv, qkv, cos, sin, cos, sin, _rotate_half_matrix())


def _ssd_swa_mixer(x2, xb, batch, seq, w_in, b_qkv, conv_w, conv_b, dt_bias, a_log, d_skip, norm_w, sinks,
                   w_out, b_out):
    o1 = D_MODEL
    o2 = o1 + SSM_CONV_CH
    o3 = o2 + SSM_HEADS
    w_t = jnp.swapaxes(w_in, 0, 1)
    w_t_b = w_t.astype(BF16)
    u_z = _matmul(xb, w_t_b, n=o1, w_is_nk=True)
    u_xbc = _matmul(xb, w_t_b, col_start=o1, n=SSM_CONV_CH, w_is_nk=True)
    u_dt = _matmul(x2, w_t[o2:o3], bf16x3=True, w_is_nk=True)
    qkv = _matmul(xb, w_t_b[o3:], b_qkv, w_is_nk=True)
    y_ssm = _ssd_mixer(u_z, u_xbc, u_dt, conv_w, conv_b, dt_bias, a_log, d_skip, norm_w, batch, seq)
    y_att = _swa_attention(qkv, sinks, batch, seq)
    return _matmul((y_ssm, y_att), w_out.astype(BF16), b_out)


MOE_TILE = 256
MOE_GATHER_TILE = 256
ROUTER_TILE = 256


def _router_kernel(h_ref, wt_ref, b_ref, ids_ref, wts_ref):
    logits = _dot_bf16x3(wt_ref[...], h_ref[...], NT_DIMS) + b_ref[...]
    e = jnp.exp(logits - jnp.max(logits, 0, keepdims=True))
    probs = e / jnp.sum(e, 0, keepdims=True)
    p = [probs[i:i + 1, :] for i in range(N_EXPERTS)]

    keep, score = [], []
    for g in range(N_EXPERT_GROUPS):
        members = range(g * EXPERTS_PER_GROUP, (g + 1) * EXPERTS_PER_GROUP)
        s = jnp.zeros_like(p[0])
        for i in members:
            rank = jnp.zeros_like(p[0])
            for j in members:
                if j != i:
                    beats = (p[j] > p[i]) | ((p[j] == p[i]) & (j < i)) if j < i else (p[j] > p[i])
                    rank = rank + beats.astype(F32)
            keep.append(rank < TOP_K)
            s = s + jnp.where(keep[i], p[i], 0.0)
        score.append(s)
    chosen = []
    for g in range(N_EXPERT_GROUPS):
        c = jnp.ones_like(p[0]) > 0
        for g2 in range(N_EXPERT_GROUPS):
            if g2 < g:
                c = c & (score[g] > score[g2])
            elif g2 > g:
                c = c & (score[g] >= score[g2])
        chosen.append(c)
    denom = jnp.zeros_like(p[0])
    for g in range(N_EXPERT_GROUPS):
        denom = denom + jnp.where(chosen[g], score[g], 0.0)
    count = jnp.zeros_like(p[0])
    id0 = jnp.zeros_like(p[0])
    id1 = jnp.zeros_like(p[0])
    w0 = jnp.zeros_like(p[0])
    w1 = jnp.zeros_like(p[0])
    for i in range(N_EXPERTS):
        sel = keep[i] & chosen[i // EXPERTS_PER_GROUP]
        gate = p[i] / denom
        first = sel & (count == 0.0)
        second = sel & (count == 1.0)
        id0 = jnp.where(first, float(i), id0)
        w0 = jnp.where(first, gate, w0)
        id1 = jnp.where(second, float(i), id1)
        w1 = jnp.where(second, gate, w1)
        count = count + sel.astype(F32)
    ids_ref[0:1, :] = id0.astype(jnp.int32)
    ids_ref[1:2, :] = id1.astype(jnp.int32)
    wts_ref[0:1, :] = w0
    wts_ref[1:2, :] = w1


def _router(h, router_w, router_b):
    t, d = h.shape
    tm = _tile(t, ROUTER_TILE)
    return pl.pallas_call(
        _router_kernel,
        grid=(t // tm,),
        in_specs=[
            pl.BlockSpec((tm, d), lambda i: (i, 0)),
            pl.BlockSpec((N_EXPERTS, d), lambda i: (0, 0)),
            pl.BlockSpec((N_EXPERTS, 1), lambda i: (0, 0)),
        ],
        out_specs=[pl.BlockSpec((TOP_K, tm), lambda i: (0, i)), pl.BlockSpec((TOP_K, tm), lambda i: (0, i))],
        out_shape=[jax.ShapeDtypeStruct((TOP_K, t), jnp.int32), jax.ShapeDtypeStruct((TOP_K, t), F32)],
        compiler_params=_cparams(("parallel",)),
        name="moe_router",
    )(h, router_w.T, router_b.reshape(N_EXPERTS, 1))


def _route_metadata(ids, wts, tm):
    t = ids.shape[1]
    na = TOP_K * t
    eid = ids.T.reshape(na)
    onehot = (eid[:, None] == jnp.arange(N_EXPERTS, dtype=jnp.int32)[None, :]).astype(jnp.int32)
    csum = jnp.cumsum(onehot, axis=0)
    counts = csum[-1]
    rank = jnp.sum(csum * onehot, axis=1) - 1
    padded = (counts + tm - 1) // tm * tm
    pend = jnp.cumsum(padded)
    dest = (pend - padded)[eid] + rank
    n_tiles = na // tm + N_EXPERTS
    n_rows = n_tiles * tm
    row_assign = jnp.full((n_rows,), -1, jnp.int32).at[dest].set(jnp.arange(na, dtype=jnp.int32))
    assigned = jnp.maximum(row_assign, 0)
    row_token = assigned // TOP_K
    row_weight = jnp.where(row_assign >= 0, wts.T.reshape(na)[assigned], 0.0)
    tile_start = jnp.arange(n_tiles, dtype=jnp.int32) * tm
    tile_valid = (tile_start < pend[-1]).astype(jnp.int32)
    last_expert = jnp.max(jnp.where(counts > 0, jnp.arange(N_EXPERTS, dtype=jnp.int32), 0))
    tile_expert = jnp.minimum(jnp.searchsorted(pend, tile_start, side="right").astype(jnp.int32), last_expert)
    experts = jnp.arange(N_EXPERTS, dtype=jnp.int32)
    later = jnp.where((counts > 0)[None, :] & (experts[None, :] > experts[:, None]), experts[None, :], N_EXPERTS)
    next_nonempty = jnp.min(later, axis=1)
    next_nonempty = jnp.where(next_nonempty == N_EXPERTS, tile_expert[0], next_nonempty).astype(jnp.int32)
    prev_expert = jnp.concatenate([jnp.full((1,), -1, jnp.int32), tile_expert[:-1]])
    run_first = (tile_valid == 1) & (tile_expert != prev_expert)
    runs = jnp.stack([tile_expert, tile_valid, run_first.astype(jnp.int32), next_nonempty[tile_expert],
                      (tile_expert == last_expert).astype(jnp.int32)])
    return row_token, row_weight, dest, runs


def _row_copy(src_hbm, tiles_ref, sem, tile, sub, src_row):
    return pltpu.make_async_copy(src_hbm.at[src_row], tiles_ref.at[tile, :, sub, :], sem)


def _start_row_gather(src_hbm, tiles_ref, idx_ref, n, sem):
    def body(tile, carry):
        for sub in range(SUBLANES):
            _row_copy(src_hbm, tiles_ref, sem, tile, sub, idx_ref[0, 0, tile * SUBLANES + sub]).start(
                priority=sub % 2)
        return carry

    lax.fori_loop(0, n // SUBLANES, body, 0)


def _wait_row_gather(src_hbm, tiles_ref, n, sem):
    def body(tile, carry):
        for sub in range(SUBLANES):
            _row_copy(src_hbm, tiles_ref, sem, tile, sub, 0).wait()
        return carry

    lax.fori_loop(0, n // SUBLANES, body, 0)


def _gather_step(src_hbm, slots_ref, idx_ref, idx_next_ref, sems, n, *, more):
    i = pl.program_id(0)
    slot = i % 2

    @pl.when(i == 0)
    def _():
        _start_row_gather(src_hbm, slots_ref.at[0], idx_ref, n, sems.at[0])

    @pl.when(more)
    def _():
        _start_row_gather(src_hbm, slots_ref.at[1 - slot], idx_next_ref, n, sems.at[1 - slot])

    return slot


def _moe_gather_kernel(tv_ref, rt_ref, rt_next_ref, h_hbm, o_ref, xg_ref, sems, *, tm):
    i = pl.program_id(0)
    nxt = jnp.minimum(i + 1, pl.num_programs(0) - 1)
    valid = tv_ref[i] == 1
    slot = _gather_step(h_hbm, xg_ref, rt_ref, rt_next_ref, sems, tm, more=(nxt > i) & (tv_ref[nxt] == 1))

    @pl.when(valid)
    def _():
        _wait_row_gather(h_hbm, xg_ref.at[slot], tm, sems.at[slot])
        o_ref[...] = _tiles_to_rows(xg_ref.at[slot], 0, tm // SUBLANES).astype(o_ref.dtype)

    @pl.when(jnp.logical_not(valid))
    def _():
        o_ref[...] = jnp.zeros_like(o_ref)


RUN_EXPERT, RUN_VALID, RUN_FIRST, RUN_NEXT, RUN_LAST = range(5)


def _stream_expert_weights(runs_ref, w_hbm, stage_ref, wb_ref, sems, *, layer, tn):
    j, i = pl.program_id(0), pl.program_id(1)
    last_pass = j == pl.num_programs(0) - 1

    def copies(expert, col_block):
        col = pl.multiple_of(col_block * tn, tn)
        return [pltpu.make_async_copy(w.at[layer, expert, :, pl.ds(col, tn)], stage_ref.at[m], sems.at[m])
                for m, w in enumerate(w_hbm)]

    @pl.when((j == 0) & (i == 0))
    def _():
        for c in copies(runs_ref[RUN_EXPERT, 0], 0):
            c.start()

    @pl.when(runs_ref[RUN_FIRST, i] == 1)
    def _():
        for c in copies(runs_ref[RUN_EXPERT, i], j):
            c.wait()
        for m in range(len(w_hbm)):
            wb_ref[m] = stage_ref[m].astype(BF16)
        last_run = runs_ref[RUN_LAST, i] == 1

        @pl.when(jnp.logical_not(last_run & last_pass))
        def _():
            for c in copies(runs_ref[RUN_NEXT, i], jnp.where(last_run, j + 1, j)):
                c.start()


def _moe_up_kernel(runs_ref, x_ref, wg_hbm, wu_hbm, o_ref, stage_ref, wb_ref, sems, *, layer, tn):
    _stream_expert_weights(runs_ref, (wg_hbm, wu_hbm), stage_ref, wb_ref, sems, layer=layer, tn=tn)
    valid = runs_ref[RUN_VALID, pl.program_id(1)] == 1

    @pl.when(valid)
    def _():
        xt = x_ref[...]
        gate = jnp.dot(xt, wb_ref[0], preferred_element_type=F32)
        up = jnp.dot(xt, wb_ref[1], preferred_element_type=F32)
        o_ref[...] = (_silu(gate) * up).astype(o_ref.dtype)

    @pl.when(jnp.logical_not(valid))
    def _():
        o_ref[...] = jnp.zeros_like(o_ref)


def _moe_down_kernel(runs_ref, he_ref, wd_hbm, rw_ref, o_ref, stage_ref, wb_ref, sems, *, layer, tn):
    _stream_expert_weights(runs_ref, (wd_hbm,), stage_ref, wb_ref, sems, layer=layer, tn=tn)
    valid = runs_ref[RUN_VALID, pl.program_id(1)] == 1

    @pl.when(valid)
    def _():
        _to_token_major(o_ref, jnp.dot(he_ref[...], wb_ref[0], preferred_element_type=F32) * rw_ref[...])

    @pl.when(jnp.logical_not(valid))
    def _():
        o_ref[...] = jnp.zeros_like(o_ref)


def _moe_combine_ln_kernel(pos_ref, pos_next_ref, h_ref, y_hbm, g_ref, b_ref, o_ref, ob_ref, yg_ref, sems, *, tm):
    n = TOP_K * tm
    slot = _gather_step(y_hbm, yg_ref, pos_ref, pos_next_ref, sems, n,
                        more=pl.program_id(0) + 1 < pl.num_programs(0))
    _wait_row_gather(y_hbm, yg_ref.at[slot], n, sems.at[slot])
    per = tm // SUBLANES
    ffn = _tiles_to_rows(yg_ref.at[slot], 0, per) + _tiles_to_rows(yg_ref.at[slot], per, TOP_K * per)
    r = _layer_norm_rows(ALPHA * h_ref[...] + ffn, g_ref[...], b_ref[...])
    o_ref[...] = r
    ob_ref[...] = r.astype(BF16)


def _moe_block(h, h3, router_w, router_b, w_gate, w_up, w_down, layer, ln_g, ln_b, *, tm=MOE_TILE, tn_up=512,
               tn_down=2048, tm_out=128):
    t, d = h.shape
    slabs = d // LANES
    ids, wts = _router(h, router_w, router_b)
    row_token, row_weight, dest, runs = _route_metadata(ids, wts, tm)
    tile_valid = runs[RUN_VALID]
    n_rows = row_token.shape[0]
    n_tiles = n_rows // tm
    de = w_gate.shape[-1]
    tn_up, tn_down = _tile(de, tn_up), _tile(d, tn_down)

    def gather_scratch(n):
        return [pltpu.VMEM((2, n // SUBLANES, slabs, SUBLANES, LANES), F32), pltpu.SemaphoreType.DMA((2,))]

    tg = _tile(tm, MOE_GATHER_TILE)
    per = tm // tg
    n_gather = n_rows // tg
    row_token3 = row_token.reshape(n_gather, 1, tg)
    x_rows = pl.pallas_call(
        functools.partial(_moe_gather_kernel, tm=tg),
        grid_spec=pltpu.PrefetchScalarGridSpec(
            num_scalar_prefetch=1,
            grid=(n_gather,),
            in_specs=[
                pl.BlockSpec((1, 1, tg), lambda i, tv: (i, 0, 0), memory_space=pltpu.SMEM),
                pl.BlockSpec((1, 1, tg), lambda i, tv: (jnp.minimum(i + 1, n_gather - 1), 0, 0),
                             memory_space=pltpu.SMEM),
                pl.BlockSpec(memory_space=pl.ANY),
            ],
            out_specs=pl.BlockSpec((tg, d), lambda i, tv: (i, 0)),
            scratch_shapes=gather_scratch(tg),
        ),
        out_shape=jax.ShapeDtypeStruct((n_rows, d), BF16),
        compiler_params=_cparams(("arbitrary",)),
        name="moe_gather",
    )(jnp.repeat(tile_valid, per), row_token3, row_token3, h3)

    def weight_scratch(n_mats, k, tn):
        return [pltpu.VMEM((n_mats, k, tn), F32), pltpu.VMEM((n_mats, k, tn), BF16),
                pltpu.SemaphoreType.DMA((n_mats,))]

    he = pl.pallas_call(
        functools.partial(_moe_up_kernel, layer=layer, tn=tn_up),
        grid_spec=pltpu.PrefetchScalarGridSpec(
            num_scalar_prefetch=1,
            grid=(de // tn_up, n_tiles),
            in_specs=[
                pl.BlockSpec((tm, d), lambda j, i, runs: (i, 0)),
                pl.BlockSpec(memory_space=pl.ANY),
                pl.BlockSpec(memory_space=pl.ANY),
            ],
            out_specs=pl.BlockSpec((tm, tn_up), lambda j, i, runs: (i, j)),
            scratch_shapes=weight_scratch(2, d, tn_up),
        ),
        out_shape=jax.ShapeDtypeStruct((n_rows, de), BF16),
        compiler_params=_cparams(("arbitrary", "arbitrary")),
        name="moe_gate_up",
    )(runs, x_rows, w_gate, w_up)

    y_rows = pl.pallas_call(
        functools.partial(_moe_down_kernel, layer=layer, tn=tn_down),
        grid_spec=pltpu.PrefetchScalarGridSpec(
            num_scalar_prefetch=1,
            grid=(d // tn_down, n_tiles),
            in_specs=[
                pl.BlockSpec((tm, de), lambda j, i, runs: (i, 0)),
                pl.BlockSpec(memory_space=pl.ANY),
                pl.BlockSpec((tm, 1), lambda j, i, runs: (i, 0)),
            ],
            out_specs=pl.BlockSpec((tm, tn_down // LANES, LANES), lambda j, i, runs: (i, j, 0)),
            scratch_shapes=weight_scratch(1, de, tn_down),
        ),
        out_shape=jax.ShapeDtypeStruct((n_rows, slabs, LANES), F32),
        compiler_params=_cparams(("arbitrary", "arbitrary")),
        name="moe_down",
    )(runs, he, w_down, row_weight.reshape(n_rows, 1))

    tm_out = _tile(t, tm_out)
    n_out = t // tm_out
    pos = dest.reshape(n_out, tm_out, TOP_K).transpose(0, 2, 1).reshape(n_out, 1, TOP_K * tm_out)
    row = pl.BlockSpec((tm_out, d), lambda i: (i, 0))
    vec = pl.BlockSpec((1, d), lambda i: (0, 0))
    return pl.pallas_call(
        functools.partial(_moe_combine_ln_kernel, tm=tm_out),
        grid=(n_out,),
        in_specs=[
            pl.BlockSpec((1, 1, TOP_K * tm_out), lambda i: (i, 0, 0), memory_space=pltpu.SMEM),
            pl.BlockSpec((1, 1, TOP_K * tm_out), lambda i: (jnp.minimum(i + 1, n_out - 1), 0, 0),
                         memory_space=pltpu.SMEM),
            row,
            pl.BlockSpec(memory_space=pl.ANY),
            vec, vec,
        ],
        out_specs=[row, row],
        out_shape=[jax.ShapeDtypeStruct((t, d), F32), jax.ShapeDtypeStruct((t, d), BF16)],
        scratch_shapes=gather_scratch(TOP_K * tm_out),
        compiler_params=_cparams(("arbitrary",)),
        name="moe_combine_ln",
    )(pos, pos, h, y_rows, ln_g.reshape(1, d), ln_b.reshape(1, d))


def _shift_mix_kernel(x_ref, xp_ref, mix_ref, *rest, tm, seq):
    o_refs, pad_ref = rest[:-1], rest[-1]
    i = pl.program_id(0)
    starts_sequence = (i * tm) % seq == 0
    pad_ref[0:SUBLANES, :] = jnp.where(starts_sequence, 0.0, xp_ref[...])
    pad_ref[SUBLANES:SUBLANES + tm, :] = x_ref[...]
    xv = x_ref[...]
    xx = pad_ref[SUBLANES - 1:SUBLANES - 1 + tm, :] - xv
    for m, o_ref in enumerate(o_refs):
        o_ref[...] = (xv + xx * mix_ref[m:m + 1, :]).astype(o_ref.dtype)


def _shift_mix(x, mix, seq, *, tm=128):
    t, d = x.shape
    nm = mix.shape[0]
    tm = _tile(seq, tm)
    per = tm // SUBLANES
    return pl.pallas_call(
        functools.partial(_shift_mix_kernel, tm=tm, seq=seq),
        grid=(t // tm,),
        in_specs=[
            pl.BlockSpec((tm, d), lambda i: (i, 0)),
            pl.BlockSpec((SUBLANES, d), lambda i: (jnp.maximum(i * per - 1, 0), 0)),
            pl.BlockSpec((nm, d), lambda i: (0, 0)),
        ],
        out_specs=[pl.BlockSpec((tm, d), lambda i: (i, 0))] * nm,
        out_shape=[jax.ShapeDtypeStruct((t, d), BF16)] * nm,
        scratch_shapes=[pltpu.VMEM((SUBLANES + tm, d), F32)],
        compiler_params=_cparams(("parallel",)),
        name="rwkv_shift_mix",
    )(x, x, mix)


WKV_CHUNK = 64
WKV_HEADS = 4
WKV_WIDTH = WKV_HEADS * RWKV_HEAD
WKV_BLOCK = 512
NEUMANN_STEPS = 5


def _wkv_kernel(r_ref, k_ref, v_ref, wl_ref, ar_ref, g_ref, kk_ref, ka_ref, rk_ref, lnw_ref, lnb_ref,
                o_ref, state_ref, *, tb):
    lc, hw, wd = WKV_CHUNK, RWKV_HEAD, WKV_WIDTH
    nch = tb // lc

    @pl.when(pl.program_id(2) == 0)
    def _():
        state_ref[...] = jnp.zeros_like(state_ref)

    wrow = lax.broadcasted_iota(jnp.int32, (wd, wd), 0)
    wcol = lax.broadcasted_iota(jnp.int32, (wd, wd), 1)
    same_head = wrow // hw == wcol // hw
    ones_bd = same_head.astype(BF16)

    def block_diag(m):
        return jnp.where(same_head, jnp.concatenate([m] * WKV_HEADS, axis=0), 0.0).astype(BF16)

    def head_sum(m):
        hi = m.astype(BF16)
        lo = (m - hi.astype(F32)).astype(BF16)
        return jnp.dot(hi, ones_bd, preferred_element_type=F32) + jnp.dot(lo, ones_bd, preferred_element_type=F32)

    def mm(a, b):
        return jnp.dot(a.astype(BF16), b, preferred_element_type=F32)

    r = r_ref[...]
    k = k_ref[...]
    v = v_ref[...]
    w_log = -_softplus(-wl_ref[...]) - 0.5
    lw = -jnp.exp(w_log)
    a_sig = jax.nn.sigmoid(ar_ref[...])
    kx = k * kk_ref[...]
    kk = kx / jnp.maximum(jnp.sqrt(head_sum(kx * kx)), 1e-12)
    k2 = k * (1.0 + (a_sig - 1.0) * ka_ref[...])
    a_s = -kk
    b_s = kk * a_sig

    trow = lax.broadcasted_iota(jnp.int32, (tb, tb), 0)
    tcol = lax.broadcasted_iota(jnp.int32, (tb, tb), 1)
    same_chunk = trow // lc == tcol // lc
    cum = _dot_split3(lw, (same_chunk & (trow >= tcol)).astype(BF16), split_lhs=False)
    tot = jnp.concatenate([jnp.broadcast_to(cum[(c + 1) * lc - 1:(c + 1) * lc, :], (lc, wd)) for c in range(nch)],
                          axis=0)
    grow = jnp.exp(-cum)
    rt = r * jnp.exp(cum)
    at = a_s * jnp.exp(cum - lw)
    bt = b_s * grow
    kt = k2 * grow
    rest = jnp.exp(tot - cum)
    bh = b_s * rest
    kh = k2 * rest
    p_end = jnp.exp(tot)

    t_idx = lax.broadcasted_iota(jnp.int32, (lc, wd), 0)
    s_idx = lax.broadcasted_iota(jnp.int32, (lc, wd), 1) % hw
    strict = t_idx > s_idx
    incl = t_idx >= s_idx
    eye = (t_idx == s_idx).astype(F32)
    nt = (((1,), (1,)), ((), ()))

    tn = (((0,), (0,)), ((), ()))
    chunks = [slice(c * lc, (c + 1) * lc) for c in range(nch)]

    a_ab, a_rb, a_ak, a_rk = [], [], [], []
    for sl in chunks:
        ar = jnp.concatenate([at[sl], rt[sl]], axis=0).astype(BF16)
        xb = lax.dot_general(ar, block_diag(bt[sl]), nt, preferred_element_type=F32)
        xk = lax.dot_general(ar, block_diag(kt[sl]), nt, preferred_element_type=F32)
        a_ab.append(jnp.where(strict, xb[:lc], 0.0))
        a_rb.append(jnp.where(incl, xb[lc:], 0.0))
        a_ak.append(jnp.where(strict, xk[:lc], 0.0))
        a_rk.append(jnp.where(incl, xk[lc:], 0.0))
    inv = [eye + a for a in a_ab]
    pw = [mm(a, block_diag(a)) for a in a_ab]
    for step in range(1, NEUMANN_STEPS + 1):
        last = step == NEUMANN_STEPS
        lhs = inv if last else [jnp.concatenate([t, p], axis=0) for t, p in zip(inv, pw)]
        prod = [mm(x, block_diag(p)) for x, p in zip(lhs, pw)]
        inv = [t + q[:lc] for t, q in zip(inv, prod)]
        if not last:
            pw = [q[lc:] for q in prod]
    v_bd = [block_diag(v[sl]) for sl in chunks]
    ta = [mm(t, block_diag(at[sl])) for t, sl in zip(inv, chunks)]
    av = [mm(jnp.concatenate([ak, rk], axis=0), vb) for ak, rk, vb in zip(a_ak, a_rk, v_bd)]
    y0 = [x[lc:] for x in av]
    u0 = [mm(t, block_diag(x[:lc])) for t, x in zip(inv, av)]
    gain, add = [], []
    for c, sl in enumerate(chunks):
        bh_b = bh[sl].astype(BF16)
        gain.append(jnp.where(same_head, lax.dot_general(ta[c].astype(BF16), bh_b, tn, preferred_element_type=F32),
                              0.0).astype(BF16))
        uv = jnp.concatenate([u0[c], v[sl]], axis=0).astype(BF16)
        bk = jnp.concatenate([bh_b, kh[sl].astype(BF16)], axis=0)
        add.append(jnp.where(same_head, lax.dot_general(uv, bk, tn, preferred_element_type=F32), 0.0))

    states = [state_ref[...]]
    for c in range(nch):
        s0 = states[-1]
        states.append(s0 * p_end[c * lc:c * lc + 1, :] + mm(s0, gain[c]) + add[c])
    state_ref[...] = states[nch]

    ys = []
    for c, sl in enumerate(chunks):
        tr = jnp.concatenate([ta[c], rt[sl]], axis=0).astype(BF16)
        xs = lax.dot_general(tr, states[c].astype(BF16), nt, preferred_element_type=F32)
        u = xs[:lc] + u0[c]
        ys.append(xs[lc:] + mm(a_rb[c], block_diag(u)) + y0[c])

    y = jnp.concatenate(ys, axis=0)
    mu = head_sum(y) / hw
    dev = y - mu
    var = head_sum(dev * dev) / hw
    yn = dev * lax.rsqrt(var + RWKV_GN_EPS) * lnw_ref[...] + lnb_ref[...]
    bonus = head_sum(r * k2 * rk_ref[...]) * v
    o_ref[...] = ((yn + bonus) * g_ref[...]).astype(o_ref.dtype)


def _wkv(r, k, v, wl, araw, g, k_k, k_a, r_k, ln_w, ln_b, batch, seq):
    t, d = r.shape
    tb = _tile(seq, WKV_BLOCK)
    nt = seq // tb
    wd = WKV_WIDTH
    rows = pl.BlockSpec((tb, wd), lambda b, h, n: (b * nt + n, h))
    vec = pl.BlockSpec((1, wd), lambda b, h, n: (0, h))
    return pl.pallas_call(
        functools.partial(_wkv_kernel, tb=tb),
        grid=(batch, d // wd, nt),
        in_specs=[rows] * 6 + [vec] * 5,
        out_specs=rows,
        out_shape=jax.ShapeDtypeStruct((t, d), BF16),
        scratch_shapes=[pltpu.VMEM((wd, wd), F32)],
        compiler_params=_cparams(("parallel", "parallel", "arbitrary")),
        name="rwkv_wkv",
    )(r, k, v, wl, araw, g, k_k.reshape(1, d), k_a.reshape(1, d), r_k.reshape(1, d),
      ln_w.reshape(1, d), ln_b.reshape(1, d))


def _rwkv7_time_mix(x2, batch, seq, mix, w_r, w_k, w_v, w_o, w0, w1, w2, a0, a1, a2, g1, g2, k_k, k_a, r_k,
                    ln_w, ln_b):
    xm = _shift_mix(x2, mix, seq)
    bf = lambda w: w.astype(BF16)
    r = _matmul(xm[0], bf(w_r))
    k = _matmul(xm[2], bf(w_k))
    v = _matmul(xm[3], bf(w_v))
    wl = _low_rank(xm[1], bf(w1), bf(w2), w0, act="tanh")
    araw = _low_rank(xm[4], bf(a1), bf(a2), a0)
    g = _low_rank(xm[5], bf(g1), bf(g2), act="sigmoid")
    yg = _wkv(r, k, v, wl, araw, g, k_k, k_a, r_k, ln_w, ln_b, batch, seq)
    return _matmul(yg, bf(w_o))


def kernel(x, ab_w_in, ab_b_qkv, ssm_conv_w, ssm_conv_b, ssm_dt_bias, ssm_a_log, ssm_d, ssm_norm_w, attn_sinks,
           ab_w_out, ab_b_out, rwkv_mix, rwkv_w_r, rwkv_w_k, rwkv_w_v, rwkv_w_o, rwkv_w0, rwkv_w1, rwkv_w2,
           rwkv_a0, rwkv_a1, rwkv_a2, rwkv_g1, rwkv_g2, rwkv_k_k, rwkv_k_a, rwkv_r_k, rwkv_ln_w, rwkv_ln_b,
           ln_mix_g, ln_mix_b, ln_ffn_g, ln_ffn_b, router_w, router_b, moe_w_gate, moe_w_up, moe_w_down):
    batch, seq, d = x.shape
    x2 = x.reshape(batch * seq, d)
    xb = x2.astype(BF16)
    for layer in range(DEPTH):
        i = layer // 2
        if layer % 2 == 0:
            mix = _ssd_swa_mixer(x2, xb, batch, seq, ab_w_in[i], ab_b_qkv[i], ssm_conv_w[i], ssm_conv_b[i],
                                 ssm_dt_bias[i], ssm_a_log[i], ssm_d[i], ssm_norm_w[i], attn_sinks[i],
                                 ab_w_out[i], ab_b_out[i])
        else:
            mix = _rwkv7_time_mix(x2, batch, seq, rwkv_mix[i], rwkv_w_r[i], rwkv_w_k[i], rwkv_w_v[i], rwkv_w_o[i],
                                  rwkv_w0[i], rwkv_w1[i], rwkv_w2[i], rwkv_a0[i], rwkv_a1[i], rwkv_a2[i],
                                  rwkv_g1[i], rwkv_g2[i], rwkv_k_k[i], rwkv_k_a[i], rwkv_r_k[i].reshape(-1),
                                  rwkv_ln_w[i], rwkv_ln_b[i])
        h, h3 = _add_layer_norm(x2, mix, ln_mix_g[layer], ln_mix_b[layer])
        x2, xb = _moe_block(h, h3, router_w, router_b, moe_w_gate, moe_w_up, moe_w_down, layer,
                            ln_ffn_g[layer], ln_ffn_b[layer])
    return x2.reshape(batch, seq, d)
```

```python
import functools

import jax
import jax.numpy as jnp
import numpy as np
from jax import lax
from jax.experimental import pallas as pl
from jax.experimental.pallas import tpu as pltpu

F32 = jnp.float32
BF16 = jnp.bfloat16

D_MODEL = 4096
DEPTH = 2
SSM_HEAD_DIM = 64
SSM_HEADS = 64
SSM_GROUPS = 8
SSM_HEADS_PER_GROUP = 8
SSM_STATE = 128
SSM_CONV = 4
SSM_CHUNK = 128
SSM_GROUP_WIDTH = SSM_HEADS_PER_GROUP * SSM_HEAD_DIM
SSM_CONV_CH = D_MODEL + 2 * SSM_GROUPS * SSM_STATE
SSM_NORM_EPS = 1e-5
ATT_KV_HEADS = 8
ATT_HEAD_DIM = 64
ATT_REP = 8
ATT_WINDOW = 128
ROPE_DIM = 16
ROPE_THETA = 500000.0
Q_COLS = 4096
KV_COLS = 512
RWKV_HEAD = 64
RWKV_GN_EPS = 64e-5
N_EXPERTS = 16
N_EXPERT_GROUPS = 4
EXPERTS_PER_GROUP = 4
TOP_K = 2
ALPHA = (2 * DEPTH) ** 0.25
LN_EPS = 1e-5

VMEM_LIMIT_BYTES = 56 * 1024 * 1024
LANES = 128
SUBLANES = 8


def _cparams(semantics):
    return pltpu.CompilerParams(dimension_semantics=semantics, vmem_limit_bytes=VMEM_LIMIT_BYTES)


def _tile(dim, pref):
    if dim <= pref:
        return dim
    t = pref
    while dim % t:
        t //= 2
    return t


def _silu(v):
    return v * jax.nn.sigmoid(v)


def _softplus(v):
    return jnp.maximum(v, 0.0) + jnp.log1p(jnp.exp(-jnp.abs(v)))


def _split3(v):
    hi = v.astype(BF16)
    rest = v - hi.astype(F32)
    mid = rest.astype(BF16)
    lo = (rest - mid.astype(F32)).astype(BF16)
    return hi, mid, lo


NT_DIMS = (((1,), (1,)), ((), ()))
NN_DIMS = (((1,), (0,)), ((), ()))


def _dot_bf16x3(a, b, dims):
    a_hi, b_hi = a.astype(BF16), b.astype(BF16)
    a_lo = (a - a_hi.astype(F32)).astype(BF16)
    b_lo = (b - b_hi.astype(F32)).astype(BF16)

    def dg(u, v):
        return lax.dot_general(u, v, dims, preferred_element_type=F32)

    return dg(a_hi, b_hi) + (dg(a_hi, b_lo) + dg(a_lo, b_hi))


def _dot_split3(v, m01, *, split_lhs):
    if split_lhs:
        return sum(jnp.dot(t, m01, preferred_element_type=F32) for t in _split3(v))
    return sum(jnp.dot(m01, t, preferred_element_type=F32) for t in _split3(v))


def _mm_kernel(*refs, act, bf16x3, w_is_nk):
    a_refs, (w_ref, b_ref, o_ref) = refs[:-3], refs[-3:]
    dims = NT_DIMS if w_is_nk else NN_DIMS
    r = b_ref[...]
    k0 = 0
    for a_ref in a_refs:
        k1 = k0 + a_ref.shape[1]
        w_part = w_ref[:, k0:k1] if w_is_nk else w_ref[k0:k1, :]
        if bf16x3:
            r = r + _dot_bf16x3(a_ref[...], w_part, dims)
        else:
            r = r + lax.dot_general(a_ref[...], w_part, dims, preferred_element_type=F32)
        k0 = k1
    if act == "tanh":
        r = jnp.tanh(r)
    elif act == "sigmoid":
        r = jax.nn.sigmoid(r)
    o_ref[...] = r.astype(o_ref.dtype)


MATMUL_VMEM_BUDGET = 44 * 1024 * 1024


def _matmul(a, w, bias=None, *, act=None, out_dtype=F32, bf16x3=False, tm=1024, col_start=0, n=None,
            w_is_nk=False):
    panels = a if isinstance(a, (tuple, list)) else (a,)
    m = panels[0].shape[0]
    kdim = sum(p.shape[1] for p in panels)
    w_n, w_k = (w.shape if w_is_nk else w.shape[::-1])
    n = w_n if n is None else n
    tm = _tile(m, tm)
    out_bytes = jnp.dtype(out_dtype).itemsize
    for tn in (512, 256, LANES):
        tn = _tile(n, tn)
        need = 2 * (tm * kdim * panels[0].dtype.itemsize + kdim * tn * w.dtype.itemsize + tm * tn * out_bytes)
        if need <= MATMUL_VMEM_BUDGET:
            break
    assert need <= MATMUL_VMEM_BUDGET and col_start % tn == 0 and w_k == kdim
    col0 = col_start // tn
    if bias is None:
        bias = jnp.zeros((n,), F32)
    bias = bias.reshape(1, n).astype(F32)
    if w_is_nk:
        w_spec = pl.BlockSpec((tn, kdim), lambda i, j: (col0 + j, 0))
    else:
        w_spec = pl.BlockSpec((kdim, tn), lambda i, j: (0, col0 + j))
    return pl.pallas_call(
        functools.partial(_mm_kernel, act=act, bf16x3=bf16x3, w_is_nk=w_is_nk),
        grid=(m // tm, n // tn),
        in_specs=[pl.BlockSpec((tm, p.shape[1]), lambda i, j: (i, 0)) for p in panels] + [
            w_spec,
            pl.BlockSpec((1, tn), lambda i, j: (0, j)),
        ],
        out_specs=pl.BlockSpec((tm, tn), lambda i, j: (i, j)),
        out_shape=jax.ShapeDtypeStruct((m, n), out_dtype),
        compiler_params=_cparams(("parallel", "arbitrary")),
        name="matmul",
    )(*panels, w, bias)


def _low_rank_kernel(a_ref, w1_ref, w2_ref, b_ref, o_ref, *, act):
    t = jnp.dot(a_ref[...], w1_ref[...], preferred_element_type=F32)
    if act == "tanh":
        t = jnp.tanh(t)
    elif act == "sigmoid":
        t = jax.nn.sigmoid(t)
    o_ref[...] = jnp.dot(t.astype(BF16), w2_ref[...], preferred_element_type=F32) + b_ref[...]


def _low_rank(a, w1, w2, bias=None, *, act=None, tm=512):
    m, kdim = a.shape
    r, n = w2.shape
    tm = _tile(m, tm)
    if bias is None:
        bias = jnp.zeros((n,), F32)
    return pl.pallas_call(
        functools.partial(_low_rank_kernel, act=act),
        grid=(m // tm,),
        in_specs=[
            pl.BlockSpec((tm, kdim), lambda i: (i, 0)),
            pl.BlockSpec((kdim, r), lambda i: (0, 0)),
            pl.BlockSpec((r, n), lambda i: (0, 0)),
            pl.BlockSpec((1, n), lambda i: (0, 0)),
        ],
        out_specs=pl.BlockSpec((tm, n), lambda i: (i, 0)),
        out_shape=jax.ShapeDtypeStruct((m, n), F32),
        compiler_params=_cparams(("parallel",)),
        name="low_rank",
    )(a, w1, w2, bias.reshape(1, n).astype(F32))


def _layer_norm_rows(v, g, b):
    mu = jnp.mean(v, -1, keepdims=True)
    var = jnp.mean(jnp.square(v - mu), -1, keepdims=True)
    return (v - mu) * lax.rsqrt(var + LN_EPS) * g + b


def _to_token_major(o3_ref, v):
    for s in range(v.shape[1] // LANES):
        o3_ref[:, s, :] = v[:, s * LANES:(s + 1) * LANES].astype(o3_ref.dtype)


def _tiles_to_rows(tiles_ref, t0, t1):
    return jnp.concatenate([tiles_ref[t0:t1, s].reshape((t1 - t0) * SUBLANES, LANES)
                            for s in range(tiles_ref.shape[1])], axis=1)


def _add_ln_kernel(x_ref, y_ref, g_ref, b_ref, o_ref, o3_ref):
    r = _layer_norm_rows(ALPHA * x_ref[...] + y_ref[...], g_ref[...], b_ref[...])
    o_ref[...] = r
    _to_token_major(o3_ref, r)


def _add_layer_norm(x, y, g, b, *, tm=256):
    t, d = x.shape
    tm = _tile(t, tm)
    row = pl.BlockSpec((tm, d), lambda i: (i, 0))
    row3 = pl.BlockSpec((tm, d // LANES, LANES), lambda i: (i, 0, 0))
    vec = pl.BlockSpec((1, d), lambda i: (0, 0))
    return pl.pallas_call(
        _add_ln_kernel,
        grid=(t // tm,),
        in_specs=[row, row, vec, vec],
        out_specs=[row, row3],
        out_shape=[jax.ShapeDtypeStruct((t, d), F32), jax.ShapeDtypeStruct((t, d // LANES, LANES), F32)],
        compiler_params=_cparams(("parallel",)),
        name="add_layer_norm",
    )(x, y, g.reshape(1, d), b.reshape(1, d))


CONV_HALO = SUBLANES
SSD_CH = SSM_GROUP_WIDTH + 2 * SSM_STATE


def _ssd_kernel(z_ref, x_ref, b_ref, c_ref, wx_ref, wb_ref, wc_ref, bx_ref, bb_ref, bc_ref,
                dtr_ref, dbc_ref, dbr_ref, alc_ref, alr_ref, d_ref, nw_ref,
                o_ref, state_ref, pad_ref):
    ch, gw, ns, hd = SSM_CHUNK, SSM_GROUP_WIDTH, SSM_STATE, SSM_HEAD_DIM

    @pl.when(pl.program_id(2) == 0)
    def _():
        state_ref[...] = jnp.zeros_like(state_ref)
        pad_ref[0:CONV_HALO, :] = jnp.zeros((CONV_HALO, SSD_CH), F32)

    pad_ref[CONV_HALO:CONV_HALO + ch, 0:gw] = x_ref[...]
    pad_ref[CONV_HALO:CONV_HALO + ch, gw:gw + ns] = b_ref[...]
    pad_ref[CONV_HALO:CONV_HALO + ch, gw + ns:SSD_CH] = c_ref[...]
    w = jnp.concatenate([wx_ref[...], wb_ref[...], wc_ref[...]], axis=1)
    acc = jnp.concatenate([bx_ref[...], bb_ref[...], bc_ref[...]], axis=1)
    base = CONV_HALO - (SSM_CONV - 1)
    for k in range(SSM_CONV):
        acc = acc + w[k:k + 1, :] * pad_ref[base + k:base + k + ch, :]
    pad_ref[0:CONV_HALO, :] = pad_ref[ch:ch + CONV_HALO, :]
    xbc = _silu(acc)
    xc, bm, cm = xbc[:, :gw], xbc[:, gw:gw + ns], xbc[:, gw + ns:]

    row = lax.broadcasted_iota(jnp.int32, (ch, ch), 0)
    col = lax.broadcasted_iota(jnp.int32, (ch, ch), 1)
    causal = row >= col
    dt_raw_r = dtr_ref[...]
    eye = (row == col).astype(BF16)
    dt_raw_c = sum(lax.dot_general(eye, term, (((1,), (1,)), ((), ())), preferred_element_type=F32)
                   for term in _split3(dt_raw_r))
    dt_c = _softplus(dt_raw_c + dbc_ref[0])
    dt_r = _softplus(dt_raw_r + dbr_ref[...])
    a_c = -jnp.exp(alc_ref[0])
    a_r = -jnp.exp(alr_ref[...])
    acs_c = _dot_split3(dt_c * a_c, causal.astype(BF16), split_lhs=False)
    acs_r = _dot_split3(dt_r * a_r, (row <= col).astype(BF16), split_lhs=True)

    hrow = lax.broadcasted_iota(jnp.int32, (SSM_HEADS_PER_GROUP, gw), 0)
    hcol = lax.broadcasted_iota(jnp.int32, (SSM_HEADS_PER_GROUP, gw), 1)
    expand = (hcol // hd == hrow).astype(BF16)
    dt_x = _dot_split3(dt_c, expand, split_lhs=True)
    acs_x = _dot_split3(acs_c, expand, split_lhs=True)

    xdt = xc * dt_x
    xdt_b = xdt.astype(BF16)
    bm_b, cm_b = bm.astype(BF16), cm.astype(BF16)
    cb = lax.dot_general(cm_b, bm_b, (((1,), (1,)), ((), ())), preferred_element_type=F32)
    prev = state_ref[...]
    y_off = jnp.dot(cm_b, prev.astype(BF16), preferred_element_type=F32) * jnp.exp(acs_x)

    lane = lax.broadcasted_iota(jnp.int32, (ch, 2 * hd), 1)
    pairs = []
    for pr in range(SSM_HEADS_PER_GROUP // 2):
        xp = xdt_b[:, pr * 2 * hd:(pr + 1) * 2 * hd]
        halves = []
        for q in range(2):
            r = 2 * pr + q
            diff = acs_c[:, r:r + 1] - acs_r[r:r + 1, :]
            seg = jnp.where(causal, jnp.exp(jnp.where(causal, diff, 0.0)), 0.0)
            halves.append(jnp.dot((cb * seg).astype(BF16), xp, preferred_element_type=F32))
        pairs.append(jnp.where(lane < hd, halves[0], halves[1]))
    y = jnp.concatenate(pairs, axis=1) + y_off + d_ref[...] * xc

    acs_last = acs_x[ch - 1:ch, :]
    contrib = lax.dot_general(bm_b, (xdt * jnp.exp(acs_last - acs_x)).astype(BF16),
                              (((0,), (0,)), ((), ())), preferred_element_type=F32)
    state_ref[...] = prev * jnp.exp(acs_last) + contrib

    u = y * _silu(z_ref[...])
    u = u * lax.rsqrt(jnp.mean(u * u, -1, keepdims=True) + SSM_NORM_EPS)
    o_ref[...] = (u * nw_ref[...]).astype(o_ref.dtype)


def _ssd_mixer(u_z, u_xbc, u_dt, conv_w, conv_b, dt_bias, a_log, d_skip, norm_w, batch, seq):
    t = batch * seq
    nc = seq // SSM_CHUNK
    g, r, gw, ns = SSM_GROUPS, SSM_HEADS_PER_GROUP, SSM_GROUP_WIDTH, SSM_STATE
    xblocks = D_MODEL // ns
    conv_wt = conv_w.T
    conv_b2 = conv_b.reshape(1, SSM_CONV_CH)
    dt_row = u_dt.T
    rows = lambda b, gi, c: (b * nc + c, gi)
    in_specs = [
        pl.BlockSpec((SSM_CHUNK, gw), rows),
        pl.BlockSpec((SSM_CHUNK, gw), rows),
        pl.BlockSpec((SSM_CHUNK, ns), lambda b, gi, c: (b * nc + c, xblocks + gi)),
        pl.BlockSpec((SSM_CHUNK, ns), lambda b, gi, c: (b * nc + c, xblocks + g + gi)),
        pl.BlockSpec((SSM_CONV, gw), lambda b, gi, c: (0, gi)),
        pl.BlockSpec((SSM_CONV, ns), lambda b, gi, c: (0, xblocks + gi)),
        pl.BlockSpec((SSM_CONV, ns), lambda b, gi, c: (0, xblocks + g + gi)),
        pl.BlockSpec((1, gw), lambda b, gi, c: (0, gi)),
        pl.BlockSpec((1, ns), lambda b, gi, c: (0, xblocks + gi)),
        pl.BlockSpec((1, ns), lambda b, gi, c: (0, xblocks + g + gi)),
        pl.BlockSpec((r, SSM_CHUNK), lambda b, gi, c: (gi, b * nc + c)),
        pl.BlockSpec((1, 1, r), lambda b, gi, c: (gi, 0, 0)),
        pl.BlockSpec((r, 1), lambda b, gi, c: (gi, 0)),
        pl.BlockSpec((1, 1, r), lambda b, gi, c: (gi, 0, 0)),
        pl.BlockSpec((r, 1), lambda b, gi, c: (gi, 0)),
        pl.BlockSpec((1, gw), lambda b, gi, c: (0, gi)),
        pl.BlockSpec((1, gw), lambda b, gi, c: (0, gi)),
    ]
    return pl.pallas_call(
        _ssd_kernel,
        grid=(batch, g, nc),
        in_specs=in_specs,
        out_specs=pl.BlockSpec((SSM_CHUNK, gw), rows),
        out_shape=jax.ShapeDtypeStruct((t, D_MODEL), BF16),
        scratch_shapes=[pltpu.VMEM((ns, gw), F32), pltpu.VMEM((CONV_HALO + SSM_CHUNK, SSD_CH), F32)],
        compiler_params=_cparams(("parallel", "parallel", "arbitrary")),
        name="ssd_mixer",
    )(u_z, u_xbc, u_xbc, u_xbc, conv_wt, conv_wt, conv_wt, conv_b2, conv_b2, conv_b2,
      dt_row, dt_bias.reshape(g, 1, r), dt_bias.reshape(SSM_HEADS, 1),
      a_log.reshape(g, 1, r), a_log.reshape(SSM_HEADS, 1),
      jnp.repeat(d_skip, SSM_HEAD_DIM).reshape(1, D_MODEL), norm_w.reshape(1, D_MODEL))


KV_PER_STEP = 8

def _rotary_tables(seq):
    half = ROPE_DIM // 2
    inv_freq = ROPE_THETA ** (-jnp.arange(half, dtype=F32) / half)
    ang = jnp.arange(seq, dtype=F32)[:, None] * inv_freq[None, :]
    ones = jnp.ones((seq, ATT_HEAD_DIM - ROPE_DIM), F32)
    cos = jnp.concatenate([jnp.cos(ang), jnp.cos(ang), ones], -1)
    sin = jnp.concatenate([jnp.sin(ang), jnp.sin(ang), 0.0 * ones], -1)
    return jnp.tile(cos, (1, ATT_REP)), jnp.tile(sin, (1, ATT_REP))


def _rotate_half_matrix():
    width = ATT_REP * ATT_HEAD_DIM
    half = ROPE_DIM // 2
    p = np.zeros((width, width), np.float32)
    for j in range(width):
        if j % ATT_HEAD_DIM < half:
            p[j + half, j] = -1.0
        elif j % ATT_HEAD_DIM < ROPE_DIM:
            p[j - half, j] = 1.0
    return jnp.asarray(p, BF16)


def _rotate(v, cos, sin, perm):
    hi = v.astype(BF16)
    lo = (v - hi.astype(F32)).astype(BF16)
    partner = jnp.dot(hi, perm, preferred_element_type=F32) + jnp.dot(lo, perm, preferred_element_type=F32)
    return v * cos + partner * sin


def _swa_kernel(sink_ref, q_ref, kc_ref, kp_ref, vc_ref, vp_ref, cosc_ref, sinc_ref, cosp_ref, sinp_ref,
                perm_ref, o_ref):
    w, hd = ATT_WINDOW, ATT_HEAD_DIM
    qw, kw = ATT_REP * hd, KV_PER_STEP * hd
    pair = pl.program_id(1)
    n = pl.program_id(2)
    perm = perm_ref[...]
    perm_k = perm[:kw, :kw]
    cos_c, sin_c = cosc_ref[...], sinc_ref[...]
    q = jnp.concatenate([_rotate(q_ref[:, p * qw:(p + 1) * qw], cos_c, sin_c, perm) for p in range(KV_PER_STEP)],
                        axis=1)
    k_cur = _rotate(kc_ref[...], cos_c[:, :kw], sin_c[:, :kw], perm_k)
    k_prev = _rotate(kp_ref[...], cosp_ref[:, :kw], sinp_ref[:, :kw], perm_k)
    k_pair = jnp.concatenate([k_prev, k_cur], axis=0).astype(BF16)
    v_pair = jnp.concatenate([vp_ref[...], vc_ref[...]], axis=0).astype(BF16)
    k_all = [k_pair[:, p * hd:(p + 1) * hd] for p in range(KV_PER_STEP)]
    v_all = [v_pair[:, p * hd:(p + 1) * hd] for p in range(KV_PER_STEP)]
    qi = lax.broadcasted_iota(jnp.int32, (w, 2 * w), 0) + w
    ki = lax.broadcasted_iota(jnp.int32, (w, 2 * w), 1)
    rel = qi - ki
    first = jnp.where(n > 0, 0, w)
    bias = jnp.where((rel >= 0) & (rel < ATT_WINDOW) & (ki >= first), 0.0, -jnp.inf)
    heads = range(KV_PER_STEP * ATT_REP)
    qs = [(q[:, r * hd:(r + 1) * hd] * (hd ** -0.5)).astype(BF16) for r in heads]
    sinks = [sink_ref[pair * KV_PER_STEP * ATT_REP + r] for r in heads]
    ss = [lax.dot_general(qr, k_all[r // ATT_REP], NT_DIMS, preferred_element_type=F32) + bias
          for r, qr in zip(heads, qs)]
    ms = [jnp.maximum(jnp.max(s, -1, keepdims=True), sink) for s, sink in zip(ss, sinks)]
    es = [jnp.exp(s - m) for s, m in zip(ss, ms)]
    dens = [jnp.sum(e, -1, keepdims=True) + jnp.exp(sink - m) for e, sink, m in zip(es, sinks, ms)]
    outs = [jnp.dot(e.astype(BF16), v_all[r // ATT_REP], preferred_element_type=F32) / den
            for r, e, den in zip(heads, es, dens)]
    o_ref[...] = jnp.concatenate(outs, axis=1).astype(o_ref.dtype)


def _swa_attention(qkv, sinks, batch, seq):
    t = batch * seq
    nb = seq // ATT_WINDOW
    hd, w = ATT_HEAD_DIM, ATT_WINDOW
    qw, kw = ATT_REP * hd, KV_PER_STEP * hd
    k_block0 = Q_COLS // kw
    v_block0 = (Q_COLS + KV_COLS) // kw
    cos, sin = _rotary_tables(seq)

    def kv_spec(block0, back):
        return pl.BlockSpec((w, kw), lambda b, p, n: (b * nb + jnp.maximum(n - back, 0), block0 + p))

    tab_cur = pl.BlockSpec((w, qw), lambda b, p, n: (n, 0))
    tab_prev = pl.BlockSpec((w, qw), lambda b, p, n: (jnp.maximum(n - 1, 0), 0))
    return pl.pallas_call(
        _swa_kernel,
        grid=(batch, ATT_KV_HEADS // KV_PER_STEP, nb),
        in_specs=[
            pl.BlockSpec(memory_space=pltpu.SMEM),
            pl.BlockSpec((w, KV_PER_STEP * qw), lambda b, p, n: (b * nb + n, p)),
            kv_spec(k_block0, 0), kv_spec(k_block0, 1), kv_spec(v_block0, 0), kv_spec(v_block0, 1),
            tab_cur, tab_cur, tab_prev, tab_prev,
            pl.BlockSpec((qw, qw), lambda b, p, n: (0, 0)),
        ],
        out_specs=pl.BlockSpec((w, KV_PER_STEP * qw), lambda b, p, n: (b * nb + n, p)),
        out_shape=jax.ShapeDtypeStruct((t, Q_COLS), BF16),
        compiler_params=_cparams(("parallel", "parallel", "arbitrary")),
        name="swa_attention",
    )(sinks, qkv, qkv, qkv, qkv, qkv, cos, sin, cos, sin, _rotate_half_matrix())


def _ssd_swa_mixer(x2, xb, batch, seq, w_in, b_qkv, conv_w, conv_b, dt_bias, a_log, d_skip, norm_w, sinks,
                   w_out, b_out):
    o1 = D_MODEL
    o2 = o1 + SSM_CONV_CH
    o3 = o2 + SSM_HEADS
    w_t = jnp.swapaxes(w_in, 0, 1)
    w_t_b = w_t.astype(BF16)
    u_z = _matmul(xb, w_t_b, n=o1, w_is_nk=True)
    u_xbc = _matmul(xb, w_t_b, col_start=o1, n=SSM_CONV_CH, w_is_nk=True)
    u_dt = _matmul(x2, w_t[o2:o3], bf16x3=True, w_is_nk=True)
    qkv = _matmul(xb, w_t_b[o3:], b_qkv, w_is_nk=True)
    y_ssm = _ssd_mixer(u_z, u_xbc, u_dt, conv_w, conv_b, dt_bias, a_log, d_skip, norm_w, batch, seq)
    y_att = _swa_attention(qkv, sinks, batch, seq)
    return _matmul((y_ssm, y_att), w_out.astype(BF16), b_out)


MOE_TILE = 256
MOE_GATHER_TILE = 256
ROUTER_TILE = 256


def _router_kernel(h_ref, wt_ref, b_ref, ids_ref, wts_ref):
    logits = _dot_bf16x3(wt_ref[...], h_ref[...], NT_DIMS) + b_ref[...]
    e = jnp.exp(logits - jnp.max(logits, 0, keepdims=True))
    probs = e / jnp.sum(e, 0, keepdims=True)
    p = [probs[i:i + 1, :] for i in range(N_EXPERTS)]

    keep, score = [], []
    for g in range(N_EXPERT_GROUPS):
        members = range(g * EXPERTS_PER_GROUP, (g + 1) * EXPERTS_PER_GROUP)
        s = jnp.zeros_like(p[0])
        for i in members:
            rank = jnp.zeros_like(p[0])
            for j in members:
                if j != i:
                    beats = (p[j] > p[i]) | ((p[j] == p[i]) & (j < i)) if j < i else (p[j] > p[i])
                    rank = rank + beats.astype(F32)
            keep.append(rank < TOP_K)
            s = s + jnp.where(keep[i], p[i], 0.0)
        score.append(s)
    chosen = []
    for g in range(N_EXPERT_GROUPS):
        c = jnp.ones_like(p[0]) > 0
        for g2 in range(N_EXPERT_GROUPS):
            if g2 < g:
                c = c & (score[g] > score[g2])
            elif g2 > g:
                c = c & (score[g] >= score[g2])
        chosen.append(c)
    denom = jnp.zeros_like(p[0])
    for g in range(N_EXPERT_GROUPS):
        denom = denom + jnp.where(chosen[g], score[g], 0.0)
    count = jnp.zeros_like(p[0])
    id0 = jnp.zeros_like(p[0])
    id1 = jnp.zeros_like(p[0])
    w0 = jnp.zeros_like(p[0])
    w1 = jnp.zeros_like(p[0])
    for i in range(N_EXPERTS):
        sel = keep[i] & chosen[i // EXPERTS_PER_GROUP]
        gate = p[i] / denom
        first = sel & (count == 0.0)
        second = sel & (count == 1.0)
        id0 = jnp.where(first, float(i), id0)
        w0 = jnp.where(first, gate, w0)
        id1 = jnp.where(second, float(i), id1)
        w1 = jnp.where(second, gate, w1)
        count = count + sel.astype(F32)
    ids_ref[0:1, :] = id0.astype(jnp.int32)
    ids_ref[1:2, :] = id1.astype(jnp.int32)
    wts_ref[0:1, :] = w0
    wts_ref[1:2, :] = w1


def _router(h, router_w, router_b):
    t, d = h.shape
    tm = _tile(t, ROUTER_TILE)
    return pl.pallas_call(
        _router_kernel,
        grid=(t // tm,),
        in_specs=[
            pl.BlockSpec((tm, d), lambda i: (i, 0)),
            pl.BlockSpec((N_EXPERTS, d), lambda i: (0, 0)),
            pl.BlockSpec((N_EXPERTS, 1), lambda i: (0, 0)),
        ],
        out_specs=[pl.BlockSpec((TOP_K, tm), lambda i: (0, i)), pl.BlockSpec((TOP_K, tm), lambda i: (0, i))],
        out_shape=[jax.ShapeDtypeStruct((TOP_K, t), jnp.int32), jax.ShapeDtypeStruct((TOP_K, t), F32)],
        compiler_params=_cparams(("parallel",)),
        name="moe_router",
    )(h, router_w.T, router_b.reshape(N_EXPERTS, 1))


def _route_metadata(ids, wts, tm):
    t = ids.shape[1]
    na = TOP_K * t
    eid = ids.T.reshape(na)
    onehot = (eid[:, None] == jnp.arange(N_EXPERTS, dtype=jnp.int32)[None, :]).astype(jnp.int32)
    csum = jnp.cumsum(onehot, axis=0)
    counts = csum[-1]
    rank = jnp.sum(csum * onehot, axis=1) - 1
    padded = (counts + tm - 1) // tm * tm
    pend = jnp.cumsum(padded)
    dest = (pend - padded)[eid] + rank
    n_tiles = na // tm + N_EXPERTS
    n_rows = n_tiles * tm
    row_assign = jnp.full((n_rows,), -1, jnp.int32).at[dest].set(jnp.arange(na, dtype=jnp.int32))
    assigned = jnp.maximum(row_assign, 0)
    row_token = assigned // TOP_K
    row_weight = jnp.where(row_assign >= 0, wts.T.reshape(na)[assigned], 0.0)
    tile_start = jnp.arange(n_tiles, dtype=jnp.int32) * tm
    tile_valid = (tile_start < pend[-1]).astype(jnp.int32)
    last_expert = jnp.max(jnp.where(counts > 0, jnp.arange(N_EXPERTS, dtype=jnp.int32), 0))
    tile_expert = jnp.minimum(jnp.searchsorted(pend, tile_start, side="right").astype(jnp.int32), last_expert)
    experts = jnp.arange(N_EXPERTS, dtype=jnp.int32)
    later = jnp.where((counts > 0)[None, :] & (experts[None, :] > experts[:, None]), experts[None, :], N_EXPERTS)
    next_nonempty = jnp.min(later, axis=1)
    next_nonempty = jnp.where(next_nonempty == N_EXPERTS, tile_expert[0], next_nonempty).astype(jnp.int32)
    prev_expert = jnp.concatenate([jnp.full((1,), -1, jnp.int32), tile_expert[:-1]])
    run_first = (tile_valid == 1) & (tile_expert != prev_expert)
    runs = jnp.stack([tile_expert, tile_valid, run_first.astype(jnp.int32), next_nonempty[tile_expert],
                      (tile_expert == last_expert).astype(jnp.int32)])
    return row_token, row_weight, dest, runs


def _row_copy(src_hbm, tiles_ref, sem, tile, sub, src_row):
    return pltpu.make_async_copy(src_hbm.at[src_row], tiles_ref.at[tile, :, sub, :], sem)


def _start_row_gather(src_hbm, tiles_ref, idx_ref, n, sem):
    def body(tile, carry):
        for sub in range(SUBLANES):
            _row_copy(src_hbm, tiles_ref, sem, tile, sub, idx_ref[0, 0, tile * SUBLANES + sub]).start(
                priority=sub % 2)
        return carry

    lax.fori_loop(0, n // SUBLANES, body, 0)


def _wait_row_gather(src_hbm, tiles_ref, n, sem):
    def body(tile, carry):
        for sub in range(SUBLANES):
            _row_copy(src_hbm, tiles_ref, sem, tile, sub, 0).wait()
        return carry

    lax.fori_loop(0, n // SUBLANES, body, 0)


def _gather_step(src_hbm, slots_ref, idx_ref, idx_next_ref, sems, n, *, more):
    i = pl.program_id(0)
    slot = i % 2

    @pl.when(i == 0)
    def _():
        _start_row_gather(src_hbm, slots_ref.at[0], idx_ref, n, sems.at[0])

    @pl.when(more)
    def _():
        _start_row_gather(src_hbm, slots_ref.at[1 - slot], idx_next_ref, n, sems.at[1 - slot])

    return slot


def _moe_gather_kernel(tv_ref, rt_ref, rt_next_ref, h_hbm, o_ref, xg_ref, sems, *, tm):
    i = pl.program_id(0)
    nxt = jnp.minimum(i + 1, pl.num_programs(0) - 1)
    valid = tv_ref[i] == 1
    slot = _gather_step(h_hbm, xg_ref, rt_ref, rt_next_ref, sems, tm, more=(nxt > i) & (tv_ref[nxt] == 1))

    @pl.when(valid)
    def _():
        _wait_row_gather(h_hbm, xg_ref.at[slot], tm, sems.at[slot])
        o_ref[...] = _tiles_to_rows(xg_ref.at[slot], 0, tm // SUBLANES).astype(o_ref.dtype)

    @pl.when(jnp.logical_not(valid))
    def _():
        o_ref[...] = jnp.zeros_like(o_ref)


RUN_EXPERT, RUN_VALID, RUN_FIRST, RUN_NEXT, RUN_LAST = range(5)


def _stream_expert_weights(runs_ref, w_hbm, stage_ref, wb_ref, sems, *, layer, tn):
    j, i = pl.program_id(0), pl.program_id(1)
    last_pass = j == pl.num_programs(0) - 1

    def copies(expert, col_block):
        col = pl.multiple_of(col_block * tn, tn)
        return [pltpu.make_async_copy(w.at[layer, expert, :, pl.ds(col, tn)], stage_ref.at[m], sems.at[m])
                for m, w in enumerate(w_hbm)]

    @pl.when((j == 0) & (i == 0))
    def _():
        for c in copies(runs_ref[RUN_EXPERT, 0], 0):
            c.start()

    @pl.when(runs_ref[RUN_FIRST, i] == 1)
    def _():
        for c in copies(runs_ref[RUN_EXPERT, i], j):
            c.wait()
        for m in range(len(w_hbm)):
            wb_ref[m] = stage_ref[m].astype(BF16)
        last_run = runs_ref[RUN_LAST, i] == 1

        @pl.when(jnp.logical_not(last_run & last_pass))
        def _():
            for c in copies(runs_ref[RUN_NEXT, i], jnp.where(last_run, j + 1, j)):
                c.start()


def _moe_up_kernel(runs_ref, x_ref, wg_hbm, wu_hbm, o_ref, stage_ref, wb_ref, sems, *, layer, tn):
    _stream_expert_weights(runs_ref, (wg_hbm, wu_hbm), stage_ref, wb_ref, sems, layer=layer, tn=tn)
    valid = runs_ref[RUN_VALID, pl.program_id(1)] == 1

    @pl.when(valid)
    def _():
        xt = x_ref[...]
        gate = jnp.dot(xt, wb_ref[0], preferred_element_type=F32)
        up = jnp.dot(xt, wb_ref[1], preferred_element_type=F32)
        o_ref[...] = (_silu(gate) * up).astype(o_ref.dtype)

    @pl.when(jnp.logical_not(valid))
    def _():
        o_ref[...] = jnp.zeros_like(o_ref)


def _moe_down_kernel(runs_ref, he_ref, wd_hbm, rw_ref, o_ref, stage_ref, wb_ref, sems, *, layer, tn):
    _stream_expert_weights(runs_ref, (wd_hbm,), stage_ref, wb_ref, sems, layer=layer, tn=tn)
    valid = runs_ref[RUN_VALID, pl.program_id(1)] == 1

    @pl.when(valid)
    def _():
        _to_token_major(o_ref, jnp.dot(he_ref[...], wb_ref[0], preferred_element_type=F32) * rw_ref[...])

    @pl.when(jnp.logical_not(valid))
    def _():
        o_ref[...] = jnp.zeros_like(o_ref)


def _moe_combine_ln_kernel(pos_ref, pos_next_ref, h_ref, y_hbm, g_ref, b_ref, o_ref, ob_ref, yg_ref, sems, *, tm):
    n = TOP_K * tm
    slot = _gather_step(y_hbm, yg_ref, pos_ref, pos_next_ref, sems, n,
                        more=pl.program_id(0) + 1 < pl.num_programs(0))
    _wait_row_gather(y_hbm, yg_ref.at[slot], n, sems.at[slot])
    per = tm // SUBLANES
    ffn = _tiles_to_rows(yg_ref.at[slot], 0, per) + _tiles_to_rows(yg_ref.at[slot], per, TOP_K * per)
    r = _layer_norm_rows(ALPHA * h_ref[...] + ffn, g_ref[...], b_ref[...])
    o_ref[...] = r
    ob_ref[...] = r.astype(BF16)


def _moe_block(h, h3, router_w, router_b, w_gate, w_up, w_down, layer, ln_g, ln_b, *, tm=MOE_TILE, tn_up=512,
               tn_down=2048, tm_out=128):
    t, d = h.shape
    slabs = d // LANES
    ids, wts = _router(h, router_w, router_b)
    row_token, row_weight, dest, runs = _route_metadata(ids, wts, tm)
    tile_valid = runs[RUN_VALID]
    n_rows = row_token.shape[0]
    n_tiles = n_rows // tm
    de = w_gate.shape[-1]
    tn_up, tn_down = _tile(de, tn_up), _tile(d, tn_down)

    def gather_scratch(n):
        return [pltpu.VMEM((2, n // SUBLANES, slabs, SUBLANES, LANES), F32), pltpu.SemaphoreType.DMA((2,))]

    tg = _tile(tm, MOE_GATHER_TILE)
    per = tm // tg
    n_gather = n_rows // tg
    row_token3 = row_token.reshape(n_gather, 1, tg)
    x_rows = pl.pallas_call(
        functools.partial(_moe_gather_kernel, tm=tg),
        grid_spec=pltpu.PrefetchScalarGridSpec(
            num_scalar_prefetch=1,
            grid=(n_gather,),
            in_specs=[
                pl.BlockSpec((1, 1, tg), lambda i, tv: (i, 0, 0), memory_space=pltpu.SMEM),
                pl.BlockSpec((1, 1, tg), lambda i, tv: (jnp.minimum(i + 1, n_gather - 1), 0, 0),
                             memory_space=pltpu.SMEM),
                pl.BlockSpec(memory_space=pl.ANY),
            ],
            out_specs=pl.BlockSpec((tg, d), lambda i, tv: (i, 0)),
            scratch_shapes=gather_scratch(tg),
        ),
        out_shape=jax.ShapeDtypeStruct((n_rows, d), BF16),
        compiler_params=_cparams(("arbitrary",)),
        name="moe_gather",
    )(jnp.repeat(tile_valid, per), row_token3, row_token3, h3)

    def weight_scratch(n_mats, k, tn):
        return [pltpu.VMEM((n_mats, k, tn), F32), pltpu.VMEM((n_mats, k, tn), BF16),
                pltpu.SemaphoreType.DMA((n_mats,))]

    he = pl.pallas_call(
        functools.partial(_moe_up_kernel, layer=layer, tn=tn_up),
        grid_spec=pltpu.PrefetchScalarGridSpec(
            num_scalar_prefetch=1,
            grid=(de // tn_up, n_tiles),
            in_specs=[
                pl.BlockSpec((tm, d), lambda j, i, runs: (i, 0)),
                pl.BlockSpec(memory_space=pl.ANY),
                pl.BlockSpec(memory_space=pl.ANY),
            ],
            out_specs=pl.BlockSpec((tm, tn_up), lambda j, i, runs: (i, j)),
            scratch_shapes=weight_scratch(2, d, tn_up),
        ),
        out_shape=jax.ShapeDtypeStruct((n_rows, de), BF16),
        compiler_params=_cparams(("arbitrary", "arbitrary")),
        name="moe_gate_up",
    )(runs, x_rows, w_gate, w_up)

    y_rows = pl.pallas_call(
        functools.partial(_moe_down_kernel, layer=layer, tn=tn_down),
        grid_spec=pltpu.PrefetchScalarGridSpec(
            num_scalar_prefetch=1,
            grid=(d // tn_down, n_tiles),
            in_specs=[
                pl.BlockSpec((tm, de), lambda j, i, runs: (i, 0)),
                pl.BlockSpec(memory_space=pl.ANY),
                pl.BlockSpec((tm, 1), lambda j, i, runs: (i, 0)),
            ],
            out_specs=pl.BlockSpec((tm, tn_down // LANES, LANES), lambda j, i, runs: (i, j, 0)),
            scratch_shapes=weight_scratch(1, de, tn_down),
        ),
        out_shape=jax.ShapeDtypeStruct((n_rows, slabs, LANES), F32),
        compiler_params=_cparams(("arbitrary", "arbitrary")),
        name="moe_down",
    )(runs, he, w_down, row_weight.reshape(n_rows, 1))

    tm_out = _tile(t, tm_out)
    n_out = t // tm_out
    pos = dest.reshape(n_out, tm_out, TOP_K).transpose(0, 2, 1).reshape(n_out, 1, TOP_K * tm_out)
    row = pl.BlockSpec((tm_out, d), lambda i: (i, 0))
    vec = pl.BlockSpec((1, d), lambda i: (0, 0))
    return pl.pallas_call(
        functools.partial(_moe_combine_ln_kernel, tm=tm_out),
        grid=(n_out,),
        in_specs=[
            pl.BlockSpec((1, 1, TOP_K * tm_out), lambda i: (i, 0, 0), memory_space=pltpu.SMEM),
            pl.BlockSpec((1, 1, TOP_K * tm_out), lambda i: (jnp.minimum(i + 1, n_out - 1), 0, 0),
                         memory_space=pltpu.SMEM),
            row,
            pl.BlockSpec(memory_space=pl.ANY),
            vec, vec,
        ],
        out_specs=[row, row],
        out_shape=[jax.ShapeDtypeStruct((t, d), F32), jax.ShapeDtypeStruct((t, d), BF16)],
        scratch_shapes=gather_scratch(TOP_K * tm_out),
        compiler_params=_cparams(("arbitrary",)),
        name="moe_combine_ln",
    )(pos, pos, h, y_rows, ln_g.reshape(1, d), ln_b.reshape(1, d))


def _shift_mix_kernel(x_ref, xp_ref, mix_ref, *rest, tm, seq):
    o_refs, pad_ref = rest[:-1], rest[-1]
    i = pl.program_id(0)
    starts_sequence = (i * tm) % seq == 0
    pad_ref[0:SUBLANES, :] = jnp.where(starts_sequence, 0.0, xp_ref[...])
    pad_ref[SUBLANES:SUBLANES + tm, :] = x_ref[...]
    xv = x_ref[...]
    xx = pad_ref[SUBLANES - 1:SUBLANES - 1 + tm, :] - xv
    for m, o_ref in enumerate(o_refs):
        o_ref[...] = (xv + xx * mix_ref[m:m + 1, :]).astype(o_ref.dtype)


def _shift_mix(x, mix, seq, *, tm=128):
    t, d = x.shape
    nm = mix.shape[0]
    tm = _tile(seq, tm)
    per = tm // SUBLANES
    return pl.pallas_call(
        functools.partial(_shift_mix_kernel, tm=tm, seq=seq),
        grid=(t // tm,),
        in_specs=[
            pl.BlockSpec((tm, d), lambda i: (i, 0)),
            pl.BlockSpec((SUBLANES, d), lambda i: (jnp.maximum(i * per - 1, 0), 0)),
            pl.BlockSpec((nm, d), lambda i: (0, 0)),
        ],
        out_specs=[pl.BlockSpec((tm, d), lambda i: (i, 0))] * nm,
        out_shape=[jax.ShapeDtypeStruct((t, d), BF16)] * nm,
        scratch_shapes=[pltpu.VMEM((SUBLANES + tm, d), F32)],
        compiler_params=_cparams(("parallel",)),
        name="rwkv_shift_mix",
    )(x, x, mix)


WKV_CHUNK = 64
WKV_HEADS = 4
WKV_WIDTH = WKV_HEADS * RWKV_HEAD
WKV_BLOCK = 512
NEUMANN_STEPS = 5


def _wkv_kernel(r_ref, k_ref, v_ref, wl_ref, ar_ref, g_ref, kk_ref, ka_ref, rk_ref, lnw_ref, lnb_ref,
                o_ref, state_ref, *, tb):
    lc, hw, wd = WKV_CHUNK, RWKV_HEAD, WKV_WIDTH
    nch = tb // lc

    @pl.when(pl.program_id(2) == 0)
    def _():
        state_ref[...] = jnp.zeros_like(state_ref)

    wrow = lax.broadcasted_iota(jnp.int32, (wd, wd), 0)
    wcol = lax.broadcasted_iota(jnp.int32, (wd, wd), 1)
    same_head = wrow // hw == wcol // hw
    ones_bd = same_head.astype(BF16)

    def block_diag(m):
        return jnp.where(same_head, jnp.concatenate([m] * WKV_HEADS, axis=0), 0.0).astype(BF16)

    def head_sum(m):
        hi = m.astype(BF16)
        lo = (m - hi.astype(F32)).astype(BF16)
        return jnp.dot(hi, ones_bd, preferred_element_type=F32) + jnp.dot(lo, ones_bd, preferred_element_type=F32)

    def mm(a, b):
        return jnp.dot(a.astype(BF16), b, preferred_element_type=F32)

    r = r_ref[...]
    k = k_ref[...]
    v = v_ref[...]
    w_log = -_softplus(-wl_ref[...]) - 0.5
    lw = -jnp.exp(w_log)
    a_sig = jax.nn.sigmoid(ar_ref[...])
    kx = k * kk_ref[...]
    kk = kx / jnp.maximum(jnp.sqrt(head_sum(kx * kx)), 1e-12)
    k2 = k * (1.0 + (a_sig - 1.0) * ka_ref[...])
    a_s = -kk
    b_s = kk * a_sig

    trow = lax.broadcasted_iota(jnp.int32, (tb, tb), 0)
    tcol = lax.broadcasted_iota(jnp.int32, (tb, tb), 1)
    same_chunk = trow // lc == tcol // lc
    cum = _dot_split3(lw, (same_chunk & (trow >= tcol)).astype(BF16), split_lhs=False)
    tot = jnp.concatenate([jnp.broadcast_to(cum[(c + 1) * lc - 1:(c + 1) * lc, :], (lc, wd)) for c in range(nch)],
                          axis=0)
    grow = jnp.exp(-cum)
    rt = r * jnp.exp(cum)
    at = a_s * jnp.exp(cum - lw)
    bt = b_s * grow
    kt = k2 * grow
    rest = jnp.exp(tot - cum)
    bh = b_s * rest
    kh = k2 * rest
    p_end = jnp.exp(tot)

    t_idx = lax.broadcasted_iota(jnp.int32, (lc, wd), 0)
    s_idx = lax.broadcasted_iota(jnp.int32, (lc, wd), 1) % hw
    strict = t_idx > s_idx
    incl = t_idx >= s_idx
    eye = (t_idx == s_idx).astype(F32)
    nt = (((1,), (1,)), ((), ()))

    tn = (((0,), (0,)), ((), ()))
    chunks = [slice(c * lc, (c + 1) * lc) for c in range(nch)]

    a_ab, a_rb, a_ak, a_rk = [], [], [], []
    for sl in chunks:
        ar = jnp.concatenate([at[sl], rt[sl]], axis=0).astype(BF16)
        xb = lax.dot_general(ar, block_diag(bt[sl]), nt, preferred_element_type=F32)
        xk = lax.dot_general(ar, block_diag(kt[sl]), nt, preferred_element_type=F32)
        a_ab.append(jnp.where(strict, xb[:lc], 0.0))
        a_rb.append(jnp.where(incl, xb[lc:], 0.0))
        a_ak.append(jnp.where(strict, xk[:lc], 0.0))
        a_rk.append(jnp.where(incl, xk[lc:], 0.0))
    inv = [eye + a for a in a_ab]
    pw = [mm(a, block_diag(a)) for a in a_ab]
    for step in range(1, NEUMANN_STEPS + 1):
        last = step == NEUMANN_STEPS
        lhs = inv if last else [jnp.concatenate([t, p], axis=0) for t, p in zip(inv, pw)]
        prod = [mm(x, block_diag(p)) for x, p in zip(lhs, pw)]
        inv = [t + q[:lc] for t, q in zip(inv, prod)]
        if not last:
            pw = [q[lc:] for q in prod]
    v_bd = [block_diag(v[sl]) for sl in chunks]
    ta = [mm(t, block_diag(at[sl])) for t, sl in zip(inv, chunks)]
    av = [mm(jnp.concatenate([ak, rk], axis=0), vb) for ak, rk, vb in zip(a_ak, a_rk, v_bd)]
    y0 = [x[lc:] for x in av]
    u0 = [mm(t, block_diag(x[:lc])) for t, x in zip(inv, av)]
    gain, add = [], []
    for c, sl in enumerate(chunks):
        bh_b = bh[sl].astype(BF16)
        gain.append(jnp.where(same_head, lax.dot_general(ta[c].astype(BF16), bh_b, tn, preferred_element_type=F32),
                              0.0).astype(BF16))
        uv = jnp.concatenate([u0[c], v[sl]], axis=0).astype(BF16)
        bk = jnp.concatenate([bh_b, kh[sl].astype(BF16)], axis=0)
        add.append(jnp.where(same_head, lax.dot_general(uv, bk, tn, preferred_element_type=F32), 0.0))

    states = [state_ref[...]]
    for c in range(nch):
        s0 = states[-1]
        states.append(s0 * p_end[c * lc:c * lc + 1, :] + mm(s0, gain[c]) + add[c])
    state_ref[...] = states[nch]

    ys = []
    for c, sl in enumerate(chunks):
        tr = jnp.concatenate([ta[c], rt[sl]], axis=0).astype(BF16)
        xs = lax.dot_general(tr, states[c].astype(BF16), nt, preferred_element_type=F32)
        u = xs[:lc] + u0[c]
        ys.append(xs[lc:] + mm(a_rb[c], block_diag(u)) + y0[c])

    y = jnp.concatenate(ys, axis=0)
    mu = head_sum(y) / hw
    dev = y - mu
    var = head_sum(dev * dev) / hw
    yn = dev * lax.rsqrt(var + RWKV_GN_EPS) * lnw_ref[...] + lnb_ref[...]
    bonus = head_sum(r * k2 * rk_ref[...]) * v
    o_ref[...] = ((yn + bonus) * g_ref[...]).astype(o_ref.dtype)


def _wkv(r, k, v, wl, araw, g, k_k, k_a, r_k, ln_w, ln_b, batch, seq):
    t, d = r.shape
    tb = _tile(seq, WKV_BLOCK)
    nt = seq // tb
    wd = WKV_WIDTH
    rows = pl.BlockSpec((tb, wd), lambda b, h, n: (b * nt + n, h))
    vec = pl.BlockSpec((1, wd), lambda b, h, n: (0, h))
    return pl.pallas_call(
        functools.partial(_wkv_kernel, tb=tb),
        grid=(batch, d // wd, nt),
        in_specs=[rows] * 6 + [vec] * 5,
        out_specs=rows,
        out_shape=jax.ShapeDtypeStruct((t, d), BF16),
        scratch_shapes=[pltpu.VMEM((wd, wd), F32)],
        compiler_params=_cparams(("parallel", "parallel", "arbitrary")),
        name="rwkv_wkv",
    )(r, k, v, wl, araw, g, k_k.reshape(1, d), k_a.reshape(1, d), r_k.reshape(1, d),
      ln_w.reshape(1, d), ln_b.reshape(1, d))


def _rwkv7_time_mix(x2, batch, seq, mix, w_r, w_k, w_v, w_o, w0, w1, w2, a0, a1, a2, g1, g2, k_k, k_a, r_k,
                    ln_w, ln_b):
    xm = _shift_mix(x2, mix, seq)
    bf = lambda w: w.astype(BF16)
    r = _matmul(xm[0], bf(w_r))
    k = _matmul(xm[2], bf(w_k))
    v = _matmul(xm[3], bf(w_v))
    wl = _low_rank(xm[1], bf(w1), bf(w2), w0, act="tanh")
    araw = _low_rank(xm[4], bf(a1), bf(a2), a0)
    g = _low_rank(xm[5], bf(g1), bf(g2), act="sigmoid")
    yg = _wkv(r, k, v, wl, araw, g, k_k, k_a, r_k, ln_w, ln_b, batch, seq)
    return _matmul(yg, bf(w_o))


def kernel(x, ab_w_in, ab_b_qkv, ssm_conv_w, ssm_conv_b, ssm_dt_bias, ssm_a_log, ssm_d, ssm_norm_w, attn_sinks,
           ab_w_out, ab_b_out, rwkv_mix, rwkv_w_r, rwkv_w_k, rwkv_w_v, rwkv_w_o, rwkv_w0, rwkv_w1, rwkv_w2,
           rwkv_a0, rwkv_a1, rwkv_a2, rwkv_g1, rwkv_g2, rwkv_k_k, rwkv_k_a, rwkv_r_k, rwkv_ln_w, rwkv_ln_b,
           ln_mix_g, ln_mix_b, ln_ffn_g, ln_ffn_b, router_w, router_b, moe_w_gate, moe_w_up, moe_w_down):
    batch, seq, d = x.shape
    x2 = x.reshape(batch * seq, d)
    xb = x2.astype(BF16)
    for layer in range(DEPTH):
        i = layer // 2
        if layer % 2 == 0:
            mix = _ssd_swa_mixer(x2, xb, batch, seq, ab_w_in[i], ab_b_qkv[i], ssm_conv_w[i], ssm_conv_b[i],
                                 ssm_dt_bias[i], ssm_a_log[i], ssm_d[i], ssm_norm_w[i], attn_sinks[i],
                                 ab_w_out[i], ab_b_out[i])
        else:
            mix = _rwkv7_time_mix(x2, batch, seq, rwkv_mix[i], rwkv_w_r[i], rwkv_w_k[i], rwkv_w_v[i], rwkv_w_o[i],
                                  rwkv_w0[i], rwkv_w1[i], rwkv_w2[i], rwkv_a0[i], rwkv_a1[i], rwkv_a2[i],
                                  rwkv_g1[i], rwkv_g2[i], rwkv_k_k[i], rwkv_k_a[i], rwkv_r_k[i].reshape(-1),
                                  rwkv_ln_w[i], rwkv_ln_b[i])
        h, h3 = _add_layer_norm(x2, mix, ln_mix_g[layer], ln_mix_b[layer])
        x2, xb = _moe_block(h, h3, router_w, router_b, moe_w_gate, moe_w_up, moe_w_down, layer,
                            ln_ffn_g[layer], ln_ffn_b[layer])
    return x2.reshape(batch, seq, d)
```

```python
import functools

import jax
import jax.numpy as jnp
import numpy as np
from jax import lax
from jax.experimental import pallas as pl
from jax.experimental.pallas import tpu as pltpu

F32 = jnp.float32
BF16 = jnp.bfloat16

D_MODEL = 4096
DEPTH = 2
SSM_HEAD_DIM = 64
SSM_HEADS = 64
SSM_GROUPS = 8
SSM_HEADS_PER_GROUP = 8
SSM_STATE = 128
SSM_CONV = 4
SSM_CHUNK = 128
SSM_GROUP_WIDTH = SSM_HEADS_PER_GROUP * SSM_HEAD_DIM
SSM_CONV_CH = D_MODEL + 2 * SSM_GROUPS * SSM_STATE
SSM_NORM_EPS = 1e-5
ATT_KV_HEADS = 8
ATT_HEAD_DIM = 64
ATT_REP = 8
ATT_WINDOW = 128
ROPE_DIM = 16
ROPE_THETA = 500000.0
Q_COLS = 4096
KV_COLS = 512
RWKV_HEAD = 64
RWKV_GN_EPS = 64e-5
N_EXPERTS = 16
N_EXPERT_GROUPS = 4
EXPERTS_PER_GROUP = 4
TOP_K = 2
ALPHA = (2 * DEPTH) ** 0.25
LN_EPS = 1e-5

VMEM_LIMIT_BYTES = 56 * 1024 * 1024
LANES = 128
SUBLANES = 8


def _cparams(semantics):
    return pltpu.CompilerParams(dimension_semantics=semantics, vmem_limit_bytes=VMEM_LIMIT_BYTES)


def _tile(dim, pref):
    if dim <= pref:
        return dim
    t = pref
    while dim % t:
        t //= 2
    return t


def _silu(v):
    return v * jax.nn.sigmoid(v)


def _softplus(v):
    return jnp.maximum(v, 0.0) + jnp.log1p(jnp.exp(-jnp.abs(v)))


def _split3(v):
    hi = v.astype(BF16)
    rest = v - hi.astype(F32)
    mid = rest.astype(BF16)
    lo = (rest - mid.astype(F32)).astype(BF16)
    return hi, mid, lo


NT_DIMS = (((1,), (1,)), ((), ()))
NN_DIMS = (((1,), (0,)), ((), ()))


def _dot_bf16x3(a, b, dims):
    a_hi, b_hi = a.astype(BF16), b.astype(BF16)
    a_lo = (a - a_hi.astype(F32)).astype(BF16)
    b_lo = (b - b_hi.astype(F32)).astype(BF16)

    def dg(u, v):
        return lax.dot_general(u, v, dims, preferred_element_type=F32)

    return dg(a_hi, b_hi) + (dg(a_hi, b_lo) + dg(a_lo, b_hi))


def _dot_split3(v, m01, *, split_lhs):
    if split_lhs:
        return sum(jnp.dot(t, m01, preferred_element_type=F32) for t in _split3(v))
    return sum(jnp.dot(m01, t, preferred_element_type=F32) for t in _split3(v))


def _mm_kernel(*refs, act, bf16x3, w_is_nk):
    a_refs, (w_ref, b_ref, o_ref) = refs[:-3], refs[-3:]
    dims = NT_DIMS if w_is_nk else NN_DIMS
    r = b_ref[...]
    k0 = 0
    for a_ref in a_refs:
        k1 = k0 + a_ref.shape[1]
        w_part = w_ref[:, k0:k1] if w_is_nk else w_ref[k0:k1, :]
        if bf16x3:
            r = r + _dot_bf16x3(a_ref[...], w_part, dims)
        else:
            r = r + lax.dot_general(a_ref[...], w_part, dims, preferred_element_type=F32)
        k0 = k1
    if act == "tanh":
        r = jnp.tanh(r)
    elif act == "sigmoid":
        r = jax.nn.sigmoid(r)
    o_ref[...] = r.astype(o_ref.dtype)


MATMUL_VMEM_BUDGET = 44 * 1024 * 1024


def _matmul(a, w, bias=None, *, act=None, out_dtype=F32, bf16x3=False, tm=1024, col_start=0, n=None,
            w_is_nk=False):
    panels = a if isinstance(a, (tuple, list)) else (a,)
    m = panels[0].shape[0]
    kdim = sum(p.shape[1] for p in panels)
    w_n, w_k = (w.shape if w_is_nk else w.shape[::-1])
    n = w_n if n is None else n
    tm = _tile(m, tm)
    out_bytes = jnp.dtype(out_dtype).itemsize
    for tn in (512, 256, LANES):
        tn = _tile(n, tn)
        need = 2 * (tm * kdim * panels[0].dtype.itemsize + kdim * tn * w.dtype.itemsize + tm * tn * out_bytes)
        if need <= MATMUL_VMEM_BUDGET:
            break
    assert need <= MATMUL_VMEM_BUDGET and col_start % tn == 0 and w_k == kdim
    col0 = col_start // tn
    if bias is None:
        bias = jnp.zeros((n,), F32)
    bias = bias.reshape(1, n).astype(F32)
    if w_is_nk:
        w_spec = pl.BlockSpec((tn, kdim), lambda i, j: (col0 + j, 0))
    else:
        w_spec = pl.BlockSpec((kdim, tn), lambda i, j: (0, col0 + j))
    return pl.pallas_call(
        functools.partial(_mm_kernel, act=act, bf16x3=bf16x3, w_is_nk=w_is_nk),
        grid=(m // tm, n // tn),
        in_specs=[pl.BlockSpec((tm, p.shape[1]), lambda i, j: (i, 0)) for p in panels] + [
            w_spec,
            pl.BlockSpec((1, tn), lambda i, j: (0, j)),
        ],
        out_specs=pl.BlockSpec((tm, tn), lambda i, j: (i, j)),
        out_shape=jax.ShapeDtypeStruct((m, n), out_dtype),
        compiler_params=_cparams(("parallel", "arbitrary")),
        name="matmul",
    )(*panels, w, bias)


def _low_rank_kernel(a_ref, w1_ref, w2_ref, b_ref, o_ref, *, act):
    t = jnp.dot(a_ref[...], w1_ref[...], preferred_element_type=F32)
    if act == "tanh":
        t = jnp.tanh(t)
    elif act == "sigmoid":
        t = jax.nn.sigmoid(t)
    o_ref[...] = jnp.dot(t.astype(BF16), w2_ref[...], preferred_element_type=F32) + b_ref[...]


def _low_rank(a, w1, w2, bias=None, *, act=None, tm=512):
    m, kdim = a.shape
    r, n = w2.shape
    tm = _tile(m, tm)
    if bias is None:
        bias = jnp.zeros((n,), F32)
    return pl.pallas_call(
        functools.partial(_low_rank_kernel, act=act),
        grid=(m // tm,),
        in_specs=[
            pl.BlockSpec((tm, kdim), lambda i: (i, 0)),
            pl.BlockSpec((kdim, r), lambda i: (0, 0)),
            pl.BlockSpec((r, n), lambda i: (0, 0)),
            pl.BlockSpec((1, n), lambda i: (0, 0)),
        ],
        out_specs=pl.BlockSpec((tm, n), lambda i: (i, 0)),
        out_shape=jax.ShapeDtypeStruct((m, n), F32),
        compiler_params=_cparams(("parallel",)),
        name="low_rank",
    )(a, w1, w2, bias.reshape(1, n).astype(F32))


def _layer_norm_rows(v, g, b):
    mu = jnp.mean(v, -1, keepdims=True)
    var = jnp.mean(jnp.square(v - mu), -1, keepdims=True)
    return (v - mu) * lax.rsqrt(var + LN_EPS) * g + b


def _to_token_major(o3_ref, v):
    for s in range(v.shape[1] // LANES):
        o3_ref[:, s, :] = v[:, s * LANES:(s + 1) * LANES].astype(o3_ref.dtype)


def _tiles_to_rows(tiles_ref, t0, t1):
    return jnp.concatenate([tiles_ref[t0:t1, s].reshape((t1 - t0) * SUBLANES, LANES)
                            for s in range(tiles_ref.shape[1])], axis=1)


def _add_ln_kernel(x_ref, y_ref, g_ref, b_ref, o_ref, o3_ref):
    r = _layer_norm_rows(ALPHA * x_ref[...] + y_ref[...], g_ref[...], b_ref[...])
    o_ref[...] = r
    _to_token_major(o3_ref, r)


def _add_layer_norm(x, y, g, b, *, tm=256):
    t, d = x.shape
    tm = _tile(t, tm)
    row = pl.BlockSpec((tm, d), lambda i: (i, 0))
    row3 = pl.BlockSpec((tm, d // LANES, LANES), lambda i: (i, 0, 0))
    vec = pl.BlockSpec((1, d), lambda i: (0, 0))
    return pl.pallas_call(
        _add_ln_kernel,
        grid=(t // tm,),
        in_specs=[row, row, vec, vec],
        out_specs=[row, row3],
        out_shape=[jax.ShapeDtypeStruct((t, d), F32), jax.ShapeDtypeStruct((t, d // LANES, LANES), F32)],
        compiler_params=_cparams(("parallel",)),
        name="add_layer_norm",
    )(x, y, g.reshape(1, d), b.reshape(1, d))


CONV_HALO = SUBLANES
SSD_CH = SSM_GROUP_WIDTH + 2 * SSM_STATE


def _ssd_kernel(z_ref, x_ref, b_ref, c_ref, wx_ref, wb_ref, wc_ref, bx_ref, bb_ref, bc_ref,
                dtr_ref, dbc_ref, dbr_ref, alc_ref, alr_ref, d_ref, nw_ref,
                o_ref, state_ref, pad_ref):
    ch, gw, ns, hd = SSM_CHUNK, SSM_GROUP_WIDTH, SSM_STATE, SSM_HEAD_DIM

    @pl.when(pl.program_id(2) == 0)
    def _():
        state_ref[...] = jnp.zeros_like(state_ref)
        pad_ref[0:CONV_HALO, :] = jnp.zeros((CONV_HALO, SSD_CH), F32)

    pad_ref[CONV_HALO:CONV_HALO + ch, 0:gw] = x_ref[...]
    pad_ref[CONV_HALO:CONV_HALO + ch, gw:gw + ns] = b_ref[...]
    pad_ref[CONV_HALO:CONV_HALO + ch, gw + ns:SSD_CH] = c_ref[...]
    w = jnp.concatenate([wx_ref[...], wb_ref[...], wc_ref[...]], axis=1)
    acc = jnp.concatenate([bx_ref[...], bb_ref[...], bc_ref[...]], axis=1)
    base = CONV_HALO - (SSM_CONV - 1)
    for k in range(SSM_CONV):
        acc = acc + w[k:k + 1, :] * pad_ref[base + k:base + k + ch, :]
    pad_ref[0:CONV_HALO, :] = pad_ref[ch:ch + CONV_HALO, :]
    xbc = _silu(acc)
    xc, bm, cm = xbc[:, :gw], xbc[:, gw:gw + ns], xbc[:, gw + ns:]

    row = lax.broadcasted_iota(jnp.int32, (ch, ch), 0)
    col = lax.broadcasted_iota(jnp.int32, (ch, ch), 1)
    causal = row >= col
    dt_raw_r = dtr_ref[...]
    eye = (row == col).astype(BF16)
    dt_raw_c = sum(lax.dot_general(eye, term, (((1,), (1,)), ((), ())), preferred_element_type=F32)
                   for term in _split3(dt_raw_r))
    dt_c = _softplus(dt_raw_c + dbc_ref[0])
    dt_r = _softplus(dt_raw_r + dbr_ref[...])
    a_c = -jnp.exp(alc_ref[0])
    a_r = -jnp.exp(alr_ref[...])
    acs_c = _dot_split3(dt_c * a_c, causal.astype(BF16), split_lhs=False)
    acs_r = _dot_split3(dt_r * a_r, (row <= col).astype(BF16), split_lhs=True)

    hrow = lax.broadcasted_iota(jnp.int32, (SSM_HEADS_PER_GROUP, gw), 0)
    hcol = lax.broadcasted_iota(jnp.int32, (SSM_HEADS_PER_GROUP, gw), 1)
    expand = (hcol // hd == hrow).astype(BF16)
    dt_x = _dot_split3(dt_c, expand, split_lhs=True)
    acs_x = _dot_split3(acs_c, expand, split_lhs=True)

    xdt = xc * dt_x
    xdt_b = xdt.astype(BF16)
    bm_b, cm_b = bm.astype(BF16), cm.astype(BF16)
    cb = lax.dot_general(cm_b, bm_b, (((1,), (1,)), ((), ())), preferred_element_type=F32)
    prev = state_ref[...]
    y_off = jnp.dot(cm_b, prev.astype(BF16), preferred_element_type=F32) * jnp.exp(acs_x)

    lane = lax.broadcasted_iota(jnp.int32, (ch, 2 * hd), 1)
    pairs = []
    for pr in range(SSM_HEADS_PER_GROUP // 2):
        xp = xdt_b[:, pr * 2 * hd:(pr + 1) * 2 * hd]
        halves = []
        for q in range(2):
            r = 2 * pr + q
            diff = acs_c[:, r:r + 1] - acs_r[r:r + 1, :]
            seg = jnp.where(causal, jnp.exp(jnp.where(causal, diff, 0.0)), 0.0)
            halves.append(jnp.dot((cb * seg).astype(BF16), xp, preferred_element_type=F32))
        pairs.append(jnp.where(lane < hd, halves[0], halves[1]))
    y = jnp.concatenate(pairs, axis=1) + y_off + d_ref[...] * xc

    acs_last = acs_x[ch - 1:ch, :]
    contrib = lax.dot_general(bm_b, (xdt * jnp.exp(acs_last - acs_x)).astype(BF16),
                              (((0,), (0,)), ((), ())), preferred_element_type=F32)
    state_ref[...] = prev * jnp.exp(acs_last) + contrib

    u = y * _silu(z_ref[...])
    u = u * lax.rsqrt(jnp.mean(u * u, -1, keepdims=True) + SSM_NORM_EPS)
    o_ref[...] = (u * nw_ref[...]).astype(o_ref.dtype)


def _ssd_mixer(u_z, u_xbc, u_dt, conv_w, conv_b, dt_bias, a_log, d_skip, norm_w, batch, seq):
    t = batch * seq
    nc = seq // SSM_CHUNK
    g, r, gw, ns = SSM_GROUPS, SSM_HEADS_PER_GROUP, SSM_GROUP_WIDTH, SSM_STATE
    xblocks = D_MODEL // ns
    conv_wt = conv_w.T
    conv_b2 = conv_b.reshape(1, SSM_CONV_CH)
    dt_row = u_dt.T
    rows = lambda b, gi, c: (b * nc + c, gi)
    in_specs = [
        pl.BlockSpec((SSM_CHUNK, gw), rows),
        pl.BlockSpec((SSM_CHUNK, gw), rows),
        pl.BlockSpec((SSM_CHUNK, ns), lambda b, gi, c: (b * nc + c, xblocks + gi)),
        pl.BlockSpec((SSM_CHUNK, ns), lambda b, gi, c: (b * nc + c, xblocks + g + gi)),
        pl.BlockSpec((SSM_CONV, gw), lambda b, gi, c: (0, gi)),
        pl.BlockSpec((SSM_CONV, ns), lambda b, gi, c: (0, xblocks + gi)),
        pl.BlockSpec((SSM_CONV, ns), lambda b, gi, c: (0, xblocks + g + gi)),
        pl.BlockSpec((1, gw), lambda b, gi, c: (0, gi)),
        pl.BlockSpec((1, ns), lambda b, gi, c: (0, xblocks + gi)),
        pl.BlockSpec((1, ns), lambda b, gi, c: (0, xblocks + g + gi)),
        pl.BlockSpec((r, SSM_CHUNK), lambda b, gi, c: (gi, b * nc + c)),
        pl.BlockSpec((1, 1, r), lambda b, gi, c: (gi, 0, 0)),
        pl.BlockSpec((r, 1), lambda b, gi, c: (gi, 0)),
        pl.BlockSpec((1, 1, r), lambda b, gi, c: (gi, 0, 0)),
        pl.BlockSpec((r, 1), lambda b, gi, c: (gi, 0)),
        pl.BlockSpec((1, gw), lambda b, gi, c: (0, gi)),
        pl.BlockSpec((1, gw), lambda b, gi, c: (0, gi)),
    ]
    return pl.pallas_call(
        _ssd_kernel,
        grid=(batch, g, nc),
        in_specs=in_specs,
        out_specs=pl.BlockSpec((SSM_CHUNK, gw), rows),
        out_shape=jax.ShapeDtypeStruct((t, D_MODEL), BF16),
        scratch_shapes=[pltpu.VMEM((ns, gw), F32), pltpu.VMEM((CONV_HALO + SSM_CHUNK, SSD_CH), F32)],
        compiler_params=_cparams(("parallel", "parallel", "arbitrary")),
        name="ssd_mixer",
    )(u_z, u_xbc, u_xbc, u_xbc, conv_wt, conv_wt, conv_wt, conv_b2, conv_b2, conv_b2,
      dt_row, dt_bias.reshape(g, 1, r), dt_bias.reshape(SSM_HEADS, 1),
      a_log.reshape(g, 1, r), a_log.reshape(SSM_HEADS, 1),
      jnp.repeat(d_skip, SSM_HEAD_DIM).reshape(1, D_MODEL), norm_w.reshape(1, D_MODEL))


KV_PER_STEP = 8

def _rotary_tables(seq):
    half = ROPE_DIM // 2
    inv_freq = ROPE_THETA ** (-jnp.arange(half, dtype=F32) / half)
    ang = jnp.arange(seq, dtype=F32)[:, None] * inv_freq[None, :]
    ones = jnp.ones((seq, ATT_HEAD_DIM - ROPE_DIM), F32)
    cos = jnp.concatenate([jnp.cos(ang), jnp.cos(ang), ones], -1)
    sin = jnp.concatenate([jnp.sin(ang), jnp.sin(ang), 0.0 * ones], -1)
    return jnp.tile(cos, (1, ATT_REP)), jnp.tile(sin, (1, ATT_REP))


def _rotate_half_matrix():
    width = ATT_REP * ATT_HEAD_DIM
    half = ROPE_DIM // 2
    p = np.zeros((width, width), np.float32)
    for j in range(width):
        if j % ATT_HEAD_DIM < half:
            p[j + half, j] = -1.0
        elif j % ATT_HEAD_DIM < ROPE_DIM:
            p[j - half, j] = 1.0
    return jnp.asarray(p, BF16)


def _rotate(v, cos, sin, perm):
    hi = v.astype(BF16)
    lo = (v - hi.astype(F32)).astype(BF16)
    partner = jnp.dot(hi, perm, preferred_element_type=F32) + jnp.dot(lo, perm, preferred_element_type=F32)
    return v * cos + partner * sin


def _swa_kernel(sink_ref, q_ref, kc_ref, kp_ref, vc_ref, vp_ref, cosc_ref, sinc_ref, cosp_ref, sinp_ref,
                perm_ref, o_ref):
    w, hd = ATT_WINDOW, ATT_HEAD_DIM
    qw, kw = ATT_REP * hd, KV_PER_STEP * hd
    pair = pl.program_id(1)
    n = pl.program_id(2)
    perm = perm_ref[...]
    perm_k = perm[:kw, :kw]
    cos_c, sin_c = cosc_ref[...], sinc_ref[...]
    q = jnp.concatenate([_rotate(q_ref[:, p * qw:(p + 1) * qw], cos_c, sin_c, perm) for p in range(KV_PER_STEP)],
                        axis=1)
    k_cur = _rotate(kc_ref[...], cos_c[:, :kw], sin_c[:, :kw], perm_k)
    k_prev = _rotate(kp_ref[...], cosp_ref[:, :kw], sinp_ref[:, :kw], perm_k)
    k_pair = jnp.concatenate([k_prev, k_cur], axis=0).astype(BF16)
    v_pair = jnp.concatenate([vp_ref[...], vc_ref[...]], axis=0).astype(BF16)
    k_all = [k_pair[:, p * hd:(p + 1) * hd] for p in range(KV_PER_STEP)]
    v_all = [v_pair[:, p * hd:(p + 1) * hd] for p in range(KV_PER_STEP)]
    qi = lax.broadcasted_iota(jnp.int32, (w, 2 * w), 0) + w
    ki = lax.broadcasted_iota(jnp.int32, (w, 2 * w), 1)
    rel = qi - ki
    first = jnp.where(n > 0, 0, w)
    bias = jnp.where((rel >= 0) & (rel < ATT_WINDOW) & (ki >= first), 0.0, -jnp.inf)
    heads = range(KV_PER_STEP * ATT_REP)
    qs = [(q[:, r * hd:(r + 1) * hd] * (hd ** -0.5)).astype(BF16) for r in heads]
    sinks = [sink_ref[pair * KV_PER_STEP * ATT_REP + r] for r in heads]
    ss = [lax.dot_general(qr, k_all[r // ATT_REP], NT_DIMS, preferred_element_type=F32) + bias
          for r, qr in zip(heads, qs)]
    ms = [jnp.maximum(jnp.max(s, -1, keepdims=True), sink) for s, sink in zip(ss, sinks)]
    es = [jnp.exp(s - m) for s, m in zip(ss, ms)]
    dens = [jnp.sum(e, -1, keepdims=True) + jnp.exp(sink - m) for e, sink, m in zip(es, sinks, ms)]
    outs = [jnp.dot(e.astype(BF16), v_all[r // ATT_REP], preferred_element_type=F32) / den
            for r, e, den in zip(heads, es, dens)]
    o_ref[...] = jnp.concatenate(outs, axis=1).astype(o_ref.dtype)


def _swa_attention(qkv, sinks, batch, seq):
    t = batch * seq
    nb = seq // ATT_WINDOW
    hd, w = ATT_HEAD_DIM, ATT_WINDOW
    qw, kw = ATT_REP * hd, KV_PER_STEP * hd
    k_block0 = Q_COLS // kw
    v_block0 = (Q_COLS + KV_COLS) // kw
    cos, sin = _rotary_tables(seq)

    def kv_spec(block0, back):
        return pl.BlockSpec((w, kw), lambda b, p, n: (b * nb + jnp.maximum(n - back, 0), block0 + p))

    tab_cur = pl.BlockSpec((w, qw), lambda b, p, n: (n, 0))
    tab_prev = pl.BlockSpec((w, qw), lambda b, p, n: (jnp.maximum(n - 1, 0), 0))
    return pl.pallas_call(
        _swa_kernel,
        grid=(batch, ATT_KV_HEADS // KV_PER_STEP, nb),
        in_specs=[
            pl.BlockSpec(memory_space=pltpu.SMEM),
            pl.BlockSpec((w, KV_PER_STEP * qw), lambda b, p, n: (b * nb + n, p)),
            kv_spec(k_block0, 0), kv_spec(k_block0, 1), kv_spec(v_block0, 0), kv_spec(v_block0, 1),
            tab_cur, tab_cur, tab_prev, tab_prev,
            pl.BlockSpec((qw, qw), lambda b, p, n: (0, 0)),
        ],
        out_specs=pl.BlockSpec((w, KV_PER_STEP * qw), lambda b, p, n: (b * nb + n, p)),
        out_shape=jax.ShapeDtypeStruct((t, Q_COLS), BF16),
        compiler_params=_cparams(("parallel", "parallel", "arbitrary")),
        name="swa_attention",
    )(sinks, qkv, qkv, qkv, qkv, qkv, cos, sin, cos, sin, _rotate_half_matrix())


def _ssd_swa_mixer(x2, xb, batch, seq, w_in, b_qkv, conv_w, conv_b, dt_bias, a_log, d_skip, norm_w, sinks,
                   w_out, b_out):
    o1 = D_MODEL
    o2 = o1 + SSM_CONV_CH
    o3 = o2 + SSM_HEADS
    w_t = jnp.swapaxes(w_in, 0, 1)
    w_t_b = w_t.astype(BF16)
    u_z = _matmul(xb, w_t_b, n=o1, w_is_nk=True)
    u_xbc = _matmul(xb, w_t_b, col_start=o1, n=SSM_CONV_CH, w_is_nk=True)
    u_dt = _matmul(x2, w_t[o2:o3], bf16x3=True, w_is_nk=True)
    qkv = _matmul(xb, w_t_b[o3:], b_qkv, w_is_nk=True)
    y_ssm = _ssd_mixer(u_z, u_xbc, u_dt, conv_w, conv_b, dt_bias, a_log, d_skip, norm_w, batch, seq)
    y_att = _swa_attention(qkv, sinks, batch, seq)
    return _matmul((y_ssm, y_att), w_out.astype(BF16), b_out)


MOE_TILE = 256
MOE_GATHER_TILE = 256
ROUTER_TILE = 256


def _router_kernel(h_ref, wt_ref, b_ref, ids_ref, wts_ref):
    logits = _dot_bf16x3(wt_ref[...], h_ref[...], NT_DIMS) + b_ref[...]
    e = jnp.exp(logits - jnp.max(logits, 0, keepdims=True))
    probs = e / jnp.sum(e, 0, keepdims=True)
    p = [probs[i:i + 1, :] for i in range(N_EXPERTS)]

    keep, score = [], []
    for g in range(N_EXPERT_GROUPS):
        members = range(g * EXPERTS_PER_GROUP, (g + 1) * EXPERTS_PER_GROUP)
        s = jnp.zeros_like(p[0])
        for i in members:
            rank = jnp.zeros_like(p[0])
            for j in members:
                if j != i:
                    beats = (p[j] > p[i]) | ((p[j] == p[i]) & (j < i)) if j < i else (p[j] > p[i])
                    rank = rank + beats.astype(F32)
            keep.append(rank < TOP_K)
            s = s + jnp.where(keep[i], p[i], 0.0)
        score.append(s)
    chosen = []
    for g in range(N_EXPERT_GROUPS):
        c = jnp.ones_like(p[0]) > 0
        for g2 in range(N_EXPERT_GROUPS):
            if g2 < g:
                c = c & (score[g] > score[g2])
            elif g2 > g:
                c = c & (score[g] >= score[g2])
        chosen.append(c)
    denom = jnp.zeros_like(p[0])
    for g in range(N_EXPERT_GROUPS):
        denom = denom + jnp.where(chosen[g], score[g], 0.0)
    count = jnp.zeros_like(p[0])
    id0 = jnp.zeros_like(p[0])
    id1 = jnp.zeros_like(p[0])
    w0 = jnp.zeros_like(p[0])
    w1 = jnp.zeros_like(p[0])
    for i in range(N_EXPERTS):
        sel = keep[i] & chosen[i // EXPERTS_PER_GROUP]
        gate = p[i] / denom
        first = sel & (count == 0.0)
        second = sel & (count == 1.0)
        id0 = jnp.where(first, float(i), id0)
        w0 = jnp.where(first, gate, w0)
        id1 = jnp.where(second, float(i), id1)
        w1 = jnp.where(second, gate, w1)
        count = count + sel.astype(F32)
    ids_ref[0:1, :] = id0.astype(jnp.int32)
    ids_ref[1:2, :] = id1.astype(jnp.int32)
    wts_ref[0:1, :] = w0
    wts_ref[1:2, :] = w1


def _router(h, router_w, router_b):
    t, d = h.shape
    tm = _tile(t, ROUTER_TILE)
    return pl.pallas_call(
        _router_kernel,
        grid=(t // tm,),
        in_specs=[
            pl.BlockSpec((tm, d), lambda i: (i, 0)),
            pl.BlockSpec((N_EXPERTS, d), lambda i: (0, 0)),
            pl.BlockSpec((N_EXPERTS, 1), lambda i: (0, 0)),
        ],
        out_specs=[pl.BlockSpec((TOP_K, tm), lambda i: (0, i)), pl.BlockSpec((TOP_K, tm), lambda i: (0, i))],
        out_shape=[jax.ShapeDtypeStruct((TOP_K, t), jnp.int32), jax.ShapeDtypeStruct((TOP_K, t), F32)],
        compiler_params=_cparams(("parallel",)),
        name="moe_router",
    )(h, router_w.T, router_b.reshape(N_EXPERTS, 1))


def _route_metadata(ids, wts, tm):
    t = ids.shape[1]
    na = TOP_K * t
    eid = ids.T.reshape(na)
    onehot = (eid[:, None] == jnp.arange(N_EXPERTS, dtype=jnp.int32)[None, :]).astype(jnp.int32)
    csum = jnp.cumsum(onehot, axis=0)
    counts = csum[-1]
    rank = jnp.sum(csum * onehot, axis=1) - 1
    padded = (counts + tm - 1) // tm * tm
    pend = jnp.cumsum(padded)
    dest = (pend - padded)[eid] + rank
    n_tiles = na // tm + N_EXPERTS
    n_rows = n_tiles * tm
    row_assign = jnp.full((n_rows,), -1, jnp.int32).at[dest].set(jnp.arange(na, dtype=jnp.int32))
    assigned = jnp.maximum(row_assign, 0)
    row_token = assigned // TOP_K
    row_weight = jnp.where(row_assign >= 0, wts.T.reshape(na)[assigned], 0.0)
    tile_start = jnp.arange(n_tiles, dtype=jnp.int32) * tm
    tile_valid = (tile_start < pend[-1]).astype(jnp.int32)
    last_expert = jnp.max(jnp.where(counts > 0, jnp.arange(N_EXPERTS, dtype=jnp.int32), 0))
    tile_expert = jnp.minimum(jnp.searchsorted(pend, tile_start, side="right").astype(jnp.int32), last_expert)
    experts = jnp.arange(N_EXPERTS, dtype=jnp.int32)
    later = jnp.where((counts > 0)[None, :] & (experts[None, :] > experts[:, None]), experts[None, :], N_EXPERTS)
    next_nonempty = jnp.min(later, axis=1)
    next_nonempty = jnp.where(next_nonempty == N_EXPERTS, tile_expert[0], next_nonempty).astype(jnp.int32)
    prev_expert = jnp.concatenate([jnp.full((1,), -1, jnp.int32), tile_expert[:-1]])
    run_first = (tile_valid == 1) & (tile_expert != prev_expert)
    runs = jnp.stack([tile_expert, tile_valid, run_first.astype(jnp.int32), next_nonempty[tile_expert],
                      (tile_expert == last_expert).astype(jnp.int32)])
    return row_token, row_weight, dest, runs


def _row_copy(src_hbm, tiles_ref, sem, tile, sub, src_row):
    return pltpu.make_async_copy(src_hbm.at[src_row], tiles_ref.at[tile, :, sub, :], sem)


def _start_row_gather(src_hbm, tiles_ref, idx_ref, n, sem):
    def body(tile, carry):
        for sub in range(SUBLANES):
            _row_copy(src_hbm, tiles_ref, sem, tile, sub, idx_ref[0, 0, tile * SUBLANES + sub]).start(
                priority=sub % 2)
        return carry

    lax.fori_loop(0, n // SUBLANES, body, 0)


def _wait_row_gather(src_hbm, tiles_ref, n, sem):
    def body(tile, carry):
        for sub in range(SUBLANES):
            _row_copy(src_hbm, tiles_ref, sem, tile, sub, 0).wait()
        return carry

    lax.fori_loop(0, n // SUBLANES, body, 0)


def _gather_step(src_hbm, slots_ref, idx_ref, idx_next_ref, sems, n, *, more):
    i = pl.program_id(0)
    slot = i % 2

    @pl.when(i == 0)
    def _():
        _start_row_gather(src_hbm, slots_ref.at[0], idx_ref, n, sems.at[0])

    @pl.when(more)
    def _():
        _start_row_gather(src_hbm, slots_ref.at[1 - slot], idx_next_ref, n, sems.at[1 - slot])

    return slot


def _moe_gather_kernel(tv_ref, rt_ref, rt_next_ref, h_hbm, o_ref, xg_ref, sems, *, tm):
    i = pl.program_id(0)
    nxt = jnp.minimum(i + 1, pl.num_programs(0) - 1)
    valid = tv_ref[i] == 1
    slot = _gather_step(h_hbm, xg_ref, rt_ref, rt_next_ref, sems, tm, more=(nxt > i) & (tv_ref[nxt] == 1))

    @pl.when(valid)
    def _():
        _wait_row_gather(h_hbm, xg_ref.at[slot], tm, sems.at[slot])
        o_ref[...] = _tiles_to_rows(xg_ref.at[slot], 0, tm // SUBLANES).astype(o_ref.dtype)

    @pl.when(jnp.logical_not(valid))
    def _():
        o_ref[...] = jnp.zeros_like(o_ref)


RUN_EXPERT, RUN_VALID, RUN_FIRST, RUN_NEXT, RUN_LAST = range(5)


def _stream_expert_weights(runs_ref, w_hbm, stage_ref, wb_ref, sems, *, layer, tn):
    j, i = pl.program_id(0), pl.program_id(1)
    last_pass = j == pl.num_programs(0) - 1

    def copies(expert, col_block):
        col = pl.multiple_of(col_block * tn, tn)
        return [pltpu.make_async_copy(w.at[layer, expert, :, pl.ds(col, tn)], stage_ref.at[m], sems.at[m])
                for m, w in enumerate(w_hbm)]

    @pl.when((j == 0) & (i == 0))
    def _():
        for c in copies(runs_ref[RUN_EXPERT, 0], 0):
            c.start()

    @pl.when(runs_ref[RUN_FIRST, i] == 1)
    def _():
        for c in copies(runs_ref[RUN_EXPERT, i], j):
            c.wait()
        for m in range(len(w_hbm)):
            wb_ref[m] = stage_ref[m].astype(BF16)
        last_run = runs_ref[RUN_LAST, i] == 1

        @pl.when(jnp.logical_not(last_run & last_pass))
        def _():
            for c in copies(runs_ref[RUN_NEXT, i], jnp.where(last_run, j + 1, j)):
                c.start()


def _moe_up_kernel(runs_ref, x_ref, wg_hbm, wu_hbm, o_ref, stage_ref, wb_ref, sems, *, layer, tn):
    _stream_expert_weights(runs_ref, (wg_hbm, wu_hbm), stage_ref, wb_ref, sems, layer=layer, tn=tn)
    valid = runs_ref[RUN_VALID, pl.program_id(1)] == 1

    @pl.when(valid)
    def _():
        xt = x_ref[...]
        gate = jnp.dot(xt, wb_ref[0], preferred_element_type=F32)
        up = jnp.dot(xt, wb_ref[1], preferred_element_type=F32)
        o_ref[...] = (_silu(gate) * up).astype(o_ref.dtype)

    @pl.when(jnp.logical_not(valid))
    def _():
        o_ref[...] = jnp.zeros_like(o_ref)


def _moe_down_kernel(runs_ref, he_ref, wd_hbm, rw_ref, o_ref, stage_ref, wb_ref, sems, *, layer, tn):
    _stream_expert_weights(runs_ref, (wd_hbm,), stage_ref, wb_ref, sems, layer=layer, tn=tn)
    valid = runs_ref[RUN_VALID, pl.program_id(1)] == 1

    @pl.when(valid)
    def _():
        _to_token_major(o_ref, jnp.dot(he_ref[...], wb_ref[0], preferred_element_type=F32) * rw_ref[...])

    @pl.when(jnp.logical_not(valid))
    def _():
        o_ref[...] = jnp.zeros_like(o_ref)


def _moe_combine_ln_kernel(pos_ref, pos_next_ref, h_ref, y_hbm, g_ref, b_ref, o_ref, ob_ref, yg_ref, sems, *, tm):
    n = TOP_K * tm
    slot = _gather_step(y_hbm, yg_ref, pos_ref, pos_next_ref, sems, n,
                        more=pl.program_id(0) + 1 < pl.num_programs(0))
    _wait_row_gather(y_hbm, yg_ref.at[slot], n, sems.at[slot])
    per = tm // SUBLANES
    ffn = _tiles_to_rows(yg_ref.at[slot], 0, per) + _tiles_to_rows(yg_ref.at[slot], per, TOP_K * per)
    r = _layer_norm_rows(ALPHA * h_ref[...] + ffn, g_ref[...], b_ref[...])
    o_ref[...] = r
    ob_ref[...] = r.astype(BF16)


def _moe_block(h, h3, router_w, router_b, w_gate, w_up, w_down, layer, ln_g, ln_b, *, tm=MOE_TILE, tn_up=512,
               tn_down=4096, tm_out=256):
    t, d = h.shape
    slabs = d // LANES
    ids, wts = _router(h, router_w, router_b)
    row_token, row_weight, dest, runs = _route_metadata(ids, wts, tm)
    tile_valid = runs[RUN_VALID]
    n_rows = row_token.shape[0]
    n_tiles = n_rows // tm
    de = w_gate.shape[-1]
    tn_up, tn_down = _tile(de, tn_up), _tile(d, tn_down)

    def gather_scratch(n):
        return [pltpu.VMEM((2, n // SUBLANES, slabs, SUBLANES, LANES), F32), pltpu.SemaphoreType.DMA((2,))]

    tg = _tile(tm, MOE_GATHER_TILE)
    per = tm // tg
    n_gather = n_rows // tg
    row_token3 = row_token.reshape(n_gather, 1, tg)
    x_rows = pl.pallas_call(
        functools.partial(_moe_gather_kernel, tm=tg),
        grid_spec=pltpu.PrefetchScalarGridSpec(
            num_scalar_prefetch=1,
            grid=(n_gather,),
            in_specs=[
                pl.BlockSpec((1, 1, tg), lambda i, tv: (i, 0, 0), memory_space=pltpu.SMEM),
                pl.BlockSpec((1, 1, tg), lambda i, tv: (jnp.minimum(i + 1, n_gather - 1), 0, 0),
                             memory_space=pltpu.SMEM),
                pl.BlockSpec(memory_space=pl.ANY),
            ],
            out_specs=pl.BlockSpec((tg, d), lambda i, tv: (i, 0)),
            scratch_shapes=gather_scratch(tg),
        ),
        out_shape=jax.ShapeDtypeStruct((n_rows, d), BF16),
        compiler_params=_cparams(("arbitrary",)),
        name="moe_gather",
    )(jnp.repeat(tile_valid, per), row_token3, row_token3, h3)

    def weight_scratch(n_mats, k, tn):
        return [pltpu.VMEM((n_mats, k, tn), F32), pltpu.VMEM((n_mats, k, tn), BF16),
                pltpu.SemaphoreType.DMA((n_mats,))]

    he = pl.pallas_call(
        functools.partial(_moe_up_kernel, layer=layer, tn=tn_up),
        grid_spec=pltpu.PrefetchScalarGridSpec(
            num_scalar_prefetch=1,
            grid=(de // tn_up, n_tiles),
            in_specs=[
                pl.BlockSpec((tm, d), lambda j, i, runs: (i, 0)),
                pl.BlockSpec(memory_space=pl.ANY),
                pl.BlockSpec(memory_space=pl.ANY),
            ],
            out_specs=pl.BlockSpec((tm, tn_up), lambda j, i, runs: (i, j)),
            scratch_shapes=weight_scratch(2, d, tn_up),
        ),
        out_shape=jax.ShapeDtypeStruct((n_rows, de), BF16),
        compiler_params=_cparams(("arbitrary", "arbitrary")),
        name="moe_gate_up",
    )(runs, x_rows, w_gate, w_up)

    y_rows = pl.pallas_call(
        functools.partial(_moe_down_kernel, layer=layer, tn=tn_down),
        grid_spec=pltpu.PrefetchScalarGridSpec(
            num_scalar_prefetch=1,
            grid=(d // tn_down, n_tiles),
            in_specs=[
                pl.BlockSpec((tm, de), lambda j, i, runs: (i, 0)),
                pl.BlockSpec(memory_space=pl.ANY),
                pl.BlockSpec((tm, 1), lambda j, i, runs: (i, 0)),
            ],
            out_specs=pl.BlockSpec((tm, tn_down // LANES, LANES), lambda j, i, runs: (i, j, 0)),
            scratch_shapes=weight_scratch(1, de, tn_down),
        ),
        out_shape=jax.ShapeDtypeStruct((n_rows, slabs, LANES), F32),
        compiler_params=_cparams(("arbitrary", "arbitrary")),
        name="moe_down",
    )(runs, he, w_down, row_weight.reshape(n_rows, 1))

    tm_out = _tile(t, tm_out)
    n_out = t // tm_out
    pos = dest.reshape(n_out, tm_out, TOP_K).transpose(0, 2, 1).reshape(n_out, 1, TOP_K * tm_out)
    row = pl.BlockSpec((tm_out, d), lambda i: (i, 0))
    vec = pl.BlockSpec((1, d), lambda i: (0, 0))
    return pl.pallas_call(
        functools.partial(_moe_combine_ln_kernel, tm=tm_out),
        grid=(n_out,),
        in_specs=[
            pl.BlockSpec((1, 1, TOP_K * tm_out), lambda i: (i, 0, 0), memory_space=pltpu.SMEM),
            pl.BlockSpec((1, 1, TOP_K * tm_out), lambda i: (jnp.minimum(i + 1, n_out - 1), 0, 0),
                         memory_space=pltpu.SMEM),
            row,
            pl.BlockSpec(memory_space=pl.ANY),
            vec, vec,
        ],
        out_specs=[row, row],
        out_shape=[jax.ShapeDtypeStruct((t, d), F32), jax.ShapeDtypeStruct((t, d), BF16)],
        scratch_shapes=gather_scratch(TOP_K * tm_out),
        compiler_params=_cparams(("arbitrary",)),
        name="moe_combine_ln",
    )(pos, pos, h, y_rows, ln_g.reshape(1, d), ln_b.reshape(1, d))


def _shift_mix_kernel(x_ref, xp_ref, mix_ref, *rest, tm, seq):
    o_refs, pad_ref = rest[:-1], rest[-1]
    i = pl.program_id(0)
    starts_sequence = (i * tm) % seq == 0
    pad_ref[0:SUBLANES, :] = jnp.where(starts_sequence, 0.0, xp_ref[...])
    pad_ref[SUBLANES:SUBLANES + tm, :] = x_ref[...]
    xv = x_ref[...]
    xx = pad_ref[SUBLANES - 1:SUBLANES - 1 + tm, :] - xv
    for m, o_ref in enumerate(o_refs):
        o_ref[...] = (xv + xx * mix_ref[m:m + 1, :]).astype(o_ref.dtype)


def _shift_mix(x, mix, seq, *, tm=128):
    t, d = x.shape
    nm = mix.shape[0]
    tm = _tile(seq, tm)
    per = tm // SUBLANES
    return pl.pallas_call(
        functools.partial(_shift_mix_kernel, tm=tm, seq=seq),
        grid=(t // tm,),
        in_specs=[
            pl.BlockSpec((tm, d), lambda i: (i, 0)),
            pl.BlockSpec((SUBLANES, d), lambda i: (jnp.maximum(i * per - 1, 0), 0)),
            pl.BlockSpec((nm, d), lambda i: (0, 0)),
        ],
        out_specs=[pl.BlockSpec((tm, d), lambda i: (i, 0))] * nm,
        out_shape=[jax.ShapeDtypeStruct((t, d), BF16)] * nm,
        scratch_shapes=[pltpu.VMEM((SUBLANES + tm, d), F32)],
        compiler_params=_cparams(("parallel",)),
        name="rwkv_shift_mix",
    )(x, x, mix)


WKV_CHUNK = 64
WKV_HEADS = 4
WKV_WIDTH = WKV_HEADS * RWKV_HEAD
WKV_BLOCK = 512
NEUMANN_STEPS = 5


def _wkv_kernel(r_ref, k_ref, v_ref, wl_ref, ar_ref, g_ref, kk_ref, ka_ref, rk_ref, lnw_ref, lnb_ref,
                o_ref, state_ref, *, tb):
    lc, hw, wd = WKV_CHUNK, RWKV_HEAD, WKV_WIDTH
    nch = tb // lc

    @pl.when(pl.program_id(2) == 0)
    def _():
        state_ref[...] = jnp.zeros_like(state_ref)

    wrow = lax.broadcasted_iota(jnp.int32, (wd, wd), 0)
    wcol = lax.broadcasted_iota(jnp.int32, (wd, wd), 1)
    same_head = wrow // hw == wcol // hw
    ones_bd = same_head.astype(BF16)

    def block_diag(m):
        return jnp.where(same_head, jnp.concatenate([m] * WKV_HEADS, axis=0), 0.0).astype(BF16)

    def head_sum(m):
        hi = m.astype(BF16)
        lo = (m - hi.astype(F32)).astype(BF16)
        return jnp.dot(hi, ones_bd, preferred_element_type=F32) + jnp.dot(lo, ones_bd, preferred_element_type=F32)

    def mm(a, b):
        return jnp.dot(a.astype(BF16), b, preferred_element_type=F32)

    r = r_ref[...]
    k = k_ref[...]
    v = v_ref[...]
    w_log = -_softplus(-wl_ref[...]) - 0.5
    lw = -jnp.exp(w_log)
    a_sig = jax.nn.sigmoid(ar_ref[...])
    kx = k * kk_ref[...]
    kk = kx / jnp.maximum(jnp.sqrt(head_sum(kx * kx)), 1e-12)
    k2 = k * (1.0 + (a_sig - 1.0) * ka_ref[...])
    a_s = -kk
    b_s = kk * a_sig

    trow = lax.broadcasted_iota(jnp.int32, (tb, tb), 0)
    tcol = lax.broadcasted_iota(jnp.int32, (tb, tb), 1)
    same_chunk = trow // lc == tcol // lc
    cum = _dot_split3(lw, (same_chunk & (trow >= tcol)).astype(BF16), split_lhs=False)
    tot = jnp.concatenate([jnp.broadcast_to(cum[(c + 1) * lc - 1:(c + 1) * lc, :], (lc, wd)) for c in range(nch)],
                          axis=0)
    grow = jnp.exp(-cum)
    rt = r * jnp.exp(cum)
    at = a_s * jnp.exp(cum - lw)
    bt = b_s * grow
    kt = k2 * grow
    rest = jnp.exp(tot - cum)
    bh = b_s * rest
    kh = k2 * rest
    p_end = jnp.exp(tot)

    t_idx = lax.broadcasted_iota(jnp.int32, (lc, wd), 0)
    s_idx = lax.broadcasted_iota(jnp.int32, (lc, wd), 1) % hw
    strict = t_idx > s_idx
    incl = t_idx >= s_idx
    eye = (t_idx == s_idx).astype(F32)
    nt = (((1,), (1,)), ((), ()))

    tn = (((0,), (0,)), ((), ()))
    chunks = [slice(c * lc, (c + 1) * lc) for c in range(nch)]

    a_ab, a_rb, a_ak, a_rk = [], [], [], []
    for sl in chunks:
        ar = jnp.concatenate([at[sl], rt[sl]], axis=0).astype(BF16)
        xb = lax.dot_general(ar, block_diag(bt[sl]), nt, preferred_element_type=F32)
        xk = lax.dot_general(ar, block_diag(kt[sl]), nt, preferred_element_type=F32)
        a_ab.append(jnp.where(strict, xb[:lc], 0.0))
        a_rb.append(jnp.where(incl, xb[lc:], 0.0))
        a_ak.append(jnp.where(strict, xk[:lc], 0.0))
        a_rk.append(jnp.where(incl, xk[lc:], 0.0))
    inv = [eye + a for a in a_ab]
    pw = [mm(a, block_diag(a)) for a in a_ab]
    for step in range(1, NEUMANN_STEPS + 1):
        last = step == NEUMANN_STEPS
        lhs = inv if last else [jnp.concatenate([t, p], axis=0) for t, p in zip(inv, pw)]
        prod = [mm(x, block_diag(p)) for x, p in zip(lhs, pw)]
        inv = [t + q[:lc] for t, q in zip(inv, prod)]
        if not last:
            pw = [q[lc:] for q in prod]
    v_bd = [block_diag(v[sl]) for sl in chunks]
    ta = [mm(t, block_diag(at[sl])) for t, sl in zip(inv, chunks)]
    av = [mm(jnp.concatenate([ak, rk], axis=0), vb) for ak, rk, vb in zip(a_ak, a_rk, v_bd)]
    y0 = [x[lc:] for x in av]
    u0 = [mm(t, block_diag(x[:lc])) for t, x in zip(inv, av)]
    gain, add = [], []
    for c, sl in enumerate(chunks):
        bh_b = bh[sl].astype(BF16)
        gain.append(jnp.where(same_head, lax.dot_general(ta[c].astype(BF16), bh_b, tn, preferred_element_type=F32),
                              0.0).astype(BF16))
        uv = jnp.concatenate([u0[c], v[sl]], axis=0).astype(BF16)
        bk = jnp.concatenate([bh_b, kh[sl].astype(BF16)], axis=0)
        add.append(jnp.where(same_head, lax.dot_general(uv, bk, tn, preferred_element_type=F32), 0.0))

    states = [state_ref[...]]
    for c in range(nch):
        s0 = states[-1]
        states.append(s0 * p_end[c * lc:c * lc + 1, :] + mm(s0, gain[c]) + add[c])
    state_ref[...] = states[nch]

    ys = []
    for c, sl in enumerate(chunks):
        tr = jnp.concatenate([ta[c], rt[sl]], axis=0).astype(BF16)
        xs = lax.dot_general(tr, states[c].astype(BF16), nt, preferred_element_type=F32)
        u = xs[:lc] + u0[c]
        ys.append(xs[lc:] + mm(a_rb[c], block_diag(u)) + y0[c])

    y = jnp.concatenate(ys, axis=0)
    mu = head_sum(y) / hw
    dev = y - mu
    var = head_sum(dev * dev) / hw
    yn = dev * lax.rsqrt(var + RWKV_GN_EPS) * lnw_ref[...] + lnb_ref[...]
    bonus = head_sum(r * k2 * rk_ref[...]) * v
    o_ref[...] = ((yn + bonus) * g_ref[...]).astype(o_ref.dtype)


def _wkv(r, k, v, wl, araw, g, k_k, k_a, r_k, ln_w, ln_b, batch, seq):
    t, d = r.shape
    tb = _tile(seq, WKV_BLOCK)
    nt = seq // tb
    wd = WKV_WIDTH
    rows = pl.BlockSpec((tb, wd), lambda b, h, n: (b * nt + n, h))
    vec = pl.BlockSpec((1, wd), lambda b, h, n: (0, h))
    return pl.pallas_call(
        functools.partial(_wkv_kernel, tb=tb),
        grid=(batch, d // wd, nt),
        in_specs=[rows] * 6 + [vec] * 5,
        out_specs=rows,
        out_shape=jax.ShapeDtypeStruct((t, d), BF16),
        scratch_shapes=[pltpu.VMEM((wd, wd), F32)],
        compiler_params=_cparams(("parallel", "parallel", "arbitrary")),
        name="rwkv_wkv",
    )(r, k, v, wl, araw, g, k_k.reshape(1, d), k_a.reshape(1, d), r_k.reshape(1, d),
      ln_w.reshape(1, d), ln_b.reshape(1, d))


def _rwkv7_time_mix(x2, batch, seq, mix, w_r, w_k, w_v, w_o, w0, w1, w2, a0, a1, a2, g1, g2, k_k, k_a, r_k,
                    ln_w, ln_b):
    xm = _shift_mix(x2, mix, seq)
    bf = lambda w: w.astype(BF16)
    r = _matmul(xm[0], bf(w_r))
    k = _matmul(xm[2], bf(w_k))
    v = _matmul(xm[3], bf(w_v))
    wl = _low_rank(xm[1], bf(w1), bf(w2), w0, act="tanh")
    araw = _low_rank(xm[4], bf(a1), bf(a2), a0)
    g = _low_rank(xm[5], bf(g1), bf(g2), act="sigmoid")
    yg = _wkv(r, k, v, wl, araw, g, k_k, k_a, r_k, ln_w, ln_b, batch, seq)
    return _matmul(yg, bf(w_o))


def kernel(x, ab_w_in, ab_b_qkv, ssm_conv_w, ssm_conv_b, ssm_dt_bias, ssm_a_log, ssm_d, ssm_norm_w, attn_sinks,
           ab_w_out, ab_b_out, rwkv_mix, rwkv_w_r, rwkv_w_k, rwkv_w_v, rwkv_w_o, rwkv_w0, rwkv_w1, rwkv_w2,
           rwkv_a0, rwkv_a1, rwkv_a2, rwkv_g1, rwkv_g2, rwkv_k_k, rwkv_k_a, rwkv_r_k, rwkv_ln_w, rwkv_ln_b,
           ln_mix_g, ln_mix_b, ln_ffn_g, ln_ffn_b, router_w, router_b, moe_w_gate, moe_w_up, moe_w_down):
    batch, seq, d = x.shape
    x2 = x.reshape(batch * seq, d)
    xb = x2.astype(BF16)
    for layer in range(DEPTH):
        i = layer // 2
        if layer % 2 == 0:
            mix = _ssd_swa_mixer(x2, xb, batch, seq, ab_w_in[i], ab_b_qkv[i], ssm_conv_w[i], ssm_conv_b[i],
                                 ssm_dt_bias[i], ssm_a_log[i], ssm_d[i], ssm_norm_w[i], attn_sinks[i],
                                 ab_w_out[i], ab_b_out[i])
        else:
            mix = _rwkv7_time_mix(x2, batch, seq, rwkv_mix[i], rwkv_w_r[i], rwkv_w_k[i], rwkv_w_v[i], rwkv_w_o[i],
                                  rwkv_w0[i], rwkv_w1[i], rwkv_w2[i], rwkv_a0[i], rwkv_a1[i], rwkv_a2[i],
                                  rwkv_g1[i], rwkv_g2[i], rwkv_k_k[i], rwkv_k_a[i], rwkv_r_k[i].reshape(-1),
                                  rwkv_ln_w[i], rwkv_ln_b[i])
        h, h3 = _add_layer_norm(x2, mix, ln_mix_g[layer], ln_mix_b[layer])
        x2, xb = _moe_block(h, h3, router_w, router_b, moe_w_gate, moe_w_up, moe_w_down, layer,
                            ln_ffn_g[layer], ln_ffn_b[layer])
    return x2.reshape(batch, seq, d)
```

```python
import functools

import jax
import jax.numpy as jnp
import numpy as np
from jax import lax
from jax.experimental import pallas as pl
from jax.experimental.pallas import tpu as pltpu

F32 = jnp.float32
BF16 = jnp.bfloat16

D_MODEL = 4096
DEPTH = 2
SSM_HEAD_DIM = 64
SSM_HEADS = 64
SSM_GROUPS = 8
SSM_HEADS_PER_GROUP = 8
SSM_STATE = 128
SSM_CONV = 4
SSM_CHUNK = 128
SSM_GROUP_WIDTH = SSM_HEADS_PER_GROUP * SSM_HEAD_DIM
SSM_CONV_CH = D_MODEL + 2 * SSM_GROUPS * SSM_STATE
SSM_NORM_EPS = 1e-5
ATT_KV_HEADS = 8
ATT_HEAD_DIM = 64
ATT_REP = 8
ATT_WINDOW = 128
ROPE_DIM = 16
ROPE_THETA = 500000.0
Q_COLS = 4096
KV_COLS = 512
RWKV_HEAD = 64
RWKV_GN_EPS = 64e-5
N_EXPERTS = 16
N_EXPERT_GROUPS = 4
EXPERTS_PER_GROUP = 4
TOP_K = 2
ALPHA = (2 * DEPTH) ** 0.25
LN_EPS = 1e-5

VMEM_LIMIT_BYTES = 56 * 1024 * 1024
LANES = 128
SUBLANES = 8


def _cparams(semantics):
    return pltpu.CompilerParams(dimension_semantics=semantics, vmem_limit_bytes=VMEM_LIMIT_BYTES)


def _tile(dim, pref):
    if dim <= pref:
        return dim
    t = pref
    while dim % t:
        t //= 2
    return t


def _silu(v):
    return v * jax.nn.sigmoid(v)


def _softplus(v):
    return jnp.maximum(v, 0.0) + jnp.log1p(jnp.exp(-jnp.abs(v)))


def _split3(v):
    hi = v.astype(BF16)
    rest = v - hi.astype(F32)
    mid = rest.astype(BF16)
    lo = (rest - mid.astype(F32)).astype(BF16)
    return hi, mid, lo


NT_DIMS = (((1,), (1,)), ((), ()))
NN_DIMS = (((1,), (0,)), ((), ()))


def _dot_bf16x3(a, b, dims):
    a_hi, b_hi = a.astype(BF16), b.astype(BF16)
    a_lo = (a - a_hi.astype(F32)).astype(BF16)
    b_lo = (b - b_hi.astype(F32)).astype(BF16)

    def dg(u, v):
        return lax.dot_general(u, v, dims, preferred_element_type=F32)

    return dg(a_hi, b_hi) + (dg(a_hi, b_lo) + dg(a_lo, b_hi))


def _dot_split3(v, m01, *, split_lhs):
    if split_lhs:
        return sum(jnp.dot(t, m01, preferred_element_type=F32) for t in _split3(v))
    return sum(jnp.dot(m01, t, preferred_element_type=F32) for t in _split3(v))


def _mm_kernel(*refs, act, bf16x3, w_is_nk):
    a_refs, (w_ref, b_ref, o_ref) = refs[:-3], refs[-3:]
    dims = NT_DIMS if w_is_nk else NN_DIMS
    r = b_ref[...]
    k0 = 0
    for a_ref in a_refs:
        k1 = k0 + a_ref.shape[1]
        w_part = w_ref[:, k0:k1] if w_is_nk else w_ref[k0:k1, :]
        if bf16x3:
            r = r + _dot_bf16x3(a_ref[...], w_part, dims)
        else:
            r = r + lax.dot_general(a_ref[...], w_part, dims, preferred_element_type=F32)
        k0 = k1
    if act == "tanh":
        r = jnp.tanh(r)
    elif act == "sigmoid":
        r = jax.nn.sigmoid(r)
    o_ref[...] = r.astype(o_ref.dtype)


MATMUL_VMEM_BUDGET = 44 * 1024 * 1024


def _matmul(a, w, bias=None, *, act=None, out_dtype=F32, bf16x3=False, tm=1024, col_start=0, n=None,
            w_is_nk=False):
    panels = a if isinstance(a, (tuple, list)) else (a,)
    m = panels[0].shape[0]
    kdim = sum(p.shape[1] for p in panels)
    w_n, w_k = (w.shape if w_is_nk else w.shape[::-1])
    n = w_n if n is None else n
    tm = _tile(m, tm)
    out_bytes = jnp.dtype(out_dtype).itemsize
    for tn in (512, 256, LANES):
        tn = _tile(n, tn)
        need = 2 * (tm * kdim * panels[0].dtype.itemsize + kdim * tn * w.dtype.itemsize + tm * tn * out_bytes)
        if need <= MATMUL_VMEM_BUDGET:
            break
    assert need <= MATMUL_VMEM_BUDGET and col_start % tn == 0 and w_k == kdim
    col0 = col_start // tn
    if bias is None:
        bias = jnp.zeros((n,), F32)
    bias = bias.reshape(1, n).astype(F32)
    if w_is_nk:
        w_spec = pl.BlockSpec((tn, kdim), lambda i, j: (col0 + j, 0))
    else:
        w_spec = pl.BlockSpec((kdim, tn), lambda i, j: (0, col0 + j))
    return pl.pallas_call(
        functools.partial(_mm_kernel, act=act, bf16x3=bf16x3, w_is_nk=w_is_nk),
        grid=(m // tm, n // tn),
        in_specs=[pl.BlockSpec((tm, p.shape[1]), lambda i, j: (i, 0)) for p in panels] + [
            w_spec,
            pl.BlockSpec((1, tn), lambda i, j: (0, j)),
        ],
        out_specs=pl.BlockSpec((tm, tn), lambda i, j: (i, j)),
        out_shape=jax.ShapeDtypeStruct((m, n), out_dtype),
        compiler_params=_cparams(("parallel", "arbitrary")),
        name="matmul",
    )(*panels, w, bias)


def _low_rank_kernel(a_ref, w1_ref, w2_ref, b_ref, o_ref, *, act):
    t = jnp.dot(a_ref[...], w1_ref[...], preferred_element_type=F32)
    if act == "tanh":
        t = jnp.tanh(t)
    elif act == "sigmoid":
        t = jax.nn.sigmoid(t)
    o_ref[...] = jnp.dot(t.astype(BF16), w2_ref[...], preferred_element_type=F32) + b_ref[...]


def _low_rank(a, w1, w2, bias=None, *, act=None, tm=512):
    m, kdim = a.shape
    r, n = w2.shape
    tm = _tile(m, tm)
    if bias is None:
        bias = jnp.zeros((n,), F32)
    return pl.pallas_call(
        functools.partial(_low_rank_kernel, act=act),
        grid=(m // tm,),
        in_specs=[
            pl.BlockSpec((tm, kdim), lambda i: (i, 0)),
            pl.BlockSpec((kdim, r), lambda i: (0, 0)),
            pl.BlockSpec((r, n), lambda i: (0, 0)),
            pl.BlockSpec((1, n), lambda i: (0, 0)),
        ],
        out_specs=pl.BlockSpec((tm, n), lambda i: (i, 0)),
        out_shape=jax.ShapeDtypeStruct((m, n), F32),
        compiler_params=_cparams(("parallel",)),
        name="low_rank",
    )(a, w1, w2, bias.reshape(1, n).astype(F32))


def _layer_norm_rows(v, g, b):
    mu = jnp.mean(v, -1, keepdims=True)
    var = jnp.mean(jnp.square(v - mu), -1, keepdims=True)
    return (v - mu) * lax.rsqrt(var + LN_EPS) * g + b


def _to_token_major(o3_ref, v):
    for s in range(v.shape[1] // LANES):
        o3_ref[:, s, :] = v[:, s * LANES:(s + 1) * LANES].astype(o3_ref.dtype)


def _tiles_to_rows(tiles_ref, t0, t1):
    return jnp.concatenate([tiles_ref[t0:t1, s].reshape((t1 - t0) * SUBLANES, LANES)
                            for s in range(tiles_ref.shape[1])], axis=1)


def _add_ln_kernel(x_ref, y_ref, g_ref, b_ref, o_ref, o3_ref):
    r = _layer_norm_rows(ALPHA * x_ref[...] + y_ref[...], g_ref[...], b_ref[...])
    o_ref[...] = r
    _to_token_major(o3_ref, r)


def _add_layer_norm(x, y, g, b, *, tm=256):
    t, d = x.shape
    tm = _tile(t, tm)
    row = pl.BlockSpec((tm, d), lambda i: (i, 0))
    row3 = pl.BlockSpec((tm, d // LANES, LANES), lambda i: (i, 0, 0))
    vec = pl.BlockSpec((1, d), lambda i: (0, 0))
    return pl.pallas_call(
        _add_ln_kernel,
        grid=(t // tm,),
        in_specs=[row, row, vec, vec],
        out_specs=[row, row3],
        out_shape=[jax.ShapeDtypeStruct((t, d), F32), jax.ShapeDtypeStruct((t, d // LANES, LANES), F32)],
        compiler_params=_cparams(("parallel",)),
        name="add_layer_norm",
    )(x, y, g.reshape(1, d), b.reshape(1, d))


CONV_HALO = SUBLANES
SSD_CH = SSM_GROUP_WIDTH + 2 * SSM_STATE


def _ssd_kernel(z_ref, x_ref, b_ref, c_ref, wx_ref, wb_ref, wc_ref, bx_ref, bb_ref, bc_ref,
                dtr_ref, dbc_ref, dbr_ref, alc_ref, alr_ref, d_ref, nw_ref,
                o_ref, state_ref, pad_ref):
    ch, gw, ns, hd = SSM_CHUNK, SSM_GROUP_WIDTH, SSM_STATE, SSM_HEAD_DIM

    @pl.when(pl.program_id(2) == 0)
    def _():
        state_ref[...] = jnp.zeros_like(state_ref)
        pad_ref[0:CONV_HALO, :] = jnp.zeros((CONV_HALO, SSD_CH), F32)

    pad_ref[CONV_HALO:CONV_HALO + ch, 0:gw] = x_ref[...]
    pad_ref[CONV_HALO:CONV_HALO + ch, gw:gw + ns] = b_ref[...]
    pad_ref[CONV_HALO:CONV_HALO + ch, gw + ns:SSD_CH] = c_ref[...]
    w = jnp.concatenate([wx_ref[...], wb_ref[...], wc_ref[...]], axis=1)
    acc = jnp.concatenate([bx_ref[...], bb_ref[...], bc_ref[...]], axis=1)
    base = CONV_HALO - (SSM_CONV - 1)
    for k in range(SSM_CONV):
        acc = acc + w[k:k + 1, :] * pad_ref[base + k:base + k + ch, :]
    pad_ref[0:CONV_HALO, :] = pad_ref[ch:ch + CONV_HALO, :]
    xbc = _silu(acc)
    xc, bm, cm = xbc[:, :gw], xbc[:, gw:gw + ns], xbc[:, gw + ns:]

    row = lax.broadcasted_iota(jnp.int32, (ch, ch), 0)
    col = lax.broadcasted_iota(jnp.int32, (ch, ch), 1)
    causal = row >= col
    dt_raw_r = dtr_ref[...]
    eye = (row == col).astype(BF16)
    dt_raw_c = sum(lax.dot_general(eye, term, (((1,), (1,)), ((), ())), preferred_element_type=F32)
                   for term in _split3(dt_raw_r))
    dt_c = _softplus(dt_raw_c + dbc_ref[0])
    dt_r = _softplus(dt_raw_r + dbr_ref[...])
    a_c = -jnp.exp(alc_ref[0])
    a_r = -jnp.exp(alr_ref[...])
    acs_c = _dot_split3(dt_c * a_c, causal.astype(BF16), split_lhs=False)
    acs_r = _dot_split3(dt_r * a_r, (row <= col).astype(BF16), split_lhs=True)

    hrow = lax.broadcasted_iota(jnp.int32, (SSM_HEADS_PER_GROUP, gw), 0)
    hcol = lax.broadcasted_iota(jnp.int32, (SSM_HEADS_PER_GROUP, gw), 1)
    expand = (hcol // hd == hrow).astype(BF16)
    dt_x = _dot_split3(dt_c, expand, split_lhs=True)
    acs_x = _dot_split3(acs_c, expand, split_lhs=True)

    xdt = xc * dt_x
    xdt_b = xdt.astype(BF16)
    bm_b, cm_b = bm.astype(BF16), cm.astype(BF16)
    cb = lax.dot_general(cm_b, bm_b, (((1,), (1,)), ((), ())), preferred_element_type=F32)
    prev = state_ref[...]
    y_off = jnp.dot(cm_b, prev.astype(BF16), preferred_element_type=F32) * jnp.exp(acs_x)

    lane = lax.broadcasted_iota(jnp.int32, (ch, 2 * hd), 1)
    pairs = []
    for pr in range(SSM_HEADS_PER_GROUP // 2):
        xp = xdt_b[:, pr * 2 * hd:(pr + 1) * 2 * hd]
        halves = []
        for q in range(2):
            r = 2 * pr + q
            diff = acs_c[:, r:r + 1] - acs_r[r:r + 1, :]
            seg = jnp.where(causal, jnp.exp(jnp.where(causal, diff, 0.0)), 0.0)
            halves.append(jnp.dot((cb * seg).astype(BF16), xp, preferred_element_type=F32))
        pairs.append(jnp.where(lane < hd, halves[0], halves[1]))
    y = jnp.concatenate(pairs, axis=1) + y_off + d_ref[...] * xc

    acs_last = acs_x[ch - 1:ch, :]
    contrib = lax.dot_general(bm_b, (xdt * jnp.exp(acs_last - acs_x)).astype(BF16),
                              (((0,), (0,)), ((), ())), preferred_element_type=F32)
    state_ref[...] = prev * jnp.exp(acs_last) + contrib

    u = y * _silu(z_ref[...])
    u = u * lax.rsqrt(jnp.mean(u * u, -1, keepdims=True) + SSM_NORM_EPS)
    o_ref[...] = (u * nw_ref[...]).astype(o_ref.dtype)


def _ssd_mixer(u_zx, u_dt, conv_w, conv_b, dt_bias, a_log, d_skip, norm_w, batch, seq):
    t = batch * seq
    nc = seq // SSM_CHUNK
    g, r, gw, ns = SSM_GROUPS, SSM_HEADS_PER_GROUP, SSM_GROUP_WIDTH, SSM_STATE
    xblocks = D_MODEL // ns
    conv_wt = conv_w.T
    conv_b2 = conv_b.reshape(1, SSM_CONV_CH)
    dt_row = u_dt.T
    rows = lambda b, gi, c: (b * nc + c, gi)
    in_specs = [
        pl.BlockSpec((SSM_CHUNK, gw), rows),
        pl.BlockSpec((SSM_CHUNK, gw), lambda b, gi, c: (b * nc + c, g + gi)),
        pl.BlockSpec((SSM_CHUNK, ns), lambda b, gi, c: (b * nc + c, 2 * xblocks + gi)),
        pl.BlockSpec((SSM_CHUNK, ns), lambda b, gi, c: (b * nc + c, 2 * xblocks + g + gi)),
        pl.BlockSpec((SSM_CONV, gw), lambda b, gi, c: (0, gi)),
        pl.BlockSpec((SSM_CONV, ns), lambda b, gi, c: (0, xblocks + gi)),
        pl.BlockSpec((SSM_CONV, ns), lambda b, gi, c: (0, xblocks + g + gi)),
        pl.BlockSpec((1, gw), lambda b, gi, c: (0, gi)),
        pl.BlockSpec((1, ns), lambda b, gi, c: (0, xblocks + gi)),
        pl.BlockSpec((1, ns), lambda b, gi, c: (0, xblocks + g + gi)),
        pl.BlockSpec((r, SSM_CHUNK), lambda b, gi, c: (gi, b * nc + c)),
        pl.BlockSpec((1, 1, r), lambda b, gi, c: (gi, 0, 0)),
        pl.BlockSpec((r, 1), lambda b, gi, c: (gi, 0)),
        pl.BlockSpec((1, 1, r), lambda b, gi, c: (gi, 0, 0)),
        pl.BlockSpec((r, 1), lambda b, gi, c: (gi, 0)),
        pl.BlockSpec((1, gw), lambda b, gi, c: (0, gi)),
        pl.BlockSpec((1, gw), lambda b, gi, c: (0, gi)),
    ]
    return pl.pallas_call(
        _ssd_kernel,
        grid=(batch, g, nc),
        in_specs=in_specs,
        out_specs=pl.BlockSpec((SSM_CHUNK, gw), rows),
        out_shape=jax.ShapeDtypeStruct((t, D_MODEL), BF16),
        scratch_shapes=[pltpu.VMEM((ns, gw), F32), pltpu.VMEM((CONV_HALO + SSM_CHUNK, SSD_CH), F32)],
        compiler_params=_cparams(("parallel", "parallel", "arbitrary")),
        name="ssd_mixer",
    )(u_zx, u_zx, u_zx, u_zx, conv_wt, conv_wt, conv_wt, conv_b2, conv_b2, conv_b2,
      dt_row, dt_bias.reshape(g, 1, r), dt_bias.reshape(SSM_HEADS, 1),
      a_log.reshape(g, 1, r), a_log.reshape(SSM_HEADS, 1),
      jnp.repeat(d_skip, SSM_HEAD_DIM).reshape(1, D_MODEL), norm_w.reshape(1, D_MODEL))


KV_PER_STEP = 8

def _rotary_tables(seq):
    half = ROPE_DIM // 2
    inv_freq = ROPE_THETA ** (-jnp.arange(half, dtype=F32) / half)
    ang = jnp.arange(seq, dtype=F32)[:, None] * inv_freq[None, :]
    ones = jnp.ones((seq, ATT_HEAD_DIM - ROPE_DIM), F32)
    cos = jnp.concatenate([jnp.cos(ang), jnp.cos(ang), ones], -1)
    sin = jnp.concatenate([jnp.sin(ang), jnp.sin(ang), 0.0 * ones], -1)
    return jnp.tile(cos, (1, ATT_REP)), jnp.tile(sin, (1, ATT_REP))


def _rotate_half_matrix():
    width = ATT_REP * ATT_HEAD_DIM
    half = ROPE_DIM // 2
    p = np.zeros((width, width), np.float32)
    for j in range(width):
        if j % ATT_HEAD_DIM < half:
            p[j + half, j] = -1.0
        elif j % ATT_HEAD_DIM < ROPE_DIM:
            p[j - half, j] = 1.0
    return jnp.asarray(p, BF16)


def _rotate(v, cos, sin, perm):
    hi = v.astype(BF16)
    lo = (v - hi.astype(F32)).astype(BF16)
    partner = jnp.dot(hi, perm, preferred_element_type=F32) + jnp.dot(lo, perm, preferred_element_type=F32)
    return v * cos + partner * sin


def _swa_kernel(sink_ref, q_ref, kc_ref, kp_ref, vc_ref, vp_ref, cosc_ref, sinc_ref, cosp_ref, sinp_ref,
                perm_ref, o_ref):
    w, hd = ATT_WINDOW, ATT_HEAD_DIM
    qw, kw = ATT_REP * hd, KV_PER_STEP * hd
    pair = pl.program_id(1)
    n = pl.program_id(2)
    perm = perm_ref[...]
    perm_k = perm[:kw, :kw]
    cos_c, sin_c = cosc_ref[...], sinc_ref[...]
    q = jnp.concatenate([_rotate(q_ref[:, p * qw:(p + 1) * qw], cos_c, sin_c, perm) for p in range(KV_PER_STEP)],
                        axis=1)
    k_cur = _rotate(kc_ref[...], cos_c[:, :kw], sin_c[:, :kw], perm_k)
    k_prev = _rotate(kp_ref[...], cosp_ref[:, :kw], sinp_ref[:, :kw], perm_k)
    k_pair = jnp.concatenate([k_prev, k_cur], axis=0).astype(BF16)
    v_pair = jnp.concatenate([vp_ref[...], vc_ref[...]], axis=0).astype(BF16)
    k_all = [k_pair[:, p * hd:(p + 1) * hd] for p in range(KV_PER_STEP)]
    v_all = [v_pair[:, p * hd:(p + 1) * hd] for p in range(KV_PER_STEP)]
    qi = lax.broadcasted_iota(jnp.int32, (w, 2 * w), 0) + w
    ki = lax.broadcasted_iota(jnp.int32, (w, 2 * w), 1)
    rel = qi - ki
    first = jnp.where(n > 0, 0, w)
    bias = jnp.where((rel >= 0) & (rel < ATT_WINDOW) & (ki >= first), 0.0, -jnp.inf)
    heads = range(KV_PER_STEP * ATT_REP)
    qs = [(q[:, r * hd:(r + 1) * hd] * (hd ** -0.5)).astype(BF16) for r in heads]
    sinks = [sink_ref[pair * KV_PER_STEP * ATT_REP + r] for r in heads]
    ss = [lax.dot_general(qr, k_all[r // ATT_REP], NT_DIMS, preferred_element_type=F32) + bias
          for r, qr in zip(heads, qs)]
    ms = [jnp.maximum(jnp.max(s, -1, keepdims=True), sink) for s, sink in zip(ss, sinks)]
    es = [jnp.exp(s - m) for s, m in zip(ss, ms)]
    dens = [jnp.sum(e, -1, keepdims=True) + jnp.exp(sink - m) for e, sink, m in zip(es, sinks, ms)]
    outs = [jnp.dot(e.astype(BF16), v_all[r // ATT_REP], preferred_element_type=F32) / den
            for r, e, den in zip(heads, es, dens)]
    o_ref[...] = jnp.concatenate(outs, axis=1).astype(o_ref.dtype)


def _swa_attention(qkv, sinks, batch, seq):
    t = batch * seq
    nb = seq // ATT_WINDOW
    hd, w = ATT_HEAD_DIM, ATT_WINDOW
    qw, kw = ATT_REP * hd, KV_PER_STEP * hd
    k_block0 = Q_COLS // kw
    v_block0 = (Q_COLS + KV_COLS) // kw
    cos, sin = _rotary_tables(seq)

    def kv_spec(block0, back):
        return pl.BlockSpec((w, kw), lambda b, p, n: (b * nb + jnp.maximum(n - back, 0), block0 + p))

    tab_cur = pl.BlockSpec((w, qw), lambda b, p, n: (n, 0))
    tab_prev = pl.BlockSpec((w, qw), lambda b, p, n: (jnp.maximum(n - 1, 0), 0))
    return pl.pallas_call(
        _swa_kernel,
        grid=(batch, ATT_KV_HEADS // KV_PER_STEP, nb),
        in_specs=[
            pl.BlockSpec(memory_space=pltpu.SMEM),
            pl.BlockSpec((w, KV_PER_STEP * qw), lambda b, p, n: (b * nb + n, p)),
            kv_spec(k_block0, 0), kv_spec(k_block0, 1), kv_spec(v_block0, 0), kv_spec(v_block0, 1),
            tab_cur, tab_cur, tab_prev, tab_prev,
            pl.BlockSpec((qw, qw), lambda b, p, n: (0, 0)),
        ],
        out_specs=pl.BlockSpec((w, KV_PER_STEP * qw), lambda b, p, n: (b * nb + n, p)),
        out_shape=jax.ShapeDtypeStruct((t, Q_COLS), BF16),
        compiler_params=_cparams(("parallel", "parallel", "arbitrary")),
        name="swa_attention",
    )(sinks, qkv, qkv, qkv, qkv, qkv, cos, sin, cos, sin, _rotate_half_matrix())


def _ssd_swa_mixer(x2, xb, batch, seq, w_in, b_qkv, conv_w, conv_b, dt_bias, a_log, d_skip, norm_w, sinks,
                   w_out, b_out):
    o1 = D_MODEL
    o2 = o1 + SSM_CONV_CH
    o3 = o2 + SSM_HEADS
    w_t = jnp.swapaxes(w_in, 0, 1)
    w_t_b = w_t.astype(BF16)
    u_zx = _matmul(xb, w_t_b, n=o2, w_is_nk=True)
    u_dt = _matmul(x2, w_t[o2:o3], bf16x3=True, w_is_nk=True)
    qkv = _matmul(xb, w_t_b[o3:], b_qkv, w_is_nk=True)
    y_ssm = _ssd_mixer(u_zx, u_dt, conv_w, conv_b, dt_bias, a_log, d_skip, norm_w, batch, seq)
    y_att = _swa_attention(qkv, sinks, batch, seq)
    return _matmul((y_ssm, y_att), w_out.astype(BF16), b_out)


MOE_TILE = 256
MOE_GATHER_TILE = 256
ROUTER_TILE = 256


def _router_kernel(h_ref, wt_ref, b_ref, ids_ref, wts_ref):
    logits = _dot_bf16x3(wt_ref[...], h_ref[...], NT_DIMS) + b_ref[...]
    e = jnp.exp(logits - jnp.max(logits, 0, keepdims=True))
    probs = e / jnp.sum(e, 0, keepdims=True)
    p = [probs[i:i + 1, :] for i in range(N_EXPERTS)]

    keep, score = [], []
    for g in range(N_EXPERT_GROUPS):
        members = range(g * EXPERTS_PER_GROUP, (g + 1) * EXPERTS_PER_GROUP)
        s = jnp.zeros_like(p[0])
        for i in members:
            rank = jnp.zeros_like(p[0])
            for j in members:
                if j != i:
                    beats = (p[j] > p[i]) | ((p[j] == p[i]) & (j < i)) if j < i else (p[j] > p[i])
                    rank = rank + beats.astype(F32)
            keep.append(rank < TOP_K)
            s = s + jnp.where(keep[i], p[i], 0.0)
        score.append(s)
    chosen = []
    for g in range(N_EXPERT_GROUPS):
        c = jnp.ones_like(p[0]) > 0
        for g2 in range(N_EXPERT_GROUPS):
            if g2 < g:
                c = c & (score[g] > score[g2])
            elif g2 > g:
                c = c & (score[g] >= score[g2])
        chosen.append(c)
    denom = jnp.zeros_like(p[0])
    for g in range(N_EXPERT_GROUPS):
        denom = denom + jnp.where(chosen[g], score[g], 0.0)
    count = jnp.zeros_like(p[0])
    id0 = jnp.zeros_like(p[0])
    id1 = jnp.zeros_like(p[0])
    w0 = jnp.zeros_like(p[0])
    w1 = jnp.zeros_like(p[0])
    for i in range(N_EXPERTS):
        sel = keep[i] & chosen[i // EXPERTS_PER_GROUP]
        gate = p[i] / denom
        first = sel & (count == 0.0)
        second = sel & (count == 1.0)
        id0 = jnp.where(first, float(i), id0)
        w0 = jnp.where(first, gate, w0)
        id1 = jnp.where(second, float(i), id1)
        w1 = jnp.where(second, gate, w1)
        count = count + sel.astype(F32)
    ids_ref[0:1, :] = id0.astype(jnp.int32)
    ids_ref[1:2, :] = id1.astype(jnp.int32)
    wts_ref[0:1, :] = w0
    wts_ref[1:2, :] = w1


def _router(h, router_w, router_b):
    t, d = h.shape
    tm = _tile(t, ROUTER_TILE)
    return pl.pallas_call(
        _router_kernel,
        grid=(t // tm,),
        in_specs=[
            pl.BlockSpec((tm, d), lambda i: (i, 0)),
            pl.BlockSpec((N_EXPERTS, d), lambda i: (0, 0)),
            pl.BlockSpec((N_EXPERTS, 1), lambda i: (0, 0)),
        ],
        out_specs=[pl.BlockSpec((TOP_K, tm), lambda i: (0, i)), pl.BlockSpec((TOP_K, tm), lambda i: (0, i))],
        out_shape=[jax.ShapeDtypeStruct((TOP_K, t), jnp.int32), jax.ShapeDtypeStruct((TOP_K, t), F32)],
        compiler_params=_cparams(("parallel",)),
        name="moe_router",
    )(h, router_w.T, router_b.reshape(N_EXPERTS, 1))


def _route_metadata(ids, wts, tm):
    t = ids.shape[1]
    na = TOP_K * t
    eid = ids.T.reshape(na)
    onehot = (eid[:, None] == jnp.arange(N_EXPERTS, dtype=jnp.int32)[None, :]).astype(jnp.int32)
    csum = jnp.cumsum(onehot, axis=0)
    counts = csum[-1]
    rank = jnp.sum(csum * onehot, axis=1) - 1
    padded = (counts + tm - 1) // tm * tm
    pend = jnp.cumsum(padded)
    dest = (pend - padded)[eid] + rank
    n_tiles = na // tm + N_EXPERTS
    n_rows = n_tiles * tm
    row_assign = jnp.full((n_rows,), -1, jnp.int32).at[dest].set(jnp.arange(na, dtype=jnp.int32))
    assigned = jnp.maximum(row_assign, 0)
    row_token = assigned // TOP_K
    row_weight = jnp.where(row_assign >= 0, wts.T.reshape(na)[assigned], 0.0)
    tile_start = jnp.arange(n_tiles, dtype=jnp.int32) * tm
    tile_valid = (tile_start < pend[-1]).astype(jnp.int32)
    last_expert = jnp.max(jnp.where(counts > 0, jnp.arange(N_EXPERTS, dtype=jnp.int32), 0))
    tile_expert = jnp.minimum(jnp.searchsorted(pend, tile_start, side="right").astype(jnp.int32), last_expert)
    experts = jnp.arange(N_EXPERTS, dtype=jnp.int32)
    later = jnp.where((counts > 0)[None, :] & (experts[None, :] > experts[:, None]), experts[None, :], N_EXPERTS)
    next_nonempty = jnp.min(later, axis=1)
    next_nonempty = jnp.where(next_nonempty == N_EXPERTS, tile_expert[0], next_nonempty).astype(jnp.int32)
    prev_expert = jnp.concatenate([jnp.full((1,), -1, jnp.int32), tile_expert[:-1]])
    run_first = (tile_valid == 1) & (tile_expert != prev_expert)
    runs = jnp.stack([tile_expert, tile_valid, run_first.astype(jnp.int32), next_nonempty[tile_expert],
                      (tile_expert == last_expert).astype(jnp.int32)])
    return row_token, row_weight, dest, runs


def _row_copy(src_hbm, tiles_ref, sem, tile, sub, src_row):
    return pltpu.make_async_copy(src_hbm.at[src_row], tiles_ref.at[tile, :, sub, :], sem)


def _start_row_gather(src_hbm, tiles_ref, idx_ref, n, sem):
    def body(tile, carry):
        for sub in range(SUBLANES):
            _row_copy(src_hbm, tiles_ref, sem, tile, sub, idx_ref[0, 0, tile * SUBLANES + sub]).start(
                priority=sub % 2)
        return carry

    lax.fori_loop(0, n // SUBLANES, body, 0)


def _wait_row_gather(src_hbm, tiles_ref, n, sem):
    def body(tile, carry):
        for sub in range(SUBLANES):
            _row_copy(src_hbm, tiles_ref, sem, tile, sub, 0).wait()
        return carry

    lax.fori_loop(0, n // SUBLANES, body, 0)


def _gather_step(src_hbm, slots_ref, idx_ref, idx_next_ref, sems, n, *, more):
    i = pl.program_id(0)
    slot = i % 2

    @pl.when(i == 0)
    def _():
        _start_row_gather(src_hbm, slots_ref.at[0], idx_ref, n, sems.at[0])

    @pl.when(more)
    def _():
        _start_row_gather(src_hbm, slots_ref.at[1 - slot], idx_next_ref, n, sems.at[1 - slot])

    return slot


def _moe_gather_kernel(tv_ref, rt_ref, rt_next_ref, h_hbm, o_ref, xg_ref, sems, *, tm):
    i = pl.program_id(0)
    nxt = jnp.minimum(i + 1, pl.num_programs(0) - 1)
    valid = tv_ref[i] == 1
    slot = _gather_step(h_hbm, xg_ref, rt_ref, rt_next_ref, sems, tm, more=(nxt > i) & (tv_ref[nxt] == 1))

    @pl.when(valid)
    def _():
        _wait_row_gather(h_hbm, xg_ref.at[slot], tm, sems.at[slot])
        o_ref[...] = _tiles_to_rows(xg_ref.at[slot], 0, tm // SUBLANES).astype(o_ref.dtype)

    @pl.when(jnp.logical_not(valid))
    def _():
        o_ref[...] = jnp.zeros_like(o_ref)


RUN_EXPERT, RUN_VALID, RUN_FIRST, RUN_NEXT, RUN_LAST = range(5)


def _stream_expert_weights(runs_ref, w_hbm, stage_ref, wb_ref, sems, *, layer, tn):
    j, i = pl.program_id(0), pl.program_id(1)
    last_pass = j == pl.num_programs(0) - 1

    def copies(expert, col_block):
        col = pl.multiple_of(col_block * tn, tn)
        return [pltpu.make_async_copy(w.at[layer, expert, :, pl.ds(col, tn)], stage_ref.at[m], sems.at[m])
                for m, w in enumerate(w_hbm)]

    @pl.when((j == 0) & (i == 0))
    def _():
        for c in copies(runs_ref[RUN_EXPERT, 0], 0):
            c.start()

    @pl.when(runs_ref[RUN_FIRST, i] == 1)
    def _():
        for c in copies(runs_ref[RUN_EXPERT, i], j):
            c.wait()
        for m in range(len(w_hbm)):
            wb_ref[m] = stage_ref[m].astype(BF16)
        last_run = runs_ref[RUN_LAST, i] == 1

        @pl.when(jnp.logical_not(last_run & last_pass))
        def _():
            for c in copies(runs_ref[RUN_NEXT, i], jnp.where(last_run, j + 1, j)):
                c.start()


def _moe_up_kernel(runs_ref, x_ref, wg_hbm, wu_hbm, o_ref, stage_ref, wb_ref, sems, *, layer, tn):
    _stream_expert_weights(runs_ref, (wg_hbm, wu_hbm), stage_ref, wb_ref, sems, layer=layer, tn=tn)
    valid = runs_ref[RUN_VALID, pl.program_id(1)] == 1

    @pl.when(valid)
    def _():
        xt = x_ref[...]
        gate = jnp.dot(xt, wb_ref[0], preferred_element_type=F32)
        up = jnp.dot(xt, wb_ref[1], preferred_element_type=F32)
        o_ref[...] = (_silu(gate) * up).astype(o_ref.dtype)

    @pl.when(jnp.logical_not(valid))
    def _():
        o_ref[...] = jnp.zeros_like(o_ref)


def _moe_down_kernel(runs_ref, he_ref, wd_hbm, rw_ref, o_ref, stage_ref, wb_ref, sems, *, layer, tn):
    _stream_expert_weights(runs_ref, (wd_hbm,), stage_ref, wb_ref, sems, layer=layer, tn=tn)
    valid = runs_ref[RUN_VALID, pl.program_id(1)] == 1

    @pl.when(valid)
    def _():
        _to_token_major(o_ref, jnp.dot(he_ref[...], wb_ref[0], preferred_element_type=F32) * rw_ref[...])

    @pl.when(jnp.logical_not(valid))
    def _():
        o_ref[...] = jnp.zeros_like(o_ref)


def _moe_combine_ln_kernel(pos_ref, pos_next_ref, h_ref, y_hbm, g_ref, b_ref, o_ref, ob_ref, yg_ref, sems, *, tm):
    n = TOP_K * tm
    slot = _gather_step(y_hbm, yg_ref, pos_ref, pos_next_ref, sems, n,
                        more=pl.program_id(0) + 1 < pl.num_programs(0))
    _wait_row_gather(y_hbm, yg_ref.at[slot], n, sems.at[slot])
    per = tm // SUBLANES
    ffn = _tiles_to_rows(yg_ref.at[slot], 0, per) + _tiles_to_rows(yg_ref.at[slot], per, TOP_K * per)
    r = _layer_norm_rows(ALPHA * h_ref[...] + ffn, g_ref[...], b_ref[...])
    o_ref[...] = r
    ob_ref[...] = r.astype(BF16)


def _moe_block(h, h3, router_w, router_b, w_gate, w_up, w_down, layer, ln_g, ln_b, *, tm=MOE_TILE, tn_up=512,
               tn_down=4096, tm_out=256):
    t, d = h.shape
    slabs = d // LANES
    ids, wts = _router(h, router_w, router_b)
    row_token, row_weight, dest, runs = _route_metadata(ids, wts, tm)
    tile_valid = runs[RUN_VALID]
    n_rows = row_token.shape[0]
    n_tiles = n_rows // tm
    de = w_gate.shape[-1]
    tn_up, tn_down = _tile(de, tn_up), _tile(d, tn_down)

    def gather_scratch(n):
        return [pltpu.VMEM((2, n // SUBLANES, slabs, SUBLANES, LANES), F32), pltpu.SemaphoreType.DMA((2,))]

    tg = _tile(tm, MOE_GATHER_TILE)
    per = tm // tg
    n_gather = n_rows // tg
    row_token3 = row_token.reshape(n_gather, 1, tg)
    x_rows = pl.pallas_call(
        functools.partial(_moe_gather_kernel, tm=tg),
        grid_spec=pltpu.PrefetchScalarGridSpec(
            num_scalar_prefetch=1,
            grid=(n_gather,),
            in_specs=[
                pl.BlockSpec((1, 1, tg), lambda i, tv: (i, 0, 0), memory_space=pltpu.SMEM),
                pl.BlockSpec((1, 1, tg), lambda i, tv: (jnp.minimum(i + 1, n_gather - 1), 0, 0),
                             memory_space=pltpu.SMEM),
                pl.BlockSpec(memory_space=pl.ANY),
            ],
            out_specs=pl.BlockSpec((tg, d), lambda i, tv: (i, 0)),
            scratch_shapes=gather_scratch(tg),
        ),
        out_shape=jax.ShapeDtypeStruct((n_rows, d), BF16),
        compiler_params=_cparams(("arbitrary",)),
        name="moe_gather",
    )(jnp.repeat(tile_valid, per), row_token3, row_token3, h3)

    def weight_scratch(n_mats, k, tn):
        return [pltpu.VMEM((n_mats, k, tn), F32), pltpu.VMEM((n_mats, k, tn), BF16),
                pltpu.SemaphoreType.DMA((n_mats,))]

    he = pl.pallas_call(
        functools.partial(_moe_up_kernel, layer=layer, tn=tn_up),
        grid_spec=pltpu.PrefetchScalarGridSpec(
            num_scalar_prefetch=1,
            grid=(de // tn_up, n_tiles),
            in_specs=[
                pl.BlockSpec((tm, d), lambda j, i, runs: (i, 0)),
                pl.BlockSpec(memory_space=pl.ANY),
                pl.BlockSpec(memory_space=pl.ANY),
            ],
            out_specs=pl.BlockSpec((tm, tn_up), lambda j, i, runs: (i, j)),
            scratch_shapes=weight_scratch(2, d, tn_up),
        ),
        out_shape=jax.ShapeDtypeStruct((n_rows, de), BF16),
        compiler_params=_cparams(("arbitrary", "arbitrary")),
        name="moe_gate_up",
    )(runs, x_rows, w_gate, w_up)

    y_rows = pl.pallas_call(
        functools.partial(_moe_down_kernel, layer=layer, tn=tn_down),
        grid_spec=pltpu.PrefetchScalarGridSpec(
            num_scalar_prefetch=1,
            grid=(d // tn_down, n_tiles),
            in_specs=[
                pl.BlockSpec((tm, de), lambda j, i, runs: (i, 0)),
                pl.BlockSpec(memory_space=pl.ANY),
                pl.BlockSpec((tm, 1), lambda j, i, runs: (i, 0)),
            ],
            out_specs=pl.BlockSpec((tm, tn_down // LANES, LANES), lambda j, i, runs: (i, j, 0)),
            scratch_shapes=weight_scratch(1, de, tn_down),
        ),
        out_shape=jax.ShapeDtypeStruct((n_rows, slabs, LANES), F32),
        compiler_params=_cparams(("arbitrary", "arbitrary")),
        name="moe_down",
    )(runs, he, w_down, row_weight.reshape(n_rows, 1))

    tm_out = _tile(t, tm_out)
    n_out = t // tm_out
    pos = dest.reshape(n_out, tm_out, TOP_K).transpose(0, 2, 1).reshape(n_out, 1, TOP_K * tm_out)
    row = pl.BlockSpec((tm_out, d), lambda i: (i, 0))
    vec = pl.BlockSpec((1, d), lambda i: (0, 0))
    return pl.pallas_call(
        functools.partial(_moe_combine_ln_kernel, tm=tm_out),
        grid=(n_out,),
        in_specs=[
            pl.BlockSpec((1, 1, TOP_K * tm_out), lambda i: (i, 0, 0), memory_space=pltpu.SMEM),
            pl.BlockSpec((1, 1, TOP_K * tm_out), lambda i: (jnp.minimum(i + 1, n_out - 1), 0, 0),
                         memory_space=pltpu.SMEM),
            row,
            pl.BlockSpec(memory_space=pl.ANY),
            vec, vec,
        ],
        out_specs=[row, row],
        out_shape=[jax.ShapeDtypeStruct((t, d), F32), jax.ShapeDtypeStruct((t, d), BF16)],
        scratch_shapes=gather_scratch(TOP_K * tm_out),
        compiler_params=_cparams(("arbitrary",)),
        name="moe_combine_ln",
    )(pos, pos, h, y_rows, ln_g.reshape(1, d), ln_b.reshape(1, d))


def _shift_mix_kernel(x_ref, xp_ref, mix_ref, *rest, tm, seq):
    o_refs, pad_ref = rest[:-1], rest[-1]
    i = pl.program_id(0)
    starts_sequence = (i * tm) % seq == 0
    pad_ref[0:SUBLANES, :] = jnp.where(starts_sequence, 0.0, xp_ref[...])
    pad_ref[SUBLANES:SUBLANES + tm, :] = x_ref[...]
    xv = x_ref[...]
    xx = pad_ref[SUBLANES - 1:SUBLANES - 1 + tm, :] - xv
    for m, o_ref in enumerate(o_refs):
        o_ref[...] = (xv + xx * mix_ref[m:m + 1, :]).astype(o_ref.dtype)


def _shift_mix(x, mix, seq, *, tm=128):
    t, d = x.shape
    nm = mix.shape[0]
    tm = _tile(seq, tm)
    per = tm // SUBLANES
    return pl.pallas_call(
        functools.partial(_shift_mix_kernel, tm=tm, seq=seq),
        grid=(t // tm,),
        in_specs=[
            pl.BlockSpec((tm, d), lambda i: (i, 0)),
            pl.BlockSpec((SUBLANES, d), lambda i: (jnp.maximum(i * per - 1, 0), 0)),
            pl.BlockSpec((nm, d), lambda i: (0, 0)),
        ],
        out_specs=[pl.BlockSpec((tm, d), lambda i: (i, 0))] * nm,
        out_shape=[jax.ShapeDtypeStruct((t, d), BF16)] * nm,
        scratch_shapes=[pltpu.VMEM((SUBLANES + tm, d), F32)],
        compiler_params=_cparams(("parallel",)),
        name="rwkv_shift_mix",
    )(x, x, mix)


WKV_CHUNK = 64
WKV_HEADS = 4
WKV_WIDTH = WKV_HEADS * RWKV_HEAD
WKV_BLOCK = 512
NEUMANN_STEPS = 5


def _wkv_kernel(r_ref, k_ref, v_ref, wl_ref, ar_ref, g_ref, kk_ref, ka_ref, rk_ref, lnw_ref, lnb_ref,
                o_ref, state_ref, *, tb):
    lc, hw, wd = WKV_CHUNK, RWKV_HEAD, WKV_WIDTH
    nch = tb // lc

    @pl.when(pl.program_id(2) == 0)
    def _():
        state_ref[...] = jnp.zeros_like(state_ref)

    wrow = lax.broadcasted_iota(jnp.int32, (wd, wd), 0)
    wcol = lax.broadcasted_iota(jnp.int32, (wd, wd), 1)
    same_head = wrow // hw == wcol // hw
    ones_bd = same_head.astype(BF16)

    def block_diag(m):
        return jnp.where(same_head, jnp.concatenate([m] * WKV_HEADS, axis=0), 0.0).astype(BF16)

    def head_sum(m):
        hi = m.astype(BF16)
        lo = (m - hi.astype(F32)).astype(BF16)
        return jnp.dot(hi, ones_bd, preferred_element_type=F32) + jnp.dot(lo, ones_bd, preferred_element_type=F32)

    def mm(a, b):
        return jnp.dot(a.astype(BF16), b, preferred_element_type=F32)

    r = r_ref[...]
    k = k_ref[...]
    v = v_ref[...]
    w_log = -_softplus(-wl_ref[...]) - 0.5
    lw = -jnp.exp(w_log)
    a_sig = jax.nn.sigmoid(ar_ref[...])
    kx = k * kk_ref[...]
    kk = kx / jnp.maximum(jnp.sqrt(head_sum(kx * kx)), 1e-12)
    k2 = k * (1.0 + (a_sig - 1.0) * ka_ref[...])
    a_s = -kk
    b_s = kk * a_sig

    trow = lax.broadcasted_iota(jnp.int32, (tb, tb), 0)
    tcol = lax.broadcasted_iota(jnp.int32, (tb, tb), 1)
    same_chunk = trow // lc == tcol // lc
    cum = _dot_split3(lw, (same_chunk & (trow >= tcol)).astype(BF16), split_lhs=False)
    tot = jnp.concatenate([jnp.broadcast_to(cum[(c + 1) * lc - 1:(c + 1) * lc, :], (lc, wd)) for c in range(nch)],
                          axis=0)
    grow = jnp.exp(-cum)
    rt = r * jnp.exp(cum)
    at = a_s * jnp.exp(cum - lw)
    bt = b_s * grow
    kt = k2 * grow
    rest = jnp.exp(tot - cum)
    bh = b_s * rest
    kh = k2 * rest
    p_end = jnp.exp(tot)

    t_idx = lax.broadcasted_iota(jnp.int32, (lc, wd), 0)
    s_idx = lax.broadcasted_iota(jnp.int32, (lc, wd), 1) % hw
    strict = t_idx > s_idx
    incl = t_idx >= s_idx
    eye = (t_idx == s_idx).astype(F32)
    nt = (((1,), (1,)), ((), ()))

    tn = (((0,), (0,)), ((), ()))
    chunks = [slice(c * lc, (c + 1) * lc) for c in range(nch)]

    a_ab, a_rb, a_ak, a_rk = [], [], [], []
    for sl in chunks:
        ar = jnp.concatenate([at[sl], rt[sl]], axis=0).astype(BF16)
        xb = lax.dot_general(ar, block_diag(bt[sl]), nt, preferred_element_type=F32)
        xk = lax.dot_general(ar, block_diag(kt[sl]), nt, preferred_element_type=F32)
        a_ab.append(jnp.where(strict, xb[:lc], 0.0))
        a_rb.append(jnp.where(incl, xb[lc:], 0.0))
        a_ak.append(jnp.where(strict, xk[:lc], 0.0))
        a_rk.append(jnp.where(incl, xk[lc:], 0.0))
    inv = [eye + a for a in a_ab]
    pw = [mm(a, block_diag(a)) for a in a_ab]
    for step in range(1, NEUMANN_STEPS + 1):
        last = step == NEUMANN_STEPS
        lhs = inv if last else [jnp.concatenate([t, p], axis=0) for t, p in zip(inv, pw)]
        prod = [mm(x, block_diag(p)) for x, p in zip(lhs, pw)]
        inv = [t + q[:lc] for t, q in zip(inv, prod)]
        if not last:
            pw = [q[lc:] for q in prod]
    v_bd = [block_diag(v[sl]) for sl in chunks]
    ta = [mm(t, block_diag(at[sl])) for t, sl in zip(inv, chunks)]
    av = [mm(jnp.concatenate([ak, rk], axis=0), vb) for ak, rk, vb in zip(a_ak, a_rk, v_bd)]
    y0 = [x[lc:] for x in av]
    u0 = [mm(t, block_diag(x[:lc])) for t, x in zip(inv, av)]
    gain, add = [], []
    for c, sl in enumerate(chunks):
        bh_b = bh[sl].astype(BF16)
        gain.append(jnp.where(same_head, lax.dot_general(ta[c].astype(BF16), bh_b, tn, preferred_element_type=F32),
                              0.0).astype(BF16))
        uv = jnp.concatenate([u0[c], v[sl]], axis=0).astype(BF16)
        bk = jnp.concatenate([bh_b, kh[sl].astype(BF16)], axis=0)
        add.append(jnp.where(same_head, lax.dot_general(uv, bk, tn, preferred_element_type=F32), 0.0))

    states = [state_ref[...]]
    for c in range(nch):
        s0 = states[-1]
        states.append(s0 * p_end[c * lc:c * lc + 1, :] + mm(s0, gain[c]) + add[c])
    state_ref[...] = states[nch]

    ys = []
    for c, sl in enumerate(chunks):
        tr = jnp.concatenate([ta[c], rt[sl]], axis=0).astype(BF16)
        xs = lax.dot_general(tr, states[c].astype(BF16), nt, preferred_element_type=F32)
        u = xs[:lc] + u0[c]
        ys.append(xs[lc:] + mm(a_rb[c], block_diag(u)) + y0[c])

    y = jnp.concatenate(ys, axis=0)
    mu = head_sum(y) / hw
    dev = y - mu
    var = head_sum(dev * dev) / hw
    yn = dev * lax.rsqrt(var + RWKV_GN_EPS) * lnw_ref[...] + lnb_ref[...]
    bonus = head_sum(r * k2 * rk_ref[...]) * v
    o_ref[...] = ((yn + bonus) * g_ref[...]).astype(o_ref.dtype)


def _wkv(r, k, v, wl, araw, g, k_k, k_a, r_k, ln_w, ln_b, batch, seq):
    t, d = r.shape
    tb = _tile(seq, WKV_BLOCK)
    nt = seq // tb
    wd = WKV_WIDTH
    rows = pl.BlockSpec((tb, wd), lambda b, h, n: (b * nt + n, h))
    vec = pl.BlockSpec((1, wd), lambda b, h, n: (0, h))
    return pl.pallas_call(
        functools.partial(_wkv_kernel, tb=tb),
        grid=(batch, d // wd, nt),
        in_specs=[rows] * 6 + [vec] * 5,
        out_specs=rows,
        out_shape=jax.ShapeDtypeStruct((t, d), BF16),
        scratch_shapes=[pltpu.VMEM((wd, wd), F32)],
        compiler_params=_cparams(("parallel", "parallel", "arbitrary")),
        name="rwkv_wkv",
    )(r, k, v, wl, araw, g, k_k.reshape(1, d), k_a.reshape(1, d), r_k.reshape(1, d),
      ln_w.reshape(1, d), ln_b.reshape(1, d))


def _rwkv7_time_mix(x2, batch, seq, mix, w_r, w_k, w_v, w_o, w0, w1, w2, a0, a1, a2, g1, g2, k_k, k_a, r_k,
                    ln_w, ln_b):
    xm = _shift_mix(x2, mix, seq)
    bf = lambda w: w.astype(BF16)
    r = _matmul(xm[0], bf(w_r))
    k = _matmul(xm[2], bf(w_k))
    v = _matmul(xm[3], bf(w_v))
    wl = _low_rank(xm[1], bf(w1), bf(w2), w0, act="tanh")
    araw = _low_rank(xm[4], bf(a1), bf(a2), a0)
    g = _low_rank(xm[5], bf(g1), bf(g2), act="sigmoid")
    yg = _wkv(r, k, v, wl, araw, g, k_k, k_a, r_k, ln_w, ln_b, batch, seq)
    return _matmul(yg, bf(w_o))


def kernel(x, ab_w_in, ab_b_qkv, ssm_conv_w, ssm_conv_b, ssm_dt_bias, ssm_a_log, ssm_d, ssm_norm_w, attn_sinks,
           ab_w_out, ab_b_out, rwkv_mix, rwkv_w_r, rwkv_w_k, rwkv_w_v, rwkv_w_o, rwkv_w0, rwkv_w1, rwkv_w2,
           rwkv_a0, rwkv_a1, rwkv_a2, rwkv_g1, rwkv_g2, rwkv_k_k, rwkv_k_a, rwkv_r_k, rwkv_ln_w, rwkv_ln_b,
           ln_mix_g, ln_mix_b, ln_ffn_g, ln_ffn_b, router_w, router_b, moe_w_gate, moe_w_up, moe_w_down):
    batch, seq, d = x.shape
    x2 = x.reshape(batch * seq, d)
    xb = x2.astype(BF16)
    for layer in range(DEPTH):
        i = layer // 2
        if layer % 2 == 0:
            mix = _ssd_swa_mixer(x2, xb, batch, seq, ab_w_in[i], ab_b_qkv[i], ssm_conv_w[i], ssm_conv_b[i],
                                 ssm_dt_bias[i], ssm_a_log[i], ssm_d[i], ssm_norm_w[i], attn_sinks[i],
                                 ab_w_out[i], ab_b_out[i])
        else:
            mix = _rwkv7_time_mix(x2, batch, seq, rwkv_mix[i], rwkv_w_r[i], rwkv_w_k[i], rwkv_w_v[i], rwkv_w_o[i],
                                  rwkv_w0[i], rwkv_w1[i], rwkv_w2[i], rwkv_a0[i], rwkv_a1[i], rwkv_a2[i],
                                  rwkv_g1[i], rwkv_g2[i], rwkv_k_k[i], rwkv_k_a[i], rwkv_r_k[i].reshape(-1),
                                  rwkv_ln_w[i], rwkv_ln_b[i])
        h, h3 = _add_layer_norm(x2, mix, ln_mix_g[layer], ln_mix_b[layer])
        x2, xb = _moe_block(h, h3, router_w, router_b, moe_w_gate, moe_w_up, moe_w_down, layer,
                            ln_ffn_g[layer], ln_ffn_b[layer])
    return x2.reshape(batch, seq, d)
```
